```python
import math
import jax, jax.numpy as jnp
from jax import lax
import numpy as np

D_MODEL = 4096
BATCH = 4
SEQ = 2048
DEPTH = 4
DEC_BATCH = 128
DEC_SEQ = 4
PAST_LEN = 16384
PAGE_SIZE = 128

D_MIX = D_MODEL
W_A = D_MIX // 4
S5_GROUP = 16
G_A = W_A // S5_GROUP
P_A = 64
W_B = D_MIX // 4
H_B = 8
BLK_B = W_B // H_B
LRU_C = 8.0
CONV_W = 4
W_C = D_MIX // 4
H_C = 4
DV_C = W_C // H_C
DK_C = DV_C // 2
K_C = H_C * DK_C
GATE_RANK_C = 16
GATE_TAU_C = 16.0
CHUNK_C = 64
W_D = D_MIX - W_A - W_B - W_C
H_D = 8
DV_D = W_D // H_D
DK_D = 128
K_D = H_D * DK_D
QKV_D = 2 * K_D + W_D
CHUNK_D = 64
D_PLE = 256
EPS = 1e-6
IN_SIZES = (W_A, W_A, W_B, W_B, K_C, K_C, W_C, W_C, GATE_RANK_C, QKV_D, W_D, H_D, H_D)
N_IN = sum(IN_SIZES)

kernel_name = "hybrid_s5_rglru_gla_gdn_decode_step"


def rms_norm(x, g):
    xf = x.astype(jnp.float32)
    y = xf * lax.rsqrt(jnp.mean(xf * xf, axis=-1, keepdims=True) + EPS)
    return (y * g.astype(jnp.float32)).astype(x.dtype)


def l2_norm(x):
    xf = x.astype(jnp.float32)
    return (xf * lax.rsqrt(jnp.sum(xf * xf, axis=-1, keepdims=True) + EPS)).astype(x.dtype)


def causal_conv(x, buf, w):
    T = x.shape[1]
    xp = jnp.concatenate([buf, x], axis=1)
    y = sum(xp[:, j:j + T] * w[j] for j in range(CONV_W))
    return y, xp[:, -(CONV_W - 1):]


def s5_mixer(x, h0_re, h0_im, lam_re, lam_im, log_dt, b_re, b_im, c_re, c_im, d_skip):
    Bsz, T, _ = x.shape
    xg = x.reshape(Bsz, T, G_A, S5_GROUP)
    dt = jnp.exp(log_dt)[:, None]
    ai = lam_im * dt
    mag = jnp.exp(lam_re * dt)
    abar_re, abar_im = mag * jnp.cos(ai), mag * jnp.sin(ai)
    den = lam_re * lam_re + lam_im * lam_im
    nr = abar_re - 1.0
    coef_re = (nr * lam_re + abar_im * lam_im) / den
    coef_im = (abar_im * lam_re - nr * lam_im) / den
    bb_re = coef_re[..., None] * b_re - coef_im[..., None] * b_im
    bb_im = coef_re[..., None] * b_im + coef_im[..., None] * b_re
    u_re = jnp.einsum('btgh,gph->btgp', xg, bb_re)
    u_im = jnp.einsum('btgh,gph->btgp', xg, bb_im)
    u_re = u_re.at[:, 0].add(abar_re * h0_re - abar_im * h0_im)
    u_im = u_im.at[:, 0].add(abar_re * h0_im + abar_im * h0_re)
    a_re = jnp.broadcast_to(abar_re, u_re.shape)
    a_im = jnp.broadcast_to(abar_im, u_im.shape)

    def combine(e1, e2):
        a1r, a1i, b1r, b1i = e1
        a2r, a2i, b2r, b2i = e2
        return (a2r * a1r - a2i * a1i, a2r * a1i + a2i * a1r,
                a2r * b1r - a2i * b1i + b2r, a2r * b1i + a2i * b1r + b2i)

    _, _, h_re, h_im = lax.associative_scan(combine, (a_re, a_im, u_re, u_im), axis=1)
    y = (jnp.einsum('btgp,ghp->btgh', h_re, c_re) - jnp.einsum('btgp,ghp->btgh', h_im, c_im)
         + d_skip * xg)
    return y.reshape(Bsz, T, W_A), h_re[:, -1], h_im[:, -1]


def rglru_mixer(x, h0, conv_buf, conv_w, conv_b, w_r, b_r, w_i, b_i, lam):
    Bsz, T, _ = x.shape
    xc, new_buf = causal_conv(x, conv_buf, conv_w)
    xc = xc + conv_b
    xh = xc.reshape(Bsz, T, H_B, BLK_B)
    r = jax.nn.sigmoid(jnp.einsum('bthi,hij->bthj', xh, w_r).reshape(Bsz, T, W_B) + b_r)
    ig = jax.nn.sigmoid(jnp.einsum('bthi,hij->bthj', xh, w_i).reshape(Bsz, T, W_B) + b_i)
    log_a = -LRU_C * r * jax.nn.softplus(-lam)
    a = jnp.exp(log_a)
    bx = jnp.sqrt(-jnp.expm1(2.0 * log_a)) * (ig * xc)
    bx = bx.at[:, 0].add(a[:, 0] * h0)

    def combine(e1, e2):
        return (e1[0] * e2[0], e2[0] * e1[1] + e2[1])

    _, h = lax.associative_scan(combine, (a, bx), axis=1)
    return h, h[:, -1], new_buf


def gla_mixer(q, k, v, g_low, S0, w_gate, b_gate, g_out):
    Bsz, T, _ = q.shape
    gk = jax.nn.log_sigmoid(g_low @ w_gate + b_gate) / GATE_TAU_C
    c = math.gcd(T, CHUNK_C)
    n = T // c

    def heads(t, d):
        return t.reshape(Bsz, n, c, H_C, d).transpose(1, 0, 3, 2, 4)

    xs = (heads(q * DK_C ** -0.5, DK_C), heads(k, DK_C), heads(v, DV_C), heads(gk, DK_C))
    causal = jnp.tril(jnp.ones((c, c), dtype=bool))

    def step(S, inp):
        qc, kc, vc, gc = inp
        b = jnp.cumsum(gc, axis=2)
        diff = b[:, :, :, None, :] - b[:, :, None, :, :]
        decay = jnp.exp(jnp.where(causal[:, :, None], diff, -jnp.inf))
        att = jnp.einsum('bhtd,bhsd,bhtsd->bhts', qc, kc, decay)
        o = (jnp.einsum('bhts,bhsv->bhtv', att, vc)
             + jnp.einsum('bhtd,bhdv->bhtv', qc * jnp.exp(b), S))
        b_last = b[:, :, -1:, :]
        S_new = (jnp.exp(b_last[:, :, 0, :])[..., None] * S
                 + jnp.einsum('bhsd,bhsv->bhdv', kc * jnp.exp(b_last - b), vc))
        return S_new, o

    S_fin, o = lax.scan(step, S0, xs)
    o = o.transpose(1, 0, 3, 2, 4).reshape(Bsz, T, H_C, DV_C)
    return rms_norm(o, g_out).reshape(Bsz, T, W_C), S_fin


def gated_delta_mixer(qkv, a_in, b_in, S0, conv_buf, conv_w, a_log, dt_bias, g_out):
    Bsz, T, _ = qkv.shape
    qkv, new_buf = causal_conv(qkv, conv_buf, conv_w)
    qkv = jax.nn.silu(qkv)
    q, k, v = jnp.split(qkv, [K_D, 2 * K_D], axis=-1)
    q = l2_norm(q.reshape(Bsz, T, H_D, DK_D)) * DK_D ** -0.5
    k = l2_norm(k.reshape(Bsz, T, H_D, DK_D))
    v = v.reshape(Bsz, T, H_D, DV_D)
    beta = jax.nn.sigmoid(b_in)
    g = -jnp.exp(a_log) * jax.nn.softplus(a_in + dt_bias)
    c = math.gcd(T, CHUNK_D)
    n = T // c

    def heads(t):
        return t.reshape(Bsz, n, c, H_D, t.shape[-1]).transpose(1, 0, 3, 2, 4)

    def scal(t):
        return t.reshape(Bsz, n, c, H_D).transpose(1, 0, 3, 2)

    xs = (heads(q), heads(k), heads(v), scal(beta), scal(g))
    incl = jnp.tril(jnp.ones((c, c), dtype=bool))
    strict = jnp.tril(jnp.ones((c, c), dtype=bool), -1)

    def step(S, inp):
        qc, kc, vc, bc, gc = inp
        gam = jnp.cumsum(gc, axis=-1)
        decay = jnp.exp(jnp.where(incl, gam[..., :, None] - gam[..., None, :], -jnp.inf))
        m = jnp.where(strict, bc[..., :, None] * jnp.einsum('bhtd,bhsd->bhts', kc, kc) * decay, 0.0)
        eg = jnp.exp(gam)[..., None]
        rhs = bc[..., None] * (vc - eg * jnp.einsum('bhtd,bhdv->bhtv', kc, S))
        u = lax.linalg.triangular_solve(m.astype(jnp.float32), rhs.astype(jnp.float32),
                                        left_side=True, lower=True,
                                        unit_diagonal=True).astype(vc.dtype)
        o = (eg * jnp.einsum('bhtd,bhdv->bhtv', qc, S)
             + jnp.einsum('bhts,bhsv->bhtv', jnp.einsum('bhtd,bhsd->bhts', qc, kc) * decay, u))
        S_new = (jnp.exp(gam[..., -1])[..., None, None] * S
                 + jnp.einsum('bhsd,bhsv->bhdv', kc * jnp.exp(gam[..., -1:] - gam)[..., None], u))
        return S_new, o

    S_fin, o = lax.scan(step, S0, xs)
    o = o.transpose(1, 0, 3, 2, 4).reshape(Bsz, T, H_D, DV_D)
    return rms_norm(o, g_out).reshape(Bsz, T, W_D), S_fin, new_buf


def mixer_layer(h, pe, st, wl):
    s5_re0, s5_im0, lru_h0, lru_conv0, gla_s0, dn_s0, dn_conv0 = st
    (g_norm, w_in, lam_re, lam_im, log_dt, b_re, b_im, c_re, c_im, d_skip, w_glu, b_glu,
     lru_cw, lru_cb, lru_wr, lru_br, lru_wi, lru_bi, lru_lam,
     gla_wg, gla_bg, gla_go, dn_cw, dn_alog, dn_dtb, dn_go,
     w_out, ple_w, ple_gw, ple_gb) = wl
    hn = rms_norm(h, g_norm)
    proj = hn @ w_in
    splits = np.cumsum(IN_SIZES)[:-1].tolist()
    xa, za, xb, zb, qc, kc, vc, zc, gc, qkvd, zd, ad, bd = jnp.split(proj, splits, axis=-1)
    ya, s5_re, s5_im = s5_mixer(xa, s5_re0, s5_im0, lam_re, lam_im, log_dt, b_re, b_im, c_re, c_im, d_skip)
    ga = jax.nn.gelu(ya)
    ya = ga * jax.nn.sigmoid(ga @ w_glu + b_glu)
    yb, lru_h, lru_conv = rglru_mixer(xb, lru_h0, lru_conv0, lru_cw, lru_cb, lru_wr, lru_br, lru_wi, lru_bi, lru_lam)
    yc, gla_s = gla_mixer(qc, kc, vc, gc, gla_s0, gla_wg, gla_bg, gla_go)
    yd, dn_s, dn_conv = gated_delta_mixer(qkvd, ad, bd, dn_s0, dn_conv0, dn_cw, dn_alog, dn_dtb, dn_go)
    mix = jnp.concatenate([ya * jax.nn.silu(za), yb * jax.nn.silu(zb),
                           yc * jax.nn.silu(zc), yd * jax.nn.silu(zd)], axis=-1)
    h = h + mix @ w_out
    h = h + jax.nn.sigmoid(h @ ple_gw + ple_gb) * (pe @ ple_w)
    return h, (s5_re, s5_im, lru_h, lru_conv, gla_s, dn_s, dn_conv)


def setup_inputs(seed: int = 0) -> dict:
    key = jax.random.key(seed)
    keys = list(jax.random.split(key, 64))
    f32 = jnp.float32

    def nrm(shape, scale=1.0):
        return scale * jax.random.normal(keys.pop(), shape, f32)

    def unif(shape, lo, hi):
        return jax.random.uniform(keys.pop(), shape, f32, lo, hi)

    L = DEPTH
    d = {}
    d['x_prompt'] = nrm((BATCH, SEQ, D_MODEL))
    d['x_sample'] = nrm((DEC_BATCH, DEC_SEQ, D_MODEL))
    d['state_s5_re'] = nrm((L, DEC_BATCH, G_A, P_A), 0.1)
    d['state_s5_im'] = nrm((L, DEC_BATCH, G_A, P_A), 0.1)
    d['state_lru_h'] = nrm((L, DEC_BATCH, W_B), 0.5)
    d['state_lru_conv'] = nrm((L, DEC_BATCH, CONV_W - 1, W_B))
    d['state_gla'] = nrm((L, DEC_BATCH, H_C, DK_C, DV_C), 0.5)
    d['state_delta'] = nrm((L, DEC_BATCH, H_D, DK_D, DV_D), 0.5)
    d['state_delta_conv'] = nrm((L, DEC_BATCH, CONV_W - 1, QKV_D))
    d['p_prompt'] = nrm((L, BATCH, SEQ, D_PLE))
    d['p_sample'] = nrm((L, DEC_BATCH, DEC_SEQ, D_PLE))
    d['g_norm'] = 1.0 + nrm((L, D_MODEL), 0.02)
    d['w_in'] = nrm((L, D_MODEL, N_IN), D_MODEL ** -0.5)
    d['s5_lam_re'] = -0.5 + nrm((L, G_A, P_A), 0.01)
    d['s5_lam_im'] = jnp.pi * jnp.arange(P_A, dtype=f32) + nrm((L, G_A, P_A), 0.01)
    d['s5_log_dt'] = unif((L, G_A), math.log(1e-3), math.log(1e-1))
    d['s5_b_re'] = nrm((L, G_A, P_A, S5_GROUP), (2 * S5_GROUP) ** -0.5)
    d['s5_b_im'] = nrm((L, G_A, P_A, S5_GROUP), (2 * S5_GROUP) ** -0.5)
    d['s5_c_re'] = nrm((L, G_A, S5_GROUP, P_A), P_A ** -0.5)
    d['s5_c_im'] = nrm((L, G_A, S5_GROUP, P_A), P_A ** -0.5)
    d['s5_d'] = nrm((L, G_A, S5_GROUP))
    d['s5_w_glu'] = nrm((L, W_A, W_A), W_A ** -0.5)
    d['s5_b_glu'] = nrm((L, W_A), 0.01)
    d['lru_conv_w'] = nrm((L, CONV_W, W_B), CONV_W ** -0.5)
    d['lru_conv_b'] = nrm((L, W_B), 0.01)
    d['lru_w_r'] = nrm((L, H_B, BLK_B, BLK_B), BLK_B ** -0.5)
    d['lru_b_r'] = nrm((L, W_B), 0.01)
    d['lru_w_i'] = nrm((L, H_B, BLK_B, BLK_B), BLK_B ** -0.5)
    d['lru_b_i'] = nrm((L, W_B), 0.01)
    a_c = unif((L, W_B), 0.9, 0.999)
    sig = a_c ** (1.0 / LRU_C)
    d['lru_lam'] = jnp.log(sig) - jnp.log1p(-sig)
    d['gla_w_gate'] = nrm((L, GATE_RANK_C, K_C), GATE_RANK_C ** -0.5)
    d['gla_b_gate'] = nrm((L, K_C), 0.01)
    d['gla_g_out'] = 1.0 + nrm((L, DV_C), 0.02)
    d['dn_conv_w'] = nrm((L, CONV_W, QKV_D), CONV_W ** -0.5)
    d['dn_a_log'] = jnp.log(unif((L, H_D), 1.0, 16.0))
    dt = jnp.exp(unif((L, H_D), math.log(1e-3), math.log(1e-1)))
    d['dn_dt_bias'] = dt + jnp.log(-jnp.expm1(-dt))
    d['dn_g_out'] = 1.0 + nrm((L, DV_D), 0.02)
    d['w_out'] = nrm((L, D_MIX, D_MODEL), D_MIX ** -0.5)
    d['ple_w'] = nrm((L, D_PLE, D_MODEL), D_PLE ** -0.5)
    d['ple_gate_w'] = nrm((L, D_MODEL, D_MODEL), D_MODEL ** -0.5)
    d['ple_gate_b'] = nrm((L, D_MODEL), 0.01)
    d['g_final'] = 1.0 + nrm((D_MODEL,), 0.02)
    return d


def reference(x_prompt, x_sample, state_s5_re, state_s5_im, state_lru_h, state_lru_conv,
              state_gla, state_delta, state_delta_conv, p_prompt, p_sample,
              g_norm, w_in, s5_lam_re, s5_lam_im, s5_log_dt, s5_b_re, s5_b_im, s5_c_re, s5_c_im,
              s5_d, s5_w_glu, s5_b_glu, lru_conv_w, lru_conv_b, lru_w_r, lru_b_r, lru_w_i, lru_b_i,
              lru_lam, gla_w_gate, gla_b_gate, gla_g_out, dn_conv_w, dn_a_log, dn_dt_bias, dn_g_out,
              w_out, ple_w, ple_gate_w, ple_gate_b, g_final):
    weights = (g_norm, w_in, s5_lam_re, s5_lam_im, s5_log_dt, s5_b_re, s5_b_im, s5_c_re, s5_c_im,
               s5_d, s5_w_glu, s5_b_glu, lru_conv_w, lru_conv_b, lru_w_r, lru_b_r, lru_w_i, lru_b_i,
               lru_lam, gla_w_gate, gla_b_gate, gla_g_out, dn_conv_w, dn_a_log, dn_dt_bias, dn_g_out,
               w_out, ple_w, ple_gate_w, ple_gate_b)
    Bp = x_prompt.shape[0]
    dtp = x_prompt.dtype
    hp, hs = x_prompt, x_sample
    new_p, new_s = [], []
    for i in range(DEPTH):
        wl = tuple(w[i] for w in weights)
        zero_state = (jnp.zeros((Bp, G_A, P_A), dtp), jnp.zeros((Bp, G_A, P_A), dtp),
                      jnp.zeros((Bp, W_B), dtp), jnp.zeros((Bp, CONV_W - 1, W_B), dtp),
                      jnp.zeros((Bp, H_C, DK_C, DV_C), dtp), jnp.zeros((Bp, H_D, DK_D, DV_D), dtp),
                      jnp.zeros((Bp, CONV_W - 1, QKV_D), dtp))
        hp, st_p = mixer_layer(hp, p_prompt[i], zero_state, wl)
        cache_i = (state_s5_re[i], state_s5_im[i], state_lru_h[i], state_lru_conv[i],
                   state_gla[i], state_delta[i], state_delta_conv[i])
        hs, st_s = mixer_layer(hs, p_sample[i], cache_i, wl)
        new_p.append(st_p)
        new_s.append(st_s)
    y_prompt = rms_norm(hp, g_final)
    y_sample = rms_norm(hs, g_final)

    def stk(lst, j):
        return jnp.stack([s[j] for s in lst], axis=0)

    return (y_prompt, y_sample,
            stk(new_p, 0), stk(new_p, 1), stk(new_p, 2), stk(new_p, 3), stk(new_p, 4), stk(new_p, 5), stk(new_p, 6),
            stk(new_s, 0), stk(new_s, 1), stk(new_s, 2), stk(new_s, 3), stk(new_s, 4), stk(new_s, 5), stk(new_s, 6))
```

```python
import functools
import math

import jax
import jax.numpy as jnp
from jax import lax
from jax.experimental import pallas as pl
from jax.experimental.pallas import tpu as pltpu

F32 = jnp.float32
BF16 = jnp.bfloat16
EPS = 1e-6
NEG_INF = float("-inf")

D_MODEL = 4096
DEPTH = 4
W_MIX = 1024
S5_GROUPS = 64
S5_GROUP = 16
S5_STATE = 64
S5_CH = S5_GROUPS * S5_STATE
LRU_BLOCKS = 8
LRU_BLK = 128
LRU_C = 8.0
CONV_W = 4
GLA_H = 4
GLA_DK = 128
GLA_DV = 256
GLA_RANK = 16
GLA_TAU = 16.0
GLA_CHUNK = 64
DN_H = 8
DN_DK = 128
DN_DV = 128
DN_QKV = 3072
DN_CHUNK = 64
D_PLE = 256
N_MAIN = 11264
COL_XA, COL_ZA, COL_XB, COL_ZB = 0, 1024, 2048, 3072
COL_QC, COL_KC, COL_VC = 4096, 4608, 5120
COL_QKVD, COL_ZC, COL_ZD = 6144, 9216, 10240
ORIG_ZC, ORIG_GC, ORIG_QKVD, ORIG_ZD, ORIG_AB = 6144, 7168, 7184, 10256, 11280
TAIL_W = 128
TAIL_G, TAIL_A, TAIL_B = 0, 16, 24
SAMPLE_T_PAD = 8

V7X_LANES = 128
V7X_SUBLANES = 8
VMEM_LIMIT = 52 * 1024 * 1024


def _cparams(n_axes):
    return pltpu.CompilerParams(dimension_semantics=("arbitrary",) * n_axes,
                                vmem_limit_bytes=VMEM_LIMIT)


def _sigmoid(x):
    return jax.nn.sigmoid(x)


def _silu(x):
    return x * jax.nn.sigmoid(x)


def _softplus(x):
    return jnp.maximum(x, 0.0) + jnp.log1p(jnp.exp(-jnp.abs(x)))


def _expm1(x):
    u = jnp.exp(x)
    um1 = u - 1.0
    lg = jnp.log(u)
    r = um1 * x / jnp.where(lg == 0.0, 1.0, lg)
    return jnp.where(um1 == 0.0, x, jnp.where(um1 == -1.0, -1.0, r))


def _log_sigmoid(x):
    return jnp.minimum(x, 0.0) - jnp.log1p(jnp.exp(-jnp.abs(x)))


def _gelu_tanh(x):
    c = math.sqrt(2.0 / math.pi)
    return x * (0.5 * (1.0 + jnp.tanh(c * (x + 0.044715 * (x * x * x)))))


def _dot(a, b):
    return jnp.dot(a, b, preferred_element_type=F32)


def _dot_nt(a, b):
    return lax.dot_general(a, b, (((1,), (1,)), ((), ())), preferred_element_type=F32)


def _dot_tn(a, b):
    return lax.dot_general(a, b, (((0,), (0,)), ((), ())), preferred_element_type=F32)


def _split3(x):
    h = x.astype(BF16)
    r = x - h.astype(F32)
    m = r.astype(BF16)
    l = (r - m.astype(F32)).astype(BF16)
    return h, m, l


def _cumsum_rows(x):
    c = x.shape[0]
    if c <= V7X_SUBLANES:
        row = lax.broadcasted_iota(jnp.int32, x.shape, 0)
        k = 1
        while k < c:
            x = x + jnp.where(row >= k, pltpu.roll(x, k, 0), 0.0)
            k *= 2
        return x
    tri = (lax.broadcasted_iota(jnp.int32, (c, c), 0) >= lax.broadcasted_iota(jnp.int32, (c, c), 1)).astype(BF16)
    h, m, l = _split3(x)
    return _dot(tri, h) + _dot(tri, m) + _dot(tri, l)


def _mm(a, b, dot=_dot, contract=1):
    if a.shape[contract] < 2 * V7X_SUBLANES:
        return dot(a, b)
    return dot(a.astype(BF16), b.astype(BF16))


def _dot_x3(a, b):
    ah = a.astype(BF16)
    al = (a - ah.astype(F32)).astype(BF16)
    bh = b.astype(BF16)
    bl = (b - bh.astype(F32)).astype(BF16)
    return _dot(ah, bh) + _dot(ah, bl) + _dot(al, bh)


def _in_proj_kernel(x_ref, g_ref, w_ref, wt_ref, o_ref, ot_ref, hn_ref, *, row_chunk):
    j = pl.program_id(1)
    tm = x_ref.shape[0]

    @pl.when(j == 0)
    def _():
        def body(r, carry):
            rows = pl.ds(pl.multiple_of(r * row_chunk, row_chunk), row_chunk)
            x = x_ref[rows, :]
            ms = jnp.mean(x * x, axis=-1, keepdims=True)
            hn_ref[rows, :] = ((x * lax.rsqrt(ms + EPS)) * g_ref[...]).astype(BF16)
            return carry
        lax.fori_loop(0, tm // row_chunk, body, 0)
        ot_ref[...] = _dot(hn_ref[...], wt_ref[...])

    o_ref[...] = _dot(hn_ref[...], w_ref[...])


def in_proj(h, g_norm, w_main, w_tail, layer, *, tm, tn):
    m = h.shape[0]
    grid = (m // tm, N_MAIN // tn)
    return pl.pallas_call(
        functools.partial(_in_proj_kernel, row_chunk=64),
        grid=grid,
        in_specs=[
            pl.BlockSpec((tm, D_MODEL), lambda i, j: (i, 0)),
            pl.BlockSpec((None, 1, D_MODEL), lambda i, j: (layer, 0, 0)),
            pl.BlockSpec((None, D_MODEL, tn), lambda i, j: (layer, 0, j)),
            pl.BlockSpec((None, D_MODEL, TAIL_W), lambda i, j: (layer, 0, 0)),
        ],
        out_specs=[
            pl.BlockSpec((tm, tn), lambda i, j: (i, j)),
            pl.BlockSpec((tm, TAIL_W), lambda i, j: (i, 0)),
        ],
        out_shape=[jax.ShapeDtypeStruct((m, N_MAIN), F32), jax.ShapeDtypeStruct((m, TAIL_W), F32)],
        scratch_shapes=[pltpu.VMEM((tm, D_MODEL), BF16)],
        compiler_params=_cparams(2),
        name="in_proj",
    )(h, g_norm, w_main, w_tail)


def _out_proj_kernel(ma_ref, mb_ref, mc_ref, md_ref, w_ref, h_ref, o_ref):
    acc = h_ref[...]
    for k, m_ref in enumerate((ma_ref, mb_ref, mc_ref, md_ref)):
        acc = acc + _dot(m_ref[...], w_ref[k * W_MIX:(k + 1) * W_MIX, :])
    o_ref[...] = acc


def out_proj(mixes, w_out, h, layer, *, tm, tn):
    m = h.shape[0]
    grid = (m // tm, D_MODEL // tn)
    mix_spec = pl.BlockSpec((tm, W_MIX), lambda i, j: (i, 0))
    return pl.pallas_call(
        _out_proj_kernel,
        grid=grid,
        in_specs=[mix_spec, mix_spec, mix_spec, mix_spec,
                  pl.BlockSpec((None, 4 * W_MIX, tn), lambda i, j: (layer, 0, j)),
                  pl.BlockSpec((tm, tn), lambda i, j: (i, j))],
        out_specs=pl.BlockSpec((tm, tn), lambda i, j: (i, j)),
        out_shape=jax.ShapeDtypeStruct((m, D_MODEL), F32),
        compiler_params=_cparams(2),
        name="out_proj",
    )(*mixes, w_out, h)


def _ple_kernel(hrow_ref, htile_ref, pe_ref, wg_ref, bg_ref, wp_ref, o_ref, hb_ref, *, row_chunk):
    j = pl.program_id(1)
    tm = hrow_ref.shape[0]

    @pl.when(j == 0)
    def _():
        def body(r, carry):
            rows = pl.ds(pl.multiple_of(r * row_chunk, row_chunk), row_chunk)
            hb_ref[rows, :] = hrow_ref[rows, :].astype(BF16)
            return carry
        lax.fori_loop(0, tm // row_chunk, body, 0)

    gate = _sigmoid(_dot(hb_ref[...], wg_ref[...]) + bg_ref[...])
    pv = _dot(pe_ref[...].astype(BF16), wp_ref[...])
    o_ref[...] = htile_ref[...] + gate * pv


def ple(h, pe, w_gate, b_gate, w_ple, layer, *, tm, tn):
    m = h.shape[0]
    grid = (m // tm, D_MODEL // tn)
    return pl.pallas_call(
        functools.partial(_ple_kernel, row_chunk=64),
        grid=grid,
        in_specs=[
            pl.BlockSpec((tm, D_MODEL), lambda i, j: (i, 0)),
            pl.BlockSpec((tm, tn), lambda i, j: (i, j)),
            pl.BlockSpec((None, tm, D_PLE), lambda i, j: (layer, i, 0)),
            pl.BlockSpec((None, D_MODEL, tn), lambda i, j: (layer, 0, j)),
            pl.BlockSpec((None, 1, tn), lambda i, j: (layer, 0, j)),
            pl.BlockSpec((None, D_PLE, tn), lambda i, j: (layer, 0, j)),
        ],
        out_specs=pl.BlockSpec((tm, tn), lambda i, j: (i, j)),
        out_shape=jax.ShapeDtypeStruct((m, D_MODEL), F32),
        scratch_shapes=[pltpu.VMEM((tm, D_MODEL), BF16)],
        compiler_params=_cparams(2),
        name="ple",
    )(h, h, pe, w_gate, b_gate, w_ple)


def _final_norm_kernel(x_ref, g_ref, o_ref):
    x = x_ref[...]
    ms = jnp.mean(x * x, axis=-1, keepdims=True)
    o_ref[...] = (x * lax.rsqrt(ms + EPS)) * g_ref[...]


def final_norm(h, g, *, tm):
    m = h.shape[0]
    return pl.pallas_call(
        _final_norm_kernel,
        grid=(m // tm,),
        in_specs=[pl.BlockSpec((tm, D_MODEL), lambda i: (i, 0)),
                  pl.BlockSpec((1, D_MODEL), lambda i: (0, 0))],
        out_specs=pl.BlockSpec((tm, D_MODEL), lambda i: (i, 0)),
        out_shape=jax.ShapeDtypeStruct((m, D_MODEL), F32),
        compiler_params=_cparams(1),
        name="final_norm",
    )(h, g)


def _s5_prep_kernel(lre_ref, lim_ref, ldt_ref, coef_ref, cre_ref, cim_ref):
    lre = lre_ref[...]
    lim = lim_ref[...]
    dt = jnp.exp(ldt_ref[...])
    ai = lim * dt
    mag = jnp.exp(lre * dt)
    ar = mag * jnp.cos(ai)
    aim = mag * jnp.sin(ai)
    den = lre * lre + lim * lim
    nr = ar - 1.0
    cre_ref[...] = (nr * lre + aim * lim) / den
    cim_ref[...] = (aim * lre - nr * lim) / den
    pr, pim = [ar], [aim]
    for _ in range(V7X_SUBLANES - 1):
        nr_, ni_ = pr[-1] * ar - pim[-1] * aim, pr[-1] * aim + pim[-1] * ar
        pr.append(nr_)
        pim.append(ni_)
    shape = (V7X_SUBLANES, S5_CH)
    row = lax.broadcasted_iota(jnp.int32, shape, 0)

    def bc(x):
        return jnp.broadcast_to(x, shape)

    for lvl, k in enumerate((1, 2, 4)):
        coef_ref[2 * lvl] = jnp.where(row >= k, bc(pr[k - 1]), 0.0)
        coef_ref[2 * lvl + 1] = jnp.where(row >= k, bc(pim[k - 1]), 0.0)
    accr = jnp.zeros(shape, F32)
    acci = jnp.zeros(shape, F32)
    for r in range(V7X_SUBLANES):
        accr = jnp.where(row == r, bc(pr[r]), accr)
        acci = jnp.where(row == r, bc(pim[r]), acci)
    coef_ref[6] = accr
    coef_ref[7] = acci


def s5_prep(lam_re, lam_im, log_dt_rep):
    nl = lam_re.shape[0]
    vec = pl.BlockSpec((None, 1, S5_CH), lambda l: (l, 0, 0))
    return pl.pallas_call(
        _s5_prep_kernel,
        grid=(nl,),
        in_specs=[vec, vec, vec],
        out_specs=[pl.BlockSpec((None, 8, V7X_SUBLANES, S5_CH), lambda l: (l, 0, 0, 0)), vec, vec],
        out_shape=[jax.ShapeDtypeStruct((nl, 8, V7X_SUBLANES, S5_CH), F32),
                   jax.ShapeDtypeStruct((nl, 1, S5_CH), F32),
                   jax.ShapeDtypeStruct((nl, 1, S5_CH), F32)],
        compiler_params=_cparams(1),
        name="s5_prep",
    )(lam_re, lam_im, log_dt_rep)


def _s5_bb_kernel(cr_ref, ci_ref, br_ref, bi_ref, or_ref, oi_ref):
    cr = cr_ref[...]
    ci = ci_ref[...]
    br = br_ref[...]
    bi = bi_ref[...]
    or_ref[...] = cr * br - ci * bi
    oi_ref[...] = cr * bi + ci * br


def s5_bb(coef_re_col, coef_im_col, b_re, b_im):
    nl = b_re.shape[0]
    rows = 1024
    col = pl.BlockSpec((None, rows, 1), lambda l, r: (l, r, 0))
    mat = pl.BlockSpec((None, rows, S5_GROUP), lambda l, r: (l, r, 0))
    return pl.pallas_call(
        _s5_bb_kernel,
        grid=(nl, S5_CH // rows),
        in_specs=[col, col, mat, mat],
        out_specs=[mat, mat],
        out_shape=[jax.ShapeDtypeStruct((nl, S5_CH, S5_GROUP), F32)] * 2,
        compiler_params=_cparams(2),
        name="s5_bb",
    )(coef_re_col, coef_im_col, b_re, b_im)


S5_STRIP = 256


def _s5_kernel(xa_ref, za_ref, h0r_ref, h0i_ref, coef_ref, wur_ref, wui_ref, wyr_ref, wyi_ref,
               dsk_ref, wglu_ref, bglu_ref,
               mix_ref, hro_ref, hio_ref,
               ur_scr, ui_scr, cr_scr, ci_scr, y_scr, *, nb, tt, t_last):
    tc = pl.program_id(1)
    ntc = pl.num_programs(1)

    xb16 = xa_ref[...].astype(BF16)
    for k in range(4):
        xk = xb16[:, 256 * k:256 * (k + 1)]
        ur_scr[:, 1024 * k:1024 * (k + 1)] = _dot(xk, wur_ref[k])
        ui_scr[:, 1024 * k:1024 * (k + 1)] = _dot(xk, wui_ref[k])

    @pl.when(tc == 0)
    def _():
        cr_scr[...] = h0r_ref[...]
        ci_scr[...] = h0i_ref[...]

    ng = tt // V7X_SUBLANES
    for s in range(S5_CH // S5_STRIP):
        ls = slice(s * S5_STRIP, (s + 1) * S5_STRIP)
        lv = [(coef_ref[2 * l, :, ls], coef_ref[2 * l + 1, :, ls], k) for l, k in enumerate((1, 2, 4))]
        pr = coef_ref[6, :, ls]
        pim = coef_ref[7, :, ls]

        def seq_body(bi, carry, ls=ls, lv=lv, pr=pr, pim=pim):
            def tile_body(g, st):
                cr, ci = st
                rows = pl.ds(pl.multiple_of(bi * tt + g * V7X_SUBLANES, V7X_SUBLANES), V7X_SUBLANES)
                ur = ur_scr[rows, ls]
                ui = ui_scr[rows, ls]
                for akr, aki, k in lv:
                    sr = pltpu.roll(ur, k, 0)
                    si = pltpu.roll(ui, k, 0)
                    ur, ui = ur + (akr * sr - aki * si), ui + (akr * si + aki * sr)
                hr = ur + (pr * cr - pim * ci)
                hi = ui + (pr * ci + pim * cr)
                ur_scr[rows, ls] = hr
                ui_scr[rows, ls] = hi
                return hr[V7X_SUBLANES - 1:, :], hi[V7X_SUBLANES - 1:, :]

            cr, ci = lax.fori_loop(0, ng, tile_body, (cr_scr[bi, :, ls], ci_scr[bi, :, ls]))
            cr_scr[bi, :, ls] = cr
            ci_scr[bi, :, ls] = ci
            return carry

        lax.fori_loop(0, nb, seq_body, 0)

    for j in range(8):
        hr16 = ur_scr[:, 512 * j:512 * (j + 1)].astype(BF16)
        hi16 = ui_scr[:, 512 * j:512 * (j + 1)].astype(BF16)
        y_scr[:, 128 * j:128 * (j + 1)] = _dot(hr16, wyr_ref[j]) - _dot(hi16, wyi_ref[j])
    y = y_scr[...] + dsk_ref[...] * xa_ref[...]
    ga = _gelu_tanh(y)
    ya = ga * _sigmoid(_dot(ga.astype(BF16), wglu_ref[...]) + bglu_ref[...])
    mix_ref[...] = (ya * _silu(za_ref[...])).astype(BF16)

    @pl.when(tc == ntc - 1)
    def _():
        for bi in range(nb):
            r = bi * tt + t_last
            hro_ref[bi] = ur_scr[r:r + 1, :]
            hio_ref[bi] = ui_scr[r:r + 1, :]


def s5_mixer(proj, h0_re, h0_im, coef, wur, wui, wyr, wyi, dskip, wglu, bglu, layer, state_layer,
             *, batch, seq, t_valid, nb, tt):
    ntc = seq // tt
    assert nb == 1 or ntc == 1
    rows = nb * tt
    grid = (batch // nb, ntc)
    t_last = (t_valid - 1) - (ntc - 1) * tt
    assert 0 <= t_last < tt

    def tok(cb):
        return pl.BlockSpec((rows, W_MIX), lambda b, t: (b * ntc + t, cb))

    st_in = pl.BlockSpec((None, nb, 1, S5_CH), lambda b, t: (state_layer, b, 0, 0))
    st_out = pl.BlockSpec((nb, 1, S5_CH), lambda b, t: (b, 0, 0))

    def lw(shape):
        nd = len(shape)
        return pl.BlockSpec((None,) + shape, lambda b, t: (layer,) + (0,) * nd)

    return pl.pallas_call(
        functools.partial(_s5_kernel, nb=nb, tt=tt, t_last=t_last),
        grid=grid,
        in_specs=[tok(COL_XA // W_MIX), tok(COL_ZA // W_MIX), st_in, st_in,
                  lw((8, V7X_SUBLANES, S5_CH)), lw((4, 256, 1024)), lw((4, 256, 1024)),
                  lw((8, 512, 128)), lw((8, 512, 128)), lw((1, W_MIX)), lw((W_MIX, W_MIX)), lw((1, W_MIX))],
        out_specs=[pl.BlockSpec((rows, W_MIX), lambda b, t: (b * ntc + t, 0)), st_out, st_out],
        out_shape=[jax.ShapeDtypeStruct((batch * seq, W_MIX), BF16),
                   jax.ShapeDtypeStruct((batch, 1, S5_CH), F32),
                   jax.ShapeDtypeStruct((batch, 1, S5_CH), F32)],
        scratch_shapes=[pltpu.VMEM((rows, S5_CH), F32), pltpu.VMEM((rows, S5_CH), F32),
                        pltpu.VMEM((nb, 1, S5_CH), F32), pltpu.VMEM((nb, 1, S5_CH), F32),
                        pltpu.VMEM((rows, W_MIX), F32)],
        compiler_params=_cparams(2),
        name="s5_mixer",
    )(proj, proj, h0_re, h0_im, coef, wur, wui, wyr, wyi, dskip, wglu, bglu)


LRU_STRIP = 512
CONV_PAD = V7X_SUBLANES


def _causal_conv(xp_scr, tail_scr, x_ref, cw_ref, bi, rows_n):
    rows = pl.ds(pl.multiple_of(bi * rows_n, V7X_SUBLANES), rows_n)
    xp_scr[0:CONV_PAD, :] = tail_scr[bi]
    xp_scr[CONV_PAD:CONV_PAD + rows_n, :] = x_ref[rows, :]
    acc = cw_ref[CONV_W - 1:CONV_W, :] * xp_scr[CONV_PAD:CONV_PAD + rows_n, :]
    for jj in range(CONV_W - 1):
        off = CONV_PAD - (CONV_W - 1) + jj
        acc = acc + cw_ref[jj:jj + 1, :] * xp_scr[off:off + rows_n, :]
    tail_scr[bi] = xp_scr[rows_n:rows_n + CONV_PAD, :]
    return acc, rows


def _init_conv_tail(tail_scr, cv0_ref, nb):
    tail_scr[...] = jnp.zeros(tail_scr.shape, F32)
    for bi in range(nb):
        tail_scr[bi, CONV_PAD - (CONV_W - 1):CONV_PAD, :] = cv0_ref[bi]


def _lru_kernel(xb_ref, zb_ref, h0_ref, cv0_ref, cw_ref, cb_ref, wri_ref, br_ref, bi_ref, lam_ref,
                mix_ref, ho_ref,
                xp_scr, tail_scr, xc_scr, a_scr, b_scr, c_scr, *, nb, tt, t_last):
    tc = pl.program_id(1)
    ntc = pl.num_programs(1)

    @pl.when(tc == 0)
    def _():
        c_scr[...] = h0_ref[...]
        _init_conv_tail(tail_scr, cv0_ref, nb)

    def conv_body(bi, carry):
        acc, rows = _causal_conv(xp_scr, tail_scr, xb_ref, cw_ref, bi, tt)
        xc_scr[rows, :] = acc + cb_ref[...]
        return carry

    lax.fori_loop(0, nb, conv_body, 0)

    xc = xc_scr[...]
    xc16 = xc.astype(BF16)
    sp = _softplus(-lam_ref[...])
    for blk in range(LRU_BLOCKS):
        cs = slice(blk * LRU_BLK, (blk + 1) * LRU_BLK)
        pre = _dot(xc16[:, cs], wri_ref[blk])
        r = _sigmoid(pre[:, :LRU_BLK] + br_ref[:, cs])
        ig = _sigmoid(pre[:, LRU_BLK:] + bi_ref[:, cs])
        log_a = (-LRU_C) * r * sp[:, cs]
        a_scr[:, cs] = jnp.exp(log_a)
        b_scr[:, cs] = jnp.sqrt(-_expm1(2.0 * log_a)) * (ig * xc[:, cs])

    ng = tt // V7X_SUBLANES
    for s in range(W_MIX // LRU_STRIP):
        ls = slice(s * LRU_STRIP, (s + 1) * LRU_STRIP)
        row = lax.broadcasted_iota(jnp.int32, (V7X_SUBLANES, LRU_STRIP), 0)

        def seq_body(bi, carry, ls=ls, row=row):
            def tile_body(g, c):
                rows = pl.ds(pl.multiple_of(bi * tt + g * V7X_SUBLANES, V7X_SUBLANES), V7X_SUBLANES)
                a = a_scr[rows, ls]
                b = b_scr[rows, ls]
                for k in (1, 2, 4):
                    a_sh = jnp.where(row >= k, pltpu.roll(a, k, 0), 1.0)
                    b_sh = jnp.where(row >= k, pltpu.roll(b, k, 0), 0.0)
                    b = b + a * b_sh
                    a = a * a_sh
                h = b + a * c
                b_scr[rows, ls] = h
                return h[V7X_SUBLANES - 1:, :]

            c = lax.fori_loop(0, ng, tile_body, c_scr[bi, :, ls])
            c_scr[bi, :, ls] = c
            return carry

        lax.fori_loop(0, nb, seq_body, 0)

    mix_ref[...] = (b_scr[...] * _silu(zb_ref[...])).astype(BF16)

    @pl.when(tc == ntc - 1)
    def _():
        for bi in range(nb):
            r = bi * tt + t_last
            ho_ref[bi] = b_scr[r:r + 1, :]


def lru_mixer(proj, h0, conv0, cw, cb, wri, br, bi_, lam, layer, state_layer, *, batch, seq, t_valid, nb, tt):
    ntc = seq // tt
    assert nb == 1 or ntc == 1
    rows = nb * tt
    grid = (batch // nb, ntc)
    t_last = (t_valid - 1) - (ntc - 1) * tt

    def tok(cb_):
        return pl.BlockSpec((rows, W_MIX), lambda b, t: (b * ntc + t, cb_))

    def lw(shape):
        nd = len(shape)
        return pl.BlockSpec((None,) + shape, lambda b, t: (layer,) + (0,) * nd)

    return pl.pallas_call(
        functools.partial(_lru_kernel, nb=nb, tt=tt, t_last=t_last),
        grid=grid,
        in_specs=[tok(COL_XB // W_MIX), tok(COL_ZB // W_MIX),
                  pl.BlockSpec((None, nb, 1, W_MIX), lambda b, t: (state_layer, b, 0, 0)),
                  pl.BlockSpec((None, nb, CONV_W - 1, W_MIX), lambda b, t: (state_layer, b, 0, 0)),
                  lw((CONV_W, W_MIX)), lw((1, W_MIX)), lw((LRU_BLOCKS, LRU_BLK, 2 * LRU_BLK)),
                  lw((1, W_MIX)), lw((1, W_MIX)), lw((1, W_MIX))],
        out_specs=[pl.BlockSpec((rows, W_MIX), lambda b, t: (b * ntc + t, 0)),
                   pl.BlockSpec((nb, 1, W_MIX), lambda b, t: (b, 0, 0))],
        out_shape=[jax.ShapeDtypeStruct((batch * seq, W_MIX), BF16),
                   jax.ShapeDtypeStruct((batch, 1, W_MIX), F32)],
        scratch_shapes=[pltpu.VMEM((CONV_PAD + tt, W_MIX), F32),
                        pltpu.VMEM((nb, CONV_PAD, W_MIX), F32),
                        pltpu.VMEM((rows, W_MIX), F32), pltpu.VMEM((rows, W_MIX), F32),
                        pltpu.VMEM((rows, W_MIX), F32), pltpu.VMEM((nb, 1, W_MIX), F32)],
        compiler_params=_cparams(2),
        name="lru_mixer",
    )(proj, proj, h0, conv0, cw, cb, wri, br, bi_, lam)


def _gla_kernel(q_ref, k_ref, v_ref, z_ref, tail_ref, s0_ref, wg_ref, bg_ref, go_ref,
                mix_ref, so_ref,
                s_scr, mix_scr, *, nb, c, cv):
    tc = pl.program_id(1)
    ntc = pl.num_programs(1)
    sb = min(16, c)
    nblk = c // sb

    @pl.when(tc == 0)
    def _():
        s_scr[...] = s0_ref[...]

    row_c = lax.broadcasted_iota(jnp.int32, (c, 1), 0)
    lane_c = lax.broadcasted_iota(jnp.int32, (sb, c), 1)
    row_sb = lax.broadcasted_iota(jnp.int32, (sb, 1), 0)

    def seq_body(bi, carry):
        rows = pl.ds(pl.multiple_of(bi * c, V7X_SUBLANES), c)
        x = _dot(tail_ref[rows, :].astype(BF16), wg_ref[...]) + bg_ref[...]
        gk = _log_sigmoid(x) * (1.0 / GLA_TAU)
        b_all = _cumsum_rows(gk)
        for h in range(GLA_H):
            ks = slice(h * GLA_DK, (h + 1) * GLA_DK)
            vs = slice(h * GLA_DV, (h + 1) * GLA_DV)
            q = q_ref[rows, ks] * (GLA_DK ** -0.5)
            k = k_ref[rows, ks]
            v = v_ref[rows, vs]
            b = b_all[:, ks]
            s_old = s_scr[bi, h]
            att_rows = []
            for blk in range(nblk):
                r0 = blk * sb
                q_i = q[r0:r0 + sb]
                b_i = b[r0:r0 + sb]
                if blk > 0:
                    b_ref_row = b[r0 - 1:r0]
                    qs = (q_i * jnp.exp(b_i - b_ref_row)).astype(BF16)
                    kd = (k * jnp.exp(jnp.where(row_c < r0, b_ref_row - b, NEG_INF))).astype(BF16)
                    att = _dot_nt(qs, kd)
                else:
                    att = jnp.zeros((sb, c), F32)
                for sl in range(sb):
                    s_abs = r0 + sl
                    e = jnp.exp(jnp.where(row_sb >= sl, b_i - b[s_abs:s_abs + 1], NEG_INF))
                    col = jnp.sum(q_i * (k[s_abs:s_abs + 1] * e), axis=-1, keepdims=True)
                    att = jnp.where(lane_c == s_abs, col, att)
                att_rows.append(att)
            att = att_rows[0] if nblk == 1 else jnp.concatenate(att_rows, axis=0)
            o = _mm(att, v) + _dot((q * jnp.exp(b)).astype(BF16), s_old.astype(BF16))
            b_last = b[cv - 1:cv]
            dec = b_last - b
            if cv < c:
                dec = jnp.where(row_c < cv, dec, NEG_INF)
            kd = k * jnp.exp(dec)
            d_col = jnp.transpose(jnp.broadcast_to(jnp.exp(b_last), (V7X_SUBLANES, GLA_DK)))[:, 0:1]
            s_scr[bi, h] = d_col * s_old + _mm(kd, v, dot=_dot_tn, contract=0)
            ms = jnp.mean(o * o, axis=-1, keepdims=True)
            on = (o * lax.rsqrt(ms + EPS)) * go_ref[...]
            mix_scr[rows, vs] = on * _silu(z_ref[rows, vs])
        return carry

    lax.fori_loop(0, nb, seq_body, 0)
    mix_ref[...] = mix_scr[...].astype(BF16)

    @pl.when(tc == ntc - 1)
    def _():
        so_ref[...] = s_scr[...]


def gla_mixer(proj, tail, s0, wg, bg, go, layer, state_layer, *, batch, seq, c, cv, nb):
    ntc = seq // c
    assert nb == 1 or ntc == 1
    rows = nb * c
    grid = (batch // nb, ntc)

    def tok(width, col):
        return pl.BlockSpec((rows, width), lambda b, t: (b * ntc + t, col // width))

    def lw(shape):
        nd = len(shape)
        return pl.BlockSpec((None,) + shape, lambda b, t: (layer,) + (0,) * nd)

    st_shape = (nb, GLA_H, GLA_DK, GLA_DV)
    return pl.pallas_call(
        functools.partial(_gla_kernel, nb=nb, c=c, cv=cv),
        grid=grid,
        in_specs=[tok(GLA_H * GLA_DK, COL_QC), tok(GLA_H * GLA_DK, COL_KC), tok(W_MIX, COL_VC), tok(W_MIX, COL_ZC),
                  pl.BlockSpec((rows, TAIL_W), lambda b, t: (b * ntc + t, 0)),
                  pl.BlockSpec((None,) + st_shape, lambda b, t: (state_layer, b, 0, 0, 0)),
                  lw((TAIL_W, GLA_H * GLA_DK)), lw((1, GLA_H * GLA_DK)), lw((1, GLA_DV))],
        out_specs=[pl.BlockSpec((rows, W_MIX), lambda b, t: (b * ntc + t, 0)),
                   pl.BlockSpec(st_shape, lambda b, t: (b, 0, 0, 0))],
        out_shape=[jax.ShapeDtypeStruct((batch * seq, W_MIX), BF16),
                   jax.ShapeDtypeStruct((batch, GLA_H, GLA_DK, GLA_DV), F32)],
        scratch_shapes=[pltpu.VMEM(st_shape, F32), pltpu.VMEM((rows, W_MIX), F32)],
        compiler_params=_cparams(2),
        name="gla_mixer",
    )(proj, proj, proj, proj, tail, s0, wg, bg, go)


def _l2norm(x):
    return x * lax.rsqrt(jnp.sum(x * x, axis=-1, keepdims=True) + EPS)


def _unit_lower_solve(m, rhs, c, cv):
    if c <= V7X_SUBLANES:
        row = lax.broadcasted_iota(jnp.int32, (c, 1), 0)
        u = jnp.zeros_like(rhs)
        u_rows = []
        for t in range(cv):
            ut = rhs[t:t + 1]
            for s in range(t):
                ut = ut - m[t:t + 1, s:s + 1] * u_rows[s]
            u_rows.append(ut)
            u = jnp.where(row == t, ut, u)
        return u
    x = -m
    y = rhs
    dv = rhs.shape[1]
    levels = int(math.log2(c))
    for lvl in range(levels):
        if lvl < levels - 1:
            p = _dot_x3(x, jnp.concatenate([y, x], axis=1))
            y = y + p[:, :dv]
            x = p[:, dv:]
        else:
            y = y + _dot_x3(x, y)
    return y


def _gdn_kernel(qkv_ref, z_ref, tail_ref, s0_ref, cv0_ref, cw_ref, alog_ref, dtb_ref, go_ref,
                mix_ref, so_ref,
                s_scr, xp_scr, tail_scr, mix_scr, *, nb, c, cv):
    tc = pl.program_id(1)
    ntc = pl.num_programs(1)

    @pl.when(tc == 0)
    def _():
        s_scr[...] = s0_ref[...]
        _init_conv_tail(tail_scr, cv0_ref, nb)

    row_c = lax.broadcasted_iota(jnp.int32, (c, 1), 0)
    ri = lax.broadcasted_iota(jnp.int32, (c, c), 0)
    ci = lax.broadcasted_iota(jnp.int32, (c, c), 1)
    lane_t = lax.broadcasted_iota(jnp.int32, (c, TAIL_W), 1)
    a_lanes = (lane_t >= TAIL_A) & (lane_t < TAIL_A + DN_H)

    def seq_body(bi, carry):
        conv, rows = _causal_conv(xp_scr, tail_scr, qkv_ref, cw_ref, bi, c)
        qkv = _silu(conv)
        tl = tail_ref[rows, :]
        g = jnp.where(a_lanes, -jnp.exp(alog_ref[...]) * _softplus(tl + dtb_ref[...]), 0.0)
        gam = _cumsum_rows(g)
        gam_t = jnp.transpose(gam)
        beta = _sigmoid(tl)
        for h in range(DN_H):
            q = _l2norm(qkv[:, h * DN_DK:(h + 1) * DN_DK]) * (DN_DK ** -0.5)
            k = _l2norm(qkv[:, DN_H * DN_DK + h * DN_DK:DN_H * DN_DK + (h + 1) * DN_DK])
            v = qkv[:, 2 * DN_H * DN_DK + h * DN_DV:2 * DN_H * DN_DK + (h + 1) * DN_DV]
            q16 = q.astype(BF16)
            k16 = k.astype(BF16)
            g_col = gam[:, TAIL_A + h:TAIL_A + h + 1]
            g_row = gam_t[TAIL_A + h:TAIL_A + h + 1, :]
            b_col = beta[:, TAIL_B + h:TAIL_B + h + 1]
            decay = jnp.exp(jnp.where(ri >= ci, g_col - g_row, NEG_INF))
            m = jnp.where(ri > ci, b_col * _dot_nt(k16, k16) * decay, 0.0)
            s_old = s_scr[bi, h]
            s16 = s_old.astype(BF16)
            eg = jnp.exp(g_col)
            rhs = b_col * (v - eg * _dot(k16, s16))
            u = _unit_lower_solve(m, rhs, c, cv)
            o = eg * _dot(q16, s16) + _mm(_dot_nt(q16, k16) * decay, u)
            g_last = g_col[cv - 1:cv]
            dec = g_last - g_col
            if cv < c:
                dec = jnp.where(row_c < cv, dec, NEG_INF)
            kd = k * jnp.exp(dec)
            s_scr[bi, h] = jnp.exp(g_last) * s_old + _mm(kd, u, dot=_dot_tn, contract=0)
            ms = jnp.mean(o * o, axis=-1, keepdims=True)
            on = (o * lax.rsqrt(ms + EPS)) * go_ref[...]
            vs = slice(h * DN_DV, (h + 1) * DN_DV)
            mix_scr[rows, vs] = on * _silu(z_ref[rows, vs])
        return carry

    lax.fori_loop(0, nb, seq_body, 0)
    mix_ref[...] = mix_scr[...].astype(BF16)

    @pl.when(tc == ntc - 1)
    def _():
        so_ref[...] = s_scr[...]


def gdn_mixer(proj, tail, s0, conv0, cw, alog, dtb, go, layer, state_layer, *, batch, seq, c, cv, nb):
    ntc = seq // c
    assert nb == 1 or ntc == 1
    rows = nb * c
    grid = (batch // nb, ntc)

    def tok(width, col):
        return pl.BlockSpec((rows, width), lambda b, t: (b * ntc + t, col // width))

    def lw(shape):
        nd = len(shape)
        return pl.BlockSpec((None,) + shape, lambda b, t: (layer,) + (0,) * nd)

    st_shape = (nb, DN_H, DN_DK, DN_DV)
    assert COL_QKVD % DN_QKV == 0
    return pl.pallas_call(
        functools.partial(_gdn_kernel, nb=nb, c=c, cv=cv),
        grid=grid,
        in_specs=[tok(DN_QKV, COL_QKVD),
                  tok(W_MIX, COL_ZD),
                  pl.BlockSpec((rows, TAIL_W), lambda b, t: (b * ntc + t, 0)),
                  pl.BlockSpec((None,) + st_shape, lambda b, t: (state_layer, b, 0, 0, 0)),
                  pl.BlockSpec((None, nb, CONV_W - 1, DN_QKV), lambda b, t: (state_layer, b, 0, 0)),
                  lw((CONV_W, DN_QKV)), lw((1, TAIL_W)), lw((1, TAIL_W)), lw((1, DN_DV))],
        out_specs=[pl.BlockSpec((rows, W_MIX), lambda b, t: (b * ntc + t, 0)),
                   pl.BlockSpec(st_shape, lambda b, t: (b, 0, 0, 0))],
        out_shape=[jax.ShapeDtypeStruct((batch * seq, W_MIX), BF16),
                   jax.ShapeDtypeStruct((batch, DN_H, DN_DK, DN_DV), F32)],
        scratch_shapes=[pltpu.VMEM(st_shape, F32),
                        pltpu.VMEM((CONV_PAD + c, DN_QKV), F32),
                        pltpu.VMEM((nb, CONV_PAD, DN_QKV), F32),
                        pltpu.VMEM((rows, W_MIX), F32)],
        compiler_params=_cparams(2),
        name="gdn_mixer",
    )(proj, proj, tail, s0, conv0, cw, alog, dtb, go)


def _prepare_weights(w_in, s5_lam_re, s5_lam_im, s5_log_dt, s5_b_re, s5_b_im, s5_c_re, s5_c_im, s5_d,
                     s5_w_glu, lru_w_r, lru_w_i, gla_w_gate, dn_a_log, dn_dt_bias, w_out, ple_w, ple_gate_w):
    nl = w_in.shape[0]
    p = {}
    p["w_main"] = jnp.concatenate([w_in[:, :, :ORIG_ZC], w_in[:, :, ORIG_QKVD:ORIG_QKVD + DN_QKV],
                                   w_in[:, :, ORIG_ZC:ORIG_ZC + W_MIX], w_in[:, :, ORIG_ZD:ORIG_ZD + W_MIX]],
                                  axis=2).astype(BF16)
    p["w_tail"] = jnp.concatenate([w_in[:, :, ORIG_GC:ORIG_GC + GLA_RANK], w_in[:, :, ORIG_AB:ORIG_AB + 2 * DN_H],
                                   jnp.zeros((nl, D_MODEL, TAIL_W - GLA_RANK - 2 * DN_H), F32)], axis=2).astype(BF16)
    p["w_out"] = w_out.astype(BF16)
    p["ple_w"] = ple_w.astype(BF16)
    p["ple_gate_w"] = ple_gate_w.astype(BF16)
    p["w_glu"] = s5_w_glu.astype(BF16)
    ldt = jnp.repeat(s5_log_dt, S5_STATE, axis=1).reshape(nl, 1, S5_CH)
    coef, cre, cim = s5_prep(s5_lam_re.reshape(nl, 1, S5_CH), s5_lam_im.reshape(nl, 1, S5_CH), ldt)
    bb_re, bb_im = s5_bb(cre.reshape(nl, S5_CH, 1), cim.reshape(nl, S5_CH, 1),
                         s5_b_re.reshape(nl, S5_CH, S5_GROUP), s5_b_im.reshape(nl, S5_CH, S5_GROUP))
    eye16 = jnp.eye(16, dtype=F32)
    eye8 = jnp.eye(8, dtype=F32)

    def pack_u(bb):
        t = bb.reshape(nl, 4, 16, S5_STATE, S5_GROUP)
        return jnp.einsum("lkgph,gG->lkghGp", t, eye16).reshape(nl, 4, 256, 1024).astype(BF16)

    def pack_y(cc):
        t = cc.reshape(nl, 8, 8, S5_GROUP, S5_STATE)
        return jnp.einsum("ljghp,gG->ljgpGh", t, eye8).reshape(nl, 8, 512, 128).astype(BF16)

    p["s5_coef"] = coef
    p["s5_wur"], p["s5_wui"] = pack_u(bb_re), pack_u(bb_im)
    p["s5_wyr"], p["s5_wyi"] = pack_y(s5_c_re), pack_y(s5_c_im)
    p["s5_d"] = s5_d.reshape(nl, 1, W_MIX)
    p["lru_wri"] = jnp.concatenate([lru_w_r, lru_w_i], axis=-1).astype(BF16)
    p["gla_wg"] = jnp.concatenate([gla_w_gate, jnp.zeros((nl, TAIL_W - GLA_RANK, GLA_H * GLA_DK), F32)],
                                  axis=1).astype(BF16)

    def tail_row(x):
        return jnp.pad(x, ((0, 0), (TAIL_A, TAIL_W - TAIL_A - DN_H))).reshape(nl, 1, TAIL_W)

    p["dn_alog"] = tail_row(dn_a_log)
    p["dn_dtb"] = tail_row(dn_dt_bias)
    return p


def kernel(x_prompt, x_sample, state_s5_re, state_s5_im, state_lru_h, state_lru_conv, state_gla, state_delta, state_delta_conv, p_prompt, p_sample, g_norm, w_in, s5_lam_re, s5_lam_im, s5_log_dt, s5_b_re, s5_b_im, s5_c_re, s5_c_im, s5_d, s5_w_glu, s5_b_glu, lru_conv_w, lru_conv_b, lru_w_r, lru_b_r, lru_w_i, lru_b_i, lru_lam, gla_w_gate, gla_b_gate, gla_g_out, dn_conv_w, dn_a_log, dn_dt_bias, dn_g_out, w_out, ple_w, ple_gate_w, ple_gate_b, g_final):
    nl = w_in.shape[0]
    bp, tp, _ = x_prompt.shape
    bs, ts, _ = x_sample.shape
    tsp = SAMPLE_T_PAD
    mp, ms = bp * tp, bs * ts

    p = _prepare_weights(w_in, s5_lam_re, s5_lam_im, s5_log_dt, s5_b_re, s5_b_im, s5_c_re, s5_c_im, s5_d,
                         s5_w_glu, lru_w_r, lru_w_i, gla_w_gate, dn_a_log, dn_dt_bias, w_out, ple_w, ple_gate_w)

    def vec(x):
        return x.reshape(nl, 1, x.shape[-1])

    g_norm3, b_glu3, cb3 = vec(g_norm), vec(s5_b_glu), vec(lru_conv_b)
    br3, bi3, lam3 = vec(lru_b_r), vec(lru_b_i), vec(lru_lam)
    bg3, go_c3, go_d3, pgb3 = vec(gla_b_gate), vec(gla_g_out), vec(dn_g_out), vec(ple_gate_b)
    pe_p = p_prompt.reshape(nl, mp, D_PLE)
    pe_s = p_sample.reshape(nl, ms, D_PLE)

    z_s5 = jnp.zeros((1, bp, 1, S5_CH), F32)
    z_lru = jnp.zeros((1, bp, 1, W_MIX), F32)
    z_lconv = jnp.zeros((1, bp, CONV_W - 1, W_MIX), F32)
    z_gla = jnp.zeros((1, bp, GLA_H, GLA_DK, GLA_DV), F32)
    z_dn = jnp.zeros((1, bp, DN_H, DN_DK, DN_DV), F32)
    z_dconv = jnp.zeros((1, bp, CONV_W - 1, DN_QKV), F32)
    c_s5r = state_s5_re.reshape(nl, bs, 1, S5_CH)
    c_s5i = state_s5_im.reshape(nl, bs, 1, S5_CH)
    c_lru = state_lru_h.reshape(nl, bs, 1, W_MIX)

    def mixers(proj, tail, layer, *, batch, seq, t_valid, st, sl, nb_scan, tt, c, nb_mat):
        s5r0, s5i0, lru0, lconv0, gla0, dn0, dconv0 = st
        mix_a, s5r, s5i = s5_mixer(proj, s5r0, s5i0, p["s5_coef"], p["s5_wur"], p["s5_wui"], p["s5_wyr"],
                                   p["s5_wyi"], p["s5_d"], p["w_glu"], b_glu3, layer, sl,
                                   batch=batch, seq=seq, t_valid=t_valid, nb=nb_scan, tt=tt)
        mix_b, lruh = lru_mixer(proj, lru0, lconv0, lru_conv_w, cb3, p["lru_wri"], br3, bi3, lam3, layer, sl,
                                batch=batch, seq=seq, t_valid=t_valid, nb=nb_scan, tt=tt)
        mix_c, glas = gla_mixer(proj, tail, gla0, p["gla_wg"], bg3, go_c3, layer, sl,
                                batch=batch, seq=seq, c=c, cv=min(c, t_valid), nb=nb_mat)
        mix_d, dns = gdn_mixer(proj, tail, dn0, dconv0, dn_conv_w, p["dn_alog"], p["dn_dtb"], go_d3, layer, sl,
                               batch=batch, seq=seq, c=c, cv=min(c, t_valid), nb=nb_mat)
        return (mix_a, mix_b, mix_c, mix_d), (s5r, s5i, lruh, glas, dns)

    hp = x_prompt.reshape(mp, D_MODEL)
    hs = x_sample.reshape(ms, D_MODEL)
    new_p, new_s = [], []
    for i in range(nl):
        proj, tail = in_proj(hp, g_norm3, p["w_main"], p["w_tail"], i, tm=512, tn=512)
        mixes, (s5r, s5i, lruh, glas, dns) = mixers(
            proj, tail, i, batch=bp, seq=tp, t_valid=tp,
            st=(z_s5, z_s5, z_lru, z_lconv, z_gla, z_dn, z_dconv), sl=0,
            nb_scan=1, tt=256, c=GLA_CHUNK, nb_mat=1)
        proj3 = proj.reshape(bp, tp, N_MAIN)
        new_p.append((s5r.reshape(bp, S5_GROUPS, S5_STATE), s5i.reshape(bp, S5_GROUPS, S5_STATE),
                      lruh.reshape(bp, W_MIX), proj3[:, tp - (CONV_W - 1):, COL_XB:COL_XB + W_MIX],
                      glas, dns, proj3[:, tp - (CONV_W - 1):, COL_QKVD:COL_QKVD + DN_QKV]))
        hp = out_proj(mixes, p["w_out"], hp, i, tm=512, tn=512)
        hp = ple(hp, pe_p, p["ple_gate_w"], pgb3, p["ple_w"], i, tm=512, tn=512)

        proj, tail = in_proj(hs, g_norm3, p["w_main"], p["w_tail"], i, tm=512, tn=512)
        proj3 = proj.reshape(bs, ts, N_MAIN)
        proj_pad = jnp.pad(proj3, ((0, 0), (0, tsp - ts), (0, 0))).reshape(bs * tsp, N_MAIN)
        tail_pad = jnp.pad(tail.reshape(bs, ts, TAIL_W), ((0, 0), (0, tsp - ts), (0, 0))).reshape(bs * tsp, TAIL_W)
        mixes, (s5r, s5i, lruh, glas, dns) = mixers(
            proj_pad, tail_pad, i, batch=bs, seq=tsp, t_valid=ts,
            st=(c_s5r, c_s5i, c_lru, state_lru_conv, state_gla, state_delta, state_delta_conv), sl=i,
            nb_scan=32, tt=tsp, c=tsp, nb_mat=8)
        mixes = tuple(m.reshape(bs, tsp, W_MIX)[:, :ts].reshape(ms, W_MIX) for m in mixes)
        new_s.append((s5r.reshape(bs, S5_GROUPS, S5_STATE), s5i.reshape(bs, S5_GROUPS, S5_STATE),
                      lruh.reshape(bs, W_MIX), proj3[:, ts - (CONV_W - 1):, COL_XB:COL_XB + W_MIX],
                      glas, dns, proj3[:, ts - (CONV_W - 1):, COL_QKVD:COL_QKVD + DN_QKV]))
        hs = out_proj(mixes, p["w_out"], hs, i, tm=512, tn=512)
        hs = ple(hs, pe_s, p["ple_gate_w"], pgb3, p["ple_w"], i, tm=512, tn=512)

    g_fin = g_final.reshape(1, D_MODEL)
    y_prompt = final_norm(hp, g_fin, tm=256).reshape(bp, tp, D_MODEL)
    y_sample = final_norm(hs, g_fin, tm=256).reshape(bs, ts, D_MODEL)

    def stk(lst, j):
        return jnp.stack([s[j] for s in lst], axis=0)

    return (y_prompt, y_sample,
            stk(new_p, 0), stk(new_p, 1), stk(new_p, 2), stk(new_p, 3), stk(new_p, 4), stk(new_p, 5), stk(new_p, 6),
            stk(new_s, 0), stk(new_s, 1), stk(new_s, 2), stk(new_s, 3), stk(new_s, 4), stk(new_s, 5), stk(new_s, 6))
```

```python
import functools
import math

import jax
import jax.numpy as jnp
from jax import lax
from jax.experimental import pallas as pl
from jax.experimental.pallas import tpu as pltpu

F32 = jnp.float32
BF16 = jnp.bfloat16
EPS = 1e-6
NEG_INF = float("-inf")

D_MODEL = 4096
DEPTH = 4
W_MIX = 1024
S5_GROUPS = 64
S5_GROUP = 16
S5_STATE = 64
S5_CH = S5_GROUPS * S5_STATE
LRU_BLOCKS = 8
LRU_BLK = 128
LRU_C = 8.0
CONV_W = 4
GLA_H = 4
GLA_DK = 128
GLA_DV = 256
GLA_RANK = 16
GLA_TAU = 16.0
GLA_CHUNK = 64
DN_H = 8
DN_DK = 128
DN_DV = 128
DN_QKV = 3072
DN_CHUNK = 64
D_PLE = 256
N_MAIN = 11264
COL_XA, COL_ZA, COL_XB, COL_ZB = 0, 1024, 2048, 3072
COL_QC, COL_KC, COL_VC = 4096, 4608, 5120
COL_QKVD, COL_ZC, COL_ZD = 6144, 9216, 10240
ORIG_ZC, ORIG_GC, ORIG_QKVD, ORIG_ZD, ORIG_AB = 6144, 7168, 7184, 10256, 11280
TAIL_W = 128
TAIL_G, TAIL_A, TAIL_B = 0, 16, 24
SAMPLE_T_PAD = 8

V7X_LANES = 128
V7X_SUBLANES = 8
VMEM_LIMIT = 52 * 1024 * 1024


def _cparams(n_axes):
    return pltpu.CompilerParams(dimension_semantics=("arbitrary",) * n_axes,
                                vmem_limit_bytes=VMEM_LIMIT)


def _sigmoid(x):
    return jax.nn.sigmoid(x)


def _silu(x):
    return x * jax.nn.sigmoid(x)


def _softplus(x):
    return jnp.maximum(x, 0.0) + jnp.log1p(jnp.exp(-jnp.abs(x)))


def _expm1(x):
    u = jnp.exp(x)
    um1 = u - 1.0
    lg = jnp.log(u)
    r = um1 * x / jnp.where(lg == 0.0, 1.0, lg)
    return jnp.where(um1 == 0.0, x, jnp.where(um1 == -1.0, -1.0, r))


def _log_sigmoid(x):
    return jnp.minimum(x, 0.0) - jnp.log1p(jnp.exp(-jnp.abs(x)))


def _gelu_tanh(x):
    c = math.sqrt(2.0 / math.pi)
    return x * (0.5 * (1.0 + jnp.tanh(c * (x + 0.044715 * (x * x * x)))))


def _dot(a, b):
    return jnp.dot(a, b, preferred_element_type=F32)


def _dot_nt(a, b):
    return lax.dot_general(a, b, (((1,), (1,)), ((), ())), preferred_element_type=F32)


def _dot_tn(a, b):
    return lax.dot_general(a, b, (((0,), (0,)), ((), ())), preferred_element_type=F32)


def _split3(x):
    h = x.astype(BF16)
    r = x - h.astype(F32)
    m = r.astype(BF16)
    l = (r - m.astype(F32)).astype(BF16)
    return h, m, l


def _cumsum_rows(x):
    c = x.shape[0]
    if c <= V7X_SUBLANES:
        row = lax.broadcasted_iota(jnp.int32, x.shape, 0)
        k = 1
        while k < c:
            x = x + jnp.where(row >= k, pltpu.roll(x, k, 0), 0.0)
            k *= 2
        return x
    tri = (lax.broadcasted_iota(jnp.int32, (c, c), 0) >= lax.broadcasted_iota(jnp.int32, (c, c), 1)).astype(BF16)
    h, m, l = _split3(x)
    return _dot(tri, h) + _dot(tri, m) + _dot(tri, l)


def _mm(a, b, dot=_dot, contract=1):
    if a.shape[contract] < 2 * V7X_SUBLANES:
        return dot(a, b)
    return dot(a.astype(BF16), b.astype(BF16))


def _dot_x3(a, b):
    ah = a.astype(BF16)
    al = (a - ah.astype(F32)).astype(BF16)
    bh = b.astype(BF16)
    bl = (b - bh.astype(F32)).astype(BF16)
    return _dot(ah, bh) + _dot(ah, bl) + _dot(al, bh)


def _in_proj_kernel(x_ref, g_ref, w_ref, wt_ref, o_ref, ot_ref, hn_ref, *, row_chunk):
    j = pl.program_id(1)
    tm = x_ref.shape[0]

    @pl.when(j == 0)
    def _():
        def body(r, carry):
            rows = pl.ds(pl.multiple_of(r * row_chunk, row_chunk), row_chunk)
            x = x_ref[rows, :]
            ms = jnp.mean(x * x, axis=-1, keepdims=True)
            hn_ref[rows, :] = ((x * lax.rsqrt(ms + EPS)) * g_ref[...]).astype(BF16)
            return carry
        lax.fori_loop(0, tm // row_chunk, body, 0)
        ot_ref[...] = _dot(hn_ref[...], wt_ref[...])

    o_ref[...] = _dot(hn_ref[...], w_ref[...])


def in_proj(h, g_norm, w_main, w_tail, layer, *, tm, tn):
    m = h.shape[0]
    grid = (m // tm, N_MAIN // tn)
    return pl.pallas_call(
        functools.partial(_in_proj_kernel, row_chunk=64),
        grid=grid,
        in_specs=[
            pl.BlockSpec((tm, D_MODEL), lambda i, j: (i, 0)),
            pl.BlockSpec((None, 1, D_MODEL), lambda i, j: (layer, 0, 0)),
            pl.BlockSpec((None, D_MODEL, tn), lambda i, j: (layer, 0, j)),
            pl.BlockSpec((None, D_MODEL, TAIL_W), lambda i, j: (layer, 0, 0)),
        ],
        out_specs=[
            pl.BlockSpec((tm, tn), lambda i, j: (i, j)),
            pl.BlockSpec((tm, TAIL_W), lambda i, j: (i, 0)),
        ],
        out_shape=[jax.ShapeDtypeStruct((m, N_MAIN), F32), jax.ShapeDtypeStruct((m, TAIL_W), F32)],
        scratch_shapes=[pltpu.VMEM((tm, D_MODEL), BF16)],
        compiler_params=_cparams(2),
        name="in_proj",
    )(h, g_norm, w_main, w_tail)


def _out_proj_kernel(ma_ref, mb_ref, mc_ref, md_ref, w_ref, h_ref, o_ref):
    acc = h_ref[...]
    for k, m_ref in enumerate((ma_ref, mb_ref, mc_ref, md_ref)):
        acc = acc + _dot(m_ref[...], w_ref[k * W_MIX:(k + 1) * W_MIX, :])
    o_ref[...] = acc


def out_proj(mixes, w_out, h, layer, *, tm, tn):
    m = h.shape[0]
    grid = (m // tm, D_MODEL // tn)
    mix_spec = pl.BlockSpec((tm, W_MIX), lambda i, j: (i, 0))
    return pl.pallas_call(
        _out_proj_kernel,
        grid=grid,
        in_specs=[mix_spec, mix_spec, mix_spec, mix_spec,
                  pl.BlockSpec((None, 4 * W_MIX, tn), lambda i, j: (layer, 0, j)),
                  pl.BlockSpec((tm, tn), lambda i, j: (i, j))],
        out_specs=pl.BlockSpec((tm, tn), lambda i, j: (i, j)),
        out_shape=jax.ShapeDtypeStruct((m, D_MODEL), F32),
        compiler_params=_cparams(2),
        name="out_proj",
    )(*mixes, w_out, h)


def _ple_kernel(hrow_ref, htile_ref, pe_ref, wg_ref, bg_ref, wp_ref, o_ref, hb_ref, *, row_chunk):
    j = pl.program_id(1)
    tm = hrow_ref.shape[0]

    @pl.when(j == 0)
    def _():
        def body(r, carry):
            rows = pl.ds(pl.multiple_of(r * row_chunk, row_chunk), row_chunk)
            hb_ref[rows, :] = hrow_ref[rows, :].astype(BF16)
            return carry
        lax.fori_loop(0, tm // row_chunk, body, 0)

    gate = _sigmoid(_dot(hb_ref[...], wg_ref[...]) + bg_ref[...])
    pv = _dot(pe_ref[...].astype(BF16), wp_ref[...])
    o_ref[...] = htile_ref[...] + gate * pv


def ple(h, pe, w_gate, b_gate, w_ple, layer, *, tm, tn):
    m = h.shape[0]
    grid = (m // tm, D_MODEL // tn)
    return pl.pallas_call(
        functools.partial(_ple_kernel, row_chunk=64),
        grid=grid,
        in_specs=[
            pl.BlockSpec((tm, D_MODEL), lambda i, j: (i, 0)),
            pl.BlockSpec((tm, tn), lambda i, j: (i, j)),
            pl.BlockSpec((None, tm, D_PLE), lambda i, j: (layer, i, 0)),
            pl.BlockSpec((None, D_MODEL, tn), lambda i, j: (layer, 0, j)),
            pl.BlockSpec((None, 1, tn), lambda i, j: (layer, 0, j)),
            pl.BlockSpec((None, D_PLE, tn), lambda i, j: (layer, 0, j)),
        ],
        out_specs=pl.BlockSpec((tm, tn), lambda i, j: (i, j)),
        out_shape=jax.ShapeDtypeStruct((m, D_MODEL), F32),
        scratch_shapes=[pltpu.VMEM((tm, D_MODEL), BF16)],
        compiler_params=_cparams(2),
        name="ple",
    )(h, h, pe, w_gate, b_gate, w_ple)


def _final_norm_kernel(x_ref, g_ref, o_ref):
    x = x_ref[...]
    ms = jnp.mean(x * x, axis=-1, keepdims=True)
    o_ref[...] = (x * lax.rsqrt(ms + EPS)) * g_ref[...]


def final_norm(h, g, *, tm):
    m = h.shape[0]
    return pl.pallas_call(
        _final_norm_kernel,
        grid=(m // tm,),
        in_specs=[pl.BlockSpec((tm, D_MODEL), lambda i: (i, 0)),
                  pl.BlockSpec((1, D_MODEL), lambda i: (0, 0))],
        out_specs=pl.BlockSpec((tm, D_MODEL), lambda i: (i, 0)),
        out_shape=jax.ShapeDtypeStruct((m, D_MODEL), F32),
        compiler_params=_cparams(1),
        name="final_norm",
    )(h, g)


def _s5_prep_kernel(lre_ref, lim_ref, ldt_ref, coef_ref, cre_ref, cim_ref):
    lre = lre_ref[...]
    lim = lim_ref[...]
    dt = jnp.exp(ldt_ref[...])
    ai = lim * dt
    mag = jnp.exp(lre * dt)
    ar = mag * jnp.cos(ai)
    aim = mag * jnp.sin(ai)
    den = lre * lre + lim * lim
    nr = ar - 1.0
    cre_ref[...] = (nr * lre + aim * lim) / den
    cim_ref[...] = (aim * lre - nr * lim) / den
    pr, pim = [ar], [aim]
    for _ in range(V7X_SUBLANES - 1):
        nr_, ni_ = pr[-1] * ar - pim[-1] * aim, pr[-1] * aim + pim[-1] * ar
        pr.append(nr_)
        pim.append(ni_)
    shape = (V7X_SUBLANES, S5_CH)
    row = lax.broadcasted_iota(jnp.int32, shape, 0)

    def bc(x):
        return jnp.broadcast_to(x, shape)

    for lvl, k in enumerate((1, 2, 4)):
        coef_ref[2 * lvl] = jnp.where(row >= k, bc(pr[k - 1]), 0.0)
        coef_ref[2 * lvl + 1] = jnp.where(row >= k, bc(pim[k - 1]), 0.0)
    accr = jnp.zeros(shape, F32)
    acci = jnp.zeros(shape, F32)
    for r in range(V7X_SUBLANES):
        accr = jnp.where(row == r, bc(pr[r]), accr)
        acci = jnp.where(row == r, bc(pim[r]), acci)
    coef_ref[6] = accr
    coef_ref[7] = acci


def s5_prep(lam_re, lam_im, log_dt_rep):
    nl = lam_re.shape[0]
    vec = pl.BlockSpec((None, 1, S5_CH), lambda l: (l, 0, 0))
    return pl.pallas_call(
        _s5_prep_kernel,
        grid=(nl,),
        in_specs=[vec, vec, vec],
        out_specs=[pl.BlockSpec((None, 8, V7X_SUBLANES, S5_CH), lambda l: (l, 0, 0, 0)), vec, vec],
        out_shape=[jax.ShapeDtypeStruct((nl, 8, V7X_SUBLANES, S5_CH), F32),
                   jax.ShapeDtypeStruct((nl, 1, S5_CH), F32),
                   jax.ShapeDtypeStruct((nl, 1, S5_CH), F32)],
        compiler_params=_cparams(1),
        name="s5_prep",
    )(lam_re, lam_im, log_dt_rep)


def _s5_bb_kernel(cr_ref, ci_ref, br_ref, bi_ref, or_ref, oi_ref):
    cr = cr_ref[...]
    ci = ci_ref[...]
    br = br_ref[...]
    bi = bi_ref[...]
    or_ref[...] = cr * br - ci * bi
    oi_ref[...] = cr * bi + ci * br


def s5_bb(coef_re_col, coef_im_col, b_re, b_im):
    nl = b_re.shape[0]
    rows = 1024
    col = pl.BlockSpec((None, rows, 1), lambda l, r: (l, r, 0))
    mat = pl.BlockSpec((None, rows, S5_GROUP), lambda l, r: (l, r, 0))
    return pl.pallas_call(
        _s5_bb_kernel,
        grid=(nl, S5_CH // rows),
        in_specs=[col, col, mat, mat],
        out_specs=[mat, mat],
        out_shape=[jax.ShapeDtypeStruct((nl, S5_CH, S5_GROUP), F32)] * 2,
        compiler_params=_cparams(2),
        name="s5_bb",
    )(coef_re_col, coef_im_col, b_re, b_im)


S5_STRIP = 256


def _s5_kernel(xa_ref, za_ref, h0r_ref, h0i_ref, coef_ref, wur_ref, wui_ref, wyr_ref, wyi_ref,
               dsk_ref, wglu_ref, bglu_ref,
               mix_ref, hro_ref, hio_ref,
               ur_scr, ui_scr, cr_scr, ci_scr, y_scr, *, nb, tt, t_last):
    tc = pl.program_id(1)
    ntc = pl.num_programs(1)

    xb16 = xa_ref[...].astype(BF16)
    for k in range(4):
        xk = xb16[:, 256 * k:256 * (k + 1)]
        ur_scr[:, 1024 * k:1024 * (k + 1)] = _dot(xk, wur_ref[k])
        ui_scr[:, 1024 * k:1024 * (k + 1)] = _dot(xk, wui_ref[k])

    @pl.when(tc == 0)
    def _():
        cr_scr[...] = h0r_ref[...]
        ci_scr[...] = h0i_ref[...]

    ng = tt // V7X_SUBLANES
    for s in range(S5_CH // S5_STRIP):
        ls = slice(s * S5_STRIP, (s + 1) * S5_STRIP)
        lv = [(coef_ref[2 * l, :, ls], coef_ref[2 * l + 1, :, ls], k) for l, k in enumerate((1, 2, 4))]
        pr = coef_ref[6, :, ls]
        pim = coef_ref[7, :, ls]

        def seq_body(bi, carry, ls=ls, lv=lv, pr=pr, pim=pim):
            def tile_body(g, st):
                cr, ci = st
                rows = pl.ds(pl.multiple_of(bi * tt + g * V7X_SUBLANES, V7X_SUBLANES), V7X_SUBLANES)
                ur = ur_scr[rows, ls]
                ui = ui_scr[rows, ls]
                for akr, aki, k in lv:
                    sr = pltpu.roll(ur, k, 0)
                    si = pltpu.roll(ui, k, 0)
                    ur, ui = ur + (akr * sr - aki * si), ui + (akr * si + aki * sr)
                hr = ur + (pr * cr - pim * ci)
                hi = ui + (pr * ci + pim * cr)
                ur_scr[rows, ls] = hr
                ui_scr[rows, ls] = hi
                return hr[V7X_SUBLANES - 1:, :], hi[V7X_SUBLANES - 1:, :]

            cr, ci = lax.fori_loop(0, ng, tile_body, (cr_scr[bi, :, ls], ci_scr[bi, :, ls]))
            cr_scr[bi, :, ls] = cr
            ci_scr[bi, :, ls] = ci
            return carry

        lax.fori_loop(0, nb, seq_body, 0)

    for j in range(8):
        hr16 = ur_scr[:, 512 * j:512 * (j + 1)].astype(BF16)
        hi16 = ui_scr[:, 512 * j:512 * (j + 1)].astype(BF16)
        y_scr[:, 128 * j:128 * (j + 1)] = _dot(hr16, wyr_ref[j]) - _dot(hi16, wyi_ref[j])
    y = y_scr[...] + dsk_ref[...] * xa_ref[...]
    ga = _gelu_tanh(y)
    ya = ga * _sigmoid(_dot(ga.astype(BF16), wglu_ref[...]) + bglu_ref[...])
    mix_ref[...] = (ya * _silu(za_ref[...])).astype(BF16)

    @pl.when(tc == ntc - 1)
    def _():
        for bi in range(nb):
            r = bi * tt + t_last
            hro_ref[bi] = ur_scr[r:r + 1, :]
            hio_ref[bi] = ui_scr[r:r + 1, :]


def s5_mixer(proj, h0_re, h0_im, coef, wur, wui, wyr, wyi, dskip, wglu, bglu, layer, state_layer,
             *, batch, seq, t_valid, nb, tt):
    ntc = seq // tt
    assert nb == 1 or ntc == 1
    rows = nb * tt
    grid = (batch // nb, ntc)
    t_last = (t_valid - 1) - (ntc - 1) * tt
    assert 0 <= t_last < tt

    def tok(cb):
        return pl.BlockSpec((rows, W_MIX), lambda b, t: (b * ntc + t, cb))

    st_in = pl.BlockSpec((None, nb, 1, S5_CH), lambda b, t: (state_layer, b, 0, 0))
    st_out = pl.BlockSpec((nb, 1, S5_CH), lambda b, t: (b, 0, 0))

    def lw(shape):
        nd = len(shape)
        return pl.BlockSpec((None,) + shape, lambda b, t: (layer,) + (0,) * nd)

    return pl.pallas_call(
        functools.partial(_s5_kernel, nb=nb, tt=tt, t_last=t_last),
        grid=grid,
        in_specs=[tok(COL_XA // W_MIX), tok(COL_ZA // W_MIX), st_in, st_in,
                  lw((8, V7X_SUBLANES, S5_CH)), lw((4, 256, 1024)), lw((4, 256, 1024)),
                  lw((8, 512, 128)), lw((8, 512, 128)), lw((1, W_MIX)), lw((W_MIX, W_MIX)), lw((1, W_MIX))],
        out_specs=[pl.BlockSpec((rows, W_MIX), lambda b, t: (b * ntc + t, 0)), st_out, st_out],
        out_shape=[jax.ShapeDtypeStruct((batch * seq, W_MIX), BF16),
                   jax.ShapeDtypeStruct((batch, 1, S5_CH), F32),
                   jax.ShapeDtypeStruct((batch, 1, S5_CH), F32)],
        scratch_shapes=[pltpu.VMEM((rows, S5_CH), F32), pltpu.VMEM((rows, S5_CH), F32),
                        pltpu.VMEM((nb, 1, S5_CH), F32), pltpu.VMEM((nb, 1, S5_CH), F32),
                        pltpu.VMEM((rows, W_MIX), F32)],
        compiler_params=_cparams(2),
        name="s5_mixer",
    )(proj, proj, h0_re, h0_im, coef, wur, wui, wyr, wyi, dskip, wglu, bglu)


LRU_STRIP = 512
CONV_PAD = V7X_SUBLANES


def _causal_conv(xp_scr, tail_scr, x_ref, cw_ref, bi, rows_n):
    rows = pl.ds(pl.multiple_of(bi * rows_n, V7X_SUBLANES), rows_n)
    xp_scr[0:CONV_PAD, :] = tail_scr[bi]
    xp_scr[CONV_PAD:CONV_PAD + rows_n, :] = x_ref[rows, :]
    acc = cw_ref[CONV_W - 1:CONV_W, :] * xp_scr[CONV_PAD:CONV_PAD + rows_n, :]
    for jj in range(CONV_W - 1):
        off = CONV_PAD - (CONV_W - 1) + jj
        acc = acc + cw_ref[jj:jj + 1, :] * xp_scr[off:off + rows_n, :]
    tail_scr[bi] = xp_scr[rows_n:rows_n + CONV_PAD, :]
    return acc, rows


def _init_conv_tail(tail_scr, cv0_ref, nb):
    tail_scr[...] = jnp.zeros(tail_scr.shape, F32)
    for bi in range(nb):
        tail_scr[bi, CONV_PAD - (CONV_W - 1):CONV_PAD, :] = cv0_ref[bi]


def _lru_kernel(xb_ref, zb_ref, h0_ref, cv0_ref, cw_ref, cb_ref, wri_ref, br_ref, bi_ref, lam_ref,
                mix_ref, ho_ref,
                xp_scr, tail_scr, xc_scr, a_scr, b_scr, c_scr, *, nb, tt, t_last):
    tc = pl.program_id(1)
    ntc = pl.num_programs(1)

    @pl.when(tc == 0)
    def _():
        c_scr[...] = h0_ref[...]
        _init_conv_tail(tail_scr, cv0_ref, nb)

    def conv_body(bi, carry):
        acc, rows = _causal_conv(xp_scr, tail_scr, xb_ref, cw_ref, bi, tt)
        xc_scr[rows, :] = acc + cb_ref[...]
        return carry

    lax.fori_loop(0, nb, conv_body, 0)

    xc = xc_scr[...]
    xc16 = xc.astype(BF16)
    sp = _softplus(-lam_ref[...])
    for blk in range(LRU_BLOCKS):
        cs = slice(blk * LRU_BLK, (blk + 1) * LRU_BLK)
        pre = _dot(xc16[:, cs], wri_ref[blk])
        r = _sigmoid(pre[:, :LRU_BLK] + br_ref[:, cs])
        ig = _sigmoid(pre[:, LRU_BLK:] + bi_ref[:, cs])
        log_a = (-LRU_C) * r * sp[:, cs]
        a_scr[:, cs] = jnp.exp(log_a)
        b_scr[:, cs] = jnp.sqrt(-_expm1(2.0 * log_a)) * (ig * xc[:, cs])

    ng = tt // V7X_SUBLANES
    for s in range(W_MIX // LRU_STRIP):
        ls = slice(s * LRU_STRIP, (s + 1) * LRU_STRIP)
        row = lax.broadcasted_iota(jnp.int32, (V7X_SUBLANES, LRU_STRIP), 0)

        def seq_body(bi, carry, ls=ls, row=row):
            def tile_body(g, c):
                rows = pl.ds(pl.multiple_of(bi * tt + g * V7X_SUBLANES, V7X_SUBLANES), V7X_SUBLANES)
                a = a_scr[rows, ls]
                b = b_scr[rows, ls]
                for k in (1, 2, 4):
                    a_sh = jnp.where(row >= k, pltpu.roll(a, k, 0), 1.0)
                    b_sh = jnp.where(row >= k, pltpu.roll(b, k, 0), 0.0)
                    b = b + a * b_sh
                    a = a * a_sh
                h = b + a * c
                b_scr[rows, ls] = h
                return h[V7X_SUBLANES - 1:, :]

            c = lax.fori_loop(0, ng, tile_body, c_scr[bi, :, ls])
            c_scr[bi, :, ls] = c
            return carry

        lax.fori_loop(0, nb, seq_body, 0)

    mix_ref[...] = (b_scr[...] * _silu(zb_ref[...])).astype(BF16)

    @pl.when(tc == ntc - 1)
    def _():
        for bi in range(nb):
            r = bi * tt + t_last
            ho_ref[bi] = b_scr[r:r + 1, :]


def lru_mixer(proj, h0, conv0, cw, cb, wri, br, bi_, lam, layer, state_layer, *, batch, seq, t_valid, nb, tt):
    ntc = seq // tt
    assert nb == 1 or ntc == 1
    rows = nb * tt
    grid = (batch // nb, ntc)
    t_last = (t_valid - 1) - (ntc - 1) * tt

    def tok(cb_):
        return pl.BlockSpec((rows, W_MIX), lambda b, t: (b * ntc + t, cb_))

    def lw(shape):
        nd = len(shape)
        return pl.BlockSpec((None,) + shape, lambda b, t: (layer,) + (0,) * nd)

    return pl.pallas_call(
        functools.partial(_lru_kernel, nb=nb, tt=tt, t_last=t_last),
        grid=grid,
        in_specs=[tok(COL_XB // W_MIX), tok(COL_ZB // W_MIX),
                  pl.BlockSpec((None, nb, 1, W_MIX), lambda b, t: (state_layer, b, 0, 0)),
                  pl.BlockSpec((None, nb, CONV_W - 1, W_MIX), lambda b, t: (state_layer, b, 0, 0)),
                  lw((CONV_W, W_MIX)), lw((1, W_MIX)), lw((LRU_BLOCKS, LRU_BLK, 2 * LRU_BLK)),
                  lw((1, W_MIX)), lw((1, W_MIX)), lw((1, W_MIX))],
        out_specs=[pl.BlockSpec((rows, W_MIX), lambda b, t: (b * ntc + t, 0)),
                   pl.BlockSpec((nb, 1, W_MIX), lambda b, t: (b, 0, 0))],
        out_shape=[jax.ShapeDtypeStruct((batch * seq, W_MIX), BF16),
                   jax.ShapeDtypeStruct((batch, 1, W_MIX), F32)],
        scratch_shapes=[pltpu.VMEM((CONV_PAD + tt, W_MIX), F32),
                        pltpu.VMEM((nb, CONV_PAD, W_MIX), F32),
                        pltpu.VMEM((rows, W_MIX), F32), pltpu.VMEM((rows, W_MIX), F32),
                        pltpu.VMEM((rows, W_MIX), F32), pltpu.VMEM((nb, 1, W_MIX), F32)],
        compiler_params=_cparams(2),
        name="lru_mixer",
    )(proj, proj, h0, conv0, cw, cb, wri, br, bi_, lam)


def _gla_kernel(q_ref, k_ref, v_ref, z_ref, tail_ref, s0_ref, wg_ref, bg_ref, go_ref, stack_ref,
                mix_ref, so_ref,
                s_scr, mix_scr, *, nb, c, cv):
    del stack_ref
    tc = pl.program_id(1)
    ntc = pl.num_programs(1)
    sb = min(16, c)
    nblk = c // sb

    @pl.when(tc == 0)
    def _():
        s_scr[...] = s0_ref[...]

    row_c = lax.broadcasted_iota(jnp.int32, (c, 1), 0)
    lane_c = lax.broadcasted_iota(jnp.int32, (sb, c), 1)
    row_sb = lax.broadcasted_iota(jnp.int32, (sb, 1), 0)

    def seq_body(bi, carry):
        rows = pl.ds(pl.multiple_of(bi * c, V7X_SUBLANES), c)
        x = _dot(tail_ref[rows, :].astype(BF16), wg_ref[...]) + bg_ref[...]
        gk = _log_sigmoid(x) * (1.0 / GLA_TAU)
        b_all = _cumsum_rows(gk)
        heads = range(GLA_H)
        s_old = [s_scr[bi, h] for h in heads]
        q = [q_ref[rows, h * GLA_DK:(h + 1) * GLA_DK] * (GLA_DK ** -0.5) for h in heads]
        k = [k_ref[rows, h * GLA_DK:(h + 1) * GLA_DK] for h in heads]
        v = [v_ref[rows, h * GLA_DV:(h + 1) * GLA_DV] for h in heads]
        b = [b_all[:, h * GLA_DK:(h + 1) * GLA_DK] for h in heads]
        o_state = [_dot((q[h] * jnp.exp(b[h])).astype(BF16), s_old[h].astype(BF16)) for h in heads]
        s_new = []
        for h in heads:
            b_last = b[h][cv - 1:cv]
            dec = b_last - b[h]
            if cv < c:
                dec = jnp.where(row_c < cv, dec, NEG_INF)
            kd = k[h] * jnp.exp(dec)
            d_col = jnp.transpose(jnp.broadcast_to(jnp.exp(b_last), (V7X_SUBLANES, GLA_DK)))[:, 0:1]
            s_new.append(d_col * s_old[h] + _mm(kd, v[h], dot=_dot_tn, contract=0))
        att_off = []
        for h in heads:
            per_blk = [jnp.zeros((sb, c), F32)]
            for blk in range(1, nblk):
                r0 = blk * sb
                b_ref_row = b[h][r0 - 1:r0]
                qs = (q[h][r0:r0 + sb] * jnp.exp(b[h][r0:r0 + sb] - b_ref_row)).astype(BF16)
                kd = (k[h] * jnp.exp(jnp.where(row_c < r0, b_ref_row - b[h], NEG_INF))).astype(BF16)
                per_blk.append(_dot_nt(qs, kd))
            att_off.append(per_blk)
        att = []
        for h in heads:
            att_rows = []
            for blk in range(nblk):
                r0 = blk * sb
                q_i = q[h][r0:r0 + sb]
                b_i = b[h][r0:r0 + sb]
                a = att_off[h][blk]
                for sl in range(sb):
                    s_abs = r0 + sl
                    e = jnp.exp(jnp.where(row_sb >= sl, b_i - b[h][s_abs:s_abs + 1], NEG_INF))
                    col = jnp.sum(q_i * (k[h][s_abs:s_abs + 1] * e), axis=-1, keepdims=True)
                    a = jnp.where(lane_c == s_abs, col, a)
                att_rows.append(a)
            att.append(att_rows[0] if nblk == 1 else jnp.concatenate(att_rows, axis=0))
        o = [_mm(att[h], v[h]) + o_state[h] for h in heads]
        for h in heads:
            vs = slice(h * GLA_DV, (h + 1) * GLA_DV)
            ms = jnp.mean(o[h] * o[h], axis=-1, keepdims=True)
            on = (o[h] * lax.rsqrt(ms + EPS)) * go_ref[...]
            mix_scr[rows, vs] = on * _silu(z_ref[rows, vs])
            s_scr[bi, h] = s_new[h]
        return carry

    lax.fori_loop(0, nb, seq_body, 0)
    mix_ref[...] = mix_scr[...].astype(BF16)

    @pl.when(tc == ntc - 1)
    def _():
        so_ref[...] = s_scr[...]


def _stack_alias(stack, n_inputs_before):
    spec = pl.BlockSpec(memory_space=pl.ANY)
    if stack is None:
        return jnp.zeros((V7X_SUBLANES, V7X_LANES), F32), spec, {}
    return stack, spec, {n_inputs_before: 1}


def gla_mixer(proj, tail, s0, wg, bg, go, layer, state_layer, *, batch, seq, c, cv, nb, n_layers, stack=None):
    ntc = seq // c
    assert nb == 1 or ntc == 1
    rows = nb * c
    grid = (batch // nb, ntc)

    def tok(width, col):
        return pl.BlockSpec((rows, width), lambda b, t: (b * ntc + t, col // width))

    def lw(shape):
        nd = len(shape)
        return pl.BlockSpec((None,) + shape, lambda b, t: (layer,) + (0,) * nd)

    st_shape = (nb, GLA_H, GLA_DK, GLA_DV)
    stack_arg, stack_spec, aliases = _stack_alias(stack, 9)
    return pl.pallas_call(
        functools.partial(_gla_kernel, nb=nb, c=c, cv=cv),
        grid=grid,
        in_specs=[tok(GLA_H * GLA_DK, COL_QC), tok(GLA_H * GLA_DK, COL_KC), tok(W_MIX, COL_VC), tok(W_MIX, COL_ZC),
                  pl.BlockSpec((rows, TAIL_W), lambda b, t: (b * ntc + t, 0)),
                  pl.BlockSpec((None,) + st_shape, lambda b, t: (state_layer, b, 0, 0, 0)),
                  lw((TAIL_W, GLA_H * GLA_DK)), lw((1, GLA_H * GLA_DK)), lw((1, GLA_DV)), stack_spec],
        out_specs=[pl.BlockSpec((rows, W_MIX), lambda b, t: (b * ntc + t, 0)),
                   pl.BlockSpec((None,) + st_shape, lambda b, t: (layer, b, 0, 0, 0))],
        out_shape=[jax.ShapeDtypeStruct((batch * seq, W_MIX), BF16),
                   jax.ShapeDtypeStruct((n_layers, batch, GLA_H, GLA_DK, GLA_DV), F32)],
        scratch_shapes=[pltpu.VMEM(st_shape, F32), pltpu.VMEM((rows, W_MIX), F32)],
        input_output_aliases=aliases,
        compiler_params=_cparams(2),
        name="gla_mixer",
    )(proj, proj, proj, proj, tail, s0, wg, bg, go, stack_arg)


def _l2norm(x):
    return x * lax.rsqrt(jnp.sum(x * x, axis=-1, keepdims=True) + EPS)


def _unit_lower_solve(ms, rhss, c, cv):
    if c <= V7X_SUBLANES:
        row = lax.broadcasted_iota(jnp.int32, (c, 1), 0)
        us = []
        for m, rhs in zip(ms, rhss):
            u = jnp.zeros_like(rhs)
            u_rows = []
            for t in range(cv):
                ut = rhs[t:t + 1]
                for s in range(t):
                    ut = ut - m[t:t + 1, s:s + 1] * u_rows[s]
                u_rows.append(ut)
                u = jnp.where(row == t, ut, u)
            us.append(u)
        return us
    xs = [-m for m in ms]
    ys = list(rhss)
    dv = rhss[0].shape[1]
    levels = int(math.log2(c))
    for lvl in range(levels):
        last = lvl == levels - 1
        ps = [_dot_x3(x, y if last else jnp.concatenate([y, x], axis=1)) for x, y in zip(xs, ys)]
        ys = [y + p[:, :dv] for y, p in zip(ys, ps)]
        if not last:
            xs = [p[:, dv:] for p in ps]
    return ys


def _gdn_kernel(qkv_ref, z_ref, tail_ref, s0_ref, cv0_ref, cw_ref, alog_ref, dtb_ref, go_ref, stack_ref,
                mix_ref, so_ref,
                s_scr, xp_scr, tail_scr, mix_scr, *, nb, c, cv):
    del stack_ref
    tc = pl.program_id(1)
    ntc = pl.num_programs(1)

    @pl.when(tc == 0)
    def _():
        s_scr[...] = s0_ref[...]
        _init_conv_tail(tail_scr, cv0_ref, nb)

    row_c = lax.broadcasted_iota(jnp.int32, (c, 1), 0)
    ri = lax.broadcasted_iota(jnp.int32, (c, c), 0)
    ci = lax.broadcasted_iota(jnp.int32, (c, c), 1)
    lane_t = lax.broadcasted_iota(jnp.int32, (c, TAIL_W), 1)
    a_lanes = (lane_t >= TAIL_A) & (lane_t < TAIL_A + DN_H)

    def seq_body(bi, carry):
        conv, rows = _causal_conv(xp_scr, tail_scr, qkv_ref, cw_ref, bi, c)
        qkv = _silu(conv)
        tl = tail_ref[rows, :]
        g = jnp.where(a_lanes, -jnp.exp(alog_ref[...]) * _softplus(tl + dtb_ref[...]), 0.0)
        gam = _cumsum_rows(g)
        gam_t = jnp.transpose(gam)
        beta = _sigmoid(tl)
        heads = range(DN_H)
        s_old = [s_scr[bi, h] for h in heads]
        s16 = [s.astype(BF16) for s in s_old]
        q16, k16, kf, v = [], [], [], []
        for h in heads:
            q = _l2norm(qkv[:, h * DN_DK:(h + 1) * DN_DK]) * (DN_DK ** -0.5)
            k = _l2norm(qkv[:, DN_H * DN_DK + h * DN_DK:DN_H * DN_DK + (h + 1) * DN_DK])
            q16.append(q.astype(BF16))
            k16.append(k.astype(BF16))
            kf.append(k)
            v.append(qkv[:, 2 * DN_H * DN_DK + h * DN_DV:2 * DN_H * DN_DK + (h + 1) * DN_DV])
        g_col = [gam[:, TAIL_A + h:TAIL_A + h + 1] for h in heads]
        g_row = [gam_t[TAIL_A + h:TAIL_A + h + 1, :] for h in heads]
        b_col = [beta[:, TAIL_B + h:TAIL_B + h + 1] for h in heads]
        decay = [jnp.exp(jnp.where(ri >= ci, g_col[h] - g_row[h], NEG_INF)) for h in heads]
        eg = [jnp.exp(g_col[h]) for h in heads]
        kk = [_dot_nt(k16[h], k16[h]) for h in heads]
        k_s = [_dot(k16[h], s16[h]) for h in heads]
        q_s = [_dot(q16[h], s16[h]) for h in heads]
        qk = [_dot_nt(q16[h], k16[h]) for h in heads]
        m = [jnp.where(ri > ci, b_col[h] * kk[h] * decay[h], 0.0) for h in heads]
        rhs = [b_col[h] * (v[h] - eg[h] * k_s[h]) for h in heads]
        u = _unit_lower_solve(m, rhs, c, cv)
        o = [eg[h] * q_s[h] + _mm(qk[h] * decay[h], u[h]) for h in heads]
        kd = []
        for h in heads:
            dec = g_col[h][cv - 1:cv] - g_col[h]
            if cv < c:
                dec = jnp.where(row_c < cv, dec, NEG_INF)
            kd.append(kf[h] * jnp.exp(dec))
        s_new = [jnp.exp(g_col[h][cv - 1:cv]) * s_old[h] + _mm(kd[h], u[h], dot=_dot_tn, contract=0) for h in heads]
        for h in heads:
            ms = jnp.mean(o[h] * o[h], axis=-1, keepdims=True)
            on = (o[h] * lax.rsqrt(ms + EPS)) * go_ref[...]
            mix_scr[rows, h * DN_DV:(h + 1) * DN_DV] = on * _silu(z_ref[rows, h * DN_DV:(h + 1) * DN_DV])
            s_scr[bi, h] = s_new[h]
        return carry

    lax.fori_loop(0, nb, seq_body, 0)
    mix_ref[...] = mix_scr[...].astype(BF16)

    @pl.when(tc == ntc - 1)
    def _():
        so_ref[...] = s_scr[...]


def gdn_mixer(proj, tail, s0, conv0, cw, alog, dtb, go, layer, state_layer, *, batch, seq, c, cv, nb, n_layers,
              stack=None):
    ntc = seq // c
    assert nb == 1 or ntc == 1
    rows = nb * c
    grid = (batch // nb, ntc)

    def tok(width, col):
        return pl.BlockSpec((rows, width), lambda b, t: (b * ntc + t, col // width))

    def lw(shape):
        nd = len(shape)
        return pl.BlockSpec((None,) + shape, lambda b, t: (layer,) + (0,) * nd)

    st_shape = (nb, DN_H, DN_DK, DN_DV)
    assert COL_QKVD % DN_QKV == 0
    stack_arg, stack_spec, aliases = _stack_alias(stack, 9)
    return pl.pallas_call(
        functools.partial(_gdn_kernel, nb=nb, c=c, cv=cv),
        grid=grid,
        in_specs=[tok(DN_QKV, COL_QKVD),
                  tok(W_MIX, COL_ZD),
                  pl.BlockSpec((rows, TAIL_W), lambda b, t: (b * ntc + t, 0)),
                  pl.BlockSpec((None,) + st_shape, lambda b, t: (state_layer, b, 0, 0, 0)),
                  pl.BlockSpec((None, nb, CONV_W - 1, DN_QKV), lambda b, t: (state_layer, b, 0, 0)),
                  lw((CONV_W, DN_QKV)), lw((1, TAIL_W)), lw((1, TAIL_W)), lw((1, DN_DV)), stack_spec],
        out_specs=[pl.BlockSpec((rows, W_MIX), lambda b, t: (b * ntc + t, 0)),
                   pl.BlockSpec((None,) + st_shape, lambda b, t: (layer, b, 0, 0, 0))],
        out_shape=[jax.ShapeDtypeStruct((batch * seq, W_MIX), BF16),
                   jax.ShapeDtypeStruct((n_layers, batch, DN_H, DN_DK, DN_DV), F32)],
        input_output_aliases=aliases,
        scratch_shapes=[pltpu.VMEM(st_shape, F32),
                        pltpu.VMEM((CONV_PAD + c, DN_QKV), F32),
                        pltpu.VMEM((nb, CONV_PAD, DN_QKV), F32),
                        pltpu.VMEM((rows, W_MIX), F32)],
        compiler_params=_cparams(2),
        name="gdn_mixer",
    )(proj, proj, tail, s0, conv0, cw, alog, dtb, go, stack_arg)


def _prepare_weights(w_in, s5_lam_re, s5_lam_im, s5_log_dt, s5_b_re, s5_b_im, s5_c_re, s5_c_im, s5_d,
                     s5_w_glu, lru_w_r, lru_w_i, gla_w_gate, dn_a_log, dn_dt_bias, w_out, ple_w, ple_gate_w):
    nl = w_in.shape[0]
    p = {}
    p["w_main"] = jnp.concatenate([w_in[:, :, :ORIG_ZC], w_in[:, :, ORIG_QKVD:ORIG_QKVD + DN_QKV],
                                   w_in[:, :, ORIG_ZC:ORIG_ZC + W_MIX], w_in[:, :, ORIG_ZD:ORIG_ZD + W_MIX]],
                                  axis=2).astype(BF16)
    p["w_tail"] = jnp.concatenate([w_in[:, :, ORIG_GC:ORIG_GC + GLA_RANK], w_in[:, :, ORIG_AB:ORIG_AB + 2 * DN_H],
                                   jnp.zeros((nl, D_MODEL, TAIL_W - GLA_RANK - 2 * DN_H), F32)], axis=2).astype(BF16)
    p["w_out"] = w_out.astype(BF16)
    p["ple_w"] = ple_w.astype(BF16)
    p["ple_gate_w"] = ple_gate_w.astype(BF16)
    p["w_glu"] = s5_w_glu.astype(BF16)
    ldt = jnp.repeat(s5_log_dt, S5_STATE, axis=1).reshape(nl, 1, S5_CH)
    coef, cre, cim = s5_prep(s5_lam_re.reshape(nl, 1, S5_CH), s5_lam_im.reshape(nl, 1, S5_CH), ldt)
    bb_re, bb_im = s5_bb(cre.reshape(nl, S5_CH, 1), cim.reshape(nl, S5_CH, 1),
                         s5_b_re.reshape(nl, S5_CH, S5_GROUP), s5_b_im.reshape(nl, S5_CH, S5_GROUP))
    eye16 = jnp.eye(16, dtype=F32)
    eye8 = jnp.eye(8, dtype=F32)

    def pack_u(bb):
        t = bb.reshape(nl, 4, 16, S5_STATE, S5_GROUP)
        return jnp.einsum("lkgph,gG->lkghGp", t, eye16).reshape(nl, 4, 256, 1024).astype(BF16)

    def pack_y(cc):
        t = cc.reshape(nl, 8, 8, S5_GROUP, S5_STATE)
        return jnp.einsum("ljghp,gG->ljgpGh", t, eye8).reshape(nl, 8, 512, 128).astype(BF16)

    p["s5_coef"] = coef
    p["s5_wur"], p["s5_wui"] = pack_u(bb_re), pack_u(bb_im)
    p["s5_wyr"], p["s5_wyi"] = pack_y(s5_c_re), pack_y(s5_c_im)
    p["s5_d"] = s5_d.reshape(nl, 1, W_MIX)
    p["lru_wri"] = jnp.concatenate([lru_w_r, lru_w_i], axis=-1).astype(BF16)
    p["gla_wg"] = jnp.concatenate([gla_w_gate, jnp.zeros((nl, TAIL_W - GLA_RANK, GLA_H * GLA_DK), F32)],
                                  axis=1).astype(BF16)

    def tail_row(x):
        return jnp.pad(x, ((0, 0), (TAIL_A, TAIL_W - TAIL_A - DN_H))).reshape(nl, 1, TAIL_W)

    p["dn_alog"] = tail_row(dn_a_log)
    p["dn_dtb"] = tail_row(dn_dt_bias)
    return p


def kernel(x_prompt, x_sample, state_s5_re, state_s5_im, state_lru_h, state_lru_conv, state_gla, state_delta, state_delta_conv, p_prompt, p_sample, g_norm, w_in, s5_lam_re, s5_lam_im, s5_log_dt, s5_b_re, s5_b_im, s5_c_re, s5_c_im, s5_d, s5_w_glu, s5_b_glu, lru_conv_w, lru_conv_b, lru_w_r, lru_b_r, lru_w_i, lru_b_i, lru_lam, gla_w_gate, gla_b_gate, gla_g_out, dn_conv_w, dn_a_log, dn_dt_bias, dn_g_out, w_out, ple_w, ple_gate_w, ple_gate_b, g_final):
    nl = w_in.shape[0]
    bp, tp, _ = x_prompt.shape
    bs, ts, _ = x_sample.shape
    tsp = SAMPLE_T_PAD
    mp, ms = bp * tp, bs * ts

    p = _prepare_weights(w_in, s5_lam_re, s5_lam_im, s5_log_dt, s5_b_re, s5_b_im, s5_c_re, s5_c_im, s5_d,
                         s5_w_glu, lru_w_r, lru_w_i, gla_w_gate, dn_a_log, dn_dt_bias, w_out, ple_w, ple_gate_w)

    def vec(x):
        return x.reshape(nl, 1, x.shape[-1])

    g_norm3, b_glu3, cb3 = vec(g_norm), vec(s5_b_glu), vec(lru_conv_b)
    br3, bi3, lam3 = vec(lru_b_r), vec(lru_b_i), vec(lru_lam)
    bg3, go_c3, go_d3, pgb3 = vec(gla_b_gate), vec(gla_g_out), vec(dn_g_out), vec(ple_gate_b)
    pe_p = p_prompt.reshape(nl, mp, D_PLE)
    pe_s = p_sample.reshape(nl, ms, D_PLE)

    z_s5 = jnp.zeros((1, bp, 1, S5_CH), F32)
    z_lru = jnp.zeros((1, bp, 1, W_MIX), F32)
    z_lconv = jnp.zeros((1, bp, CONV_W - 1, W_MIX), F32)
    z_gla = jnp.zeros((1, bp, GLA_H, GLA_DK, GLA_DV), F32)
    z_dn = jnp.zeros((1, bp, DN_H, DN_DK, DN_DV), F32)
    z_dconv = jnp.zeros((1, bp, CONV_W - 1, DN_QKV), F32)
    c_s5r = state_s5_re.reshape(nl, bs, 1, S5_CH)
    c_s5i = state_s5_im.reshape(nl, bs, 1, S5_CH)
    c_lru = state_lru_h.reshape(nl, bs, 1, W_MIX)

    def mixers(proj, tail, layer, *, batch, seq, t_valid, st, sl, nb_scan, tt, c, nb_mat, stacks):
        s5r0, s5i0, lru0, lconv0, gla0, dn0, dconv0 = st
        gla_stack, dn_stack = stacks
        mix_a, s5r, s5i = s5_mixer(proj, s5r0, s5i0, p["s5_coef"], p["s5_wur"], p["s5_wui"], p["s5_wyr"],
                                   p["s5_wyi"], p["s5_d"], p["w_glu"], b_glu3, layer, sl,
                                   batch=batch, seq=seq, t_valid=t_valid, nb=nb_scan, tt=tt)
        mix_b, lruh = lru_mixer(proj, lru0, lconv0, lru_conv_w, cb3, p["lru_wri"], br3, bi3, lam3, layer, sl,
                                batch=batch, seq=seq, t_valid=t_valid, nb=nb_scan, tt=tt)
        mix_c, glas = gla_mixer(proj, tail, gla0, p["gla_wg"], bg3, go_c3, layer, sl,
                                batch=batch, seq=seq, c=c, cv=min(c, t_valid), nb=nb_mat, n_layers=nl,
                                stack=gla_stack)
        mix_d, dns = gdn_mixer(proj, tail, dn0, dconv0, dn_conv_w, p["dn_alog"], p["dn_dtb"], go_d3, layer, sl,
                               batch=batch, seq=seq, c=c, cv=min(c, t_valid), nb=nb_mat, n_layers=nl,
                               stack=dn_stack)
        return (mix_a, mix_b, mix_c, mix_d), (s5r, s5i, lruh, glas, dns)

    hp = x_prompt.reshape(mp, D_MODEL)
    hs = x_sample.reshape(ms, D_MODEL)
    new_p, new_s = [], []
    stacks_p = stacks_s = (None, None)
    for i in range(nl):
        proj, tail = in_proj(hp, g_norm3, p["w_main"], p["w_tail"], i, tm=512, tn=512)
        mixes, (s5r, s5i, lruh, glas, dns) = mixers(
            proj, tail, i, batch=bp, seq=tp, t_valid=tp,
            st=(z_s5, z_s5, z_lru, z_lconv, z_gla, z_dn, z_dconv), sl=0,
            nb_scan=1, tt=256, c=GLA_CHUNK, nb_mat=1, stacks=stacks_p)
        stacks_p = (glas, dns)
        proj3 = proj.reshape(bp, tp, N_MAIN)
        new_p.append((s5r.reshape(bp, S5_GROUPS, S5_STATE), s5i.reshape(bp, S5_GROUPS, S5_STATE),
                      lruh.reshape(bp, W_MIX), proj3[:, tp - (CONV_W - 1):, COL_XB:COL_XB + W_MIX],
                      None, None, proj3[:, tp - (CONV_W - 1):, COL_QKVD:COL_QKVD + DN_QKV]))
        hp = out_proj(mixes, p["w_out"], hp, i, tm=512, tn=512)
        hp = ple(hp, pe_p, p["ple_gate_w"], pgb3, p["ple_w"], i, tm=512, tn=512)

        proj, tail = in_proj(hs, g_norm3, p["w_main"], p["w_tail"], i, tm=512, tn=512)
        proj3 = proj.reshape(bs, ts, N_MAIN)
        proj_pad = jnp.pad(proj3, ((0, 0), (0, tsp - ts), (0, 0))).reshape(bs * tsp, N_MAIN)
        tail_pad = jnp.pad(tail.reshape(bs, ts, TAIL_W), ((0, 0), (0, tsp - ts), (0, 0))).reshape(bs * tsp, TAIL_W)
        mixes, (s5r, s5i, lruh, glas, dns) = mixers(
            proj_pad, tail_pad, i, batch=bs, seq=tsp, t_valid=ts,
            st=(c_s5r, c_s5i, c_lru, state_lru_conv, state_gla, state_delta, state_delta_conv), sl=i,
            nb_scan=32, tt=tsp, c=tsp, nb_mat=8, stacks=stacks_s)
        stacks_s = (glas, dns)
        mixes = tuple(m.reshape(bs, tsp, W_MIX)[:, :ts].reshape(ms, W_MIX) for m in mixes)
        new_s.append((s5r.reshape(bs, S5_GROUPS, S5_STATE), s5i.reshape(bs, S5_GROUPS, S5_STATE),
                      lruh.reshape(bs, W_MIX), proj3[:, ts - (CONV_W - 1):, COL_XB:COL_XB + W_MIX],
                      None, None, proj3[:, ts - (CONV_W - 1):, COL_QKVD:COL_QKVD + DN_QKV]))
        hs = out_proj(mixes, p["w_out"], hs, i, tm=512, tn=512)
        hs = ple(hs, pe_s, p["ple_gate_w"], pgb3, p["ple_w"], i, tm=512, tn=512)

    g_fin = g_final.reshape(1, D_MODEL)
    y_prompt = final_norm(hp, g_fin, tm=256).reshape(bp, tp, D_MODEL)
    y_sample = final_norm(hs, g_fin, tm=256).reshape(bs, ts, D_MODEL)

    def stk(lst, j):
        return jnp.stack([s[j] for s in lst], axis=0)

    return (y_prompt, y_sample,
            stk(new_p, 0), stk(new_p, 1), stk(new_p, 2), stk(new_p, 3), stacks_p[0], stacks_p[1], stk(new_p, 6),
            stk(new_s, 0), stk(new_s, 1), stk(new_s, 2), stk(new_s, 3), stacks_s[0], stacks_s[1], stk(new_s, 6))
```

```python
import functools
import math

import jax
import jax.numpy as jnp
import numpy as np
from jax import lax
from jax.experimental import pallas as pl
from jax.experimental.pallas import tpu as pltpu

F32 = jnp.float32
BF16 = jnp.bfloat16
EPS = 1e-6
NEG_INF = float("-inf")

D_MODEL = 4096
DEPTH = 4
W_MIX = 1024
S5_GROUPS = 64
S5_GROUP = 16
S5_STATE = 64
S5_CH = S5_GROUPS * S5_STATE
LRU_BLOCKS = 8
LRU_BLK = 128
LRU_C = 8.0
CONV_W = 4
GLA_H = 4
GLA_DK = 128
GLA_DV = 256
GLA_RANK = 16
GLA_TAU = 16.0
GLA_CHUNK = 64
DN_H = 8
DN_DK = 128
DN_DV = 128
DN_QKV = 3072
DN_CHUNK = 64
D_PLE = 256
N_MAIN = 11264
COL_XA, COL_ZA, COL_XB, COL_ZB = 0, 1024, 2048, 3072
COL_QC, COL_KC, COL_VC = 4096, 4608, 5120
COL_QKVD, COL_ZC, COL_ZD = 6144, 9216, 10240
ORIG_ZC, ORIG_GC, ORIG_QKVD, ORIG_ZD, ORIG_AB = 6144, 7168, 7184, 10256, 11280
TAIL_W = 128
TAIL_G, TAIL_A, TAIL_B = 0, 16, 24
SAMPLE_T_PAD = 8

V7X_LANES = 128
V7X_SUBLANES = 8
VMEM_LIMIT = 52 * 1024 * 1024
DENSE_TM = 512
DENSE_TN = 1024


def _cparams(n_axes):
    return pltpu.CompilerParams(dimension_semantics=("arbitrary",) * n_axes,
                                vmem_limit_bytes=VMEM_LIMIT)


def _sigmoid(x):
    return jax.nn.sigmoid(x)


def _silu(x):
    return x * jax.nn.sigmoid(x)


def _softplus(x):
    return jnp.maximum(x, 0.0) + jnp.log1p(jnp.exp(-jnp.abs(x)))


def _expm1(x):
    u = jnp.exp(x)
    um1 = u - 1.0
    lg = jnp.log(u)
    r = um1 * x / jnp.where(lg == 0.0, 1.0, lg)
    return jnp.where(um1 == 0.0, x, jnp.where(um1 == -1.0, -1.0, r))


def _log_sigmoid(x):
    return jnp.minimum(x, 0.0) - jnp.log1p(jnp.exp(-jnp.abs(x)))


def _gelu_tanh(x):
    c = math.sqrt(2.0 / math.pi)
    return x * (0.5 * (1.0 + jnp.tanh(c * (x + 0.044715 * (x * x * x)))))


def _dot(a, b):
    return jnp.dot(a, b, preferred_element_type=F32)


def _dot_nt(a, b):
    return lax.dot_general(a, b, (((1,), (1,)), ((), ())), preferred_element_type=F32)


def _dot_tn(a, b):
    return lax.dot_general(a, b, (((0,), (0,)), ((), ())), preferred_element_type=F32)


def _split3(x):
    h = x.astype(BF16)
    r = x - h.astype(F32)
    m = r.astype(BF16)
    l = (r - m.astype(F32)).astype(BF16)
    return h, m, l


def _cumsum_rows(x):
    c = x.shape[0]
    if c <= V7X_SUBLANES:
        row = lax.broadcasted_iota(jnp.int32, x.shape, 0)
        k = 1
        while k < c:
            x = x + jnp.where(row >= k, pltpu.roll(x, k, 0), 0.0)
            k *= 2
        return x
    tri = (lax.broadcasted_iota(jnp.int32, (c, c), 0) >= lax.broadcasted_iota(jnp.int32, (c, c), 1)).astype(BF16)
    h, m, l = _split3(x)
    return _dot(tri, h) + _dot(tri, m) + _dot(tri, l)


def _mm(a, b, dot=_dot, contract=1):
    if a.shape[contract] < 2 * V7X_SUBLANES:
        return dot(a, b)
    return dot(a.astype(BF16), b.astype(BF16))


def _dot_x3(a, b):
    ah = a.astype(BF16)
    al = (a - ah.astype(F32)).astype(BF16)
    bh = b.astype(BF16)
    bl = (b - bh.astype(F32)).astype(BF16)
    return _dot(ah, bh) + _dot(ah, bl) + _dot(al, bh)


def _in_proj_kernel(x_ref, g_ref, w_ref, wt_ref, o_ref, ot_ref, hn_ref, *, row_chunk):
    j = pl.program_id(1)
    tm = x_ref.shape[0]

    @pl.when(j == 0)
    def _():
        def body(r, carry):
            rows = pl.ds(pl.multiple_of(r * row_chunk, row_chunk), row_chunk)
            x = x_ref[rows, :]
            ms = jnp.mean(x * x, axis=-1, keepdims=True)
            hn_ref[rows, :] = ((x * lax.rsqrt(ms + EPS)) * g_ref[...]).astype(BF16)
            return carry
        lax.fori_loop(0, tm // row_chunk, body, 0)
        ot_ref[...] = _dot(hn_ref[...], wt_ref[...])

    o_ref[...] = _dot(hn_ref[...], w_ref[...])


def in_proj(h, g_norm, w_main, w_tail, layer, *, tm, tn):
    m = h.shape[0]
    grid = (m // tm, N_MAIN // tn)
    return pl.pallas_call(
        functools.partial(_in_proj_kernel, row_chunk=64),
        grid=grid,
        in_specs=[
            pl.BlockSpec((tm, D_MODEL), lambda i, j: (i, 0)),
            pl.BlockSpec((None, 1, D_MODEL), lambda i, j: (layer, 0, 0)),
            pl.BlockSpec((None, D_MODEL, tn), lambda i, j: (layer, 0, j)),
            pl.BlockSpec((None, D_MODEL, TAIL_W), lambda i, j: (layer, 0, 0)),
        ],
        out_specs=[
            pl.BlockSpec((tm, tn), lambda i, j: (i, j)),
            pl.BlockSpec((tm, TAIL_W), lambda i, j: (i, 0)),
        ],
        out_shape=[jax.ShapeDtypeStruct((m, N_MAIN), F32), jax.ShapeDtypeStruct((m, TAIL_W), F32)],
        scratch_shapes=[pltpu.VMEM((tm, D_MODEL), BF16)],
        compiler_params=_cparams(2),
        name="in_proj",
    )(h, g_norm, w_main, w_tail)


def _out_proj_kernel(ma_ref, mb_ref, mc_ref, md_ref, w_ref, h_ref, o_ref):
    acc = h_ref[...]
    for k, m_ref in enumerate((ma_ref, mb_ref, mc_ref, md_ref)):
        acc = acc + _dot(m_ref[...], w_ref[k * W_MIX:(k + 1) * W_MIX, :])
    o_ref[...] = acc


def out_proj(mixes, w_out, h, layer, *, tm, tn):
    m = h.shape[0]
    grid = (m // tm, D_MODEL // tn)
    mix_spec = pl.BlockSpec((tm, W_MIX), lambda i, j: (i, 0))
    return pl.pallas_call(
        _out_proj_kernel,
        grid=grid,
        in_specs=[mix_spec, mix_spec, mix_spec, mix_spec,
                  pl.BlockSpec((None, 4 * W_MIX, tn), lambda i, j: (layer, 0, j)),
                  pl.BlockSpec((tm, tn), lambda i, j: (i, j))],
        out_specs=pl.BlockSpec((tm, tn), lambda i, j: (i, j)),
        out_shape=jax.ShapeDtypeStruct((m, D_MODEL), F32),
        compiler_params=_cparams(2),
        name="out_proj",
    )(*mixes, w_out, h)


def _ple_kernel(hrow_ref, htile_ref, pe_ref, wg_ref, bg_ref, wp_ref, o_ref, hb_ref, *, row_chunk):
    j = pl.program_id(1)
    tm = hrow_ref.shape[0]

    @pl.when(j == 0)
    def _():
        def body(r, carry):
            rows = pl.ds(pl.multiple_of(r * row_chunk, row_chunk), row_chunk)
            hb_ref[rows, :] = hrow_ref[rows, :].astype(BF16)
            return carry
        lax.fori_loop(0, tm // row_chunk, body, 0)

    gate = _sigmoid(_dot(hb_ref[...], wg_ref[...]) + bg_ref[...])
    pv = _dot(pe_ref[...].astype(BF16), wp_ref[...])
    o_ref[...] = htile_ref[...] + gate * pv


def ple(h, pe, w_gate, b_gate, w_ple, layer, *, tm, tn):
    m = h.shape[0]
    grid = (m // tm, D_MODEL // tn)
    return pl.pallas_call(
        functools.partial(_ple_kernel, row_chunk=64),
        grid=grid,
        in_specs=[
            pl.BlockSpec((tm, D_MODEL), lambda i, j: (i, 0)),
            pl.BlockSpec((tm, tn), lambda i, j: (i, j)),
            pl.BlockSpec((None, tm, D_PLE), lambda i, j: (layer, i, 0)),
            pl.BlockSpec((None, D_MODEL, tn), lambda i, j: (layer, 0, j)),
            pl.BlockSpec((None, 1, tn), lambda i, j: (layer, 0, j)),
            pl.BlockSpec((None, D_PLE, tn), lambda i, j: (layer, 0, j)),
        ],
        out_specs=pl.BlockSpec((tm, tn), lambda i, j: (i, j)),
        out_shape=jax.ShapeDtypeStruct((m, D_MODEL), F32),
        scratch_shapes=[pltpu.VMEM((tm, D_MODEL), BF16)],
        compiler_params=_cparams(2),
        name="ple",
    )(h, h, pe, w_gate, b_gate, w_ple)


REPACK_W = 1024
ORIG_SHIFT = ORIG_QKVD % V7X_LANES


def _repack_kernel(a_ref, b_ref, g_ref, ab_ref, o_ref, t_ref, *, shifted):
    j = pl.program_id(2)
    is_shifted = functools.reduce(jnp.logical_or, [j == t for t in shifted])

    @pl.when(is_shifted)
    def _():
        ext = jnp.concatenate([a_ref[...], b_ref[...]], axis=1)
        o_ref[...] = ext[:, ORIG_SHIFT:ORIG_SHIFT + REPACK_W].astype(BF16)

    @pl.when(jnp.logical_not(is_shifted))
    def _():
        o_ref[...] = a_ref[...].astype(BF16)

    @pl.when(j == 0)
    def _():
        lane = lax.broadcasted_iota(jnp.int32, t_ref.shape, 1)
        tail = jnp.where(lane < TAIL_A, g_ref[...], jnp.where(lane < TAIL_B + DN_H, ab_ref[...], 0.0))
        t_ref[...] = tail.astype(BF16)


def repack_w_in(w_in, *, rb=1024):
    nl = w_in.shape[0]
    assert ORIG_GC % V7X_LANES == 0 and ORIG_AB % V7X_LANES == ORIG_SHIFT == TAIL_A
    assert ORIG_ZD % REPACK_W == ORIG_SHIFT and (ORIG_QKVD - ORIG_SHIFT) % REPACK_W == 0
    n_tiles = N_MAIN // REPACK_W
    t_qkvd, t_zc, t_zd = COL_QKVD // REPACK_W, COL_ZC // REPACK_W, COL_ZD // REPACK_W
    src_qkvd = (ORIG_QKVD - ORIG_SHIFT) // REPACK_W
    lanes_per_tile = REPACK_W // V7X_LANES

    def a_tile(j):
        return jnp.where(j < t_qkvd, j, jnp.where(j < t_zc, j - t_qkvd + src_qkvd,
                         jnp.where(j == t_zc, ORIG_ZC // REPACK_W, ORIG_ZD // REPACK_W)))

    def b_tile(j):
        return (a_tile(j) + 1) * lanes_per_tile

    shifted = tuple(range(t_qkvd, t_zc)) + (t_zd,)
    return pl.pallas_call(
        functools.partial(_repack_kernel, shifted=shifted),
        grid=(nl, D_MODEL // rb, n_tiles),
        in_specs=[pl.BlockSpec((None, rb, REPACK_W), lambda l, r, j: (l, r, a_tile(j))),
                  pl.BlockSpec((None, rb, V7X_LANES), lambda l, r, j: (l, r, b_tile(j))),
                  pl.BlockSpec((None, rb, V7X_LANES), lambda l, r, j: (l, r, ORIG_GC // V7X_LANES)),
                  pl.BlockSpec((None, rb, V7X_LANES), lambda l, r, j: (l, r, ORIG_AB // V7X_LANES))],
        out_specs=[pl.BlockSpec((None, rb, REPACK_W), lambda l, r, j: (l, r, j)),
                   pl.BlockSpec((None, rb, TAIL_W), lambda l, r, j: (l, r, 0))],
        out_shape=[jax.ShapeDtypeStruct((nl, D_MODEL, N_MAIN), BF16),
                   jax.ShapeDtypeStruct((nl, D_MODEL, TAIL_W), BF16)],
        compiler_params=_cparams(3),
        name="repack_w_in",
    )(w_in, w_in, w_in, w_in)


def _final_norm_kernel(x_ref, g_ref, o_ref):
    x = x_ref[...]
    ms = jnp.mean(x * x, axis=-1, keepdims=True)
    o_ref[...] = (x * lax.rsqrt(ms + EPS)) * g_ref[...]


def final_norm(h, g, *, tm):
    m = h.shape[0]
    return pl.pallas_call(
        _final_norm_kernel,
        grid=(m // tm,),
        in_specs=[pl.BlockSpec((tm, D_MODEL), lambda i: (i, 0)),
                  pl.BlockSpec((1, D_MODEL), lambda i: (0, 0))],
        out_specs=pl.BlockSpec((tm, D_MODEL), lambda i: (i, 0)),
        out_shape=jax.ShapeDtypeStruct((m, D_MODEL), F32),
        compiler_params=_cparams(1),
        name="final_norm",
    )(h, g)


S5_SEG = 32


def _s5_prep_kernel(lre_ref, lim_ref, ldt_ref, coef_ref, ptab_ref, cre_ref, cim_ref):
    lre = lre_ref[...]
    lim = lim_ref[...]
    dt = jnp.exp(ldt_ref[...])
    ai = lim * dt
    mag = jnp.exp(lre * dt)
    ar = mag * jnp.cos(ai)
    aim = mag * jnp.sin(ai)
    den = lre * lre + lim * lim
    nr = ar - 1.0
    cre_ref[...] = (nr * lre + aim * lim) / den
    cim_ref[...] = (aim * lre - nr * lim) / den
    pr, pim = [ar], [aim]
    for _ in range(S5_SEG - 1):
        nr_, ni_ = pr[-1] * ar - pim[-1] * aim, pr[-1] * aim + pim[-1] * ar
        pr.append(nr_)
        pim.append(ni_)
    shape = (V7X_SUBLANES, S5_CH)
    coef_ref[0] = jnp.broadcast_to(ar, shape)
    coef_ref[1] = jnp.broadcast_to(aim, shape)
    coef_ref[2] = jnp.broadcast_to(pr[S5_SEG - 1], shape)
    coef_ref[3] = jnp.broadcast_to(pim[S5_SEG - 1], shape)
    tshape = (S5_SEG, S5_CH)
    row = lax.broadcasted_iota(jnp.int32, tshape, 0)
    accr = jnp.zeros(tshape, F32)
    acci = jnp.zeros(tshape, F32)
    for j in range(S5_SEG):
        accr = jnp.where(row == j, jnp.broadcast_to(pr[j], tshape), accr)
        acci = jnp.where(row == j, jnp.broadcast_to(pim[j], tshape), acci)
    ptab_ref[0] = accr
    ptab_ref[1] = acci


def s5_prep(lam_re, lam_im, log_dt_rep):
    nl = lam_re.shape[0]
    vec = pl.BlockSpec((None, 1, S5_CH), lambda l: (l, 0, 0))
    return pl.pallas_call(
        _s5_prep_kernel,
        grid=(nl,),
        in_specs=[vec, vec, vec],
        out_specs=[pl.BlockSpec((None, 4, V7X_SUBLANES, S5_CH), lambda l: (l, 0, 0, 0)),
                   pl.BlockSpec((None, 2, S5_SEG, S5_CH), lambda l: (l, 0, 0, 0)), vec, vec],
        out_shape=[jax.ShapeDtypeStruct((nl, 4, V7X_SUBLANES, S5_CH), F32),
                   jax.ShapeDtypeStruct((nl, 2, S5_SEG, S5_CH), F32),
                   jax.ShapeDtypeStruct((nl, 1, S5_CH), F32),
                   jax.ShapeDtypeStruct((nl, 1, S5_CH), F32)],
        compiler_params=_cparams(1),
        name="s5_prep",
    )(lam_re, lam_im, log_dt_rep)


def _s5_bb_kernel(cr_ref, ci_ref, br_ref, bi_ref, or_ref, oi_ref):
    cr = cr_ref[...]
    ci = ci_ref[...]
    br = br_ref[...]
    bi = bi_ref[...]
    or_ref[...] = cr * br - ci * bi
    oi_ref[...] = cr * bi + ci * br


def s5_bb(coef_re_col, coef_im_col, b_re, b_im):
    nl = b_re.shape[0]
    rows = 1024
    col = pl.BlockSpec((None, rows, 1), lambda l, r: (l, r, 0))
    mat = pl.BlockSpec((None, rows, S5_GROUP), lambda l, r: (l, r, 0))
    return pl.pallas_call(
        _s5_bb_kernel,
        grid=(nl, S5_CH // rows),
        in_specs=[col, col, mat, mat],
        out_specs=[mat, mat],
        out_shape=[jax.ShapeDtypeStruct((nl, S5_CH, S5_GROUP), F32)] * 2,
        compiler_params=_cparams(2),
        name="s5_bb",
    )(coef_re_col, coef_im_col, b_re, b_im)


S5_STRIP = 512


def _s5_kernel(xa_ref, za_ref, h0r_ref, h0i_ref, coef_ref, ptab_ref, perm_ref, permt_ref,
               wur_ref, wui_ref, wyr_ref, wyi_ref, dsk_ref, wglu_ref, bglu_ref,
               mix_ref, hro_ref, hio_ref,
               ur_scr, ui_scr, cr_scr, ci_scr, y_scr, *, nb, tt, t_last):
    tc = pl.program_id(1)
    ntc = pl.num_programs(1)
    split = nb == 1
    lc = tt // V7X_SUBLANES if split else tt
    n_sets = 1 if split else nb // V7X_SUBLANES
    set_rows = V7X_SUBLANES * lc

    xb16 = _dot(perm_ref[...], xa_ref[...].astype(BF16)).astype(BF16)
    for k in range(4):
        xk = xb16[:, 256 * k:256 * (k + 1)]
        ur_scr[:, 1024 * k:1024 * (k + 1)] = _dot(xk, wur_ref[k])
        ui_scr[:, 1024 * k:1024 * (k + 1)] = _dot(xk, wui_ref[k])

    @pl.when(tc == 0)
    def _():
        cr_scr[...] = h0r_ref[...]
        ci_scr[...] = h0i_ref[...]

    row8 = lax.broadcasted_iota(jnp.int32, (V7X_SUBLANES, S5_STRIP), 0)
    for st in range(S5_CH // S5_STRIP):
        ls = slice(st * S5_STRIP, (st + 1) * S5_STRIP)
        ar = coef_ref[0, :, ls]
        ai = coef_ref[1, :, ls]
        for s in range(n_sets):
            base = s * set_rows
            if split:
                h0r = jnp.zeros((V7X_SUBLANES, S5_STRIP), F32)
                h0i = jnp.zeros((V7X_SUBLANES, S5_STRIP), F32)
            else:
                h0r = jnp.zeros((V7X_SUBLANES, S5_STRIP), F32)
                h0i = jnp.zeros((V7X_SUBLANES, S5_STRIP), F32)
                for k in range(V7X_SUBLANES):
                    h0r = jnp.where(row8 == k, cr_scr[s * V7X_SUBLANES + k, :, ls], h0r)
                    h0i = jnp.where(row8 == k, ci_scr[s * V7X_SUBLANES + k, :, ls], h0i)

            def step(j, h, base=base, ls=ls, ar=ar, ai=ai):
                hr, hi = h
                rows = pl.ds(pl.multiple_of(base + j * V7X_SUBLANES, V7X_SUBLANES), V7X_SUBLANES)
                nr = (ar * hr - ai * hi) + ur_scr[rows, ls]
                ni = (ar * hi + ai * hr) + ui_scr[rows, ls]
                ur_scr[rows, ls] = nr
                ui_scr[rows, ls] = ni
                return nr, ni

            fr, fi = lax.fori_loop(0, lc, step, (h0r, h0i), unroll=4)
            if split:
                alr = coef_ref[2, 0:1, ls]
                ali = coef_ref[3, 0:1, ls]
                cr = cr_scr[0, :, ls]
                ci = ci_scr[0, :, ls]
                init_r = jnp.zeros((V7X_SUBLANES, S5_STRIP), F32)
                init_i = jnp.zeros((V7X_SUBLANES, S5_STRIP), F32)
                for k in range(V7X_SUBLANES):
                    init_r = jnp.where(row8 == k, cr, init_r)
                    init_i = jnp.where(row8 == k, ci, init_i)
                    cr, ci = (alr * cr - ali * ci) + fr[k:k + 1], (alr * ci + ali * cr) + fi[k:k + 1]
                cr_scr[0, :, ls] = cr
                ci_scr[0, :, ls] = ci

                def fix(j, carry, base=base, ls=ls, init_r=init_r, init_i=init_i):
                    rows = pl.ds(pl.multiple_of(base + j * V7X_SUBLANES, V7X_SUBLANES), V7X_SUBLANES)
                    pr = ptab_ref[0, pl.ds(j, 1), ls]
                    pim = ptab_ref[1, pl.ds(j, 1), ls]
                    ur_scr[rows, ls] = ur_scr[rows, ls] + (pr * init_r - pim * init_i)
                    ui_scr[rows, ls] = ui_scr[rows, ls] + (pr * init_i + pim * init_r)
                    return carry

                lax.fori_loop(0, lc, fix, 0, unroll=4)

    for j in range(8):
        hr16 = ur_scr[:, 512 * j:512 * (j + 1)].astype(BF16)
        hi16 = ui_scr[:, 512 * j:512 * (j + 1)].astype(BF16)
        y_scr[:, 128 * j:128 * (j + 1)] = _dot(hr16, wyr_ref[j]) - _dot(hi16, wyi_ref[j])
    yh, ym, yl = _split3(y_scr[...])
    pt = permt_ref[...]
    y = (_dot(pt, yh) + _dot(pt, ym) + _dot(pt, yl)) + dsk_ref[...] * xa_ref[...]
    ga = _gelu_tanh(y)
    ya = ga * _sigmoid(_dot(ga.astype(BF16), wglu_ref[...]) + bglu_ref[...])
    mix_ref[...] = (ya * _silu(za_ref[...])).astype(BF16)

    @pl.when(tc == ntc - 1)
    def _():
        if split:
            hro_ref[...] = cr_scr[...]
            hio_ref[...] = ci_scr[...]
        else:
            for s in range(n_sets):
                for k in range(V7X_SUBLANES):
                    r = s * set_rows + V7X_SUBLANES * t_last + k
                    hro_ref[s * V7X_SUBLANES + k] = ur_scr[r:r + 1, :]
                    hio_ref[s * V7X_SUBLANES + k] = ui_scr[r:r + 1, :]


def s5_mixer(proj, h0_re, h0_im, coef, ptab, wur, wui, wyr, wyi, dskip, wglu, bglu, layer, state_layer,
             *, batch, seq, t_valid, nb, tt):
    ntc = seq // tt
    assert nb == 1 or ntc == 1
    rows = nb * tt
    grid = (batch // nb, ntc)
    t_last = (t_valid - 1) - (ntc - 1) * tt
    assert 0 <= t_last < tt
    if nb == 1:
        assert tt == V7X_SUBLANES * S5_SEG and t_last == tt - 1
        lc = S5_SEG
    else:
        assert nb % V7X_SUBLANES == 0
        lc = tt
    idx = np.arange(rows)
    set_rows = V7X_SUBLANES * lc
    src = (idx // set_rows) * set_rows + (idx % V7X_SUBLANES) * lc + (idx % set_rows) // V7X_SUBLANES
    perm_np = np.zeros((rows, rows), np.float32)
    perm_np[idx, src] = 1.0
    perm = jnp.asarray(perm_np, BF16)
    perm_t = jnp.asarray(perm_np.T, BF16)
    sq = pl.BlockSpec((rows, rows), lambda b, t: (0, 0))

    def tok(cb):
        return pl.BlockSpec((rows, W_MIX), lambda b, t: (b * ntc + t, cb))

    st_in = pl.BlockSpec((None, nb, 1, S5_CH), lambda b, t: (state_layer, b, 0, 0))
    st_out = pl.BlockSpec((nb, 1, S5_CH), lambda b, t: (b, 0, 0))

    def lw(shape):
        nd = len(shape)
        return pl.BlockSpec((None,) + shape, lambda b, t: (layer,) + (0,) * nd)

    return pl.pallas_call(
        functools.partial(_s5_kernel, nb=nb, tt=tt, t_last=t_last),
        grid=grid,
        in_specs=[tok(COL_XA // W_MIX), tok(COL_ZA // W_MIX), st_in, st_in,
                  lw((4, V7X_SUBLANES, S5_CH)), lw((2, S5_SEG, S5_CH)), sq, sq, lw((4, 256, 1024)), lw((4, 256, 1024)),
                  lw((8, 512, 128)), lw((8, 512, 128)), lw((1, W_MIX)), lw((W_MIX, W_MIX)), lw((1, W_MIX))],
        out_specs=[pl.BlockSpec((rows, W_MIX), lambda b, t: (b * ntc + t, 0)), st_out, st_out],
        out_shape=[jax.ShapeDtypeStruct((batch * seq, W_MIX), BF16),
                   jax.ShapeDtypeStruct((batch, 1, S5_CH), F32),
                   jax.ShapeDtypeStruct((batch, 1, S5_CH), F32)],
        scratch_shapes=[pltpu.VMEM((rows, S5_CH), F32), pltpu.VMEM((rows, S5_CH), F32),
                        pltpu.VMEM((nb, 1, S5_CH), F32), pltpu.VMEM((nb, 1, S5_CH), F32),
                        pltpu.VMEM((rows, W_MIX), F32)],
        compiler_params=_cparams(2),
        name="s5_mixer",
    )(proj, proj, h0_re, h0_im, coef, ptab, perm, perm_t, wur, wui, wyr, wyi, dskip, wglu, bglu)


LRU_STRIP = 512
CONV_PAD = V7X_SUBLANES


def _causal_conv(xp_scr, tail_scr, x_ref, cw_ref, bi, rows_n):
    rows = pl.ds(pl.multiple_of(bi * rows_n, V7X_SUBLANES), rows_n)
    xp_scr[0:CONV_PAD, :] = tail_scr[bi]
    xp_scr[CONV_PAD:CONV_PAD + rows_n, :] = x_ref[rows, :]
    acc = cw_ref[CONV_W - 1:CONV_W, :] * xp_scr[CONV_PAD:CONV_PAD + rows_n, :]
    for jj in range(CONV_W - 1):
        off = CONV_PAD - (CONV_W - 1) + jj
        acc = acc + cw_ref[jj:jj + 1, :] * xp_scr[off:off + rows_n, :]
    tail_scr[bi] = xp_scr[rows_n:rows_n + CONV_PAD, :]
    return acc, rows


def _init_conv_tail(tail_scr, cv0_ref, nb):
    tail_scr[...] = jnp.zeros(tail_scr.shape, F32)
    for bi in range(nb):
        tail_scr[bi, CONV_PAD - (CONV_W - 1):CONV_PAD, :] = cv0_ref[bi]


def _lru_kernel(xb_ref, zb_ref, h0_ref, cv0_ref, cw_ref, cb_ref, wri_ref, br_ref, bi_ref, lam_ref,
                mix_ref, ho_ref,
                xp_scr, tail_scr, xc_scr, a_scr, b_scr, c_scr, *, nb, tt, t_last):
    tc = pl.program_id(1)
    ntc = pl.num_programs(1)

    @pl.when(tc == 0)
    def _():
        c_scr[...] = h0_ref[...]
        _init_conv_tail(tail_scr, cv0_ref, nb)

    def conv_body(bi, carry):
        acc, rows = _causal_conv(xp_scr, tail_scr, xb_ref, cw_ref, bi, tt)
        xc_scr[rows, :] = acc + cb_ref[...]
        return carry

    lax.fori_loop(0, nb, conv_body, 0)

    xc = xc_scr[...]
    xc16 = xc.astype(BF16)
    sp = _softplus(-lam_ref[...])
    for blk in range(LRU_BLOCKS):
        cs = slice(blk * LRU_BLK, (blk + 1) * LRU_BLK)
        pre = _dot(xc16[:, cs], wri_ref[blk])
        r = _sigmoid(pre[:, :LRU_BLK] + br_ref[:, cs])
        ig = _sigmoid(pre[:, LRU_BLK:] + bi_ref[:, cs])
        log_a = (-LRU_C) * r * sp[:, cs]
        a_scr[:, cs] = jnp.exp(log_a)
        b_scr[:, cs] = jnp.sqrt(-_expm1(2.0 * log_a)) * (ig * xc[:, cs])

    ng = tt // V7X_SUBLANES
    for s in range(W_MIX // LRU_STRIP):
        ls = slice(s * LRU_STRIP, (s + 1) * LRU_STRIP)
        row = lax.broadcasted_iota(jnp.int32, (V7X_SUBLANES, LRU_STRIP), 0)

        def seq_body(bi, carry, ls=ls, row=row):
            def tile_body(g, c):
                rows = pl.ds(pl.multiple_of(bi * tt + g * V7X_SUBLANES, V7X_SUBLANES), V7X_SUBLANES)
                a = a_scr[rows, ls]
                b = b_scr[rows, ls]
                for k in (1, 2, 4):
                    a_sh = jnp.where(row >= k, pltpu.roll(a, k, 0), 1.0)
                    b_sh = jnp.where(row >= k, pltpu.roll(b, k, 0), 0.0)
                    b = b + a * b_sh
                    a = a * a_sh
                h = b + a * c
                b_scr[rows, ls] = h
                return h[V7X_SUBLANES - 1:, :]

            c = lax.fori_loop(0, ng, tile_body, c_scr[bi, :, ls])
            c_scr[bi, :, ls] = c
            return carry

        lax.fori_loop(0, nb, seq_body, 0)

    mix_ref[...] = (b_scr[...] * _silu(zb_ref[...])).astype(BF16)

    @pl.when(tc == ntc - 1)
    def _():
        for bi in range(nb):
            r = bi * tt + t_last
            ho_ref[bi] = b_scr[r:r + 1, :]


def lru_mixer(proj, h0, conv0, cw, cb, wri, br, bi_, lam, layer, state_layer, *, batch, seq, t_valid, nb, tt):
    ntc = seq // tt
    assert nb == 1 or ntc == 1
    rows = nb * tt
    grid = (batch // nb, ntc)
    t_last = (t_valid - 1) - (ntc - 1) * tt

    def tok(cb_):
        return pl.BlockSpec((rows, W_MIX), lambda b, t: (b * ntc + t, cb_))

    def lw(shape):
        nd = len(shape)
        return pl.BlockSpec((None,) + shape, lambda b, t: (layer,) + (0,) * nd)

    return pl.pallas_call(
        functools.partial(_lru_kernel, nb=nb, tt=tt, t_last=t_last),
        grid=grid,
        in_specs=[tok(COL_XB // W_MIX), tok(COL_ZB // W_MIX),
                  pl.BlockSpec((None, nb, 1, W_MIX), lambda b, t: (state_layer, b, 0, 0)),
                  pl.BlockSpec((None, nb, CONV_W - 1, W_MIX), lambda b, t: (state_layer, b, 0, 0)),
                  lw((CONV_W, W_MIX)), lw((1, W_MIX)), lw((LRU_BLOCKS, LRU_BLK, 2 * LRU_BLK)),
                  lw((1, W_MIX)), lw((1, W_MIX)), lw((1, W_MIX))],
        out_specs=[pl.BlockSpec((rows, W_MIX), lambda b, t: (b * ntc + t, 0)),
                   pl.BlockSpec((nb, 1, W_MIX), lambda b, t: (b, 0, 0))],
        out_shape=[jax.ShapeDtypeStruct((batch * seq, W_MIX), BF16),
                   jax.ShapeDtypeStruct((batch, 1, W_MIX), F32)],
        scratch_shapes=[pltpu.VMEM((CONV_PAD + tt, W_MIX), F32),
                        pltpu.VMEM((nb, CONV_PAD, W_MIX), F32),
                        pltpu.VMEM((rows, W_MIX), F32), pltpu.VMEM((rows, W_MIX), F32),
                        pltpu.VMEM((rows, W_MIX), F32), pltpu.VMEM((nb, 1, W_MIX), F32)],
        compiler_params=_cparams(2),
        name="lru_mixer",
    )(proj, proj, h0, conv0, cw, cb, wri, br, bi_, lam)


def _gla_kernel(q_ref, k_ref, v_ref, z_ref, tail_ref, s0_ref, wg_ref, bg_ref, go_ref, stack_ref,
                mix_ref, so_ref,
                s_scr, mix_scr, *, nb, c, cv):
    del stack_ref
    tc = pl.program_id(1)
    ntc = pl.num_programs(1)
    sb = min(16, c)
    nblk = c // sb

    @pl.when(tc == 0)
    def _():
        s_scr[...] = s0_ref[...]

    row_c = lax.broadcasted_iota(jnp.int32, (c, 1), 0)
    lane_c = lax.broadcasted_iota(jnp.int32, (sb, c), 1)
    row_sb = lax.broadcasted_iota(jnp.int32, (sb, 1), 0)

    def seq_body(bi, carry):
        rows = pl.ds(pl.multiple_of(bi * c, V7X_SUBLANES), c)
        x = _dot(tail_ref[rows, :].astype(BF16), wg_ref[...]) + bg_ref[...]
        gk = _log_sigmoid(x) * (1.0 / GLA_TAU)
        b_all = _cumsum_rows(gk)
        heads = range(GLA_H)
        s_old = [s_scr[bi, h] for h in heads]
        q = [q_ref[rows, h * GLA_DK:(h + 1) * GLA_DK] * (GLA_DK ** -0.5) for h in heads]
        k = [k_ref[rows, h * GLA_DK:(h + 1) * GLA_DK] for h in heads]
        v = [v_ref[rows, h * GLA_DV:(h + 1) * GLA_DV] for h in heads]
        b = [b_all[:, h * GLA_DK:(h + 1) * GLA_DK] for h in heads]
        o_state = [_dot((q[h] * jnp.exp(b[h])).astype(BF16), s_old[h].astype(BF16)) for h in heads]
        s_new = []
        for h in heads:
            b_last = b[h][cv - 1:cv]
            dec = b_last - b[h]
            if cv < c:
                dec = jnp.where(row_c < cv, dec, NEG_INF)
            kd = k[h] * jnp.exp(dec)
            d_col = jnp.transpose(jnp.broadcast_to(jnp.exp(b_last), (V7X_SUBLANES, GLA_DK)))[:, 0:1]
            s_new.append(d_col * s_old[h] + _mm(kd, v[h], dot=_dot_tn, contract=0))
        att_off = []
        for h in heads:
            per_blk = [jnp.zeros((sb, c), F32)]
            for blk in range(1, nblk):
                r0 = blk * sb
                b_ref_row = b[h][r0 - 1:r0]
                qs = (q[h][r0:r0 + sb] * jnp.exp(b[h][r0:r0 + sb] - b_ref_row)).astype(BF16)
                kd = (k[h] * jnp.exp(jnp.where(row_c < r0, b_ref_row - b[h], NEG_INF))).astype(BF16)
                per_blk.append(_dot_nt(qs, kd))
            att_off.append(per_blk)
        att = []
        for h in heads:
            att_rows = []
            for blk in range(nblk):
                r0 = blk * sb
                q_i = q[h][r0:r0 + sb]
                b_i = b[h][r0:r0 + sb]
                a = att_off[h][blk]
                for sl in range(sb):
                    s_abs = r0 + sl
                    e = jnp.exp(jnp.where(row_sb >= sl, b_i - b[h][s_abs:s_abs + 1], NEG_INF))
                    col = jnp.sum(q_i * (k[h][s_abs:s_abs + 1] * e), axis=-1, keepdims=True)
                    a = jnp.where(lane_c == s_abs, col, a)
                att_rows.append(a)
            att.append(att_rows[0] if nblk == 1 else jnp.concatenate(att_rows, axis=0))
        o = [_mm(att[h], v[h]) + o_state[h] for h in heads]
        for h in heads:
            vs = slice(h * GLA_DV, (h + 1) * GLA_DV)
            ms = jnp.mean(o[h] * o[h], axis=-1, keepdims=True)
            on = (o[h] * lax.rsqrt(ms + EPS)) * go_ref[...]
            mix_scr[rows, vs] = on * _silu(z_ref[rows, vs])
            s_scr[bi, h] = s_new[h]
        return carry

    lax.fori_loop(0, nb, seq_body, 0)
    mix_ref[...] = mix_scr[...].astype(BF16)

    @pl.when(tc == ntc - 1)
    def _():
        so_ref[...] = s_scr[...]


def _stack_alias(stack, n_inputs_before):
    spec = pl.BlockSpec(memory_space=pl.ANY)
    if stack is None:
        return jnp.zeros((V7X_SUBLANES, V7X_LANES), F32), spec, {}
    return stack, spec, {n_inputs_before: 1}


def gla_mixer(proj, tail, s0, wg, bg, go, layer, state_layer, *, batch, seq, c, cv, nb, n_layers, stack=None):
    ntc = seq // c
    assert nb == 1 or ntc == 1
    rows = nb * c
    grid = (batch // nb, ntc)

    def tok(width, col):
        return pl.BlockSpec((rows, width), lambda b, t: (b * ntc + t, col // width))

    def lw(shape):
        nd = len(shape)
        return pl.BlockSpec((None,) + shape, lambda b, t: (layer,) + (0,) * nd)

    st_shape = (nb, GLA_H, GLA_DK, GLA_DV)
    stack_arg, stack_spec, aliases = _stack_alias(stack, 9)
    return pl.pallas_call(
        functools.partial(_gla_kernel, nb=nb, c=c, cv=cv),
        grid=grid,
        in_specs=[tok(GLA_H * GLA_DK, COL_QC), tok(GLA_H * GLA_DK, COL_KC), tok(W_MIX, COL_VC), tok(W_MIX, COL_ZC),
                  pl.BlockSpec((rows, TAIL_W), lambda b, t: (b * ntc + t, 0)),
                  pl.BlockSpec((None,) + st_shape, lambda b, t: (state_layer, b, 0, 0, 0)),
                  lw((TAIL_W, GLA_H * GLA_DK)), lw((1, GLA_H * GLA_DK)), lw((1, GLA_DV)), stack_spec],
        out_specs=[pl.BlockSpec((rows, W_MIX), lambda b, t: (b * ntc + t, 0)),
                   pl.BlockSpec((None,) + st_shape, lambda b, t: (layer, b, 0, 0, 0))],
        out_shape=[jax.ShapeDtypeStruct((batch * seq, W_MIX), BF16),
                   jax.ShapeDtypeStruct((n_layers, batch, GLA_H, GLA_DK, GLA_DV), F32)],
        scratch_shapes=[pltpu.VMEM(st_shape, F32), pltpu.VMEM((rows, W_MIX), F32)],
        input_output_aliases=aliases,
        compiler_params=_cparams(2),
        name="gla_mixer",
    )(proj, proj, proj, proj, tail, s0, wg, bg, go, stack_arg)


def _l2norm(x):
    return x * lax.rsqrt(jnp.sum(x * x, axis=-1, keepdims=True) + EPS)


def _unit_lower_solve(ms, rhss, c, cv):
    if c <= V7X_SUBLANES:
        row = lax.broadcasted_iota(jnp.int32, (c, 1), 0)
        us = []
        for m, rhs in zip(ms, rhss):
            u = jnp.zeros_like(rhs)
            u_rows = []
            for t in range(cv):
                ut = rhs[t:t + 1]
                for s in range(t):
                    ut = ut - m[t:t + 1, s:s + 1] * u_rows[s]
                u_rows.append(ut)
                u = jnp.where(row == t, ut, u)
            us.append(u)
        return us
    xs = [-m for m in ms]
    ys = list(rhss)
    dv = rhss[0].shape[1]
    levels = int(math.log2(c))
    for lvl in range(levels):
        last = lvl == levels - 1
        ps = [_dot_x3(x, y if last else jnp.concatenate([y, x], axis=1)) for x, y in zip(xs, ys)]
        ys = [y + p[:, :dv] for y, p in zip(ys, ps)]
        if not last:
            xs = [p[:, dv:] for p in ps]
    return ys


def _gdn_kernel(qkv_ref, z_ref, tail_ref, s0_ref, cv0_ref, cw_ref, alog_ref, dtb_ref, go_ref, stack_ref,
                mix_ref, so_ref,
                s_scr, xp_scr, tail_scr, mix_scr, *, nb, c, cv):
    del stack_ref
    tc = pl.program_id(1)
    ntc = pl.num_programs(1)

    @pl.when(tc == 0)
    def _():
        s_scr[...] = s0_ref[...]
        _init_conv_tail(tail_scr, cv0_ref, nb)

    row_c = lax.broadcasted_iota(jnp.int32, (c, 1), 0)
    ri = lax.broadcasted_iota(jnp.int32, (c, c), 0)
    ci = lax.broadcasted_iota(jnp.int32, (c, c), 1)
    lane_t = lax.broadcasted_iota(jnp.int32, (c, TAIL_W), 1)
    a_lanes = (lane_t >= TAIL_A) & (lane_t < TAIL_A + DN_H)

    def seq_body(bi, carry):
        conv, rows = _causal_conv(xp_scr, tail_scr, qkv_ref, cw_ref, bi, c)
        qkv = _silu(conv)
        tl = tail_ref[rows, :]
        g = jnp.where(a_lanes, -jnp.exp(alog_ref[...]) * _softplus(tl + dtb_ref[...]), 0.0)
        gam = _cumsum_rows(g)
        gam_t = jnp.transpose(gam)
        beta = _sigmoid(tl)
        heads = range(DN_H)
        s_old = [s_scr[bi, h] for h in heads]
        s16 = [s.astype(BF16) for s in s_old]
        q16, k16, kf, v = [], [], [], []
        for h in heads:
            q = _l2norm(qkv[:, h * DN_DK:(h + 1) * DN_DK]) * (DN_DK ** -0.5)
            k = _l2norm(qkv[:, DN_H * DN_DK + h * DN_DK:DN_H * DN_DK + (h + 1) * DN_DK])
            q16.append(q.astype(BF16))
            k16.append(k.astype(BF16))
            kf.append(k)
            v.append(qkv[:, 2 * DN_H * DN_DK + h * DN_DV:2 * DN_H * DN_DK + (h + 1) * DN_DV])
        g_col = [gam[:, TAIL_A + h:TAIL_A + h + 1] for h in heads]
        g_row = [gam_t[TAIL_A + h:TAIL_A + h + 1, :] for h in heads]
        b_col = [beta[:, TAIL_B + h:TAIL_B + h + 1] for h in heads]
        decay = [jnp.exp(jnp.where(ri >= ci, g_col[h] - g_row[h], NEG_INF)) for h in heads]
        eg = [jnp.exp(g_col[h]) for h in heads]
        kk = [_dot_nt(k16[h], k16[h]) for h in heads]
        k_s = [_dot(k16[h], s16[h]) for h in heads]
        q_s = [_dot(q16[h], s16[h]) for h in heads]
        qk = [_dot_nt(q16[h], k16[h]) for h in heads]
        m = [jnp.where(ri > ci, b_col[h] * kk[h] * decay[h], 0.0) for h in heads]
        rhs = [b_col[h] * (v[h] - eg[h] * k_s[h]) for h in heads]
        u = _unit_lower_solve(m, rhs, c, cv)
        o = [eg[h] * q_s[h] + _mm(qk[h] * decay[h], u[h]) for h in heads]
        kd = []
        for h in heads:
            dec = g_col[h][cv - 1:cv] - g_col[h]
            if cv < c:
                dec = jnp.where(row_c < cv, dec, NEG_INF)
            kd.append(kf[h] * jnp.exp(dec))
        s_new = [jnp.exp(g_col[h][cv - 1:cv]) * s_old[h] + _mm(kd[h], u[h], dot=_dot_tn, contract=0) for h in heads]
        for h in heads:
            ms = jnp.mean(o[h] * o[h], axis=-1, keepdims=True)
            on = (o[h] * lax.rsqrt(ms + EPS)) * go_ref[...]
            mix_scr[rows, h * DN_DV:(h + 1) * DN_DV] = on * _silu(z_ref[rows, h * DN_DV:(h + 1) * DN_DV])
            s_scr[bi, h] = s_new[h]
        return carry

    lax.fori_loop(0, nb, seq_body, 0)
    mix_ref[...] = mix_scr[...].astype(BF16)

    @pl.when(tc == ntc - 1)
    def _():
        so_ref[...] = s_scr[...]


def gdn_mixer(proj, tail, s0, conv0, cw, alog, dtb, go, layer, state_layer, *, batch, seq, c, cv, nb, n_layers,
              stack=None):
    ntc = seq // c
    assert nb == 1 or ntc == 1
    rows = nb * c
    grid = (batch // nb, ntc)

    def tok(width, col):
        return pl.BlockSpec((rows, width), lambda b, t: (b * ntc + t, col // width))

    def lw(shape):
        nd = len(shape)
        return pl.BlockSpec((None,) + shape, lambda b, t: (layer,) + (0,) * nd)

    st_shape = (nb, DN_H, DN_DK, DN_DV)
    assert COL_QKVD % DN_QKV == 0
    stack_arg, stack_spec, aliases = _stack_alias(stack, 9)
    return pl.pallas_call(
        functools.partial(_gdn_kernel, nb=nb, c=c, cv=cv),
        grid=grid,
        in_specs=[tok(DN_QKV, COL_QKVD),
                  tok(W_MIX, COL_ZD),
                  pl.BlockSpec((rows, TAIL_W), lambda b, t: (b * ntc + t, 0)),
                  pl.BlockSpec((None,) + st_shape, lambda b, t: (state_layer, b, 0, 0, 0)),
                  pl.BlockSpec((None, nb, CONV_W - 1, DN_QKV), lambda b, t: (state_layer, b, 0, 0)),
                  lw((CONV_W, DN_QKV)), lw((1, TAIL_W)), lw((1, TAIL_W)), lw((1, DN_DV)), stack_spec],
        out_specs=[pl.BlockSpec((rows, W_MIX), lambda b, t: (b * ntc + t, 0)),
                   pl.BlockSpec((None,) + st_shape, lambda b, t: (layer, b, 0, 0, 0))],
        out_shape=[jax.ShapeDtypeStruct((batch * seq, W_MIX), BF16),
                   jax.ShapeDtypeStruct((n_layers, batch, DN_H, DN_DK, DN_DV), F32)],
        input_output_aliases=aliases,
        scratch_shapes=[pltpu.VMEM(st_shape, F32),
                        pltpu.VMEM((CONV_PAD + c, DN_QKV), F32),
                        pltpu.VMEM((nb, CONV_PAD, DN_QKV), F32),
                        pltpu.VMEM((rows, W_MIX), F32)],
        compiler_params=_cparams(2),
        name="gdn_mixer",
    )(proj, proj, tail, s0, conv0, cw, alog, dtb, go, stack_arg)


def _prepare_weights(w_in, s5_lam_re, s5_lam_im, s5_log_dt, s5_b_re, s5_b_im, s5_c_re, s5_c_im, s5_d,
                     s5_w_glu, lru_w_r, lru_w_i, gla_w_gate, dn_a_log, dn_dt_bias, w_out, ple_w, ple_gate_w):
    nl = w_in.shape[0]
    p = {}
    p["w_main"], p["w_tail"] = repack_w_in(w_in)
    p["w_out"] = w_out.astype(BF16)
    p["ple_w"] = ple_w.astype(BF16)
    p["ple_gate_w"] = ple_gate_w.astype(BF16)
    p["w_glu"] = s5_w_glu.astype(BF16)
    ldt = jnp.repeat(s5_log_dt, S5_STATE, axis=1).reshape(nl, 1, S5_CH)
    coef, ptab, cre, cim = s5_prep(s5_lam_re.reshape(nl, 1, S5_CH), s5_lam_im.reshape(nl, 1, S5_CH), ldt)
    bb_re, bb_im = s5_bb(cre.reshape(nl, S5_CH, 1), cim.reshape(nl, S5_CH, 1),
                         s5_b_re.reshape(nl, S5_CH, S5_GROUP), s5_b_im.reshape(nl, S5_CH, S5_GROUP))
    eye16 = jnp.eye(16, dtype=F32)
    eye8 = jnp.eye(8, dtype=F32)

    def pack_u(bb):
        t = bb.reshape(nl, 4, 16, S5_STATE, S5_GROUP)
        return jnp.einsum("lkgph,gG->lkghGp", t, eye16).reshape(nl, 4, 256, 1024).astype(BF16)

    def pack_y(cc):
        t = cc.reshape(nl, 8, 8, S5_GROUP, S5_STATE)
        return jnp.einsum("ljghp,gG->ljgpGh", t, eye8).reshape(nl, 8, 512, 128).astype(BF16)

    p["s5_coef"] = coef
    p["s5_ptab"] = ptab
    p["s5_wur"], p["s5_wui"] = pack_u(bb_re), pack_u(bb_im)
    p["s5_wyr"], p["s5_wyi"] = pack_y(s5_c_re), pack_y(s5_c_im)
    p["s5_d"] = s5_d.reshape(nl, 1, W_MIX)
    p["lru_wri"] = jnp.concatenate([lru_w_r, lru_w_i], axis=-1).astype(BF16)
    p["gla_wg"] = jnp.concatenate([gla_w_gate, jnp.zeros((nl, TAIL_W - GLA_RANK, GLA_H * GLA_DK), F32)],
                                  axis=1).astype(BF16)

    def tail_row(x):
        return jnp.pad(x, ((0, 0), (TAIL_A, TAIL_W - TAIL_A - DN_H))).reshape(nl, 1, TAIL_W)

    p["dn_alog"] = tail_row(dn_a_log)
    p["dn_dtb"] = tail_row(dn_dt_bias)
    return p


def kernel(x_prompt, x_sample, state_s5_re, state_s5_im, state_lru_h, state_lru_conv, state_gla, state_delta, state_delta_conv, p_prompt, p_sample, g_norm, w_in, s5_lam_re, s5_lam_im, s5_log_dt, s5_b_re, s5_b_im, s5_c_re, s5_c_im, s5_d, s5_w_glu, s5_b_glu, lru_conv_w, lru_conv_b, lru_w_r, lru_b_r, lru_w_i, lru_b_i, lru_lam, gla_w_gate, gla_b_gate, gla_g_out, dn_conv_w, dn_a_log, dn_dt_bias, dn_g_out, w_out, ple_w, ple_gate_w, ple_gate_b, g_final):
    nl = w_in.shape[0]
    bp, tp, _ = x_prompt.shape
    bs, ts, _ = x_sample.shape
    tsp = SAMPLE_T_PAD
    mp, ms = bp * tp, bs * ts

    p = _prepare_weights(w_in, s5_lam_re, s5_lam_im, s5_log_dt, s5_b_re, s5_b_im, s5_c_re, s5_c_im, s5_d,
                         s5_w_glu, lru_w_r, lru_w_i, gla_w_gate, dn_a_log, dn_dt_bias, w_out, ple_w, ple_gate_w)

    def vec(x):
        return x.reshape(nl, 1, x.shape[-1])

    g_norm3, b_glu3, cb3 = vec(g_norm), vec(s5_b_glu), vec(lru_conv_b)
    br3, bi3, lam3 = vec(lru_b_r), vec(lru_b_i), vec(lru_lam)
    bg3, go_c3, go_d3, pgb3 = vec(gla_b_gate), vec(gla_g_out), vec(dn_g_out), vec(ple_gate_b)
    pe_p = p_prompt.reshape(nl, mp, D_PLE)
    pe_s = p_sample.reshape(nl, ms, D_PLE)

    z_s5 = jnp.zeros((1, bp, 1, S5_CH), F32)
    z_lru = jnp.zeros((1, bp, 1, W_MIX), F32)
    z_lconv = jnp.zeros((1, bp, CONV_W - 1, W_MIX), F32)
    z_gla = jnp.zeros((1, bp, GLA_H, GLA_DK, GLA_DV), F32)
    z_dn = jnp.zeros((1, bp, DN_H, DN_DK, DN_DV), F32)
    z_dconv = jnp.zeros((1, bp, CONV_W - 1, DN_QKV), F32)
    c_s5r = state_s5_re.reshape(nl, bs, 1, S5_CH)
    c_s5i = state_s5_im.reshape(nl, bs, 1, S5_CH)
    c_lru = state_lru_h.reshape(nl, bs, 1, W_MIX)

    def mixers(proj, tail, layer, *, batch, seq, t_valid, st, sl, nb_scan, tt, c, nb_mat, stacks):
        s5r0, s5i0, lru0, lconv0, gla0, dn0, dconv0 = st
        gla_stack, dn_stack = stacks
        mix_a, s5r, s5i = s5_mixer(proj, s5r0, s5i0, p["s5_coef"], p["s5_ptab"], p["s5_wur"], p["s5_wui"], p["s5_wyr"],
                                   p["s5_wyi"], p["s5_d"], p["w_glu"], b_glu3, layer, sl,
                                   batch=batch, seq=seq, t_valid=t_valid, nb=nb_scan, tt=tt)
        mix_b, lruh = lru_mixer(proj, lru0, lconv0, lru_conv_w, cb3, p["lru_wri"], br3, bi3, lam3, layer, sl,
                                batch=batch, seq=seq, t_valid=t_valid, nb=nb_scan, tt=tt)
        mix_c, glas = gla_mixer(proj, tail, gla0, p["gla_wg"], bg3, go_c3, layer, sl,
                                batch=batch, seq=seq, c=c, cv=min(c, t_valid), nb=nb_mat, n_layers=nl,
                                stack=gla_stack)
        mix_d, dns = gdn_mixer(proj, tail, dn0, dconv0, dn_conv_w, p["dn_alog"], p["dn_dtb"], go_d3, layer, sl,
                               batch=batch, seq=seq, c=c, cv=min(c, t_valid), nb=nb_mat, n_layers=nl,
                               stack=dn_stack)
        return (mix_a, mix_b, mix_c, mix_d), (s5r, s5i, lruh, glas, dns)

    hp = x_prompt.reshape(mp, D_MODEL)
    hs = x_sample.reshape(ms, D_MODEL)
    new_p, new_s = [], []
    stacks_p = stacks_s = (None, None)
    for i in range(nl):
        proj, tail = in_proj(hp, g_norm3, p["w_main"], p["w_tail"], i, tm=DENSE_TM, tn=DENSE_TN)
        mixes, (s5r, s5i, lruh, glas, dns) = mixers(
            proj, tail, i, batch=bp, seq=tp, t_valid=tp,
            st=(z_s5, z_s5, z_lru, z_lconv, z_gla, z_dn, z_dconv), sl=0,
            nb_scan=1, tt=256, c=GLA_CHUNK, nb_mat=1, stacks=stacks_p)
        stacks_p = (glas, dns)
        proj3 = proj.reshape(bp, tp, N_MAIN)
        new_p.append((s5r.reshape(bp, S5_GROUPS, S5_STATE), s5i.reshape(bp, S5_GROUPS, S5_STATE),
                      lruh.reshape(bp, W_MIX), proj3[:, tp - (CONV_W - 1):, COL_XB:COL_XB + W_MIX],
                      None, None, proj3[:, tp - (CONV_W - 1):, COL_QKVD:COL_QKVD + DN_QKV]))
        hp = out_proj(mixes, p["w_out"], hp, i, tm=DENSE_TM, tn=DENSE_TN)
        hp = ple(hp, pe_p, p["ple_gate_w"], pgb3, p["ple_w"], i, tm=DENSE_TM, tn=DENSE_TN)

        proj, tail = in_proj(hs, g_norm3, p["w_main"], p["w_tail"], i, tm=DENSE_TM, tn=DENSE_TN)
        proj3 = proj.reshape(bs, ts, N_MAIN)
        proj_pad = jnp.pad(proj3, ((0, 0), (0, tsp - ts), (0, 0))).reshape(bs * tsp, N_MAIN)
        tail_pad = jnp.pad(tail.reshape(bs, ts, TAIL_W), ((0, 0), (0, tsp - ts), (0, 0))).reshape(bs * tsp, TAIL_W)
        mixes, (s5r, s5i, lruh, glas, dns) = mixers(
            proj_pad, tail_pad, i, batch=bs, seq=tsp, t_valid=ts,
            st=(c_s5r, c_s5i, c_lru, state_lru_conv, state_gla, state_delta, state_delta_conv), sl=i,
            nb_scan=32, tt=tsp, c=tsp, nb_mat=8, stacks=stacks_s)
        stacks_s = (glas, dns)
        mixes = tuple(m.reshape(bs, tsp, W_MIX)[:, :ts].reshape(ms, W_MIX) for m in mixes)
        new_s.append((s5r.reshape(bs, S5_GROUPS, S5_STATE), s5i.reshape(bs, S5_GROUPS, S5_STATE),
                      lruh.reshape(bs, W_MIX), proj3[:, ts - (CONV_W - 1):, COL_XB:COL_XB + W_MIX],
                      None, None, proj3[:, ts - (CONV_W - 1):, COL_QKVD:COL_QKVD + DN_QKV]))
        hs = out_proj(mixes, p["w_out"], hs, i, tm=DENSE_TM, tn=DENSE_TN)
        hs = ple(hs, pe_s, p["ple_gate_w"], pgb3, p["ple_w"], i, tm=DENSE_TM, tn=DENSE_TN)

    g_fin = g_final.reshape(1, D_MODEL)
    y_prompt = final_norm(hp, g_fin, tm=256).reshape(bp, tp, D_MODEL)
    y_sample = final_norm(hs, g_fin, tm=256).reshape(bs, ts, D_MODEL)

    def stk(lst, j):
        return jnp.stack([s[j] for s in lst], axis=0)

    return (y_prompt, y_sample,
            stk(new_p, 0), stk(new_p, 1), stk(new_p, 2), stk(new_p, 3), stacks_p[0], stacks_p[1], stk(new_p, 6),
            stk(new_s, 0), stk(new_s, 1), stk(new_s, 2), stk(new_s, 3), stacks_s[0], stacks_s[1], stk(new_s, 6))
```

```python
import functools
import math

import jax
import jax.numpy as jnp
import numpy as np
from jax import lax
from jax.experimental import pallas as pl
from jax.experimental.pallas import tpu as pltpu

F32 = jnp.float32
BF16 = jnp.bfloat16
EPS = 1e-6
NEG_INF = float("-inf")

D_MODEL = 4096
DEPTH = 4
W_MIX = 1024
S5_GROUPS = 64
S5_GROUP = 16
S5_STATE = 64
S5_CH = S5_GROUPS * S5_STATE
LRU_BLOCKS = 8
LRU_BLK = 128
LRU_C = 8.0
CONV_W = 4
GLA_H = 4
GLA_DK = 128
GLA_DV = 256
GLA_RANK = 16
GLA_TAU = 16.0
GLA_CHUNK = 64
DN_H = 8
DN_DK = 128
DN_DV = 128
DN_QKV = 3072
DN_CHUNK = 64
D_PLE = 256
N_MAIN = 11264
COL_XA, COL_ZA, COL_XB, COL_ZB = 0, 1024, 2048, 3072
COL_QC, COL_KC, COL_VC = 4096, 4608, 5120
COL_QKVD, COL_ZC, COL_ZD = 6144, 9216, 10240
ORIG_ZC, ORIG_GC, ORIG_QKVD, ORIG_ZD, ORIG_AB = 6144, 7168, 7184, 10256, 11280
TAIL_W = 128
TAIL_G, TAIL_A, TAIL_B = 0, 16, 24
SAMPLE_T_PAD = 8

V7X_LANES = 128
V7X_SUBLANES = 8
VMEM_LIMIT = 52 * 1024 * 1024
DENSE_TILES_PROMPT = dict(tm=1024, tn=512)
DENSE_TILES_SAMPLE = dict(tm=512, tn=1024)


def _cparams(n_axes):
    return pltpu.CompilerParams(dimension_semantics=("arbitrary",) * n_axes,
                                vmem_limit_bytes=VMEM_LIMIT)


def _sigmoid(x):
    return jax.nn.sigmoid(x)


def _silu(x):
    return x * jax.nn.sigmoid(x)


def _softplus(x):
    return jnp.maximum(x, 0.0) + jnp.log1p(jnp.exp(-jnp.abs(x)))


def _expm1(x):
    u = jnp.exp(x)
    um1 = u - 1.0
    lg = jnp.log(u)
    r = um1 * x / jnp.where(lg == 0.0, 1.0, lg)
    return jnp.where(um1 == 0.0, x, jnp.where(um1 == -1.0, -1.0, r))


def _log_sigmoid(x):
    return jnp.minimum(x, 0.0) - jnp.log1p(jnp.exp(-jnp.abs(x)))


def _gelu_tanh(x):
    c = math.sqrt(2.0 / math.pi)
    return x * (0.5 * (1.0 + jnp.tanh(c * (x + 0.044715 * (x * x * x)))))


def _dot(a, b):
    return jnp.dot(a, b, preferred_element_type=F32)


def _dot_nt(a, b):
    return lax.dot_general(a, b, (((1,), (1,)), ((), ())), preferred_element_type=F32)


def _dot_tn(a, b):
    return lax.dot_general(a, b, (((0,), (0,)), ((), ())), preferred_element_type=F32)


def _split3(x):
    h = x.astype(BF16)
    r = x - h.astype(F32)
    m = r.astype(BF16)
    l = (r - m.astype(F32)).astype(BF16)
    return h, m, l


def _cumsum_rows(x):
    c = x.shape[0]
    if c <= V7X_SUBLANES:
        row = lax.broadcasted_iota(jnp.int32, x.shape, 0)
        k = 1
        while k < c:
            x = x + jnp.where(row >= k, pltpu.roll(x, k, 0), 0.0)
            k *= 2
        return x
    tri = (lax.broadcasted_iota(jnp.int32, (c, c), 0) >= lax.broadcasted_iota(jnp.int32, (c, c), 1)).astype(BF16)
    h, m, l = _split3(x)
    return _dot(tri, h) + _dot(tri, m) + _dot(tri, l)


def _mm(a, b, dot=_dot, contract=1):
    if a.shape[contract] < 2 * V7X_SUBLANES:
        return dot(a, b)
    return dot(a.astype(BF16), b.astype(BF16))


def _dot_x3(a, b):
    ah = a.astype(BF16)
    al = (a - ah.astype(F32)).astype(BF16)
    bh = b.astype(BF16)
    bl = (b - bh.astype(F32)).astype(BF16)
    return _dot(ah, bh) + _dot(ah, bl) + _dot(al, bh)


def _prenorm_kernel(x_ref, g_ref, hg_ref, ssq_ref):
    x = x_ref[...]
    hg_ref[...] = (x * g_ref[...]).astype(BF16)
    ssq_ref[...] = jnp.sum(x * x, axis=-1, keepdims=True)


def prenorm(x, g_norm, layer, *, tm):
    m = x.shape[0]
    return pl.pallas_call(
        _prenorm_kernel,
        grid=(m // tm,),
        in_specs=[pl.BlockSpec((tm, D_MODEL), lambda i: (i, 0)),
                  pl.BlockSpec((None, 1, D_MODEL), lambda i: (layer, 0, 0))],
        out_specs=[pl.BlockSpec((tm, D_MODEL), lambda i: (i, 0)), pl.BlockSpec((tm, 1), lambda i: (i, 0))],
        out_shape=[jax.ShapeDtypeStruct((m, D_MODEL), BF16), jax.ShapeDtypeStruct((m, 1), F32)],
        compiler_params=_cparams(1),
        name="prenorm",
    )(x, g_norm)


def _in_proj_kernel(hg_ref, ssq_ref, w_ref, wt_ref, o_ref, ot_ref):
    r = lax.rsqrt(ssq_ref[...] * (1.0 / D_MODEL) + EPS)

    @pl.when(pl.program_id(1) == 0)
    def _():
        ot_ref[...] = _dot_nt(hg_ref[...], wt_ref[...]) * r

    o_ref[...] = _dot_nt(hg_ref[...], w_ref[...]) * r


def in_proj(hg, ssq, w_main, w_tail, layer, *, tm, tn):
    m = hg.shape[0]
    grid = (m // tm, N_MAIN // tn)
    return pl.pallas_call(
        _in_proj_kernel,
        grid=grid,
        in_specs=[
            pl.BlockSpec((tm, D_MODEL), lambda i, j: (i, 0)),
            pl.BlockSpec((tm, 1), lambda i, j: (i, 0)),
            pl.BlockSpec((None, tn, D_MODEL), lambda i, j: (layer, j, 0)),
            pl.BlockSpec((None, TAIL_W, D_MODEL), lambda i, j: (layer, 0, 0)),
        ],
        out_specs=[
            pl.BlockSpec((tm, tn), lambda i, j: (i, j)),
            pl.BlockSpec((tm, TAIL_W), lambda i, j: (i, 0)),
        ],
        out_shape=[jax.ShapeDtypeStruct((m, N_MAIN), F32), jax.ShapeDtypeStruct((m, TAIL_W), F32)],
        compiler_params=_cparams(2),
        name="in_proj",
    )(hg, ssq, w_main, w_tail)


def _out_proj_kernel(ma_ref, mb_ref, mc_ref, md_ref, w_ref, h_ref, o_ref, ob_ref):
    acc = h_ref[...]
    for k, m_ref in enumerate((ma_ref, mb_ref, mc_ref, md_ref)):
        acc = acc + _dot(m_ref[...], w_ref[k * W_MIX:(k + 1) * W_MIX, :])
    o_ref[...] = acc
    ob_ref[...] = acc.astype(BF16)


def out_proj(mixes, w_out, h, layer, *, tm, tn):
    m = h.shape[0]
    grid = (m // tm, D_MODEL // tn)
    mix_spec = pl.BlockSpec((tm, W_MIX), lambda i, j: (i, 0))
    tile = pl.BlockSpec((tm, tn), lambda i, j: (i, j))
    return pl.pallas_call(
        _out_proj_kernel,
        grid=grid,
        in_specs=[mix_spec, mix_spec, mix_spec, mix_spec,
                  pl.BlockSpec((None, 4 * W_MIX, tn), lambda i, j: (layer, 0, j)), tile],
        out_specs=[tile, tile],
        out_shape=[jax.ShapeDtypeStruct((m, D_MODEL), F32), jax.ShapeDtypeStruct((m, D_MODEL), BF16)],
        compiler_params=_cparams(2),
        name="out_proj",
    )(*mixes, w_out, h)


def _ple_kernel(hrow_ref, htile_ref, pe_ref, wg_ref, bg_ref, wp_ref, gn_ref, o_ref, hg_ref, ssq_ref):
    gate = _sigmoid(_dot(hrow_ref[...], wg_ref[...]) + bg_ref[...])
    pv = _dot(pe_ref[...].astype(BF16), wp_ref[...])
    h = htile_ref[...] + gate * pv
    o_ref[...] = h
    hg_ref[...] = (h * gn_ref[...]).astype(BF16)
    part = jnp.sum(h * h, axis=-1, keepdims=True)

    @pl.when(pl.program_id(1) == 0)
    def _():
        ssq_ref[...] = part

    @pl.when(pl.program_id(1) != 0)
    def _():
        ssq_ref[...] = ssq_ref[...] + part


def ple(h, hb, pe, w_gate, b_gate, w_ple, g_next, layer, next_layer, *, tm, tn):
    m = h.shape[0]
    grid = (m // tm, D_MODEL // tn)
    tile = pl.BlockSpec((tm, tn), lambda i, j: (i, j))
    return pl.pallas_call(
        _ple_kernel,
        grid=grid,
        in_specs=[
            pl.BlockSpec((tm, D_MODEL), lambda i, j: (i, 0)),
            tile,
            pl.BlockSpec((None, tm, D_PLE), lambda i, j: (layer, i, 0)),
            pl.BlockSpec((None, D_MODEL, tn), lambda i, j: (layer, 0, j)),
            pl.BlockSpec((None, 1, tn), lambda i, j: (layer, 0, j)),
            pl.BlockSpec((None, D_PLE, tn), lambda i, j: (layer, 0, j)),
            pl.BlockSpec((None, 1, tn), lambda i, j: (next_layer, 0, j)),
        ],
        out_specs=[tile, tile, pl.BlockSpec((tm, 1), lambda i, j: (i, 0))],
        out_shape=[jax.ShapeDtypeStruct((m, D_MODEL), F32), jax.ShapeDtypeStruct((m, D_MODEL), BF16),
                   jax.ShapeDtypeStruct((m, 1), F32)],
        compiler_params=_cparams(2),
        name="ple",
    )(hb, h, pe, w_gate, b_gate, w_ple, g_next)


REPACK_ROWS = 512


def _repack_kernel(a_ref, g_ref, ab_ref, o_ref, t_ref):
    o_ref[...] = a_ref[...].astype(BF16)

    @pl.when(pl.program_id(1) == 0)
    def _():
        t_ref[...] = jnp.zeros(t_ref.shape, BF16)
        t_ref[TAIL_G:TAIL_G + GLA_RANK, :] = g_ref[...].astype(BF16)
        t_ref[TAIL_A:TAIL_A + 2 * DN_H, :] = ab_ref[...].astype(BF16)


def repack_w_in(w_in_t):
    nl = w_in_t.shape[0]
    n_tiles = N_MAIN // REPACK_ROWS
    t_qkvd, t_zc, t_zd = COL_QKVD // REPACK_ROWS, COL_ZC // REPACK_ROWS, COL_ZD // REPACK_ROWS

    def src_row(j):
        return jnp.where(j < t_qkvd, j * REPACK_ROWS,
                         jnp.where(j < t_zc, ORIG_QKVD + (j - t_qkvd) * REPACK_ROWS,
                                   jnp.where(j < t_zd, ORIG_ZC + (j - t_zc) * REPACK_ROWS,
                                             ORIG_ZD + (j - t_zd) * REPACK_ROWS)))

    row_align = 2 * V7X_SUBLANES
    assert all(o % row_align == 0 for o in (ORIG_QKVD, ORIG_ZC, ORIG_ZD, ORIG_GC, ORIG_AB))

    def rows(n, start):
        return pl.BlockSpec((None, pl.Element(n), pl.Element(D_MODEL)),
                            lambda l, j: (l, pl.multiple_of(start(j), row_align), 0))

    return pl.pallas_call(
        _repack_kernel,
        grid=(nl, n_tiles),
        in_specs=[rows(REPACK_ROWS, src_row), rows(GLA_RANK, lambda j: ORIG_GC), rows(2 * DN_H, lambda j: ORIG_AB)],
        out_specs=[pl.BlockSpec((None, REPACK_ROWS, D_MODEL), lambda l, j: (l, j, 0)),
                   pl.BlockSpec((None, TAIL_W, D_MODEL), lambda l, j: (l, 0, 0))],
        out_shape=[jax.ShapeDtypeStruct((nl, N_MAIN, D_MODEL), BF16),
                   jax.ShapeDtypeStruct((nl, TAIL_W, D_MODEL), BF16)],
        compiler_params=_cparams(2),
        name="repack_w_in",
    )(w_in_t, w_in_t, w_in_t)


def _final_norm_kernel(x_ref, g_ref, o_ref):
    x = x_ref[...]
    ms = jnp.mean(x * x, axis=-1, keepdims=True)
    o_ref[...] = (x * lax.rsqrt(ms + EPS)) * g_ref[...]


def final_norm(h, g, *, tm):
    m = h.shape[0]
    return pl.pallas_call(
        _final_norm_kernel,
        grid=(m // tm,),
        in_specs=[pl.BlockSpec((tm, D_MODEL), lambda i: (i, 0)),
                  pl.BlockSpec((1, D_MODEL), lambda i: (0, 0))],
        out_specs=pl.BlockSpec((tm, D_MODEL), lambda i: (i, 0)),
        out_shape=jax.ShapeDtypeStruct((m, D_MODEL), F32),
        compiler_params=_cparams(1),
        name="final_norm",
    )(h, g)


S5_SEG = 32


def _s5_prep_kernel(lre_ref, lim_ref, ldt_ref, coef_ref, ptab_ref, cre_ref, cim_ref):
    lre = lre_ref[...]
    lim = lim_ref[...]
    dt = jnp.exp(ldt_ref[...])
    ai = lim * dt
    mag = jnp.exp(lre * dt)
    ar = mag * jnp.cos(ai)
    aim = mag * jnp.sin(ai)
    den = lre * lre + lim * lim
    nr = ar - 1.0
    cre_ref[...] = (nr * lre + aim * lim) / den
    cim_ref[...] = (aim * lre - nr * lim) / den
    pr, pim = [ar], [aim]
    for _ in range(S5_SEG - 1):
        nr_, ni_ = pr[-1] * ar - pim[-1] * aim, pr[-1] * aim + pim[-1] * ar
        pr.append(nr_)
        pim.append(ni_)
    shape = (V7X_SUBLANES, S5_CH)
    coef_ref[0] = jnp.broadcast_to(ar, shape)
    coef_ref[1] = jnp.broadcast_to(aim, shape)
    coef_ref[2] = jnp.broadcast_to(pr[S5_SEG - 1], shape)
    coef_ref[3] = jnp.broadcast_to(pim[S5_SEG - 1], shape)
    tshape = (S5_SEG, S5_CH)
    row = lax.broadcasted_iota(jnp.int32, tshape, 0)
    accr = jnp.zeros(tshape, F32)
    acci = jnp.zeros(tshape, F32)
    for j in range(S5_SEG):
        accr = jnp.where(row == j, jnp.broadcast_to(pr[j], tshape), accr)
        acci = jnp.where(row == j, jnp.broadcast_to(pim[j], tshape), acci)
    ptab_ref[0] = accr
    ptab_ref[1] = acci


def s5_prep(lam_re, lam_im, log_dt_rep):
    nl = lam_re.shape[0]
    vec = pl.BlockSpec((None, 1, S5_CH), lambda l: (l, 0, 0))
    return pl.pallas_call(
        _s5_prep_kernel,
        grid=(nl,),
        in_specs=[vec, vec, vec],
        out_specs=[pl.BlockSpec((None, 4, V7X_SUBLANES, S5_CH), lambda l: (l, 0, 0, 0)),
                   pl.BlockSpec((None, 2, S5_SEG, S5_CH), lambda l: (l, 0, 0, 0)), vec, vec],
        out_shape=[jax.ShapeDtypeStruct((nl, 4, V7X_SUBLANES, S5_CH), F32),
                   jax.ShapeDtypeStruct((nl, 2, S5_SEG, S5_CH), F32),
                   jax.ShapeDtypeStruct((nl, 1, S5_CH), F32),
                   jax.ShapeDtypeStruct((nl, 1, S5_CH), F32)],
        compiler_params=_cparams(1),
        name="s5_prep",
    )(lam_re, lam_im, log_dt_rep)


def _s5_bb_kernel(cr_ref, ci_ref, br_ref, bi_ref, or_ref, oi_ref):
    cr = cr_ref[...]
    ci = ci_ref[...]
    br = br_ref[...]
    bi = bi_ref[...]
    or_ref[...] = cr * br - ci * bi
    oi_ref[...] = cr * bi + ci * br


def s5_bb(coef_re_col, coef_im_col, b_re, b_im):
    nl = b_re.shape[0]
    rows = 1024
    col = pl.BlockSpec((None, rows, 1), lambda l, r: (l, r, 0))
    mat = pl.BlockSpec((None, rows, S5_GROUP), lambda l, r: (l, r, 0))
    return pl.pallas_call(
        _s5_bb_kernel,
        grid=(nl, S5_CH // rows),
        in_specs=[col, col, mat, mat],
        out_specs=[mat, mat],
        out_shape=[jax.ShapeDtypeStruct((nl, S5_CH, S5_GROUP), F32)] * 2,
        compiler_params=_cparams(2),
        name="s5_bb",
    )(coef_re_col, coef_im_col, b_re, b_im)


S5_STRIP = 512


def _s5_kernel(xa_ref, za_ref, h0r_ref, h0i_ref, coef_ref, ptab_ref, perm_ref, permt_ref,
               wur_ref, wui_ref, wyr_ref, wyi_ref, dsk_ref, wglu_ref, bglu_ref,
               mix_ref, hro_ref, hio_ref,
               ur_scr, ui_scr, cr_scr, ci_scr, y_scr, *, nb, tt, t_last):
    tc = pl.program_id(1)
    ntc = pl.num_programs(1)
    split = nb == 1
    lc = tt // V7X_SUBLANES if split else tt
    n_sets = 1 if split else nb // V7X_SUBLANES
    set_rows = V7X_SUBLANES * lc

    xb16 = _dot(perm_ref[...], xa_ref[...].astype(BF16)).astype(BF16)
    for k in range(4):
        xk = xb16[:, 256 * k:256 * (k + 1)]
        ur_scr[:, 1024 * k:1024 * (k + 1)] = _dot(xk, wur_ref[k])
        ui_scr[:, 1024 * k:1024 * (k + 1)] = _dot(xk, wui_ref[k])

    @pl.when(tc == 0)
    def _():
        cr_scr[...] = h0r_ref[...]
        ci_scr[...] = h0i_ref[...]

    row8 = lax.broadcasted_iota(jnp.int32, (V7X_SUBLANES, S5_STRIP), 0)
    for st in range(S5_CH // S5_STRIP):
        ls = slice(st * S5_STRIP, (st + 1) * S5_STRIP)
        ar = coef_ref[0, :, ls]
        ai = coef_ref[1, :, ls]
        for s in range(n_sets):
            base = s * set_rows
            if split:
                h0r = jnp.zeros((V7X_SUBLANES, S5_STRIP), F32)
                h0i = jnp.zeros((V7X_SUBLANES, S5_STRIP), F32)
            else:
                h0r = jnp.zeros((V7X_SUBLANES, S5_STRIP), F32)
                h0i = jnp.zeros((V7X_SUBLANES, S5_STRIP), F32)
                for k in range(V7X_SUBLANES):
                    h0r = jnp.where(row8 == k, cr_scr[s * V7X_SUBLANES + k, :, ls], h0r)
                    h0i = jnp.where(row8 == k, ci_scr[s * V7X_SUBLANES + k, :, ls], h0i)

            def step(j, h, base=base, ls=ls, ar=ar, ai=ai):
                hr, hi = h
                rows = pl.ds(pl.multiple_of(base + j * V7X_SUBLANES, V7X_SUBLANES), V7X_SUBLANES)
                nr = (ar * hr - ai * hi) + ur_scr[rows, ls]
                ni = (ar * hi + ai * hr) + ui_scr[rows, ls]
                ur_scr[rows, ls] = nr
                ui_scr[rows, ls] = ni
                return nr, ni

            fr, fi = lax.fori_loop(0, lc, step, (h0r, h0i), unroll=4)
            if split:
                alr = coef_ref[2, 0:1, ls]
                ali = coef_ref[3, 0:1, ls]
                cr = cr_scr[0, :, ls]
                ci = ci_scr[0, :, ls]
                init_r = jnp.zeros((V7X_SUBLANES, S5_STRIP), F32)
                init_i = jnp.zeros((V7X_SUBLANES, S5_STRIP), F32)
                for k in range(V7X_SUBLANES):
                    init_r = jnp.where(row8 == k, cr, init_r)
                    init_i = jnp.where(row8 == k, ci, init_i)
                    cr, ci = (alr * cr - ali * ci) + fr[k:k + 1], (alr * ci + ali * cr) + fi[k:k + 1]
                cr_scr[0, :, ls] = cr
                ci_scr[0, :, ls] = ci

                def fix(j, carry, base=base, ls=ls, init_r=init_r, init_i=init_i):
                    rows = pl.ds(pl.multiple_of(base + j * V7X_SUBLANES, V7X_SUBLANES), V7X_SUBLANES)
                    pr = ptab_ref[0, pl.ds(j, 1), ls]
                    pim = ptab_ref[1, pl.ds(j, 1), ls]
                    ur_scr[rows, ls] = ur_scr[rows, ls] + (pr * init_r - pim * init_i)
                    ui_scr[rows, ls] = ui_scr[rows, ls] + (pr * init_i + pim * init_r)
                    return carry

                lax.fori_loop(0, lc, fix, 0, unroll=4)

    for j in range(8):
        hr16 = ur_scr[:, 512 * j:512 * (j + 1)].astype(BF16)
        hi16 = ui_scr[:, 512 * j:512 * (j + 1)].astype(BF16)
        y_scr[:, 128 * j:128 * (j + 1)] = _dot(hr16, wyr_ref[j]) - _dot(hi16, wyi_ref[j])
    yh, ym, yl = _split3(y_scr[...])
    pt = permt_ref[...]
    y = (_dot(pt, yh) + _dot(pt, ym) + _dot(pt, yl)) + dsk_ref[...] * xa_ref[...]
    ga = _gelu_tanh(y)
    ya = ga * _sigmoid(_dot(ga.astype(BF16), wglu_ref[...]) + bglu_ref[...])
    mix_ref[...] = (ya * _silu(za_ref[...])).astype(BF16)

    @pl.when(tc == ntc - 1)
    def _():
        if split:
            hro_ref[...] = cr_scr[...]
            hio_ref[...] = ci_scr[...]
        else:
            for s in range(n_sets):
                for k in range(V7X_SUBLANES):
                    r = s * set_rows + V7X_SUBLANES * t_last + k
                    hro_ref[s * V7X_SUBLANES + k] = ur_scr[r:r + 1, :]
                    hio_ref[s * V7X_SUBLANES + k] = ui_scr[r:r + 1, :]


def s5_mixer(proj, h0_re, h0_im, coef, ptab, wur, wui, wyr, wyi, dskip, wglu, bglu, layer, state_layer,
             *, batch, seq, t_valid, nb, tt):
    ntc = seq // tt
    assert nb == 1 or ntc == 1
    rows = nb * tt
    grid = (batch // nb, ntc)
    t_last = (t_valid - 1) - (ntc - 1) * tt
    assert 0 <= t_last < tt
    if nb == 1:
        assert tt == V7X_SUBLANES * S5_SEG and t_last == tt - 1
        lc = S5_SEG
    else:
        assert nb % V7X_SUBLANES == 0
        lc = tt
    idx = np.arange(rows)
    set_rows = V7X_SUBLANES * lc
    src = (idx // set_rows) * set_rows + (idx % V7X_SUBLANES) * lc + (idx % set_rows) // V7X_SUBLANES
    perm_np = np.zeros((rows, rows), np.float32)
    perm_np[idx, src] = 1.0
    perm = jnp.asarray(perm_np, BF16)
    perm_t = jnp.asarray(perm_np.T, BF16)
    sq = pl.BlockSpec((rows, rows), lambda b, t: (0, 0))

    def tok(cb):
        return pl.BlockSpec((rows, W_MIX), lambda b, t: (b * ntc + t, cb))

    st_in = pl.BlockSpec((None, nb, 1, S5_CH), lambda b, t: (state_layer, b, 0, 0))
    st_out = pl.BlockSpec((nb, 1, S5_CH), lambda b, t: (b, 0, 0))

    def lw(shape):
        nd = len(shape)
        return pl.BlockSpec((None,) + shape, lambda b, t: (layer,) + (0,) * nd)

    return pl.pallas_call(
        functools.partial(_s5_kernel, nb=nb, tt=tt, t_last=t_last),
        grid=grid,
        in_specs=[tok(COL_XA // W_MIX), tok(COL_ZA // W_MIX), st_in, st_in,
                  lw((4, V7X_SUBLANES, S5_CH)), lw((2, S5_SEG, S5_CH)), sq, sq, lw((4, 256, 1024)), lw((4, 256, 1024)),
                  lw((8, 512, 128)), lw((8, 512, 128)), lw((1, W_MIX)), lw((W_MIX, W_MIX)), lw((1, W_MIX))],
        out_specs=[pl.BlockSpec((rows, W_MIX), lambda b, t: (b * ntc + t, 0)), st_out, st_out],
        out_shape=[jax.ShapeDtypeStruct((batch * seq, W_MIX), BF16),
                   jax.ShapeDtypeStruct((batch, 1, S5_CH), F32),
                   jax.ShapeDtypeStruct((batch, 1, S5_CH), F32)],
        scratch_shapes=[pltpu.VMEM((rows, S5_CH), F32), pltpu.VMEM((rows, S5_CH), F32),
                        pltpu.VMEM((nb, 1, S5_CH), F32), pltpu.VMEM((nb, 1, S5_CH), F32),
                        pltpu.VMEM((rows, W_MIX), F32)],
        compiler_params=_cparams(2),
        name="s5_mixer",
    )(proj, proj, h0_re, h0_im, coef, ptab, perm, perm_t, wur, wui, wyr, wyi, dskip, wglu, bglu)


LRU_STRIP = 512
CONV_PAD = V7X_SUBLANES


def _causal_conv(xp_scr, tail_scr, x_ref, cw_ref, bi, rows_n):
    rows = pl.ds(pl.multiple_of(bi * rows_n, V7X_SUBLANES), rows_n)
    xp_scr[0:CONV_PAD, :] = tail_scr[bi]
    xp_scr[CONV_PAD:CONV_PAD + rows_n, :] = x_ref[rows, :]
    acc = cw_ref[CONV_W - 1:CONV_W, :] * xp_scr[CONV_PAD:CONV_PAD + rows_n, :]
    for jj in range(CONV_W - 1):
        off = CONV_PAD - (CONV_W - 1) + jj
        acc = acc + cw_ref[jj:jj + 1, :] * xp_scr[off:off + rows_n, :]
    tail_scr[bi] = xp_scr[rows_n:rows_n + CONV_PAD, :]
    return acc, rows


def _init_conv_tail(tail_scr, cv0_ref, nb):
    tail_scr[...] = jnp.zeros(tail_scr.shape, F32)
    for bi in range(nb):
        tail_scr[bi, CONV_PAD - (CONV_W - 1):CONV_PAD, :] = cv0_ref[bi]


def _lru_kernel(xb_ref, zb_ref, h0_ref, cv0_ref, cw_ref, cb_ref, wri_ref, br_ref, bi_ref, lam_ref,
                mix_ref, ho_ref,
                xp_scr, tail_scr, xc_scr, a_scr, b_scr, c_scr, *, nb, tt, t_last):
    tc = pl.program_id(1)
    ntc = pl.num_programs(1)

    @pl.when(tc == 0)
    def _():
        c_scr[...] = h0_ref[...]
        _init_conv_tail(tail_scr, cv0_ref, nb)

    def conv_body(bi, carry):
        acc, rows = _causal_conv(xp_scr, tail_scr, xb_ref, cw_ref, bi, tt)
        xc_scr[rows, :] = acc + cb_ref[...]
        return carry

    lax.fori_loop(0, nb, conv_body, 0)

    xc = xc_scr[...]
    xc16 = xc.astype(BF16)
    sp = _softplus(-lam_ref[...])
    for blk in range(LRU_BLOCKS):
        cs = slice(blk * LRU_BLK, (blk + 1) * LRU_BLK)
        pre = _dot(xc16[:, cs], wri_ref[blk])
        r = _sigmoid(pre[:, :LRU_BLK] + br_ref[:, cs])
        ig = _sigmoid(pre[:, LRU_BLK:] + bi_ref[:, cs])
        log_a = (-LRU_C) * r * sp[:, cs]
        a_scr[:, cs] = jnp.exp(log_a)
        b_scr[:, cs] = jnp.sqrt(-_expm1(2.0 * log_a)) * (ig * xc[:, cs])

    ng = tt // V7X_SUBLANES
    for s in range(W_MIX // LRU_STRIP):
        ls = slice(s * LRU_STRIP, (s + 1) * LRU_STRIP)
        row = lax.broadcasted_iota(jnp.int32, (V7X_SUBLANES, LRU_STRIP), 0)

        def seq_body(bi, carry, ls=ls, row=row):
            def tile_body(g, c):
                rows = pl.ds(pl.multiple_of(bi * tt + g * V7X_SUBLANES, V7X_SUBLANES), V7X_SUBLANES)
                a = a_scr[rows, ls]
                b = b_scr[rows, ls]
                for k in (1, 2, 4):
                    a_sh = jnp.where(row >= k, pltpu.roll(a, k, 0), 1.0)
                    b_sh = jnp.where(row >= k, pltpu.roll(b, k, 0), 0.0)
                    b = b + a * b_sh
                    a = a * a_sh
                h = b + a * c
                b_scr[rows, ls] = h
                return h[V7X_SUBLANES - 1:, :]

            c = lax.fori_loop(0, ng, tile_body, c_scr[bi, :, ls])
            c_scr[bi, :, ls] = c
            return carry

        lax.fori_loop(0, nb, seq_body, 0)

    mix_ref[...] = (b_scr[...] * _silu(zb_ref[...])).astype(BF16)

    @pl.when(tc == ntc - 1)
    def _():
        for bi in range(nb):
            r = bi * tt + t_last
            ho_ref[bi] = b_scr[r:r + 1, :]


def lru_mixer(proj, h0, conv0, cw, cb, wri, br, bi_, lam, layer, state_layer, *, batch, seq, t_valid, nb, tt):
    ntc = seq // tt
    assert nb == 1 or ntc == 1
    rows = nb * tt
    grid = (batch // nb, ntc)
    t_last = (t_valid - 1) - (ntc - 1) * tt

    def tok(cb_):
        return pl.BlockSpec((rows, W_MIX), lambda b, t: (b * ntc + t, cb_))

    def lw(shape):
        nd = len(shape)
        return pl.BlockSpec((None,) + shape, lambda b, t: (layer,) + (0,) * nd)

    return pl.pallas_call(
        functools.partial(_lru_kernel, nb=nb, tt=tt, t_last=t_last),
        grid=grid,
        in_specs=[tok(COL_XB // W_MIX), tok(COL_ZB // W_MIX),
                  pl.BlockSpec((None, nb, 1, W_MIX), lambda b, t: (state_layer, b, 0, 0)),
                  pl.BlockSpec((None, nb, CONV_W - 1, W_MIX), lambda b, t: (state_layer, b, 0, 0)),
                  lw((CONV_W, W_MIX)), lw((1, W_MIX)), lw((LRU_BLOCKS, LRU_BLK, 2 * LRU_BLK)),
                  lw((1, W_MIX)), lw((1, W_MIX)), lw((1, W_MIX))],
        out_specs=[pl.BlockSpec((rows, W_MIX), lambda b, t: (b * ntc + t, 0)),
                   pl.BlockSpec((nb, 1, W_MIX), lambda b, t: (b, 0, 0))],
        out_shape=[jax.ShapeDtypeStruct((batch * seq, W_MIX), BF16),
                   jax.ShapeDtypeStruct((batch, 1, W_MIX), F32)],
        scratch_shapes=[pltpu.VMEM((CONV_PAD + tt, W_MIX), F32),
                        pltpu.VMEM((nb, CONV_PAD, W_MIX), F32),
                        pltpu.VMEM((rows, W_MIX), F32), pltpu.VMEM((rows, W_MIX), F32),
                        pltpu.VMEM((rows, W_MIX), F32), pltpu.VMEM((nb, 1, W_MIX), F32)],
        compiler_params=_cparams(2),
        name="lru_mixer",
    )(proj, proj, h0, conv0, cw, cb, wri, br, bi_, lam)


def _gla_kernel(q_ref, k_ref, v_ref, z_ref, tail_ref, s0_ref, wg_ref, bg_ref, go_ref, stack_ref,
                mix_ref, so_ref,
                s_scr, mix_scr, *, nb, c, cv, single_chunk):
    del stack_ref
    tc = pl.program_id(1)
    ntc = pl.num_programs(1)
    sb = min(16, c)
    nblk = c // sb

    s_src, s_dst = (s0_ref, so_ref) if single_chunk else (s_scr, s_scr)
    if not single_chunk:
        @pl.when(tc == 0)
        def _():
            s_scr[...] = s0_ref[...]

    row_c = lax.broadcasted_iota(jnp.int32, (c, 1), 0)
    lane_c = lax.broadcasted_iota(jnp.int32, (sb, c), 1)
    row_sb = lax.broadcasted_iota(jnp.int32, (sb, 1), 0)

    def seq_body(bi, carry):
        rows = pl.ds(pl.multiple_of(bi * c, V7X_SUBLANES), c)
        x = _dot(tail_ref[rows, :].astype(BF16), wg_ref[...]) + bg_ref[...]
        gk = _log_sigmoid(x) * (1.0 / GLA_TAU)
        b_all = _cumsum_rows(gk)
        heads = range(GLA_H)
        s_old = [s_src[bi, h] for h in heads]
        q = [q_ref[rows, h * GLA_DK:(h + 1) * GLA_DK] * (GLA_DK ** -0.5) for h in heads]
        k = [k_ref[rows, h * GLA_DK:(h + 1) * GLA_DK] for h in heads]
        v = [v_ref[rows, h * GLA_DV:(h + 1) * GLA_DV] for h in heads]
        b = [b_all[:, h * GLA_DK:(h + 1) * GLA_DK] for h in heads]
        o_state = [_dot((q[h] * jnp.exp(b[h])).astype(BF16), s_old[h].astype(BF16)) for h in heads]
        s_new = []
        for h in heads:
            b_last = b[h][cv - 1:cv]
            dec = b_last - b[h]
            if cv < c:
                dec = jnp.where(row_c < cv, dec, NEG_INF)
            kd = k[h] * jnp.exp(dec)
            d_col = jnp.transpose(jnp.broadcast_to(jnp.exp(b_last), (V7X_SUBLANES, GLA_DK)))[:, 0:1]
            s_new.append(d_col * s_old[h] + _mm(kd, v[h], dot=_dot_tn, contract=0))
        att_off = []
        for h in heads:
            per_blk = [jnp.zeros((sb, c), F32)]
            for blk in range(1, nblk):
                r0 = blk * sb
                b_ref_row = b[h][r0 - 1:r0]
                qs = (q[h][r0:r0 + sb] * jnp.exp(b[h][r0:r0 + sb] - b_ref_row)).astype(BF16)
                kd = (k[h] * jnp.exp(jnp.where(row_c < r0, b_ref_row - b[h], NEG_INF))).astype(BF16)
                per_blk.append(_dot_nt(qs, kd))
            att_off.append(per_blk)
        att = []
        for h in heads:
            att_rows = []
            for blk in range(nblk):
                r0 = blk * sb
                q_i = q[h][r0:r0 + sb]
                b_i = b[h][r0:r0 + sb]
                a = att_off[h][blk]
                for sl in range(sb):
                    s_abs = r0 + sl
                    e = jnp.exp(jnp.where(row_sb >= sl, b_i - b[h][s_abs:s_abs + 1], NEG_INF))
                    col = jnp.sum(q_i * (k[h][s_abs:s_abs + 1] * e), axis=-1, keepdims=True)
                    a = jnp.where(lane_c == s_abs, col, a)
                att_rows.append(a)
            att.append(att_rows[0] if nblk == 1 else jnp.concatenate(att_rows, axis=0))
        o = [_mm(att[h], v[h]) + o_state[h] for h in heads]
        for h in heads:
            vs = slice(h * GLA_DV, (h + 1) * GLA_DV)
            ms = jnp.mean(o[h] * o[h], axis=-1, keepdims=True)
            on = (o[h] * lax.rsqrt(ms + EPS)) * go_ref[...]
            mix_scr[rows, vs] = on * _silu(z_ref[rows, vs])
            s_dst[bi, h] = s_new[h]
        return carry

    lax.fori_loop(0, nb, seq_body, 0)
    mix_ref[...] = mix_scr[...].astype(BF16)

    if not single_chunk:
        @pl.when(tc == ntc - 1)
        def _():
            so_ref[...] = s_scr[...]


def _stack_alias(stack, n_inputs_before):
    spec = pl.BlockSpec(memory_space=pl.ANY)
    if stack is None:
        return jnp.zeros((V7X_SUBLANES, V7X_LANES), F32), spec, {}
    return stack, spec, {n_inputs_before: 1}


def gla_mixer(proj, tail, s0, wg, bg, go, layer, state_layer, *, batch, seq, c, cv, nb, n_layers, stack=None):
    ntc = seq // c
    assert nb == 1 or ntc == 1
    rows = nb * c
    grid = (batch // nb, ntc)

    def tok(width, col):
        return pl.BlockSpec((rows, width), lambda b, t: (b * ntc + t, col // width))

    def lw(shape):
        nd = len(shape)
        return pl.BlockSpec((None,) + shape, lambda b, t: (layer,) + (0,) * nd)

    st_shape = (nb, GLA_H, GLA_DK, GLA_DV)
    stack_arg, stack_spec, aliases = _stack_alias(stack, 9)
    return pl.pallas_call(
        functools.partial(_gla_kernel, nb=nb, c=c, cv=cv, single_chunk=ntc == 1),
        grid=grid,
        in_specs=[tok(GLA_H * GLA_DK, COL_QC), tok(GLA_H * GLA_DK, COL_KC), tok(W_MIX, COL_VC), tok(W_MIX, COL_ZC),
                  pl.BlockSpec((rows, TAIL_W), lambda b, t: (b * ntc + t, 0)),
                  pl.BlockSpec((None,) + st_shape, lambda b, t: (state_layer, b, 0, 0, 0)),
                  lw((TAIL_W, GLA_H * GLA_DK)), lw((1, GLA_H * GLA_DK)), lw((1, GLA_DV)), stack_spec],
        out_specs=[pl.BlockSpec((rows, W_MIX), lambda b, t: (b * ntc + t, 0)),
                   pl.BlockSpec((None,) + st_shape, lambda b, t: (layer, b, 0, 0, 0))],
        out_shape=[jax.ShapeDtypeStruct((batch * seq, W_MIX), BF16),
                   jax.ShapeDtypeStruct((n_layers, batch, GLA_H, GLA_DK, GLA_DV), F32)],
        scratch_shapes=[pltpu.VMEM(st_shape, F32), pltpu.VMEM((rows, W_MIX), F32)],
        input_output_aliases=aliases,
        compiler_params=_cparams(2),
        name="gla_mixer",
    )(proj, proj, proj, proj, tail, s0, wg, bg, go, stack_arg)


def _l2norm(x):
    return x * lax.rsqrt(jnp.sum(x * x, axis=-1, keepdims=True) + EPS)


def _unit_lower_solve(ms, rhss, c, cv):
    if c <= V7X_SUBLANES:
        row = lax.broadcasted_iota(jnp.int32, (c, 1), 0)
        us = []
        for m, rhs in zip(ms, rhss):
            u = jnp.zeros_like(rhs)
            u_rows = []
            for t in range(cv):
                ut = rhs[t:t + 1]
                for s in range(t):
                    ut = ut - m[t:t + 1, s:s + 1] * u_rows[s]
                u_rows.append(ut)
                u = jnp.where(row == t, ut, u)
            us.append(u)
        return us
    xs = [-m for m in ms]
    ys = list(rhss)
    dv = rhss[0].shape[1]
    levels = int(math.log2(c))
    for lvl in range(levels):
        last = lvl == levels - 1
        ps = [_dot_x3(x, y if last else jnp.concatenate([y, x], axis=1)) for x, y in zip(xs, ys)]
        ys = [y + p[:, :dv] for y, p in zip(ys, ps)]
        if not last:
            xs = [p[:, dv:] for p in ps]
    return ys


def _gdn_kernel(qkv_ref, z_ref, tail_ref, s0_ref, cv0_ref, cw_ref, alog_ref, dtb_ref, go_ref, stack_ref,
                mix_ref, so_ref,
                s_scr, xp_scr, tail_scr, mix_scr, *, nb, c, cv, single_chunk):
    del stack_ref
    tc = pl.program_id(1)
    ntc = pl.num_programs(1)

    s_src, s_dst = (s0_ref, so_ref) if single_chunk else (s_scr, s_scr)

    @pl.when(tc == 0)
    def _():
        if not single_chunk:
            s_scr[...] = s0_ref[...]
        _init_conv_tail(tail_scr, cv0_ref, nb)

    row_c = lax.broadcasted_iota(jnp.int32, (c, 1), 0)
    ri = lax.broadcasted_iota(jnp.int32, (c, c), 0)
    ci = lax.broadcasted_iota(jnp.int32, (c, c), 1)
    lane_t = lax.broadcasted_iota(jnp.int32, (c, TAIL_W), 1)
    a_lanes = (lane_t >= TAIL_A) & (lane_t < TAIL_A + DN_H)

    def seq_body(bi, carry):
        conv, rows = _causal_conv(xp_scr, tail_scr, qkv_ref, cw_ref, bi, c)
        qkv = _silu(conv)
        tl = tail_ref[rows, :]
        g = jnp.where(a_lanes, -jnp.exp(alog_ref[...]) * _softplus(tl + dtb_ref[...]), 0.0)
        gam = _cumsum_rows(g)
        gam_t = jnp.transpose(gam)
        beta = _sigmoid(tl)
        heads = range(DN_H)
        s_old = [s_src[bi, h] for h in heads]
        s16 = [s.astype(BF16) for s in s_old]
        q16, k16, kf, v = [], [], [], []
        for h in heads:
            q = _l2norm(qkv[:, h * DN_DK:(h + 1) * DN_DK]) * (DN_DK ** -0.5)
            k = _l2norm(qkv[:, DN_H * DN_DK + h * DN_DK:DN_H * DN_DK + (h + 1) * DN_DK])
            q16.append(q.astype(BF16))
            k16.append(k.astype(BF16))
            kf.append(k)
            v.append(qkv[:, 2 * DN_H * DN_DK + h * DN_DV:2 * DN_H * DN_DK + (h + 1) * DN_DV])
        g_col = [gam[:, TAIL_A + h:TAIL_A + h + 1] for h in heads]
        g_row = [gam_t[TAIL_A + h:TAIL_A + h + 1, :] for h in heads]
        b_col = [beta[:, TAIL_B + h:TAIL_B + h + 1] for h in heads]
        decay = [jnp.exp(jnp.where(ri >= ci, g_col[h] - g_row[h], NEG_INF)) for h in heads]
        eg = [jnp.exp(g_col[h]) for h in heads]
        kk = [_dot_nt(k16[h], k16[h]) for h in heads]
        k_s = [_dot(k16[h], s16[h]) for h in heads]
        q_s = [_dot(q16[h], s16[h]) for h in heads]
        qk = [_dot_nt(q16[h], k16[h]) for h in heads]
        m = [jnp.where(ri > ci, b_col[h] * kk[h] * decay[h], 0.0) for h in heads]
        rhs = [b_col[h] * (v[h] - eg[h] * k_s[h]) for h in heads]
        u = _unit_lower_solve(m, rhs, c, cv)
        o = [eg[h] * q_s[h] + _mm(qk[h] * decay[h], u[h]) for h in heads]
        kd = []
        for h in heads:
            dec = g_col[h][cv - 1:cv] - g_col[h]
            if cv < c:
                dec = jnp.where(row_c < cv, dec, NEG_INF)
            kd.append(kf[h] * jnp.exp(dec))
        s_new = [jnp.exp(g_col[h][cv - 1:cv]) * s_old[h] + _mm(kd[h], u[h], dot=_dot_tn, contract=0) for h in heads]
        for h in heads:
            ms = jnp.mean(o[h] * o[h], axis=-1, keepdims=True)
            on = (o[h] * lax.rsqrt(ms + EPS)) * go_ref[...]
            mix_scr[rows, h * DN_DV:(h + 1) * DN_DV] = on * _silu(z_ref[rows, h * DN_DV:(h + 1) * DN_DV])
            s_dst[bi, h] = s_new[h]
        return carry

    lax.fori_loop(0, nb, seq_body, 0)
    mix_ref[...] = mix_scr[...].astype(BF16)

    if not single_chunk:
        @pl.when(tc == ntc - 1)
        def _():
            so_ref[...] = s_scr[...]


def gdn_mixer(proj, tail, s0, conv0, cw, alog, dtb, go, layer, state_layer, *, batch, seq, c, cv, nb, n_layers,
              stack=None):
    ntc = seq // c
    assert nb == 1 or ntc == 1
    rows = nb * c
    grid = (batch // nb, ntc)

    def tok(width, col):
        return pl.BlockSpec((rows, width), lambda b, t: (b * ntc + t, col // width))

    def lw(shape):
        nd = len(shape)
        return pl.BlockSpec((None,) + shape, lambda b, t: (layer,) + (0,) * nd)

    st_shape = (nb, DN_H, DN_DK, DN_DV)
    assert COL_QKVD % DN_QKV == 0
    stack_arg, stack_spec, aliases = _stack_alias(stack, 9)
    return pl.pallas_call(
        functools.partial(_gdn_kernel, nb=nb, c=c, cv=cv, single_chunk=ntc == 1),
        grid=grid,
        in_specs=[tok(DN_QKV, COL_QKVD),
                  tok(W_MIX, COL_ZD),
                  pl.BlockSpec((rows, TAIL_W), lambda b, t: (b * ntc + t, 0)),
                  pl.BlockSpec((None,) + st_shape, lambda b, t: (state_layer, b, 0, 0, 0)),
                  pl.BlockSpec((None, nb, CONV_W - 1, DN_QKV), lambda b, t: (state_layer, b, 0, 0)),
                  lw((CONV_W, DN_QKV)), lw((1, TAIL_W)), lw((1, TAIL_W)), lw((1, DN_DV)), stack_spec],
        out_specs=[pl.BlockSpec((rows, W_MIX), lambda b, t: (b * ntc + t, 0)),
                   pl.BlockSpec((None,) + st_shape, lambda b, t: (layer, b, 0, 0, 0))],
        out_shape=[jax.ShapeDtypeStruct((batch * seq, W_MIX), BF16),
                   jax.ShapeDtypeStruct((n_layers, batch, DN_H, DN_DK, DN_DV), F32)],
        input_output_aliases=aliases,
        scratch_shapes=[pltpu.VMEM(st_shape, F32),
                        pltpu.VMEM((CONV_PAD + c, DN_QKV), F32),
                        pltpu.VMEM((nb, CONV_PAD, DN_QKV), F32),
                        pltpu.VMEM((rows, W_MIX), F32)],
        compiler_params=_cparams(2),
        name="gdn_mixer",
    )(proj, proj, tail, s0, conv0, cw, alog, dtb, go, stack_arg)


def _prepare_weights(w_in, s5_lam_re, s5_lam_im, s5_log_dt, s5_b_re, s5_b_im, s5_c_re, s5_c_im, s5_d,
                     s5_w_glu, lru_w_r, lru_w_i, gla_w_gate, dn_a_log, dn_dt_bias, w_out, ple_w, ple_gate_w):
    nl = w_in.shape[0]
    p = {}
    p["w_main"], p["w_tail"] = repack_w_in(jnp.swapaxes(w_in, 1, 2))
    p["w_out"] = w_out.astype(BF16)
    p["ple_w"] = ple_w.astype(BF16)
    p["ple_gate_w"] = ple_gate_w.astype(BF16)
    p["w_glu"] = s5_w_glu.astype(BF16)
    ldt = jnp.repeat(s5_log_dt, S5_STATE, axis=1).reshape(nl, 1, S5_CH)
    coef, ptab, cre, cim = s5_prep(s5_lam_re.reshape(nl, 1, S5_CH), s5_lam_im.reshape(nl, 1, S5_CH), ldt)
    bb_re, bb_im = s5_bb(cre.reshape(nl, S5_CH, 1), cim.reshape(nl, S5_CH, 1),
                         s5_b_re.reshape(nl, S5_CH, S5_GROUP), s5_b_im.reshape(nl, S5_CH, S5_GROUP))
    eye16 = jnp.eye(16, dtype=F32)
    eye8 = jnp.eye(8, dtype=F32)

    def pack_u(bb):
        t = bb.reshape(nl, 4, 16, S5_STATE, S5_GROUP)
        return jnp.einsum("lkgph,gG->lkghGp", t, eye16).reshape(nl, 4, 256, 1024).astype(BF16)

    def pack_y(cc):
        t = cc.reshape(nl, 8, 8, S5_GROUP, S5_STATE)
        return jnp.einsum("ljghp,gG->ljgpGh", t, eye8).reshape(nl, 8, 512, 128).astype(BF16)

    p["s5_coef"] = coef
    p["s5_ptab"] = ptab
    p["s5_wur"], p["s5_wui"] = pack_u(bb_re), pack_u(bb_im)
    p["s5_wyr"], p["s5_wyi"] = pack_y(s5_c_re), pack_y(s5_c_im)
    p["s5_d"] = s5_d.reshape(nl, 1, W_MIX)
    p["lru_wri"] = jnp.concatenate([lru_w_r, lru_w_i], axis=-1).astype(BF16)
    p["gla_wg"] = jnp.concatenate([gla_w_gate, jnp.zeros((nl, TAIL_W - GLA_RANK, GLA_H * GLA_DK), F32)],
                                  axis=1).astype(BF16)

    def tail_row(x):
        return jnp.pad(x, ((0, 0), (TAIL_A, TAIL_W - TAIL_A - DN_H))).reshape(nl, 1, TAIL_W)

    p["dn_alog"] = tail_row(dn_a_log)
    p["dn_dtb"] = tail_row(dn_dt_bias)
    return p


def kernel(x_prompt, x_sample, state_s5_re, state_s5_im, state_lru_h, state_lru_conv, state_gla, state_delta, state_delta_conv, p_prompt, p_sample, g_norm, w_in, s5_lam_re, s5_lam_im, s5_log_dt, s5_b_re, s5_b_im, s5_c_re, s5_c_im, s5_d, s5_w_glu, s5_b_glu, lru_conv_w, lru_conv_b, lru_w_r, lru_b_r, lru_w_i, lru_b_i, lru_lam, gla_w_gate, gla_b_gate, gla_g_out, dn_conv_w, dn_a_log, dn_dt_bias, dn_g_out, w_out, ple_w, ple_gate_w, ple_gate_b, g_final):
    nl = w_in.shape[0]
    bp, tp, _ = x_prompt.shape
    bs, ts, _ = x_sample.shape
    tsp = SAMPLE_T_PAD
    mp, ms = bp * tp, bs * ts

    p = _prepare_weights(w_in, s5_lam_re, s5_lam_im, s5_log_dt, s5_b_re, s5_b_im, s5_c_re, s5_c_im, s5_d,
                         s5_w_glu, lru_w_r, lru_w_i, gla_w_gate, dn_a_log, dn_dt_bias, w_out, ple_w, ple_gate_w)

    def vec(x):
        return x.reshape(nl, 1, x.shape[-1])

    g_norm3, b_glu3, cb3 = vec(g_norm), vec(s5_b_glu), vec(lru_conv_b)
    br3, bi3, lam3 = vec(lru_b_r), vec(lru_b_i), vec(lru_lam)
    bg3, go_c3, go_d3, pgb3 = vec(gla_b_gate), vec(gla_g_out), vec(dn_g_out), vec(ple_gate_b)
    pe_p = p_prompt.reshape(nl, mp, D_PLE)
    pe_s = p_sample.reshape(nl, ms, D_PLE)

    z_s5 = jnp.zeros((1, bp, 1, S5_CH), F32)
    z_lru = jnp.zeros((1, bp, 1, W_MIX), F32)
    z_lconv = jnp.zeros((1, bp, CONV_W - 1, W_MIX), F32)
    z_gla = jnp.zeros((1, bp, GLA_H, GLA_DK, GLA_DV), F32)
    z_dn = jnp.zeros((1, bp, DN_H, DN_DK, DN_DV), F32)
    z_dconv = jnp.zeros((1, bp, CONV_W - 1, DN_QKV), F32)
    c_s5r = state_s5_re.reshape(nl, bs, 1, S5_CH)
    c_s5i = state_s5_im.reshape(nl, bs, 1, S5_CH)
    c_lru = state_lru_h.reshape(nl, bs, 1, W_MIX)

    def mixers(proj, tail, layer, *, batch, seq, t_valid, st, sl, nb_scan, tt, c, nb_mat, stacks):
        s5r0, s5i0, lru0, lconv0, gla0, dn0, dconv0 = st
        gla_stack, dn_stack = stacks
        mix_a, s5r, s5i = s5_mixer(proj, s5r0, s5i0, p["s5_coef"], p["s5_ptab"], p["s5_wur"], p["s5_wui"], p["s5_wyr"],
                                   p["s5_wyi"], p["s5_d"], p["w_glu"], b_glu3, layer, sl,
                                   batch=batch, seq=seq, t_valid=t_valid, nb=nb_scan, tt=tt)
        mix_b, lruh = lru_mixer(proj, lru0, lconv0, lru_conv_w, cb3, p["lru_wri"], br3, bi3, lam3, layer, sl,
                                batch=batch, seq=seq, t_valid=t_valid, nb=nb_scan, tt=tt)
        mix_c, glas = gla_mixer(proj, tail, gla0, p["gla_wg"], bg3, go_c3, layer, sl,
                                batch=batch, seq=seq, c=c, cv=min(c, t_valid), nb=nb_mat, n_layers=nl,
                                stack=gla_stack)
        mix_d, dns = gdn_mixer(proj, tail, dn0, dconv0, dn_conv_w, p["dn_alog"], p["dn_dtb"], go_d3, layer, sl,
                               batch=batch, seq=seq, c=c, cv=min(c, t_valid), nb=nb_mat, n_layers=nl,
                               stack=dn_stack)
        return (mix_a, mix_b, mix_c, mix_d), (s5r, s5i, lruh, glas, dns)

    hp = x_prompt.reshape(mp, D_MODEL)
    hs = x_sample.reshape(ms, D_MODEL)
    new_p, new_s = [], []
    stacks_p = stacks_s = (None, None)
    hgp, ssqp = prenorm(hp, g_norm3, 0, tm=DENSE_TILES_SAMPLE["tm"])
    hgs, ssqs = prenorm(hs, g_norm3, 0, tm=DENSE_TILES_SAMPLE["tm"])
    for i in range(nl):
        nxt = min(i + 1, nl - 1)
        proj, tail = in_proj(hgp, ssqp, p["w_main"], p["w_tail"], i, **DENSE_TILES_PROMPT)
        mixes, (s5r, s5i, lruh, glas, dns) = mixers(
            proj, tail, i, batch=bp, seq=tp, t_valid=tp,
            st=(z_s5, z_s5, z_lru, z_lconv, z_gla, z_dn, z_dconv), sl=0,
            nb_scan=1, tt=256, c=GLA_CHUNK, nb_mat=1, stacks=stacks_p)
        stacks_p = (glas, dns)
        proj3 = proj.reshape(bp, tp, N_MAIN)
        new_p.append((s5r.reshape(bp, S5_GROUPS, S5_STATE), s5i.reshape(bp, S5_GROUPS, S5_STATE),
                      lruh.reshape(bp, W_MIX), proj3[:, tp - (CONV_W - 1):, COL_XB:COL_XB + W_MIX],
                      None, None, proj3[:, tp - (CONV_W - 1):, COL_QKVD:COL_QKVD + DN_QKV]))
        hp, hpb = out_proj(mixes, p["w_out"], hp, i, **DENSE_TILES_PROMPT)
        hp, hgp, ssqp = ple(hp, hpb, pe_p, p["ple_gate_w"], pgb3, p["ple_w"], g_norm3, i, nxt, **DENSE_TILES_PROMPT)

        proj, tail = in_proj(hgs, ssqs, p["w_main"], p["w_tail"], i, **DENSE_TILES_SAMPLE)
        proj3 = proj.reshape(bs, ts, N_MAIN)
        proj_pad = jnp.pad(proj3, ((0, 0), (0, tsp - ts), (0, 0))).reshape(bs * tsp, N_MAIN)
        tail_pad = jnp.pad(tail.reshape(bs, ts, TAIL_W), ((0, 0), (0, tsp - ts), (0, 0))).reshape(bs * tsp, TAIL_W)
        mixes, (s5r, s5i, lruh, glas, dns) = mixers(
            proj_pad, tail_pad, i, batch=bs, seq=tsp, t_valid=ts,
            st=(c_s5r, c_s5i, c_lru, state_lru_conv, state_gla, state_delta, state_delta_conv), sl=i,
            nb_scan=32, tt=tsp, c=tsp, nb_mat=8, stacks=stacks_s)
        stacks_s = (glas, dns)
        mixes = tuple(m.reshape(bs, tsp, W_MIX)[:, :ts].reshape(ms, W_MIX) for m in mixes)
        new_s.append((s5r.reshape(bs, S5_GROUPS, S5_STATE), s5i.reshape(bs, S5_GROUPS, S5_STATE),
                      lruh.reshape(bs, W_MIX), proj3[:, ts - (CONV_W - 1):, COL_XB:COL_XB + W_MIX],
                      None, None, proj3[:, ts - (CONV_W - 1):, COL_QKVD:COL_QKVD + DN_QKV]))
        hs, hsb = out_proj(mixes, p["w_out"], hs, i, **DENSE_TILES_SAMPLE)
        hs, hgs, ssqs = ple(hs, hsb, pe_s, p["ple_gate_w"], pgb3, p["ple_w"], g_norm3, i, nxt, **DENSE_TILES_SAMPLE)

    g_fin = g_final.reshape(1, D_MODEL)
    y_prompt = final_norm(hp, g_fin, tm=256).reshape(bp, tp, D_MODEL)
    y_sample = final_norm(hs, g_fin, tm=256).reshape(bs, ts, D_MODEL)

    def stk(lst, j):
        return jnp.stack([s[j] for s in lst], axis=0)

    return (y_prompt, y_sample,
            stk(new_p, 0), stk(new_p, 1), stk(new_p, 2), stk(new_p, 3), stacks_p[0], stacks_p[1], stk(new_p, 6),
            stk(new_s, 0), stk(new_s, 1), stk(new_s, 2), stk(new_s, 3), stacks_s[0], stacks_s[1], stk(new_s, 6))
```

```python
import functools
import math

import jax
import jax.numpy as jnp
import numpy as np
from jax import lax
from jax.experimental import pallas as pl
from jax.experimental.pallas import tpu as pltpu

F32 = jnp.float32
BF16 = jnp.bfloat16
EPS = 1e-6
NEG_INF = float("-inf")

D_MODEL = 4096
DEPTH = 4
W_MIX = 1024
S5_GROUPS = 64
S5_GROUP = 16
S5_STATE = 64
S5_CH = S5_GROUPS * S5_STATE
LRU_BLOCKS = 8
LRU_BLK = 128
LRU_C = 8.0
CONV_W = 4
GLA_H = 4
GLA_DK = 128
GLA_DV = 256
GLA_RANK = 16
GLA_TAU = 16.0
GLA_CHUNK = 64
DN_H = 8
DN_DK = 128
DN_DV = 128
DN_QKV = 3072
DN_CHUNK = 64
D_PLE = 256
N_MAIN = 11264
COL_XA, COL_ZA, COL_XB, COL_ZB = 0, 1024, 2048, 3072
COL_QC, COL_KC, COL_VC = 4096, 4608, 5120
COL_QKVD, COL_ZC, COL_ZD = 6144, 9216, 10240
ORIG_ZC, ORIG_GC, ORIG_QKVD, ORIG_ZD, ORIG_AB = 6144, 7168, 7184, 10256, 11280
TAIL_W = 128
TAIL_G, TAIL_A, TAIL_B = 0, 16, 24
SAMPLE_T_PAD = 8

V7X_LANES = 128
V7X_SUBLANES = 8
V7X_MXU_DEPTH = 256
VMEM_LIMIT = 52 * 1024 * 1024
DENSE_TILES_PROMPT = dict(tm=1024, tn=512)
DENSE_TILES_SAMPLE = dict(tm=512, tn=1024)


def _cparams(n_axes):
    return pltpu.CompilerParams(dimension_semantics=("arbitrary",) * n_axes,
                                vmem_limit_bytes=VMEM_LIMIT)


def _sigmoid(x):
    return jax.nn.sigmoid(x)


def _silu(x):
    return x * jax.nn.sigmoid(x)


def _softplus(x):
    return jnp.maximum(x, 0.0) + jnp.log1p(jnp.exp(-jnp.abs(x)))


def _expm1(x):
    u = jnp.exp(x)
    um1 = u - 1.0
    lg = jnp.log(u)
    r = um1 * x / jnp.where(lg == 0.0, 1.0, lg)
    return jnp.where(um1 == 0.0, x, jnp.where(um1 == -1.0, -1.0, r))


def _log_sigmoid(x):
    return jnp.minimum(x, 0.0) - jnp.log1p(jnp.exp(-jnp.abs(x)))


def _gelu_tanh(x):
    c = math.sqrt(2.0 / math.pi)
    return x * (0.5 * (1.0 + jnp.tanh(c * (x + 0.044715 * (x * x * x)))))


def _dot(a, b):
    return jnp.dot(a, b, preferred_element_type=F32)


def _dot_nt(a, b):
    return lax.dot_general(a, b, (((1,), (1,)), ((), ())), preferred_element_type=F32)


def _dot_tn(a, b):
    return lax.dot_general(a, b, (((0,), (0,)), ((), ())), preferred_element_type=F32)


def _split3(x):
    h = x.astype(BF16)
    r = x - h.astype(F32)
    m = r.astype(BF16)
    l = (r - m.astype(F32)).astype(BF16)
    return h, m, l


def _cumsum_rows(x):
    c = x.shape[0]
    if c <= V7X_SUBLANES:
        row = lax.broadcasted_iota(jnp.int32, x.shape, 0)
        k = 1
        while k < c:
            x = x + jnp.where(row >= k, pltpu.roll(x, k, 0), 0.0)
            k *= 2
        return x
    h, m, l = _split3(x)
    if 3 * c <= V7X_MXU_DEPTH:
        col = lax.broadcasted_iota(jnp.int32, (c, 3 * c), 1)
        col = jnp.where(col >= 2 * c, col - 2 * c, jnp.where(col >= c, col - c, col))
        tri3 = (lax.broadcasted_iota(jnp.int32, (c, 3 * c), 0) >= col).astype(BF16)
        return _dot(tri3, jnp.concatenate([h, m, l], axis=0))
    tri = (lax.broadcasted_iota(jnp.int32, (c, c), 0) >= lax.broadcasted_iota(jnp.int32, (c, c), 1)).astype(BF16)
    return _dot(tri, h) + _dot(tri, m) + _dot(tri, l)


def _mm(a, b, dot=_dot, contract=1):
    if a.shape[contract] < 2 * V7X_SUBLANES:
        return dot(a, b)
    return dot(a.astype(BF16), b.astype(BF16))


def _dot_x3(a, b):
    ah32 = a.astype(BF16).astype(F32)
    al32 = a - ah32
    bh32 = b.astype(BF16).astype(F32)
    bl32 = b - bh32
    k = a.shape[1]
    if 3 * k <= V7X_MXU_DEPTH:
        lhs = jnp.concatenate([ah32, ah32, al32], axis=1).astype(BF16)
        rhs = jnp.concatenate([bh32, bl32, bh32], axis=0).astype(BF16)
        return _dot(lhs, rhs)
    ah, al, bh, bl = (t.astype(BF16) for t in (ah32, al32, bh32, bl32))
    return _dot(ah, bh) + _dot(ah, bl) + _dot(al, bh)


def _prenorm_kernel(x_ref, g_ref, hg_ref, ssq_ref):
    x = x_ref[...]
    hg_ref[...] = (x * g_ref[...]).astype(BF16)
    ssq_ref[...] = jnp.sum(x * x, axis=-1, keepdims=True)


def prenorm(x, g_norm, layer, *, tm):
    m = x.shape[0]
    return pl.pallas_call(
        _prenorm_kernel,
        grid=(m // tm,),
        in_specs=[pl.BlockSpec((tm, D_MODEL), lambda i: (i, 0)),
                  pl.BlockSpec((None, 1, D_MODEL), lambda i: (layer, 0, 0))],
        out_specs=[pl.BlockSpec((tm, D_MODEL), lambda i: (i, 0)), pl.BlockSpec((tm, 1), lambda i: (i, 0))],
        out_shape=[jax.ShapeDtypeStruct((m, D_MODEL), BF16), jax.ShapeDtypeStruct((m, 1), F32)],
        compiler_params=_cparams(1),
        name="prenorm",
    )(x, g_norm)


def _in_proj_kernel(hg_ref, ssq_ref, w_ref, wt_ref, o_ref, ot_ref):
    r = lax.rsqrt(ssq_ref[...] * (1.0 / D_MODEL) + EPS)

    @pl.when(pl.program_id(1) == 0)
    def _():
        ot_ref[...] = _dot_nt(hg_ref[...], wt_ref[...]) * r

    o_ref[...] = _dot_nt(hg_ref[...], w_ref[...]) * r


def in_proj(hg, ssq, w_main, w_tail, layer, *, tm, tn):
    m = hg.shape[0]
    grid = (m // tm, N_MAIN // tn)
    return pl.pallas_call(
        _in_proj_kernel,
        grid=grid,
        in_specs=[
            pl.BlockSpec((tm, D_MODEL), lambda i, j: (i, 0)),
            pl.BlockSpec((tm, 1), lambda i, j: (i, 0)),
            pl.BlockSpec((None, tn, D_MODEL), lambda i, j: (layer, j, 0)),
            pl.BlockSpec((None, TAIL_W, D_MODEL), lambda i, j: (layer, 0, 0)),
        ],
        out_specs=[
            pl.BlockSpec((tm, tn), lambda i, j: (i, j)),
            pl.BlockSpec((tm, TAIL_W), lambda i, j: (i, 0)),
        ],
        out_shape=[jax.ShapeDtypeStruct((m, N_MAIN), F32), jax.ShapeDtypeStruct((m, TAIL_W), F32)],
        compiler_params=_cparams(2),
        name="in_proj",
    )(hg, ssq, w_main, w_tail)


def _out_proj_kernel(ma_ref, mb_ref, mc_ref, md_ref, w_ref, h_ref, o_ref, ob_ref):
    acc = h_ref[...]
    for k, m_ref in enumerate((ma_ref, mb_ref, mc_ref, md_ref)):
        acc = acc + _dot(m_ref[...], w_ref[k * W_MIX:(k + 1) * W_MIX, :])
    o_ref[...] = acc
    ob_ref[...] = acc.astype(BF16)


def out_proj(mixes, w_out, h, layer, *, tm, tn):
    m = h.shape[0]
    grid = (m // tm, D_MODEL // tn)
    mix_spec = pl.BlockSpec((tm, W_MIX), lambda i, j: (i, 0))
    tile = pl.BlockSpec((tm, tn), lambda i, j: (i, j))
    return pl.pallas_call(
        _out_proj_kernel,
        grid=grid,
        in_specs=[mix_spec, mix_spec, mix_spec, mix_spec,
                  pl.BlockSpec((None, 4 * W_MIX, tn), lambda i, j: (layer, 0, j)), tile],
        out_specs=[tile, tile],
        out_shape=[jax.ShapeDtypeStruct((m, D_MODEL), F32), jax.ShapeDtypeStruct((m, D_MODEL), BF16)],
        compiler_params=_cparams(2),
        name="out_proj",
    )(*mixes, w_out, h)


def _ple_kernel(hrow_ref, htile_ref, pe_ref, wg_ref, bg_ref, wp_ref, gn_ref, o_ref, hg_ref, ssq_ref):
    gate = _sigmoid(_dot(hrow_ref[...], wg_ref[...]) + bg_ref[...])
    pv = _dot(pe_ref[...].astype(BF16), wp_ref[...])
    h = htile_ref[...] + gate * pv
    o_ref[...] = h
    hg_ref[...] = (h * gn_ref[...]).astype(BF16)
    part = jnp.sum(h * h, axis=-1, keepdims=True)

    @pl.when(pl.program_id(1) == 0)
    def _():
        ssq_ref[...] = part

    @pl.when(pl.program_id(1) != 0)
    def _():
        ssq_ref[...] = ssq_ref[...] + part


def ple(h, hb, pe, w_gate, b_gate, w_ple, g_next, layer, next_layer, *, tm, tn):
    m = h.shape[0]
    grid = (m // tm, D_MODEL // tn)
    tile = pl.BlockSpec((tm, tn), lambda i, j: (i, j))
    return pl.pallas_call(
        _ple_kernel,
        grid=grid,
        in_specs=[
            pl.BlockSpec((tm, D_MODEL), lambda i, j: (i, 0)),
            tile,
            pl.BlockSpec((None, tm, D_PLE), lambda i, j: (layer, i, 0)),
            pl.BlockSpec((None, D_MODEL, tn), lambda i, j: (layer, 0, j)),
            pl.BlockSpec((None, 1, tn), lambda i, j: (layer, 0, j)),
            pl.BlockSpec((None, D_PLE, tn), lambda i, j: (layer, 0, j)),
            pl.BlockSpec((None, 1, tn), lambda i, j: (next_layer, 0, j)),
        ],
        out_specs=[tile, tile, pl.BlockSpec((tm, 1), lambda i, j: (i, 0))],
        out_shape=[jax.ShapeDtypeStruct((m, D_MODEL), F32), jax.ShapeDtypeStruct((m, D_MODEL), BF16),
                   jax.ShapeDtypeStruct((m, 1), F32)],
        compiler_params=_cparams(2),
        name="ple",
    )(hb, h, pe, w_gate, b_gate, w_ple, g_next)


REPACK_ROWS = 512


def _repack_kernel(a_ref, g_ref, ab_ref, o_ref, t_ref):
    o_ref[...] = a_ref[...].astype(BF16)

    @pl.when(pl.program_id(1) == 0)
    def _():
        t_ref[...] = jnp.zeros(t_ref.shape, BF16)
        t_ref[TAIL_G:TAIL_G + GLA_RANK, :] = g_ref[...].astype(BF16)
        t_ref[TAIL_A:TAIL_A + 2 * DN_H, :] = ab_ref[...].astype(BF16)


def repack_w_in(w_in_t):
    nl = w_in_t.shape[0]
    n_tiles = N_MAIN // REPACK_ROWS
    t_qkvd, t_zc, t_zd = COL_QKVD // REPACK_ROWS, COL_ZC // REPACK_ROWS, COL_ZD // REPACK_ROWS

    def src_row(j):
        return jnp.where(j < t_qkvd, j * REPACK_ROWS,
                         jnp.where(j < t_zc, ORIG_QKVD + (j - t_qkvd) * REPACK_ROWS,
                                   jnp.where(j < t_zd, ORIG_ZC + (j - t_zc) * REPACK_ROWS,
                                             ORIG_ZD + (j - t_zd) * REPACK_ROWS)))

    row_align = 2 * V7X_SUBLANES
    assert all(o % row_align == 0 for o in (ORIG_QKVD, ORIG_ZC, ORIG_ZD, ORIG_GC, ORIG_AB))

    def rows(n, start):
        return pl.BlockSpec((None, pl.Element(n), pl.Element(D_MODEL)),
                            lambda l, j: (l, pl.multiple_of(start(j), row_align), 0))

    return pl.pallas_call(
        _repack_kernel,
        grid=(nl, n_tiles),
        in_specs=[rows(REPACK_ROWS, src_row), rows(GLA_RANK, lambda j: ORIG_GC), rows(2 * DN_H, lambda j: ORIG_AB)],
        out_specs=[pl.BlockSpec((None, REPACK_ROWS, D_MODEL), lambda l, j: (l, j, 0)),
                   pl.BlockSpec((None, TAIL_W, D_MODEL), lambda l, j: (l, 0, 0))],
        out_shape=[jax.ShapeDtypeStruct((nl, N_MAIN, D_MODEL), BF16),
                   jax.ShapeDtypeStruct((nl, TAIL_W, D_MODEL), BF16)],
        compiler_params=_cparams(2),
        name="repack_w_in",
    )(w_in_t, w_in_t, w_in_t)


def _final_norm_kernel(x_ref, g_ref, o_ref):
    x = x_ref[...]
    ms = jnp.mean(x * x, axis=-1, keepdims=True)
    o_ref[...] = (x * lax.rsqrt(ms + EPS)) * g_ref[...]


def final_norm(h, g, *, tm):
    m = h.shape[0]
    return pl.pallas_call(
        _final_norm_kernel,
        grid=(m // tm,),
        in_specs=[pl.BlockSpec((tm, D_MODEL), lambda i: (i, 0)),
                  pl.BlockSpec((1, D_MODEL), lambda i: (0, 0))],
        out_specs=pl.BlockSpec((tm, D_MODEL), lambda i: (i, 0)),
        out_shape=jax.ShapeDtypeStruct((m, D_MODEL), F32),
        compiler_params=_cparams(1),
        name="final_norm",
    )(h, g)


S5_SEG = 32


def _s5_prep_kernel(lre_ref, lim_ref, ldt_ref, coef_ref, ptab_ref, cre_ref, cim_ref):
    lre = lre_ref[...]
    lim = lim_ref[...]
    dt = jnp.exp(ldt_ref[...])
    ai = lim * dt
    mag = jnp.exp(lre * dt)
    ar = mag * jnp.cos(ai)
    aim = mag * jnp.sin(ai)
    den = lre * lre + lim * lim
    nr = ar - 1.0
    cre_ref[...] = (nr * lre + aim * lim) / den
    cim_ref[...] = (aim * lre - nr * lim) / den
    pr, pim = [ar], [aim]
    for _ in range(S5_SEG - 1):
        nr_, ni_ = pr[-1] * ar - pim[-1] * aim, pr[-1] * aim + pim[-1] * ar
        pr.append(nr_)
        pim.append(ni_)
    shape = (V7X_SUBLANES, S5_CH)
    coef_ref[0] = jnp.broadcast_to(ar, shape)
    coef_ref[1] = jnp.broadcast_to(aim, shape)
    coef_ref[2] = jnp.broadcast_to(pr[S5_SEG - 1], shape)
    coef_ref[3] = jnp.broadcast_to(pim[S5_SEG - 1], shape)
    tshape = (S5_SEG, S5_CH)
    row = lax.broadcasted_iota(jnp.int32, tshape, 0)
    accr = jnp.zeros(tshape, F32)
    acci = jnp.zeros(tshape, F32)
    for j in range(S5_SEG):
        accr = jnp.where(row == j, jnp.broadcast_to(pr[j], tshape), accr)
        acci = jnp.where(row == j, jnp.broadcast_to(pim[j], tshape), acci)
    ptab_ref[0] = accr
    ptab_ref[1] = acci


def s5_prep(lam_re, lam_im, log_dt_rep):
    nl = lam_re.shape[0]
    vec = pl.BlockSpec((None, 1, S5_CH), lambda l: (l, 0, 0))
    return pl.pallas_call(
        _s5_prep_kernel,
        grid=(nl,),
        in_specs=[vec, vec, vec],
        out_specs=[pl.BlockSpec((None, 4, V7X_SUBLANES, S5_CH), lambda l: (l, 0, 0, 0)),
                   pl.BlockSpec((None, 2, S5_SEG, S5_CH), lambda l: (l, 0, 0, 0)), vec, vec],
        out_shape=[jax.ShapeDtypeStruct((nl, 4, V7X_SUBLANES, S5_CH), F32),
                   jax.ShapeDtypeStruct((nl, 2, S5_SEG, S5_CH), F32),
                   jax.ShapeDtypeStruct((nl, 1, S5_CH), F32),
                   jax.ShapeDtypeStruct((nl, 1, S5_CH), F32)],
        compiler_params=_cparams(1),
        name="s5_prep",
    )(lam_re, lam_im, log_dt_rep)


def _s5_bb_kernel(cr_ref, ci_ref, br_ref, bi_ref, or_ref, oi_ref):
    cr = cr_ref[...]
    ci = ci_ref[...]
    br = br_ref[...]
    bi = bi_ref[...]
    or_ref[...] = cr * br - ci * bi
    oi_ref[...] = cr * bi + ci * br


def s5_bb(coef_re_col, coef_im_col, b_re, b_im):
    nl = b_re.shape[0]
    rows = 1024
    col = pl.BlockSpec((None, rows, 1), lambda l, r: (l, r, 0))
    mat = pl.BlockSpec((None, rows, S5_GROUP), lambda l, r: (l, r, 0))
    return pl.pallas_call(
        _s5_bb_kernel,
        grid=(nl, S5_CH // rows),
        in_specs=[col, col, mat, mat],
        out_specs=[mat, mat],
        out_shape=[jax.ShapeDtypeStruct((nl, S5_CH, S5_GROUP), F32)] * 2,
        compiler_params=_cparams(2),
        name="s5_bb",
    )(coef_re_col, coef_im_col, b_re, b_im)


S5_STRIP = 512


def _s5_kernel(xa_ref, za_ref, h0r_ref, h0i_ref, coef_ref, ptab_ref, perm_ref, permt_ref,
               wur_ref, wui_ref, wyr_ref, wyi_ref, dsk_ref, wglu_ref, bglu_ref,
               mix_ref, hro_ref, hio_ref,
               ur_scr, ui_scr, cr_scr, ci_scr, y_scr, *, nb, tt, t_last):
    tc = pl.program_id(1)
    ntc = pl.num_programs(1)
    split = nb == 1
    lc = tt // V7X_SUBLANES if split else tt
    n_sets = 1 if split else nb // V7X_SUBLANES
    set_rows = V7X_SUBLANES * lc

    xb16 = _dot(perm_ref[...], xa_ref[...].astype(BF16)).astype(BF16)
    for k in range(4):
        xk = xb16[:, 256 * k:256 * (k + 1)]
        ur_scr[:, 1024 * k:1024 * (k + 1)] = _dot(xk, wur_ref[k])
        ui_scr[:, 1024 * k:1024 * (k + 1)] = _dot(xk, wui_ref[k])

    @pl.when(tc == 0)
    def _():
        cr_scr[...] = h0r_ref[...]
        ci_scr[...] = h0i_ref[...]

    row8 = lax.broadcasted_iota(jnp.int32, (V7X_SUBLANES, S5_STRIP), 0)
    for st in range(S5_CH // S5_STRIP):
        ls = slice(st * S5_STRIP, (st + 1) * S5_STRIP)
        ar = coef_ref[0, :, ls]
        ai = coef_ref[1, :, ls]
        for s in range(n_sets):
            base = s * set_rows
            if split:
                h0r = jnp.zeros((V7X_SUBLANES, S5_STRIP), F32)
                h0i = jnp.zeros((V7X_SUBLANES, S5_STRIP), F32)
            else:
                h0r = jnp.zeros((V7X_SUBLANES, S5_STRIP), F32)
                h0i = jnp.zeros((V7X_SUBLANES, S5_STRIP), F32)
                for k in range(V7X_SUBLANES):
                    h0r = jnp.where(row8 == k, cr_scr[s * V7X_SUBLANES + k, :, ls], h0r)
                    h0i = jnp.where(row8 == k, ci_scr[s * V7X_SUBLANES + k, :, ls], h0i)

            def step(j, h, base=base, ls=ls, ar=ar, ai=ai):
                hr, hi = h
                rows = pl.ds(pl.multiple_of(base + j * V7X_SUBLANES, V7X_SUBLANES), V7X_SUBLANES)
                nr = (ar * hr - ai * hi) + ur_scr[rows, ls]
                ni = (ar * hi + ai * hr) + ui_scr[rows, ls]
                ur_scr[rows, ls] = nr
                ui_scr[rows, ls] = ni
                return nr, ni

            fr, fi = lax.fori_loop(0, lc, step, (h0r, h0i), unroll=4)
            if split:
                alr = coef_ref[2, 0:1, ls]
                ali = coef_ref[3, 0:1, ls]
                cr = cr_scr[0, :, ls]
                ci = ci_scr[0, :, ls]
                init_r = jnp.zeros((V7X_SUBLANES, S5_STRIP), F32)
                init_i = jnp.zeros((V7X_SUBLANES, S5_STRIP), F32)
                for k in range(V7X_SUBLANES):
                    init_r = jnp.where(row8 == k, cr, init_r)
                    init_i = jnp.where(row8 == k, ci, init_i)
                    cr, ci = (alr * cr - ali * ci) + fr[k:k + 1], (alr * ci + ali * cr) + fi[k:k + 1]
                cr_scr[0, :, ls] = cr
                ci_scr[0, :, ls] = ci

                def fix(j, carry, base=base, ls=ls, init_r=init_r, init_i=init_i):
                    rows = pl.ds(pl.multiple_of(base + j * V7X_SUBLANES, V7X_SUBLANES), V7X_SUBLANES)
                    pr = ptab_ref[0, pl.ds(j, 1), ls]
                    pim = ptab_ref[1, pl.ds(j, 1), ls]
                    ur_scr[rows, ls] = ur_scr[rows, ls] + (pr * init_r - pim * init_i)
                    ui_scr[rows, ls] = ui_scr[rows, ls] + (pr * init_i + pim * init_r)
                    return carry

                lax.fori_loop(0, lc, fix, 0, unroll=4)

    for j in range(8):
        hr16 = ur_scr[:, 512 * j:512 * (j + 1)].astype(BF16)
        hi16 = ui_scr[:, 512 * j:512 * (j + 1)].astype(BF16)
        y_scr[:, 128 * j:128 * (j + 1)] = _dot(hr16, wyr_ref[j]) - _dot(hi16, wyi_ref[j])
    yh, ym, yl = _split3(y_scr[...])
    pt = permt_ref[...]
    y = (_dot(pt, yh) + _dot(pt, ym) + _dot(pt, yl)) + dsk_ref[...] * xa_ref[...]
    ga = _gelu_tanh(y)
    ya = ga * _sigmoid(_dot(ga.astype(BF16), wglu_ref[...]) + bglu_ref[...])
    mix_ref[...] = (ya * _silu(za_ref[...])).astype(BF16)

    @pl.when(tc == ntc - 1)
    def _():
        if split:
            hro_ref[...] = cr_scr[...]
            hio_ref[...] = ci_scr[...]
        else:
            for s in range(n_sets):
                for k in range(V7X_SUBLANES):
                    r = s * set_rows + V7X_SUBLANES * t_last + k
                    hro_ref[s * V7X_SUBLANES + k] = ur_scr[r:r + 1, :]
                    hio_ref[s * V7X_SUBLANES + k] = ui_scr[r:r + 1, :]


def s5_mixer(proj, h0_re, h0_im, coef, ptab, wur, wui, wyr, wyi, dskip, wglu, bglu, layer, state_layer,
             *, batch, seq, t_valid, nb, tt):
    ntc = seq // tt
    assert nb == 1 or ntc == 1
    rows = nb * tt
    grid = (batch // nb, ntc)
    t_last = (t_valid - 1) - (ntc - 1) * tt
    assert 0 <= t_last < tt
    if nb == 1:
        assert tt == V7X_SUBLANES * S5_SEG and t_last == tt - 1
        lc = S5_SEG
    else:
        assert nb % V7X_SUBLANES == 0
        lc = tt
    idx = np.arange(rows)
    set_rows = V7X_SUBLANES * lc
    src = (idx // set_rows) * set_rows + (idx % V7X_SUBLANES) * lc + (idx % set_rows) // V7X_SUBLANES
    perm_np = np.zeros((rows, rows), np.float32)
    perm_np[idx, src] = 1.0
    perm = jnp.asarray(perm_np, BF16)
    perm_t = jnp.asarray(perm_np.T, BF16)
    sq = pl.BlockSpec((rows, rows), lambda b, t: (0, 0))

    def tok(cb):
        return pl.BlockSpec((rows, W_MIX), lambda b, t: (b * ntc + t, cb))

    st_in = pl.BlockSpec((None, nb, 1, S5_CH), lambda b, t: (state_layer, b, 0, 0))
    st_out = pl.BlockSpec((nb, 1, S5_CH), lambda b, t: (b, 0, 0))

    def lw(shape):
        nd = len(shape)
        return pl.BlockSpec((None,) + shape, lambda b, t: (layer,) + (0,) * nd)

    return pl.pallas_call(
        functools.partial(_s5_kernel, nb=nb, tt=tt, t_last=t_last),
        grid=grid,
        in_specs=[tok(COL_XA // W_MIX), tok(COL_ZA // W_MIX), st_in, st_in,
                  lw((4, V7X_SUBLANES, S5_CH)), lw((2, S5_SEG, S5_CH)), sq, sq, lw((4, 256, 1024)), lw((4, 256, 1024)),
                  lw((8, 512, 128)), lw((8, 512, 128)), lw((1, W_MIX)), lw((W_MIX, W_MIX)), lw((1, W_MIX))],
        out_specs=[pl.BlockSpec((rows, W_MIX), lambda b, t: (b * ntc + t, 0)), st_out, st_out],
        out_shape=[jax.ShapeDtypeStruct((batch * seq, W_MIX), BF16),
                   jax.ShapeDtypeStruct((batch, 1, S5_CH), F32),
                   jax.ShapeDtypeStruct((batch, 1, S5_CH), F32)],
        scratch_shapes=[pltpu.VMEM((rows, S5_CH), F32), pltpu.VMEM((rows, S5_CH), F32),
                        pltpu.VMEM((nb, 1, S5_CH), F32), pltpu.VMEM((nb, 1, S5_CH), F32),
                        pltpu.VMEM((rows, W_MIX), F32)],
        compiler_params=_cparams(2),
        name="s5_mixer",
    )(proj, proj, h0_re, h0_im, coef, ptab, perm, perm_t, wur, wui, wyr, wyi, dskip, wglu, bglu)


LRU_STRIP = 512
CONV_PAD = V7X_SUBLANES


def _causal_conv(xp_scr, tail_scr, x, cw_ref, bi):
    del xp_scr
    rows_n = x.shape[0]
    width = x.shape[1]
    n_tiles = rows_n // V7X_SUBLANES
    xe = jnp.concatenate([tail_scr[bi], x], axis=0).reshape(n_tiles + 1, V7X_SUBLANES, width)
    row = lax.broadcasted_iota(jnp.int32, (n_tiles, V7X_SUBLANES, width), 1)
    acc = cw_ref[CONV_W - 1:CONV_W, :] * x
    for jj in range(CONV_W - 1):
        shift = CONV_W - 1 - jj
        rot = pltpu.roll(xe, shift, 1)
        shifted = jnp.where(row < shift, rot[:-1], rot[1:]).reshape(rows_n, width)
        acc = acc + cw_ref[jj:jj + 1, :] * shifted
    tail_scr[bi] = xe[n_tiles]
    return acc


def _init_conv_tail(tail_scr, cv0_ref, nb):
    tail_scr[...] = jnp.zeros(tail_scr.shape, F32)
    for bi in range(nb):
        tail_scr[bi, CONV_PAD - (CONV_W - 1):CONV_PAD, :] = cv0_ref[bi]


def _lru_kernel(xb_ref, zb_ref, h0_ref, cv0_ref, cw_ref, cb_ref, wri_ref, br_ref, bi_ref, lam_ref,
                mix_ref, ho_ref,
                xp_scr, tail_scr, xc_scr, a_scr, b_scr, c_scr, *, nb, tt, t_last):
    tc = pl.program_id(1)
    ntc = pl.num_programs(1)

    @pl.when(tc == 0)
    def _():
        c_scr[...] = h0_ref[...]
        _init_conv_tail(tail_scr, cv0_ref, nb)

    def conv_body(bi, carry):
        rows = pl.ds(pl.multiple_of(bi * tt, V7X_SUBLANES), tt)
        acc = _causal_conv(xp_scr, tail_scr, xb_ref[rows, :], cw_ref, bi)
        xc_scr[rows, :] = acc + cb_ref[...]
        return carry

    lax.fori_loop(0, nb, conv_body, 0)

    xc = xc_scr[...]
    xc16 = xc.astype(BF16)
    sp = _softplus(-lam_ref[...])
    for blk in range(LRU_BLOCKS):
        cs = slice(blk * LRU_BLK, (blk + 1) * LRU_BLK)
        pre = _dot(xc16[:, cs], wri_ref[blk])
        r = _sigmoid(pre[:, :LRU_BLK] + br_ref[:, cs])
        ig = _sigmoid(pre[:, LRU_BLK:] + bi_ref[:, cs])
        log_a = (-LRU_C) * r * sp[:, cs]
        a_scr[:, cs] = jnp.exp(log_a)
        b_scr[:, cs] = jnp.sqrt(-_expm1(2.0 * log_a)) * (ig * xc[:, cs])

    ng = tt // V7X_SUBLANES
    for s in range(W_MIX // LRU_STRIP):
        ls = slice(s * LRU_STRIP, (s + 1) * LRU_STRIP)
        row = lax.broadcasted_iota(jnp.int32, (V7X_SUBLANES, LRU_STRIP), 0)

        def seq_body(bi, carry, ls=ls, row=row):
            def tile_body(g, c):
                rows = pl.ds(pl.multiple_of(bi * tt + g * V7X_SUBLANES, V7X_SUBLANES), V7X_SUBLANES)
                a = a_scr[rows, ls]
                b = b_scr[rows, ls]
                for k in (1, 2, 4):
                    a_sh = jnp.where(row >= k, pltpu.roll(a, k, 0), 1.0)
                    b_sh = jnp.where(row >= k, pltpu.roll(b, k, 0), 0.0)
                    b = b + a * b_sh
                    a = a * a_sh
                h = b + a * c
                b_scr[rows, ls] = h
                return h[V7X_SUBLANES - 1:, :]

            c = lax.fori_loop(0, ng, tile_body, c_scr[bi, :, ls])
            c_scr[bi, :, ls] = c
            return carry

        lax.fori_loop(0, nb, seq_body, 0)

    mix_ref[...] = (b_scr[...] * _silu(zb_ref[...])).astype(BF16)

    @pl.when(tc == ntc - 1)
    def _():
        for bi in range(nb):
            r = bi * tt + t_last
            ho_ref[bi] = b_scr[r:r + 1, :]


def lru_mixer(proj, h0, conv0, cw, cb, wri, br, bi_, lam, layer, state_layer, *, batch, seq, t_valid, nb, tt):
    ntc = seq // tt
    assert nb == 1 or ntc == 1
    rows = nb * tt
    grid = (batch // nb, ntc)
    t_last = (t_valid - 1) - (ntc - 1) * tt

    def tok(cb_):
        return pl.BlockSpec((rows, W_MIX), lambda b, t: (b * ntc + t, cb_))

    def lw(shape):
        nd = len(shape)
        return pl.BlockSpec((None,) + shape, lambda b, t: (layer,) + (0,) * nd)

    return pl.pallas_call(
        functools.partial(_lru_kernel, nb=nb, tt=tt, t_last=t_last),
        grid=grid,
        in_specs=[tok(COL_XB // W_MIX), tok(COL_ZB // W_MIX),
                  pl.BlockSpec((None, nb, 1, W_MIX), lambda b, t: (state_layer, b, 0, 0)),
                  pl.BlockSpec((None, nb, CONV_W - 1, W_MIX), lambda b, t: (state_layer, b, 0, 0)),
                  lw((CONV_W, W_MIX)), lw((1, W_MIX)), lw((LRU_BLOCKS, LRU_BLK, 2 * LRU_BLK)),
                  lw((1, W_MIX)), lw((1, W_MIX)), lw((1, W_MIX))],
        out_specs=[pl.BlockSpec((rows, W_MIX), lambda b, t: (b * ntc + t, 0)),
                   pl.BlockSpec((nb, 1, W_MIX), lambda b, t: (b, 0, 0))],
        out_shape=[jax.ShapeDtypeStruct((batch * seq, W_MIX), BF16),
                   jax.ShapeDtypeStruct((batch, 1, W_MIX), F32)],
        scratch_shapes=[pltpu.VMEM((CONV_PAD + tt, W_MIX), F32),
                        pltpu.VMEM((nb, CONV_PAD, W_MIX), F32),
                        pltpu.VMEM((rows, W_MIX), F32), pltpu.VMEM((rows, W_MIX), F32),
                        pltpu.VMEM((rows, W_MIX), F32), pltpu.VMEM((nb, 1, W_MIX), F32)],
        compiler_params=_cparams(2),
        name="lru_mixer",
    )(proj, proj, h0, conv0, cw, cb, wri, br, bi_, lam)


SEQ_GROUP = 2


def _for_each_group(nb, group_body):
    g = SEQ_GROUP if nb % SEQ_GROUP == 0 else 1
    if nb == g:
        group_body(list(range(nb)))
    else:
        def body(i, carry):
            group_body([i * g + j for j in range(g)])
            return carry
        lax.fori_loop(0, nb // g, body, 0)


def _gla_kernel(q_ref, k_ref, v_ref, z_ref, tail_ref, s0_ref, wg_ref, bg_ref, go_ref, stack_ref,
                mix_ref, so_ref,
                s_scr, mix_scr, *, nb, c, cv, single_chunk):
    del stack_ref
    tc = pl.program_id(1)
    ntc = pl.num_programs(1)
    sb = min(16, c)
    nblk = c // sb

    s_src, s_dst = (s0_ref, so_ref) if single_chunk else (s_scr, s_scr)
    if not single_chunk:
        @pl.when(tc == 0)
        def _():
            s_scr[...] = s0_ref[...]

    row_c = lax.broadcasted_iota(jnp.int32, (c, 1), 0)
    lane_c = lax.broadcasted_iota(jnp.int32, (sb, c), 1)
    row_sb = lax.broadcasted_iota(jnp.int32, (sb, 1), 0)

    def group_body(bis):
        pairs = [(g, h) for g in range(len(bis)) for h in range(GLA_H)]
        heads = range(len(pairs))
        b_all, q_all, k_all, v_all = [], [], [], []
        for bi in bis:
            x = _dot(tail_ref[bi].astype(BF16), wg_ref[...]) + bg_ref[...]
            b_all.append(_cumsum_rows(_log_sigmoid(x) * (1.0 / GLA_TAU)))
            q_all.append(q_ref[bi])
            k_all.append(k_ref[bi])
            v_all.append(v_ref[bi])
        s_old = [s_src[bis[g], h] for g, h in pairs]
        q = [q_all[g][:, h * GLA_DK:(h + 1) * GLA_DK] * (GLA_DK ** -0.5) for g, h in pairs]
        k = [k_all[g][:, h * GLA_DK:(h + 1) * GLA_DK] for g, h in pairs]
        v = [v_all[g][:, h * GLA_DV:(h + 1) * GLA_DV] for g, h in pairs]
        b = [b_all[g][:, h * GLA_DK:(h + 1) * GLA_DK] for g, h in pairs]
        o_state = [_dot((q[h] * jnp.exp(b[h])).astype(BF16), s_old[h].astype(BF16)) for h in heads]
        s_new = []
        for h in heads:
            b_last = b[h][cv - 1:cv]
            dec = b_last - b[h]
            if cv < c:
                dec = jnp.where(row_c < cv, dec, NEG_INF)
            kd = k[h] * jnp.exp(dec)
            d_col = jnp.transpose(jnp.broadcast_to(jnp.exp(b_last), (V7X_SUBLANES, GLA_DK)))[:, 0:1]
            s_new.append(d_col * s_old[h] + _mm(kd, v[h], dot=_dot_tn, contract=0))
        att_off = []
        for h in heads:
            per_blk = [jnp.zeros((sb, c), F32)]
            for blk in range(1, nblk):
                r0 = blk * sb
                b_ref_row = b[h][r0 - 1:r0]
                qs = (q[h][r0:r0 + sb] * jnp.exp(b[h][r0:r0 + sb] - b_ref_row)).astype(BF16)
                kd = (k[h] * jnp.exp(jnp.where(row_c < r0, b_ref_row - b[h], NEG_INF))).astype(BF16)
                per_blk.append(_dot_nt(qs, kd))
            att_off.append(per_blk)
        att = []
        for h in heads:
            att_rows = []
            for blk in range(nblk):
                r0 = blk * sb
                q_i = q[h][r0:r0 + sb]
                b_i = b[h][r0:r0 + sb]
                a = att_off[h][blk]
                for sl in range(sb):
                    s_abs = r0 + sl
                    e = jnp.exp(jnp.where(row_sb >= sl, b_i - b[h][s_abs:s_abs + 1], NEG_INF))
                    col = jnp.sum(q_i * (k[h][s_abs:s_abs + 1] * e), axis=-1, keepdims=True)
                    a = jnp.where(lane_c == s_abs, col, a)
                att_rows.append(a)
            att.append(att_rows[0] if nblk == 1 else jnp.concatenate(att_rows, axis=0))
        o = [_mm(att[h], v[h]) + o_state[h] for h in heads]
        for p, (g, h) in enumerate(pairs):
            vs = slice(h * GLA_DV, (h + 1) * GLA_DV)
            ms = jnp.mean(o[p] * o[p], axis=-1, keepdims=True)
            on = (o[p] * lax.rsqrt(ms + EPS)) * go_ref[...]
            mix_scr[bis[g], :, vs] = on * _silu(z_ref[bis[g], :, vs])
            s_dst[bis[g], h] = s_new[p]

    _for_each_group(nb, group_body)
    mix_ref[...] = mix_scr[...].astype(BF16)

    if not single_chunk:
        @pl.when(tc == ntc - 1)
        def _():
            so_ref[...] = s_scr[...]


def _stack_alias(stack, n_inputs_before):
    spec = pl.BlockSpec(memory_space=pl.ANY)
    if stack is None:
        return jnp.zeros((V7X_SUBLANES, V7X_LANES), F32), spec, {}
    return stack, spec, {n_inputs_before: 1}


def gla_mixer(proj, tail, s0, wg, bg, go, layer, state_layer, *, batch, seq, c, cv, nb, n_layers, stack=None):
    ntc = seq // c
    grid = (batch // nb, ntc)
    proj3 = proj.reshape(batch, seq, proj.shape[-1])
    tail3 = tail.reshape(batch, seq, TAIL_W)

    def tok(width, col):
        return pl.BlockSpec((nb, c, width), lambda b, t: (b, t, col // width))

    def lw(shape):
        nd = len(shape)
        return pl.BlockSpec((None,) + shape, lambda b, t: (layer,) + (0,) * nd)

    st_shape = (nb, GLA_H, GLA_DK, GLA_DV)
    stack_arg, stack_spec, aliases = _stack_alias(stack, 9)
    mix, state = pl.pallas_call(
        functools.partial(_gla_kernel, nb=nb, c=c, cv=cv, single_chunk=ntc == 1),
        grid=grid,
        in_specs=[tok(GLA_H * GLA_DK, COL_QC), tok(GLA_H * GLA_DK, COL_KC), tok(W_MIX, COL_VC), tok(W_MIX, COL_ZC),
                  tok(TAIL_W, 0),
                  pl.BlockSpec((None,) + st_shape, lambda b, t: (state_layer, b, 0, 0, 0)),
                  lw((TAIL_W, GLA_H * GLA_DK)), lw((1, GLA_H * GLA_DK)), lw((1, GLA_DV)), stack_spec],
        out_specs=[tok(W_MIX, 0),
                   pl.BlockSpec((None,) + st_shape, lambda b, t: (layer, b, 0, 0, 0))],
        out_shape=[jax.ShapeDtypeStruct((batch, seq, W_MIX), BF16),
                   jax.ShapeDtypeStruct((n_layers, batch, GLA_H, GLA_DK, GLA_DV), F32)],
        scratch_shapes=[pltpu.VMEM(st_shape, F32), pltpu.VMEM((nb, c, W_MIX), F32)],
        input_output_aliases=aliases,
        compiler_params=_cparams(2),
        name="gla_mixer",
    )(proj3, proj3, proj3, proj3, tail3, s0, wg, bg, go, stack_arg)
    return mix.reshape(batch * seq, W_MIX), state


def _l2norm(x):
    return x * lax.rsqrt(jnp.sum(x * x, axis=-1, keepdims=True) + EPS)


def _unit_lower_solve(ms, rhss, c, cv):
    if c <= V7X_SUBLANES:
        row = lax.broadcasted_iota(jnp.int32, (c, 1), 0)
        us = []
        for m, rhs in zip(ms, rhss):
            u = jnp.zeros_like(rhs)
            u_rows = []
            for t in range(cv):
                ut = rhs[t:t + 1]
                for s in range(t):
                    ut = ut - m[t:t + 1, s:s + 1] * u_rows[s]
                u_rows.append(ut)
                u = jnp.where(row == t, ut, u)
            us.append(u)
        return us
    xs = [-m for m in ms]
    ys = list(rhss)
    dv = rhss[0].shape[1]
    levels = int(math.log2(c))
    for lvl in range(levels):
        last = lvl == levels - 1
        ps = [_dot_x3(x, y if last else jnp.concatenate([y, x], axis=1)) for x, y in zip(xs, ys)]
        ys = [y + p[:, :dv] for y, p in zip(ys, ps)]
        if not last:
            xs = [p[:, dv:] for p in ps]
    return ys


def _gdn_kernel(qkv_ref, z_ref, tail_ref, s0_ref, cv0_ref, cw_ref, alog_ref, dtb_ref, go_ref, stack_ref,
                mix_ref, so_ref,
                s_scr, xp_scr, tail_scr, mix_scr, *, nb, c, cv, single_chunk):
    del stack_ref
    tc = pl.program_id(1)
    ntc = pl.num_programs(1)

    s_src, s_dst = (s0_ref, so_ref) if single_chunk else (s_scr, s_scr)

    @pl.when(tc == 0)
    def _():
        if not single_chunk:
            s_scr[...] = s0_ref[...]
        _init_conv_tail(tail_scr, cv0_ref, nb)

    row_c = lax.broadcasted_iota(jnp.int32, (c, 1), 0)
    ri = lax.broadcasted_iota(jnp.int32, (c, c), 0)
    ci = lax.broadcasted_iota(jnp.int32, (c, c), 1)
    lane_t = lax.broadcasted_iota(jnp.int32, (c, TAIL_W), 1)
    a_lanes = (lane_t >= TAIL_A) & (lane_t < TAIL_A + DN_H)

    def group_body(bis):
        pairs = [(g, h) for g in range(len(bis)) for h in range(DN_H)]
        heads = range(len(pairs))
        qkv, gam, gam_t, beta = [], [], [], []
        for gi, bi in enumerate(bis):
            qkv.append(_silu(_causal_conv(xp_scr.at[gi], tail_scr, qkv_ref[bi], cw_ref, bi)))
            tl = tail_ref[bi]
            gg = _cumsum_rows(jnp.where(a_lanes, -jnp.exp(alog_ref[...]) * _softplus(tl + dtb_ref[...]), 0.0))
            gam.append(gg)
            gam_t.append(jnp.transpose(gg))
            beta.append(_sigmoid(tl))
        s_old = [s_src[bis[g], h] for g, h in pairs]
        s16 = [s.astype(BF16) for s in s_old]
        q16, k16, kf, v = [], [], [], []
        for g, h in pairs:
            q = _l2norm(qkv[g][:, h * DN_DK:(h + 1) * DN_DK]) * (DN_DK ** -0.5)
            k = _l2norm(qkv[g][:, DN_H * DN_DK + h * DN_DK:DN_H * DN_DK + (h + 1) * DN_DK])
            q16.append(q.astype(BF16))
            k16.append(k.astype(BF16))
            kf.append(k)
            v.append(qkv[g][:, 2 * DN_H * DN_DK + h * DN_DV:2 * DN_H * DN_DK + (h + 1) * DN_DV])
        g_col = [gam[g][:, TAIL_A + h:TAIL_A + h + 1] for g, h in pairs]
        g_row = [gam_t[g][TAIL_A + h:TAIL_A + h + 1, :] for g, h in pairs]
        b_col = [beta[g][:, TAIL_B + h:TAIL_B + h + 1] for g, h in pairs]
        decay = [jnp.exp(jnp.where(ri >= ci, g_col[h] - g_row[h], NEG_INF)) for h in heads]
        eg = [jnp.exp(g_col[h]) for h in heads]
        kk = [_dot_nt(k16[h], k16[h]) for h in heads]
        k_s = [_dot(k16[h], s16[h]) for h in heads]
        q_s = [_dot(q16[h], s16[h]) for h in heads]
        qk = [_dot_nt(q16[h], k16[h]) for h in heads]
        m = [jnp.where(ri > ci, b_col[h] * kk[h] * decay[h], 0.0) for h in heads]
        rhs = [b_col[h] * (v[h] - eg[h] * k_s[h]) for h in heads]
        u = _unit_lower_solve(m, rhs, c, cv)
        o = [eg[h] * q_s[h] + _mm(qk[h] * decay[h], u[h]) for h in heads]
        kd = []
        for h in heads:
            dec = g_col[h][cv - 1:cv] - g_col[h]
            if cv < c:
                dec = jnp.where(row_c < cv, dec, NEG_INF)
            kd.append(kf[h] * jnp.exp(dec))
        s_new = [jnp.exp(g_col[h][cv - 1:cv]) * s_old[h] + _mm(kd[h], u[h], dot=_dot_tn, contract=0) for h in heads]
        for p, (g, h) in enumerate(pairs):
            vs = slice(h * DN_DV, (h + 1) * DN_DV)
            ms = jnp.mean(o[p] * o[p], axis=-1, keepdims=True)
            on = (o[p] * lax.rsqrt(ms + EPS)) * go_ref[...]
            mix_scr[bis[g], :, vs] = on * _silu(z_ref[bis[g], :, vs])
            s_dst[bis[g], h] = s_new[p]

    _for_each_group(nb, group_body)
    mix_ref[...] = mix_scr[...].astype(BF16)

    if not single_chunk:
        @pl.when(tc == ntc - 1)
        def _():
            so_ref[...] = s_scr[...]


def gdn_mixer(proj, tail, s0, conv0, cw, alog, dtb, go, layer, state_layer, *, batch, seq, c, cv, nb, n_layers,
              stack=None):
    ntc = seq // c
    grid = (batch // nb, ntc)
    proj3 = proj.reshape(batch, seq, proj.shape[-1])
    tail3 = tail.reshape(batch, seq, TAIL_W)

    def tok(width, col):
        return pl.BlockSpec((nb, c, width), lambda b, t: (b, t, col // width))

    def lw(shape):
        nd = len(shape)
        return pl.BlockSpec((None,) + shape, lambda b, t: (layer,) + (0,) * nd)

    st_shape = (nb, DN_H, DN_DK, DN_DV)
    assert COL_QKVD % DN_QKV == 0
    group = SEQ_GROUP if nb % SEQ_GROUP == 0 else 1
    stack_arg, stack_spec, aliases = _stack_alias(stack, 9)
    mix, state = pl.pallas_call(
        functools.partial(_gdn_kernel, nb=nb, c=c, cv=cv, single_chunk=ntc == 1),
        grid=grid,
        in_specs=[tok(DN_QKV, COL_QKVD),
                  tok(W_MIX, COL_ZD),
                  tok(TAIL_W, 0),
                  pl.BlockSpec((None,) + st_shape, lambda b, t: (state_layer, b, 0, 0, 0)),
                  pl.BlockSpec((None, nb, CONV_W - 1, DN_QKV), lambda b, t: (state_layer, b, 0, 0)),
                  lw((CONV_W, DN_QKV)), lw((1, TAIL_W)), lw((1, TAIL_W)), lw((1, DN_DV)), stack_spec],
        out_specs=[tok(W_MIX, 0),
                   pl.BlockSpec((None,) + st_shape, lambda b, t: (layer, b, 0, 0, 0))],
        out_shape=[jax.ShapeDtypeStruct((batch, seq, W_MIX), BF16),
                   jax.ShapeDtypeStruct((n_layers, batch, DN_H, DN_DK, DN_DV), F32)],
        input_output_aliases=aliases,
        scratch_shapes=[pltpu.VMEM(st_shape, F32),
                        pltpu.VMEM((group, CONV_PAD + c, DN_QKV), F32),
                        pltpu.VMEM((nb, CONV_PAD, DN_QKV), F32),
                        pltpu.VMEM((nb, c, W_MIX), F32)],
        compiler_params=_cparams(2),
        name="gdn_mixer",
    )(proj3, proj3, tail3, s0, conv0, cw, alog, dtb, go, stack_arg)
    return mix.reshape(batch * seq, W_MIX), state


def _prepare_weights(w_in, s5_lam_re, s5_lam_im, s5_log_dt, s5_b_re, s5_b_im, s5_c_re, s5_c_im, s5_d,
                     s5_w_glu, lru_w_r, lru_w_i, gla_w_gate, dn_a_log, dn_dt_bias, w_out, ple_w, ple_gate_w):
    nl = w_in.shape[0]
    p = {}
    p["w_main"], p["w_tail"] = repack_w_in(jnp.swapaxes(w_in, 1, 2))
    p["w_out"] = w_out.astype(BF16)
    p["ple_w"] = ple_w.astype(BF16)
    p["ple_gate_w"] = ple_gate_w.astype(BF16)
    p["w_glu"] = s5_w_glu.astype(BF16)
    ldt = jnp.repeat(s5_log_dt, S5_STATE, axis=1).reshape(nl, 1, S5_CH)
    coef, ptab, cre, cim = s5_prep(s5_lam_re.reshape(nl, 1, S5_CH), s5_lam_im.reshape(nl, 1, S5_CH), ldt)
    bb_re, bb_im = s5_bb(cre.reshape(nl, S5_CH, 1), cim.reshape(nl, S5_CH, 1),
                         s5_b_re.reshape(nl, S5_CH, S5_GROUP), s5_b_im.reshape(nl, S5_CH, S5_GROUP))
    eye16 = jnp.eye(16, dtype=F32)
    eye8 = jnp.eye(8, dtype=F32)

    def pack_u(bb):
        t = bb.reshape(nl, 4, 16, S5_STATE, S5_GROUP)
        return jnp.einsum("lkgph,gG->lkghGp", t, eye16).reshape(nl, 4, 256, 1024).astype(BF16)

    def pack_y(cc):
        t = cc.reshape(nl, 8, 8, S5_GROUP, S5_STATE)
        return jnp.einsum("ljghp,gG->ljgpGh", t, eye8).reshape(nl, 8, 512, 128).astype(BF16)

    p["s5_coef"] = coef
    p["s5_ptab"] = ptab
    p["s5_wur"], p["s5_wui"] = pack_u(bb_re), pack_u(bb_im)
    p["s5_wyr"], p["s5_wyi"] = pack_y(s5_c_re), pack_y(s5_c_im)
    p["s5_d"] = s5_d.reshape(nl, 1, W_MIX)
    p["lru_wri"] = jnp.concatenate([lru_w_r, lru_w_i], axis=-1).astype(BF16)
    p["gla_wg"] = jnp.concatenate([gla_w_gate, jnp.zeros((nl, TAIL_W - GLA_RANK, GLA_H * GLA_DK), F32)],
                                  axis=1).astype(BF16)

    def tail_row(x):
        return jnp.pad(x, ((0, 0), (TAIL_A, TAIL_W - TAIL_A - DN_H))).reshape(nl, 1, TAIL_W)

    p["dn_alog"] = tail_row(dn_a_log)
    p["dn_dtb"] = tail_row(dn_dt_bias)
    return p


def kernel(x_prompt, x_sample, state_s5_re, state_s5_im, state_lru_h, state_lru_conv, state_gla, state_delta, state_delta_conv, p_prompt, p_sample, g_norm, w_in, s5_lam_re, s5_lam_im, s5_log_dt, s5_b_re, s5_b_im, s5_c_re, s5_c_im, s5_d, s5_w_glu, s5_b_glu, lru_conv_w, lru_conv_b, lru_w_r, lru_b_r, lru_w_i, lru_b_i, lru_lam, gla_w_gate, gla_b_gate, gla_g_out, dn_conv_w, dn_a_log, dn_dt_bias, dn_g_out, w_out, ple_w, ple_gate_w, ple_gate_b, g_final):
    nl = w_in.shape[0]
    bp, tp, _ = x_prompt.shape
    bs, ts, _ = x_sample.shape
    tsp = SAMPLE_T_PAD
    mp, ms = bp * tp, bs * ts

    p = _prepare_weights(w_in, s5_lam_re, s5_lam_im, s5_log_dt, s5_b_re, s5_b_im, s5_c_re, s5_c_im, s5_d,
                         s5_w_glu, lru_w_r, lru_w_i, gla_w_gate, dn_a_log, dn_dt_bias, w_out, ple_w, ple_gate_w)

    def vec(x):
        return x.reshape(nl, 1, x.shape[-1])

    g_norm3, b_glu3, cb3 = vec(g_norm), vec(s5_b_glu), vec(lru_conv_b)
    br3, bi3, lam3 = vec(lru_b_r), vec(lru_b_i), vec(lru_lam)
    bg3, go_c3, go_d3, pgb3 = vec(gla_b_gate), vec(gla_g_out), vec(dn_g_out), vec(ple_gate_b)
    pe_p = p_prompt.reshape(nl, mp, D_PLE)
    pe_s = p_sample.reshape(nl, ms, D_PLE)

    z_s5 = jnp.zeros((1, bp, 1, S5_CH), F32)
    z_lru = jnp.zeros((1, bp, 1, W_MIX), F32)
    z_lconv = jnp.zeros((1, bp, CONV_W - 1, W_MIX), F32)
    z_gla = jnp.zeros((1, bp, GLA_H, GLA_DK, GLA_DV), F32)
    z_dn = jnp.zeros((1, bp, DN_H, DN_DK, DN_DV), F32)
    z_dconv = jnp.zeros((1, bp, CONV_W - 1, DN_QKV), F32)
    c_s5r = state_s5_re.reshape(nl, bs, 1, S5_CH)
    c_s5i = state_s5_im.reshape(nl, bs, 1, S5_CH)
    c_lru = state_lru_h.reshape(nl, bs, 1, W_MIX)

    def mixers(proj, tail, layer, *, batch, seq, t_valid, st, sl, nb_scan, tt, c, nb_mat, stacks):
        s5r0, s5i0, lru0, lconv0, gla0, dn0, dconv0 = st
        gla_stack, dn_stack = stacks
        mix_a, s5r, s5i = s5_mixer(proj, s5r0, s5i0, p["s5_coef"], p["s5_ptab"], p["s5_wur"], p["s5_wui"], p["s5_wyr"],
                                   p["s5_wyi"], p["s5_d"], p["w_glu"], b_glu3, layer, sl,
                                   batch=batch, seq=seq, t_valid=t_valid, nb=nb_scan, tt=tt)
        mix_b, lruh = lru_mixer(proj, lru0, lconv0, lru_conv_w, cb3, p["lru_wri"], br3, bi3, lam3, layer, sl,
                                batch=batch, seq=seq, t_valid=t_valid, nb=nb_scan, tt=tt)
        mix_c, glas = gla_mixer(proj, tail, gla0, p["gla_wg"], bg3, go_c3, layer, sl,
                                batch=batch, seq=seq, c=c, cv=min(c, t_valid), nb=nb_mat, n_layers=nl,
                                stack=gla_stack)
        mix_d, dns = gdn_mixer(proj, tail, dn0, dconv0, dn_conv_w, p["dn_alog"], p["dn_dtb"], go_d3, layer, sl,
                               batch=batch, seq=seq, c=c, cv=min(c, t_valid), nb=nb_mat, n_layers=nl,
                               stack=dn_stack)
        return (mix_a, mix_b, mix_c, mix_d), (s5r, s5i, lruh, glas, dns)

    hp = x_prompt.reshape(mp, D_MODEL)
    hs = x_sample.reshape(ms, D_MODEL)
    new_p, new_s = [], []
    stacks_p = stacks_s = (None, None)
    hgp, ssqp = prenorm(hp, g_norm3, 0, tm=DENSE_TILES_SAMPLE["tm"])
    hgs, ssqs = prenorm(hs, g_norm3, 0, tm=DENSE_TILES_SAMPLE["tm"])
    for i in range(nl):
        nxt = min(i + 1, nl - 1)
        proj, tail = in_proj(hgp, ssqp, p["w_main"], p["w_tail"], i, **DENSE_TILES_PROMPT)
        mixes, (s5r, s5i, lruh, glas, dns) = mixers(
            proj, tail, i, batch=bp, seq=tp, t_valid=tp,
            st=(z_s5, z_s5, z_lru, z_lconv, z_gla, z_dn, z_dconv), sl=0,
            nb_scan=1, tt=256, c=GLA_CHUNK, nb_mat=SEQ_GROUP, stacks=stacks_p)
        stacks_p = (glas, dns)
        proj3 = proj.reshape(bp, tp, N_MAIN)
        new_p.append((s5r.reshape(bp, S5_GROUPS, S5_STATE), s5i.reshape(bp, S5_GROUPS, S5_STATE),
                      lruh.reshape(bp, W_MIX), proj3[:, tp - (CONV_W - 1):, COL_XB:COL_XB + W_MIX],
                      None, None, proj3[:, tp - (CONV_W - 1):, COL_QKVD:COL_QKVD + DN_QKV]))
        hp, hpb = out_proj(mixes, p["w_out"], hp, i, **DENSE_TILES_PROMPT)
        hp, hgp, ssqp = ple(hp, hpb, pe_p, p["ple_gate_w"], pgb3, p["ple_w"], g_norm3, i, nxt, **DENSE_TILES_PROMPT)

        proj, tail = in_proj(hgs, ssqs, p["w_main"], p["w_tail"], i, **DENSE_TILES_SAMPLE)
        proj3 = proj.reshape(bs, ts, N_MAIN)
        proj_pad = jnp.pad(proj3, ((0, 0), (0, tsp - ts), (0, 0))).reshape(bs * tsp, N_MAIN)
        tail_pad = jnp.pad(tail.reshape(bs, ts, TAIL_W), ((0, 0), (0, tsp - ts), (0, 0))).reshape(bs * tsp, TAIL_W)
        mixes, (s5r, s5i, lruh, glas, dns) = mixers(
            proj_pad, tail_pad, i, batch=bs, seq=tsp, t_valid=ts,
            st=(c_s5r, c_s5i, c_lru, state_lru_conv, state_gla, state_delta, state_delta_conv), sl=i,
            nb_scan=32, tt=tsp, c=tsp, nb_mat=8, stacks=stacks_s)
        stacks_s = (glas, dns)
        mixes = tuple(m.reshape(bs, tsp, W_MIX)[:, :ts].reshape(ms, W_MIX) for m in mixes)
        new_s.append((s5r.reshape(bs, S5_GROUPS, S5_STATE), s5i.reshape(bs, S5_GROUPS, S5_STATE),
                      lruh.reshape(bs, W_MIX), proj3[:, ts - (CONV_W - 1):, COL_XB:COL_XB + W_MIX],
                      None, None, proj3[:, ts - (CONV_W - 1):, COL_QKVD:COL_QKVD + DN_QKV]))
        hs, hsb = out_proj(mixes, p["w_out"], hs, i, **DENSE_TILES_SAMPLE)
        hs, hgs, ssqs = ple(hs, hsb, pe_s, p["ple_gate_w"], pgb3, p["ple_w"], g_norm3, i, nxt, **DENSE_TILES_SAMPLE)

    g_fin = g_final.reshape(1, D_MODEL)
    y_prompt = final_norm(hp, g_fin, tm=256).reshape(bp, tp, D_MODEL)
    y_sample = final_norm(hs, g_fin, tm=256).reshape(bs, ts, D_MODEL)

    def stk(lst, j):
        return jnp.stack([s[j] for s in lst], axis=0)

    return (y_prompt, y_sample,
            stk(new_p, 0), stk(new_p, 1), stk(new_p, 2), stk(new_p, 3), stacks_p[0], stacks_p[1], stk(new_p, 6),
            stk(new_s, 0), stk(new_s, 1), stk(new_s, 2), stk(new_s, 3), stacks_s[0], stacks_s[1], stk(new_s, 6))
```

```python
import functools
import math

import jax
import jax.numpy as jnp
import numpy as np
from jax import lax
from jax.experimental import pallas as pl
from jax.experimental.pallas import tpu as pltpu

F32 = jnp.float32
BF16 = jnp.bfloat16
EPS = 1e-6
NEG_INF = float("-inf")

D_MODEL = 4096
DEPTH = 4
W_MIX = 1024
S5_GROUPS = 64
S5_GROUP = 16
S5_STATE = 64
S5_CH = S5_GROUPS * S5_STATE
LRU_BLOCKS = 8
LRU_BLK = 128
LRU_C = 8.0
CONV_W = 4
GLA_H = 4
GLA_DK = 128
GLA_DV = 256
GLA_RANK = 16
GLA_TAU = 16.0
GLA_CHUNK = 64
DN_H = 8
DN_DK = 128
DN_DV = 128
DN_QKV = 3072
DN_CHUNK = 64
D_PLE = 256
N_MAIN = 11264
COL_XA, COL_ZA, COL_XB, COL_ZB = 0, 1024, 2048, 3072
COL_QC, COL_KC, COL_VC = 4096, 4608, 5120
COL_QKVD, COL_ZC, COL_ZD = 6144, 9216, 10240
ORIG_ZC, ORIG_GC, ORIG_QKVD, ORIG_ZD, ORIG_AB = 6144, 7168, 7184, 10256, 11280
TAIL_W = 128
TAIL_G, TAIL_A, TAIL_B = 0, 16, 24
SAMPLE_T_PAD = 8

V7X_LANES = 128
V7X_SUBLANES = 8
V7X_MXU_DEPTH = 256
VMEM_LIMIT = 52 * 1024 * 1024
DENSE_TILES_PROMPT = dict(tm=1024, tn=512)
DENSE_TILES_SAMPLE = dict(tm=512, tn=1024)


def _cparams(n_axes):
    return pltpu.CompilerParams(dimension_semantics=("arbitrary",) * n_axes,
                                vmem_limit_bytes=VMEM_LIMIT)


def _sigmoid(x):
    return jax.nn.sigmoid(x)


def _silu(x):
    return x * jax.nn.sigmoid(x)


def _softplus(x):
    return jnp.maximum(x, 0.0) + jnp.log1p(jnp.exp(-jnp.abs(x)))


def _neg_expm1_nonpos(x):
    t = jnp.tanh(0.5 * x)
    return (-2.0 * t) / (1.0 - t)


def _log_sigmoid(x):
    return jnp.minimum(x, 0.0) - jnp.log1p(jnp.exp(-jnp.abs(x)))


def _gelu_tanh(x):
    c = math.sqrt(2.0 / math.pi)
    return x * (0.5 * (1.0 + jnp.tanh(c * (x + 0.044715 * (x * x * x)))))


def _dot(a, b):
    return jnp.dot(a, b, preferred_element_type=F32)


def _dot_nt(a, b):
    return lax.dot_general(a, b, (((1,), (1,)), ((), ())), preferred_element_type=F32)


def _dot_tn(a, b):
    return lax.dot_general(a, b, (((0,), (0,)), ((), ())), preferred_element_type=F32)


def _split3(x):
    h = x.astype(BF16)
    r = x - h.astype(F32)
    m = r.astype(BF16)
    l = (r - m.astype(F32)).astype(BF16)
    return h, m, l


def _cumsum_rows(x):
    c = x.shape[0]
    if c <= V7X_SUBLANES:
        row = lax.broadcasted_iota(jnp.int32, x.shape, 0)
        k = 1
        while k < c:
            x = x + jnp.where(row >= k, pltpu.roll(x, k, 0), 0.0)
            k *= 2
        return x
    h, m, l = _split3(x)
    if 3 * c <= V7X_MXU_DEPTH:
        col = lax.broadcasted_iota(jnp.int32, (c, 3 * c), 1)
        col = jnp.where(col >= 2 * c, col - 2 * c, jnp.where(col >= c, col - c, col))
        tri3 = (lax.broadcasted_iota(jnp.int32, (c, 3 * c), 0) >= col).astype(BF16)
        return _dot(tri3, jnp.concatenate([h, m, l], axis=0))
    tri = (lax.broadcasted_iota(jnp.int32, (c, c), 0) >= lax.broadcasted_iota(jnp.int32, (c, c), 1)).astype(BF16)
    return _dot(tri, h) + _dot(tri, m) + _dot(tri, l)


def _mm(a, b, dot=_dot, contract=1):
    if a.shape[contract] < 2 * V7X_SUBLANES:
        return dot(a, b)
    return dot(a.astype(BF16), b.astype(BF16))


def _dot_x3(a, b):
    ah32 = a.astype(BF16).astype(F32)
    al32 = a - ah32
    bh32 = b.astype(BF16).astype(F32)
    bl32 = b - bh32
    k = a.shape[1]
    if 3 * k <= V7X_MXU_DEPTH:
        lhs = jnp.concatenate([ah32, ah32, al32], axis=1).astype(BF16)
        rhs = jnp.concatenate([bh32, bl32, bh32], axis=0).astype(BF16)
        return _dot(lhs, rhs)
    ah, al, bh, bl = (t.astype(BF16) for t in (ah32, al32, bh32, bl32))
    return _dot(ah, bh) + _dot(ah, bl) + _dot(al, bh)


def _prenorm_kernel(x_ref, g_ref, hg_ref, ssq_ref):
    x = x_ref[...]
    hg_ref[...] = (x * g_ref[...]).astype(BF16)
    ssq_ref[...] = jnp.sum(x * x, axis=-1, keepdims=True)


def prenorm(x, g_norm, layer, *, tm):
    m = x.shape[0]
    return pl.pallas_call(
        _prenorm_kernel,
        grid=(m // tm,),
        in_specs=[pl.BlockSpec((tm, D_MODEL), lambda i: (i, 0)),
                  pl.BlockSpec((None, 1, D_MODEL), lambda i: (layer, 0, 0))],
        out_specs=[pl.BlockSpec((tm, D_MODEL), lambda i: (i, 0)), pl.BlockSpec((tm, 1), lambda i: (i, 0))],
        out_shape=[jax.ShapeDtypeStruct((m, D_MODEL), BF16), jax.ShapeDtypeStruct((m, 1), F32)],
        compiler_params=_cparams(1),
        name="prenorm",
    )(x, g_norm)


def _in_proj_kernel(hg_ref, ssq_ref, w_ref, wt_ref, o_ref, ot_ref):
    r = lax.rsqrt(ssq_ref[...] * (1.0 / D_MODEL) + EPS)

    @pl.when(pl.program_id(1) == 0)
    def _():
        ot_ref[...] = _dot_nt(hg_ref[...], wt_ref[...]) * r

    o_ref[...] = _dot_nt(hg_ref[...], w_ref[...]) * r


def in_proj(hg, ssq, w_main, w_tail, layer, *, tm, tn):
    m = hg.shape[0]
    grid = (m // tm, N_MAIN // tn)
    return pl.pallas_call(
        _in_proj_kernel,
        grid=grid,
        in_specs=[
            pl.BlockSpec((tm, D_MODEL), lambda i, j: (i, 0)),
            pl.BlockSpec((tm, 1), lambda i, j: (i, 0)),
            pl.BlockSpec((None, tn, D_MODEL), lambda i, j: (layer, j, 0)),
            pl.BlockSpec((None, TAIL_W, D_MODEL), lambda i, j: (layer, 0, 0)),
        ],
        out_specs=[
            pl.BlockSpec((tm, tn), lambda i, j: (i, j)),
            pl.BlockSpec((tm, TAIL_W), lambda i, j: (i, 0)),
        ],
        out_shape=[jax.ShapeDtypeStruct((m, N_MAIN), F32), jax.ShapeDtypeStruct((m, TAIL_W), F32)],
        compiler_params=_cparams(2),
        name="in_proj",
    )(hg, ssq, w_main, w_tail)


def _out_proj_kernel(ma_ref, mb_ref, mc_ref, md_ref, w_ref, h_ref, o_ref, ob_ref):
    acc = h_ref[...]
    for k, m_ref in enumerate((ma_ref, mb_ref, mc_ref, md_ref)):
        acc = acc + _dot(m_ref[...], w_ref[k * W_MIX:(k + 1) * W_MIX, :])
    o_ref[...] = acc
    ob_ref[...] = acc.astype(BF16)


def out_proj(mixes, w_out, h, layer, *, tm, tn):
    m = h.shape[0]
    grid = (m // tm, D_MODEL // tn)
    mix_spec = pl.BlockSpec((tm, W_MIX), lambda i, j: (i, 0))
    tile = pl.BlockSpec((tm, tn), lambda i, j: (i, j))
    return pl.pallas_call(
        _out_proj_kernel,
        grid=grid,
        in_specs=[mix_spec, mix_spec, mix_spec, mix_spec,
                  pl.BlockSpec((None, 4 * W_MIX, tn), lambda i, j: (layer, 0, j)), tile],
        out_specs=[tile, tile],
        out_shape=[jax.ShapeDtypeStruct((m, D_MODEL), F32), jax.ShapeDtypeStruct((m, D_MODEL), BF16)],
        compiler_params=_cparams(2),
        name="out_proj",
    )(*mixes, w_out, h)


def _ple_kernel(hrow_ref, htile_ref, pe_ref, wg_ref, bg_ref, wp_ref, gn_ref, o_ref, hg_ref, ssq_ref):
    gate = _sigmoid(_dot(hrow_ref[...], wg_ref[...]) + bg_ref[...])
    pv = _dot(pe_ref[...].astype(BF16), wp_ref[...])
    h = htile_ref[...] + gate * pv
    o_ref[...] = h
    hg_ref[...] = (h * gn_ref[...]).astype(BF16)
    part = jnp.sum(h * h, axis=-1, keepdims=True)

    @pl.when(pl.program_id(1) == 0)
    def _():
        ssq_ref[...] = part

    @pl.when(pl.program_id(1) != 0)
    def _():
        ssq_ref[...] = ssq_ref[...] + part


def ple(h, hb, pe, w_gate, b_gate, w_ple, g_next, layer, next_layer, *, tm, tn):
    m = h.shape[0]
    grid = (m // tm, D_MODEL // tn)
    tile = pl.BlockSpec((tm, tn), lambda i, j: (i, j))
    return pl.pallas_call(
        _ple_kernel,
        grid=grid,
        in_specs=[
            pl.BlockSpec((tm, D_MODEL), lambda i, j: (i, 0)),
            tile,
            pl.BlockSpec((None, tm, D_PLE), lambda i, j: (layer, i, 0)),
            pl.BlockSpec((None, D_MODEL, tn), lambda i, j: (layer, 0, j)),
            pl.BlockSpec((None, 1, tn), lambda i, j: (layer, 0, j)),
            pl.BlockSpec((None, D_PLE, tn), lambda i, j: (layer, 0, j)),
            pl.BlockSpec((None, 1, tn), lambda i, j: (next_layer, 0, j)),
        ],
        out_specs=[tile, tile, pl.BlockSpec((tm, 1), lambda i, j: (i, 0))],
        out_shape=[jax.ShapeDtypeStruct((m, D_MODEL), F32), jax.ShapeDtypeStruct((m, D_MODEL), BF16),
                   jax.ShapeDtypeStruct((m, 1), F32)],
        compiler_params=_cparams(2),
        name="ple",
    )(hb, h, pe, w_gate, b_gate, w_ple, g_next)


REPACK_ROWS = 512


def _repack_kernel(a_ref, g_ref, ab_ref, o_ref, t_ref):
    o_ref[...] = a_ref[...].astype(BF16)

    @pl.when(pl.program_id(1) == 0)
    def _():
        t_ref[...] = jnp.zeros(t_ref.shape, BF16)
        t_ref[TAIL_G:TAIL_G + GLA_RANK, :] = g_ref[...].astype(BF16)
        t_ref[TAIL_A:TAIL_A + 2 * DN_H, :] = ab_ref[...].astype(BF16)


def repack_w_in(w_in_t):
    nl = w_in_t.shape[0]
    n_tiles = N_MAIN // REPACK_ROWS
    t_qkvd, t_zc, t_zd = COL_QKVD // REPACK_ROWS, COL_ZC // REPACK_ROWS, COL_ZD // REPACK_ROWS

    def src_row(j):
        return jnp.where(j < t_qkvd, j * REPACK_ROWS,
                         jnp.where(j < t_zc, ORIG_QKVD + (j - t_qkvd) * REPACK_ROWS,
                                   jnp.where(j < t_zd, ORIG_ZC + (j - t_zc) * REPACK_ROWS,
                                             ORIG_ZD + (j - t_zd) * REPACK_ROWS)))

    row_align = 2 * V7X_SUBLANES
    assert all(o % row_align == 0 for o in (ORIG_QKVD, ORIG_ZC, ORIG_ZD, ORIG_GC, ORIG_AB))

    def rows(n, start):
        return pl.BlockSpec((None, pl.Element(n), pl.Element(D_MODEL)),
                            lambda l, j: (l, pl.multiple_of(start(j), row_align), 0))

    return pl.pallas_call(
        _repack_kernel,
        grid=(nl, n_tiles),
        in_specs=[rows(REPACK_ROWS, src_row), rows(GLA_RANK, lambda j: ORIG_GC), rows(2 * DN_H, lambda j: ORIG_AB)],
        out_specs=[pl.BlockSpec((None, REPACK_ROWS, D_MODEL), lambda l, j: (l, j, 0)),
                   pl.BlockSpec((None, TAIL_W, D_MODEL), lambda l, j: (l, 0, 0))],
        out_shape=[jax.ShapeDtypeStruct((nl, N_MAIN, D_MODEL), BF16),
                   jax.ShapeDtypeStruct((nl, TAIL_W, D_MODEL), BF16)],
        compiler_params=_cparams(2),
        name="repack_w_in",
    )(w_in_t, w_in_t, w_in_t)


def _final_norm_kernel(x_ref, g_ref, o_ref):
    x = x_ref[...]
    ms = jnp.mean(x * x, axis=-1, keepdims=True)
    o_ref[...] = (x * lax.rsqrt(ms + EPS)) * g_ref[...]


def final_norm(h, g, *, tm):
    m = h.shape[0]
    return pl.pallas_call(
        _final_norm_kernel,
        grid=(m // tm,),
        in_specs=[pl.BlockSpec((tm, D_MODEL), lambda i: (i, 0)),
                  pl.BlockSpec((1, D_MODEL), lambda i: (0, 0))],
        out_specs=pl.BlockSpec((tm, D_MODEL), lambda i: (i, 0)),
        out_shape=jax.ShapeDtypeStruct((m, D_MODEL), F32),
        compiler_params=_cparams(1),
        name="final_norm",
    )(h, g)


S5_SEG = 32


def _s5_prep_kernel(lre_ref, lim_ref, ldt_ref, coef_ref, cre_ref, cim_ref):
    lre = lre_ref[...]
    lim = lim_ref[...]
    dt = jnp.exp(ldt_ref[...])
    ai = lim * dt
    mag = jnp.exp(lre * dt)
    ar = mag * jnp.cos(ai)
    aim = mag * jnp.sin(ai)
    den = lre * lre + lim * lim
    nr = ar - 1.0
    cre_ref[...] = (nr * lre + aim * lim) / den
    cim_ref[...] = (aim * lre - nr * lim) / den
    pr, pim = [ar], [aim]
    for _ in range(S5_SEG - 1):
        nr_, ni_ = pr[-1] * ar - pim[-1] * aim, pr[-1] * aim + pim[-1] * ar
        pr.append(nr_)
        pim.append(ni_)
    shape = (V7X_SUBLANES, S5_CH)
    coef_ref[0] = jnp.broadcast_to(ar, shape)
    coef_ref[1] = jnp.broadcast_to(aim, shape)
    coef_ref[2] = jnp.broadcast_to(pr[S5_SEG - 1], shape)
    coef_ref[3] = jnp.broadcast_to(pim[S5_SEG - 1], shape)


def s5_prep(lam_re, lam_im, log_dt_rep):
    nl = lam_re.shape[0]
    vec = pl.BlockSpec((None, 1, S5_CH), lambda l: (l, 0, 0))
    return pl.pallas_call(
        _s5_prep_kernel,
        grid=(nl,),
        in_specs=[vec, vec, vec],
        out_specs=[pl.BlockSpec((None, 4, V7X_SUBLANES, S5_CH), lambda l: (l, 0, 0, 0)), vec, vec],
        out_shape=[jax.ShapeDtypeStruct((nl, 4, V7X_SUBLANES, S5_CH), F32),
                   jax.ShapeDtypeStruct((nl, 1, S5_CH), F32),
                   jax.ShapeDtypeStruct((nl, 1, S5_CH), F32)],
        compiler_params=_cparams(1),
        name="s5_prep",
    )(lam_re, lam_im, log_dt_rep)


def _s5_bb_kernel(cr_ref, ci_ref, br_ref, bi_ref, or_ref, oi_ref):
    cr = cr_ref[...]
    ci = ci_ref[...]
    br = br_ref[...]
    bi = bi_ref[...]
    or_ref[...] = cr * br - ci * bi
    oi_ref[...] = cr * bi + ci * br


def s5_bb(coef_re_col, coef_im_col, b_re, b_im):
    nl = b_re.shape[0]
    rows = 1024
    col = pl.BlockSpec((None, rows, 1), lambda l, r: (l, r, 0))
    mat = pl.BlockSpec((None, rows, S5_GROUP), lambda l, r: (l, r, 0))
    return pl.pallas_call(
        _s5_bb_kernel,
        grid=(nl, S5_CH // rows),
        in_specs=[col, col, mat, mat],
        out_specs=[mat, mat],
        out_shape=[jax.ShapeDtypeStruct((nl, S5_CH, S5_GROUP), F32)] * 2,
        compiler_params=_cparams(2),
        name="s5_bb",
    )(coef_re_col, coef_im_col, b_re, b_im)


S5_STRIP = 512


def _s5_kernel(xa_ref, za_ref, h0r_ref, h0i_ref, coef_ref, perm_ref, permt_ref,
               wur_ref, wui_ref, wyr_ref, wyi_ref, dsk_ref, wglu_ref, bglu_ref,
               mix_ref, hro_ref, hio_ref,
               ur_scr, ui_scr, cr_scr, ci_scr, y_scr, *, nb, tt, t_last):
    tc = pl.program_id(1)
    ntc = pl.num_programs(1)
    split = nb == 1
    lc = tt // V7X_SUBLANES if split else tt
    n_sets = 1 if split else nb // V7X_SUBLANES
    set_rows = V7X_SUBLANES * lc

    xb16 = _dot(perm_ref[...], xa_ref[...].astype(BF16)).astype(BF16)
    for k in range(4):
        xk = xb16[:, 256 * k:256 * (k + 1)]
        ur_scr[:, 1024 * k:1024 * (k + 1)] = _dot(xk, wur_ref[k])
        ui_scr[:, 1024 * k:1024 * (k + 1)] = _dot(xk, wui_ref[k])

    @pl.when(tc == 0)
    def _():
        cr_scr[...] = h0r_ref[...]
        ci_scr[...] = h0i_ref[...]

    row8 = lax.broadcasted_iota(jnp.int32, (V7X_SUBLANES, S5_STRIP), 0)
    for st in range(S5_CH // S5_STRIP):
        ls = slice(st * S5_STRIP, (st + 1) * S5_STRIP)
        ar = coef_ref[0, :, ls]
        ai = coef_ref[1, :, ls]
        for s in range(n_sets):
            base = s * set_rows
            if split:
                h0r = jnp.zeros((V7X_SUBLANES, S5_STRIP), F32)
                h0i = jnp.zeros((V7X_SUBLANES, S5_STRIP), F32)
            else:
                h0r = jnp.zeros((V7X_SUBLANES, S5_STRIP), F32)
                h0i = jnp.zeros((V7X_SUBLANES, S5_STRIP), F32)
                for k in range(V7X_SUBLANES):
                    h0r = jnp.where(row8 == k, cr_scr[s * V7X_SUBLANES + k, :, ls], h0r)
                    h0i = jnp.where(row8 == k, ci_scr[s * V7X_SUBLANES + k, :, ls], h0i)

            def step(j, h, base=base, ls=ls, ar=ar, ai=ai):
                hr, hi = h
                rows = pl.ds(pl.multiple_of(base + j * V7X_SUBLANES, V7X_SUBLANES), V7X_SUBLANES)
                nr = (ar * hr - ai * hi) + ur_scr[rows, ls]
                ni = (ar * hi + ai * hr) + ui_scr[rows, ls]
                ur_scr[rows, ls] = nr
                ui_scr[rows, ls] = ni
                return nr, ni

            fr, fi = lax.fori_loop(0, lc, step, (h0r, h0i), unroll=True)
            if split:
                alr = coef_ref[2, 0:1, ls]
                ali = coef_ref[3, 0:1, ls]
                cr = cr_scr[0, :, ls]
                ci = ci_scr[0, :, ls]
                init_r = jnp.zeros((V7X_SUBLANES, S5_STRIP), F32)
                init_i = jnp.zeros((V7X_SUBLANES, S5_STRIP), F32)
                for k in range(V7X_SUBLANES):
                    init_r = jnp.where(row8 == k, cr, init_r)
                    init_i = jnp.where(row8 == k, ci, init_i)
                    cr, ci = (alr * cr - ali * ci) + fr[k:k + 1], (alr * ci + ali * cr) + fi[k:k + 1]
                cr_scr[0, :, ls] = cr
                ci_scr[0, :, ls] = ci

                def fix(j, c, base=base, ls=ls, ar=ar, ai=ai):
                    rows = pl.ds(pl.multiple_of(base + j * V7X_SUBLANES, V7X_SUBLANES), V7X_SUBLANES)
                    c_r = ar * c[0] - ai * c[1]
                    c_i = ar * c[1] + ai * c[0]
                    ur_scr[rows, ls] = ur_scr[rows, ls] + c_r
                    ui_scr[rows, ls] = ui_scr[rows, ls] + c_i
                    return c_r, c_i

                lax.fori_loop(0, lc, fix, (init_r, init_i), unroll=True)

        y_scr[:, V7X_LANES * st:V7X_LANES * (st + 1)] = (_dot(ur_scr[:, ls].astype(BF16), wyr_ref[st])
                                                           - _dot(ui_scr[:, ls].astype(BF16), wyi_ref[st]))
    yh, ym, yl = _split3(y_scr[...])
    pt = permt_ref[...]
    y = (_dot(pt, yh) + _dot(pt, ym) + _dot(pt, yl)) + dsk_ref[...] * xa_ref[...]
    ga = _gelu_tanh(y)
    ya = ga * _sigmoid(_dot(ga.astype(BF16), wglu_ref[...]) + bglu_ref[...])
    mix_ref[...] = (ya * _silu(za_ref[...])).astype(BF16)

    @pl.when(tc == ntc - 1)
    def _():
        if split:
            hro_ref[...] = cr_scr[...]
            hio_ref[...] = ci_scr[...]
        else:
            for s in range(n_sets):
                for k in range(V7X_SUBLANES):
                    r = s * set_rows + V7X_SUBLANES * t_last + k
                    hro_ref[s * V7X_SUBLANES + k] = ur_scr[r:r + 1, :]
                    hio_ref[s * V7X_SUBLANES + k] = ui_scr[r:r + 1, :]


def s5_mixer(proj, h0_re, h0_im, coef, wur, wui, wyr, wyi, dskip, wglu, bglu, layer, state_layer,
             *, batch, seq, t_valid, nb, tt):
    ntc = seq // tt
    assert nb == 1 or ntc == 1
    rows = nb * tt
    grid = (batch // nb, ntc)
    t_last = (t_valid - 1) - (ntc - 1) * tt
    assert 0 <= t_last < tt
    if nb == 1:
        assert tt == V7X_SUBLANES * S5_SEG and t_last == tt - 1
        lc = S5_SEG
    else:
        assert nb % V7X_SUBLANES == 0
        lc = tt
    idx = np.arange(rows)
    set_rows = V7X_SUBLANES * lc
    src = (idx // set_rows) * set_rows + (idx % V7X_SUBLANES) * lc + (idx % set_rows) // V7X_SUBLANES
    perm_np = np.zeros((rows, rows), np.float32)
    perm_np[idx, src] = 1.0
    perm = jnp.asarray(perm_np, BF16)
    perm_t = jnp.asarray(perm_np.T, BF16)
    sq = pl.BlockSpec((rows, rows), lambda b, t: (0, 0))

    def tok(cb):
        return pl.BlockSpec((rows, W_MIX), lambda b, t: (b * ntc + t, cb))

    st_in = pl.BlockSpec((None, nb, 1, S5_CH), lambda b, t: (state_layer, b, 0, 0))
    st_out = pl.BlockSpec((nb, 1, S5_CH), lambda b, t: (b, 0, 0))

    def lw(shape):
        nd = len(shape)
        return pl.BlockSpec((None,) + shape, lambda b, t: (layer,) + (0,) * nd)

    return pl.pallas_call(
        functools.partial(_s5_kernel, nb=nb, tt=tt, t_last=t_last),
        grid=grid,
        in_specs=[tok(COL_XA // W_MIX), tok(COL_ZA // W_MIX), st_in, st_in,
                  lw((4, V7X_SUBLANES, S5_CH)), sq, sq, lw((4, 256, 1024)), lw((4, 256, 1024)),
                  lw((8, 512, 128)), lw((8, 512, 128)), lw((1, W_MIX)), lw((W_MIX, W_MIX)), lw((1, W_MIX))],
        out_specs=[pl.BlockSpec((rows, W_MIX), lambda b, t: (b * ntc + t, 0)), st_out, st_out],
        out_shape=[jax.ShapeDtypeStruct((batch * seq, W_MIX), BF16),
                   jax.ShapeDtypeStruct((batch, 1, S5_CH), F32),
                   jax.ShapeDtypeStruct((batch, 1, S5_CH), F32)],
        scratch_shapes=[pltpu.VMEM((rows, S5_CH), F32), pltpu.VMEM((rows, S5_CH), F32),
                        pltpu.VMEM((nb, 1, S5_CH), F32), pltpu.VMEM((nb, 1, S5_CH), F32),
                        pltpu.VMEM((rows, W_MIX), F32)],
        compiler_params=_cparams(2),
        name="s5_mixer",
    )(proj, proj, h0_re, h0_im, coef, perm, perm_t, wur, wui, wyr, wyi, dskip, wglu, bglu)


LRU_STRIP = 512
CONV_PAD = V7X_SUBLANES


def _causal_conv(xp_scr, tail_scr, x, cw_ref, bi):
    del xp_scr
    rows_n = x.shape[0]
    width = x.shape[1]
    n_tiles = rows_n // V7X_SUBLANES
    xe = jnp.concatenate([tail_scr[bi], x], axis=0).reshape(n_tiles + 1, V7X_SUBLANES, width)
    row = lax.broadcasted_iota(jnp.int32, (n_tiles, V7X_SUBLANES, width), 1)
    acc = cw_ref[CONV_W - 1:CONV_W, :] * x
    for jj in range(CONV_W - 1):
        shift = CONV_W - 1 - jj
        rot = pltpu.roll(xe, shift, 1)
        shifted = jnp.where(row < shift, rot[:-1], rot[1:]).reshape(rows_n, width)
        acc = acc + cw_ref[jj:jj + 1, :] * shifted
    tail_scr[bi] = xe[n_tiles]
    return acc


def _init_conv_tail(tail_scr, cv0_ref, nb):
    tail_scr[...] = jnp.zeros(tail_scr.shape, F32)
    for bi in range(nb):
        tail_scr[bi, CONV_PAD - (CONV_W - 1):CONV_PAD, :] = cv0_ref[bi]


def _lru_kernel(xb_ref, zb_ref, h0_ref, cv0_ref, cw_ref, cb_ref, wri_ref, br_ref, bi_ref, lam_ref,
                mix_ref, ho_ref,
                xp_scr, tail_scr, xc_scr, a_scr, b_scr, c_scr, *, nb, tt, t_last):
    tc = pl.program_id(1)
    ntc = pl.num_programs(1)

    @pl.when(tc == 0)
    def _():
        c_scr[...] = h0_ref[...]
        _init_conv_tail(tail_scr, cv0_ref, nb)

    def conv_body(bi, carry):
        rows = pl.ds(pl.multiple_of(bi * tt, V7X_SUBLANES), tt)
        acc = _causal_conv(xp_scr, tail_scr, xb_ref[rows, :], cw_ref, bi)
        xc_scr[rows, :] = acc + cb_ref[...]
        return carry

    lax.fori_loop(0, nb, conv_body, 0)

    xc = xc_scr[...]
    xc16 = xc.astype(BF16)
    sp = _softplus(-lam_ref[...])
    for blk in range(LRU_BLOCKS):
        cs = slice(blk * LRU_BLK, (blk + 1) * LRU_BLK)
        pre = _dot(xc16[:, cs], wri_ref[blk])
        r = _sigmoid(pre[:, :LRU_BLK] + br_ref[:, cs])
        ig = _sigmoid(pre[:, LRU_BLK:] + bi_ref[:, cs])
        log_a = (-LRU_C) * r * sp[:, cs]
        a_scr[:, cs] = jnp.exp(log_a)
        b_scr[:, cs] = jnp.sqrt(_neg_expm1_nonpos(2.0 * log_a)) * (ig * xc[:, cs])

    ng = tt // V7X_SUBLANES
    for s in range(W_MIX // LRU_STRIP):
        ls = slice(s * LRU_STRIP, (s + 1) * LRU_STRIP)
        row = lax.broadcasted_iota(jnp.int32, (V7X_SUBLANES, LRU_STRIP), 0)

        def seq_body(bi, carry, ls=ls, row=row):
            def tile_body(g, c):
                rows = pl.ds(pl.multiple_of(bi * tt + g * V7X_SUBLANES, V7X_SUBLANES), V7X_SUBLANES)
                a = a_scr[rows, ls]
                b = b_scr[rows, ls]
                for k in (1, 2, 4):
                    a_sh = jnp.where(row >= k, pltpu.roll(a, k, 0), 1.0)
                    b_sh = jnp.where(row >= k, pltpu.roll(b, k, 0), 0.0)
                    b = b + a * b_sh
                    a = a * a_sh
                h = b + a * c
                b_scr[rows, ls] = h
                return h[V7X_SUBLANES - 1:, :]

            c = lax.fori_loop(0, ng, tile_body, c_scr[bi, :, ls])
            c_scr[bi, :, ls] = c
            return carry

        lax.fori_loop(0, nb, seq_body, 0)

    mix_ref[...] = (b_scr[...] * _silu(zb_ref[...])).astype(BF16)

    @pl.when(tc == ntc - 1)
    def _():
        for bi in range(nb):
            r = bi * tt + t_last
            ho_ref[bi] = b_scr[r:r + 1, :]


def lru_mixer(proj, h0, conv0, cw, cb, wri, br, bi_, lam, layer, state_layer, *, batch, seq, t_valid, nb, tt):
    ntc = seq // tt
    assert nb == 1 or ntc == 1
    rows = nb * tt
    grid = (batch // nb, ntc)
    t_last = (t_valid - 1) - (ntc - 1) * tt

    def tok(cb_):
        return pl.BlockSpec((rows, W_MIX), lambda b, t: (b * ntc + t, cb_))

    def lw(shape):
        nd = len(shape)
        return pl.BlockSpec((None,) + shape, lambda b, t: (layer,) + (0,) * nd)

    return pl.pallas_call(
        functools.partial(_lru_kernel, nb=nb, tt=tt, t_last=t_last),
        grid=grid,
        in_specs=[tok(COL_XB // W_MIX), tok(COL_ZB // W_MIX),
                  pl.BlockSpec((None, nb, 1, W_MIX), lambda b, t: (state_layer, b, 0, 0)),
                  pl.BlockSpec((None, nb, CONV_W - 1, W_MIX), lambda b, t: (state_layer, b, 0, 0)),
                  lw((CONV_W, W_MIX)), lw((1, W_MIX)), lw((LRU_BLOCKS, LRU_BLK, 2 * LRU_BLK)),
                  lw((1, W_MIX)), lw((1, W_MIX)), lw((1, W_MIX))],
        out_specs=[pl.BlockSpec((rows, W_MIX), lambda b, t: (b * ntc + t, 0)),
                   pl.BlockSpec((nb, 1, W_MIX), lambda b, t: (b, 0, 0))],
        out_shape=[jax.ShapeDtypeStruct((batch * seq, W_MIX), BF16),
                   jax.ShapeDtypeStruct((batch, 1, W_MIX), F32)],
        scratch_shapes=[pltpu.VMEM((CONV_PAD + tt, W_MIX), F32),
                        pltpu.VMEM((nb, CONV_PAD, W_MIX), F32),
                        pltpu.VMEM((rows, W_MIX), F32), pltpu.VMEM((rows, W_MIX), F32),
                        pltpu.VMEM((rows, W_MIX), F32), pltpu.VMEM((nb, 1, W_MIX), F32)],
        compiler_params=_cparams(2),
        name="lru_mixer",
    )(proj, proj, h0, conv0, cw, cb, wri, br, bi_, lam)


SEQ_GROUP = 2


def _for_each_group(nb, group_body):
    g = SEQ_GROUP if nb % SEQ_GROUP == 0 else 1
    if nb == g:
        group_body(list(range(nb)))
    else:
        def body(i, carry):
            group_body([i * g + j for j in range(g)])
            return carry
        lax.fori_loop(0, nb // g, body, 0)


def _gla_kernel(q_ref, k_ref, v_ref, z_ref, tail_ref, s0_ref, wg_ref, bg_ref, go_ref, stack_ref,
                mix_ref, so_ref,
                s_scr, mix_scr, *, nb, c, cv, single_chunk):
    del stack_ref
    tc = pl.program_id(1)
    ntc = pl.num_programs(1)
    sb = min(16, c)
    nblk = c // sb

    s_src, s_dst = (s0_ref, so_ref) if single_chunk else (s_scr, s_scr)
    if not single_chunk:
        @pl.when(tc == 0)
        def _():
            s_scr[...] = s0_ref[...]

    row_c = lax.broadcasted_iota(jnp.int32, (c, 1), 0)
    lane_c = lax.broadcasted_iota(jnp.int32, (sb, c), 1)
    row_sb = lax.broadcasted_iota(jnp.int32, (sb, 1), 0)

    def group_body(bis):
        pairs = [(g, h) for g in range(len(bis)) for h in range(GLA_H)]
        heads = range(len(pairs))
        b_all, q_all, k_all, v_all = [], [], [], []
        for bi in bis:
            x = _dot(tail_ref[bi].astype(BF16), wg_ref[...]) + bg_ref[...]
            b_all.append(_cumsum_rows(_log_sigmoid(x) * (1.0 / GLA_TAU)))
            q_all.append(q_ref[bi])
            k_all.append(k_ref[bi])
            v_all.append(v_ref[bi])
        s_old = [s_src[bis[g], h] for g, h in pairs]
        q = [q_all[g][:, h * GLA_DK:(h + 1) * GLA_DK] * (GLA_DK ** -0.5) for g, h in pairs]
        k = [k_all[g][:, h * GLA_DK:(h + 1) * GLA_DK] for g, h in pairs]
        v = [v_all[g][:, h * GLA_DV:(h + 1) * GLA_DV] for g, h in pairs]
        b = [b_all[g][:, h * GLA_DK:(h + 1) * GLA_DK] for g, h in pairs]
        o_state = [_dot((q[h] * jnp.exp(b[h])).astype(BF16), s_old[h].astype(BF16)) for h in heads]
        s_new = []
        for h in heads:
            b_last = b[h][cv - 1:cv]
            dec = b_last - b[h]
            if cv < c:
                dec = jnp.where(row_c < cv, dec, NEG_INF)
            kd = k[h] * jnp.exp(dec)
            d_col = jnp.transpose(jnp.broadcast_to(jnp.exp(b_last), (V7X_SUBLANES, GLA_DK)))[:, 0:1]
            s_new.append(d_col * s_old[h] + _mm(kd, v[h], dot=_dot_tn, contract=0))
        att_off = []
        for h in heads:
            per_blk = [jnp.zeros((sb, c), F32)]
            for blk in range(1, nblk):
                r0 = blk * sb
                b_ref_row = b[h][r0 - 1:r0]
                qs = (q[h][r0:r0 + sb] * jnp.exp(b[h][r0:r0 + sb] - b_ref_row)).astype(BF16)
                kd = (k[h] * jnp.exp(jnp.where(row_c < r0, b_ref_row - b[h], NEG_INF))).astype(BF16)
                per_blk.append(_dot_nt(qs, kd))
            att_off.append(per_blk)
        att = []
        for h in heads:
            att_rows = []
            for blk in range(nblk):
                r0 = blk * sb
                q_i = q[h][r0:r0 + sb]
                b_i = b[h][r0:r0 + sb]
                a = att_off[h][blk]
                for sl in range(sb):
                    s_abs = r0 + sl
                    e = jnp.exp(jnp.where(row_sb >= sl, b_i - b[h][s_abs:s_abs + 1], NEG_INF))
                    col = jnp.sum(q_i * (k[h][s_abs:s_abs + 1] * e), axis=-1, keepdims=True)
                    a = jnp.where(lane_c == s_abs, col, a)
                att_rows.append(a)
            att.append(att_rows[0] if nblk == 1 else jnp.concatenate(att_rows, axis=0))
        o = [_mm(att[h], v[h]) + o_state[h] for h in heads]
        for p, (g, h) in enumerate(pairs):
            vs = slice(h * GLA_DV, (h + 1) * GLA_DV)
            ms = jnp.mean(o[p] * o[p], axis=-1, keepdims=True)
            on = (o[p] * lax.rsqrt(ms + EPS)) * go_ref[...]
            mix_scr[bis[g], :, vs] = on * _silu(z_ref[bis[g], :, vs])
            s_dst[bis[g], h] = s_new[p]

    _for_each_group(nb, group_body)
    mix_ref[...] = mix_scr[...].astype(BF16)

    if not single_chunk:
        @pl.when(tc == ntc - 1)
        def _():
            so_ref[...] = s_scr[...]


def _stack_alias(stack, n_inputs_before):
    spec = pl.BlockSpec(memory_space=pl.ANY)
    if stack is None:
        return jnp.zeros((V7X_SUBLANES, V7X_LANES), F32), spec, {}
    return stack, spec, {n_inputs_before: 1}


def gla_mixer(proj, tail, s0, wg, bg, go, layer, state_layer, *, batch, seq, c, cv, nb, n_layers, stack=None):
    ntc = seq // c
    grid = (batch // nb, ntc)
    proj3 = proj.reshape(batch, seq, proj.shape[-1])
    tail3 = tail.reshape(batch, seq, TAIL_W)

    def tok(width, col):
        return pl.BlockSpec((nb, c, width), lambda b, t: (b, t, col // width))

    def lw(shape):
        nd = len(shape)
        return pl.BlockSpec((None,) + shape, lambda b, t: (layer,) + (0,) * nd)

    st_shape = (nb, GLA_H, GLA_DK, GLA_DV)
    stack_arg, stack_spec, aliases = _stack_alias(stack, 9)
    mix, state = pl.pallas_call(
        functools.partial(_gla_kernel, nb=nb, c=c, cv=cv, single_chunk=ntc == 1),
        grid=grid,
        in_specs=[tok(GLA_H * GLA_DK, COL_QC), tok(GLA_H * GLA_DK, COL_KC), tok(W_MIX, COL_VC), tok(W_MIX, COL_ZC),
                  tok(TAIL_W, 0),
                  pl.BlockSpec((None,) + st_shape, lambda b, t: (state_layer, b, 0, 0, 0)),
                  lw((TAIL_W, GLA_H * GLA_DK)), lw((1, GLA_H * GLA_DK)), lw((1, GLA_DV)), stack_spec],
        out_specs=[tok(W_MIX, 0),
                   pl.BlockSpec((None,) + st_shape, lambda b, t: (layer, b, 0, 0, 0))],
        out_shape=[jax.ShapeDtypeStruct((batch, seq, W_MIX), BF16),
                   jax.ShapeDtypeStruct((n_layers, batch, GLA_H, GLA_DK, GLA_DV), F32)],
        scratch_shapes=[pltpu.VMEM(st_shape, F32), pltpu.VMEM((nb, c, W_MIX), F32)],
        input_output_aliases=aliases,
        compiler_params=_cparams(2),
        name="gla_mixer",
    )(proj3, proj3, proj3, proj3, tail3, s0, wg, bg, go, stack_arg)
    return mix.reshape(batch * seq, W_MIX), state


def _l2norm(x):
    return x * lax.rsqrt(jnp.sum(x * x, axis=-1, keepdims=True) + EPS)


def _unit_lower_solve(ms, rhss, c, cv):
    if c <= V7X_SUBLANES:
        row = lax.broadcasted_iota(jnp.int32, (c, 1), 0)
        us = []
        for m, rhs in zip(ms, rhss):
            u = jnp.zeros_like(rhs)
            u_rows = []
            for t in range(cv):
                ut = rhs[t:t + 1]
                for s in range(t):
                    ut = ut - m[t:t + 1, s:s + 1] * u_rows[s]
                u_rows.append(ut)
                u = jnp.where(row == t, ut, u)
            us.append(u)
        return us
    xs = [-m for m in ms]
    ys = list(rhss)
    dv = rhss[0].shape[1]
    levels = int(math.log2(c))
    assert 3 * c <= V7X_MXU_DEPTH

    def hi_lo(t):
        hi = t.astype(BF16)
        return hi, (t - hi.astype(F32)).astype(BF16)

    for lvl in range(levels):
        last = lvl == levels - 1
        ps = []
        for x, y in zip(xs, ys):
            xh, xl = hi_lo(x)
            yh, yl = hi_lo(y)
            bh = yh if last else jnp.concatenate([yh, xh], axis=1)
            bl = yl if last else jnp.concatenate([yl, xl], axis=1)
            lhs = jnp.concatenate([xh.astype(F32), xh.astype(F32), xl.astype(F32)], axis=1).astype(BF16)
            ps.append(_dot(lhs, jnp.concatenate([bh, bl, bh], axis=0)))
        ys = [y + p[:, :dv] for y, p in zip(ys, ps)]
        if not last:
            xs = [p[:, dv:] for p in ps]
    return ys


def _gdn_kernel(qkv_ref, z_ref, tail_ref, s0_ref, cv0_ref, cw_ref, alog_ref, dtb_ref, go_ref, stack_ref,
                mix_ref, so_ref,
                s_scr, xp_scr, tail_scr, mix_scr, *, nb, c, cv, single_chunk):
    del stack_ref
    tc = pl.program_id(1)
    ntc = pl.num_programs(1)

    s_src, s_dst = (s0_ref, so_ref) if single_chunk else (s_scr, s_scr)

    @pl.when(tc == 0)
    def _():
        if not single_chunk:
            s_scr[...] = s0_ref[...]
        _init_conv_tail(tail_scr, cv0_ref, nb)

    row_c = lax.broadcasted_iota(jnp.int32, (c, 1), 0)
    ri = lax.broadcasted_iota(jnp.int32, (c, c), 0)
    ci = lax.broadcasted_iota(jnp.int32, (c, c), 1)
    lane_t = lax.broadcasted_iota(jnp.int32, (c, TAIL_W), 1)
    a_lanes = (lane_t >= TAIL_A) & (lane_t < TAIL_A + DN_H)

    def group_body(bis):
        pairs = [(g, h) for g in range(len(bis)) for h in range(DN_H)]
        heads = range(len(pairs))
        qkv, gam, gam_t, beta = [], [], [], []
        for gi, bi in enumerate(bis):
            qkv.append(_silu(_causal_conv(xp_scr.at[gi], tail_scr, qkv_ref[bi], cw_ref, bi)))
            tl = tail_ref[bi]
            gg = _cumsum_rows(jnp.where(a_lanes, -jnp.exp(alog_ref[...]) * _softplus(tl + dtb_ref[...]), 0.0))
            gam.append(gg)
            gam_t.append(jnp.transpose(gg))
            beta.append(_sigmoid(tl))
        s_old = [s_src[bis[g], h] for g, h in pairs]
        s16 = [s.astype(BF16) for s in s_old]
        q16, k16, kf, v = [], [], [], []
        for g, h in pairs:
            q = _l2norm(qkv[g][:, h * DN_DK:(h + 1) * DN_DK]) * (DN_DK ** -0.5)
            k = _l2norm(qkv[g][:, DN_H * DN_DK + h * DN_DK:DN_H * DN_DK + (h + 1) * DN_DK])
            q16.append(q.astype(BF16))
            k16.append(k.astype(BF16))
            kf.append(k)
            v.append(qkv[g][:, 2 * DN_H * DN_DK + h * DN_DV:2 * DN_H * DN_DK + (h + 1) * DN_DV])
        g_col = [gam[g][:, TAIL_A + h:TAIL_A + h + 1] for g, h in pairs]
        g_row = [gam_t[g][TAIL_A + h:TAIL_A + h + 1, :] for g, h in pairs]
        b_col = [beta[g][:, TAIL_B + h:TAIL_B + h + 1] for g, h in pairs]
        decay = [jnp.exp(jnp.where(ri >= ci, g_col[h] - g_row[h], NEG_INF)) for h in heads]
        eg = [jnp.exp(g_col[h]) for h in heads]
        kk = [_dot_nt(k16[h], k16[h]) for h in heads]
        k_s = [_dot(k16[h], s16[h]) for h in heads]
        q_s = [_dot(q16[h], s16[h]) for h in heads]
        qk = [_dot_nt(q16[h], k16[h]) for h in heads]
        m = [jnp.where(ri > ci, b_col[h] * kk[h] * decay[h], 0.0) for h in heads]
        rhs = [b_col[h] * (v[h] - eg[h] * k_s[h]) for h in heads]
        u = _unit_lower_solve(m, rhs, c, cv)
        o = [eg[h] * q_s[h] + _mm(qk[h] * decay[h], u[h]) for h in heads]
        kd = []
        for h in heads:
            dec = g_col[h][cv - 1:cv] - g_col[h]
            if cv < c:
                dec = jnp.where(row_c < cv, dec, NEG_INF)
            kd.append(kf[h] * jnp.exp(dec))
        s_new = [jnp.exp(g_col[h][cv - 1:cv]) * s_old[h] + _mm(kd[h], u[h], dot=_dot_tn, contract=0) for h in heads]
        for p, (g, h) in enumerate(pairs):
            vs = slice(h * DN_DV, (h + 1) * DN_DV)
            ms = jnp.mean(o[p] * o[p], axis=-1, keepdims=True)
            on = (o[p] * lax.rsqrt(ms + EPS)) * go_ref[...]
            mix_scr[bis[g], :, vs] = on * _silu(z_ref[bis[g], :, vs])
            s_dst[bis[g], h] = s_new[p]

    _for_each_group(nb, group_body)
    mix_ref[...] = mix_scr[...].astype(BF16)

    if not single_chunk:
        @pl.when(tc == ntc - 1)
        def _():
            so_ref[...] = s_scr[...]


def gdn_mixer(proj, tail, s0, conv0, cw, alog, dtb, go, layer, state_layer, *, batch, seq, c, cv, nb, n_layers,
              stack=None):
    ntc = seq // c
    grid = (batch // nb, ntc)
    proj3 = proj.reshape(batch, seq, proj.shape[-1])
    tail3 = tail.reshape(batch, seq, TAIL_W)

    def tok(width, col):
        return pl.BlockSpec((nb, c, width), lambda b, t: (b, t, col // width))

    def lw(shape):
        nd = len(shape)
        return pl.BlockSpec((None,) + shape, lambda b, t: (layer,) + (0,) * nd)

    st_shape = (nb, DN_H, DN_DK, DN_DV)
    assert COL_QKVD % DN_QKV == 0
    group = SEQ_GROUP if nb % SEQ_GROUP == 0 else 1
    stack_arg, stack_spec, aliases = _stack_alias(stack, 9)
    mix, state = pl.pallas_call(
        functools.partial(_gdn_kernel, nb=nb, c=c, cv=cv, single_chunk=ntc == 1),
        grid=grid,
        in_specs=[tok(DN_QKV, COL_QKVD),
                  tok(W_MIX, COL_ZD),
                  tok(TAIL_W, 0),
                  pl.BlockSpec((None,) + st_shape, lambda b, t: (state_layer, b, 0, 0, 0)),
                  pl.BlockSpec((None, nb, CONV_W - 1, DN_QKV), lambda b, t: (state_layer, b, 0, 0)),
                  lw((CONV_W, DN_QKV)), lw((1, TAIL_W)), lw((1, TAIL_W)), lw((1, DN_DV)), stack_spec],
        out_specs=[tok(W_MIX, 0),
                   pl.BlockSpec((None,) + st_shape, lambda b, t: (layer, b, 0, 0, 0))],
        out_shape=[jax.ShapeDtypeStruct((batch, seq, W_MIX), BF16),
                   jax.ShapeDtypeStruct((n_layers, batch, DN_H, DN_DK, DN_DV), F32)],
        input_output_aliases=aliases,
        scratch_shapes=[pltpu.VMEM(st_shape, F32),
                        pltpu.VMEM((group, CONV_PAD + c, DN_QKV), F32),
                        pltpu.VMEM((nb, CONV_PAD, DN_QKV), F32),
                        pltpu.VMEM((nb, c, W_MIX), F32)],
        compiler_params=_cparams(2),
        name="gdn_mixer",
    )(proj3, proj3, tail3, s0, conv0, cw, alog, dtb, go, stack_arg)
    return mix.reshape(batch * seq, W_MIX), state


def _prepare_weights(w_in, s5_lam_re, s5_lam_im, s5_log_dt, s5_b_re, s5_b_im, s5_c_re, s5_c_im, s5_d,
                     s5_w_glu, lru_w_r, lru_w_i, gla_w_gate, dn_a_log, dn_dt_bias, w_out, ple_w, ple_gate_w):
    nl = w_in.shape[0]
    p = {}
    p["w_main"], p["w_tail"] = repack_w_in(jnp.swapaxes(w_in, 1, 2))
    p["w_out"] = w_out.astype(BF16)
    p["ple_w"] = ple_w.astype(BF16)
    p["ple_gate_w"] = ple_gate_w.astype(BF16)
    p["w_glu"] = s5_w_glu.astype(BF16)
    ldt = jnp.repeat(s5_log_dt, S5_STATE, axis=1).reshape(nl, 1, S5_CH)
    coef, cre, cim = s5_prep(s5_lam_re.reshape(nl, 1, S5_CH), s5_lam_im.reshape(nl, 1, S5_CH), ldt)
    bb_re, bb_im = s5_bb(cre.reshape(nl, S5_CH, 1), cim.reshape(nl, S5_CH, 1),
                         s5_b_re.reshape(nl, S5_CH, S5_GROUP), s5_b_im.reshape(nl, S5_CH, S5_GROUP))
    eye16 = jnp.eye(16, dtype=F32)
    eye8 = jnp.eye(8, dtype=F32)

    def pack_u(bb):
        t = bb.reshape(nl, 4, 16, S5_STATE, S5_GROUP)
        return jnp.einsum("lkgph,gG->lkghGp", t, eye16).reshape(nl, 4, 256, 1024).astype(BF16)

    def pack_y(cc):
        t = cc.reshape(nl, 8, 8, S5_GROUP, S5_STATE)
        return jnp.einsum("ljghp,gG->ljgpGh", t, eye8).reshape(nl, 8, 512, 128).astype(BF16)

    p["s5_coef"] = coef
    p["s5_wur"], p["s5_wui"] = pack_u(bb_re), pack_u(bb_im)
    p["s5_wyr"], p["s5_wyi"] = pack_y(s5_c_re), pack_y(s5_c_im)
    p["s5_d"] = s5_d.reshape(nl, 1, W_MIX)
    p["lru_wri"] = jnp.concatenate([lru_w_r, lru_w_i], axis=-1).astype(BF16)
    p["gla_wg"] = jnp.concatenate([gla_w_gate, jnp.zeros((nl, TAIL_W - GLA_RANK, GLA_H * GLA_DK), F32)],
                                  axis=1).astype(BF16)

    def tail_row(x):
        return jnp.pad(x, ((0, 0), (TAIL_A, TAIL_W - TAIL_A - DN_H))).reshape(nl, 1, TAIL_W)

    p["dn_alog"] = tail_row(dn_a_log)
    p["dn_dtb"] = tail_row(dn_dt_bias)
    return p


def kernel(x_prompt, x_sample, state_s5_re, state_s5_im, state_lru_h, state_lru_conv, state_gla, state_delta, state_delta_conv, p_prompt, p_sample, g_norm, w_in, s5_lam_re, s5_lam_im, s5_log_dt, s5_b_re, s5_b_im, s5_c_re, s5_c_im, s5_d, s5_w_glu, s5_b_glu, lru_conv_w, lru_conv_b, lru_w_r, lru_b_r, lru_w_i, lru_b_i, lru_lam, gla_w_gate, gla_b_gate, gla_g_out, dn_conv_w, dn_a_log, dn_dt_bias, dn_g_out, w_out, ple_w, ple_gate_w, ple_gate_b, g_final):
    nl = w_in.shape[0]
    bp, tp, _ = x_prompt.shape
    bs, ts, _ = x_sample.shape
    tsp = SAMPLE_T_PAD
    mp, ms = bp * tp, bs * ts

    p = _prepare_weights(w_in, s5_lam_re, s5_lam_im, s5_log_dt, s5_b_re, s5_b_im, s5_c_re, s5_c_im, s5_d,
                         s5_w_glu, lru_w_r, lru_w_i, gla_w_gate, dn_a_log, dn_dt_bias, w_out, ple_w, ple_gate_w)

    def vec(x):
        return x.reshape(nl, 1, x.shape[-1])

    g_norm3, b_glu3, cb3 = vec(g_norm), vec(s5_b_glu), vec(lru_conv_b)
    br3, bi3, lam3 = vec(lru_b_r), vec(lru_b_i), vec(lru_lam)
    bg3, go_c3, go_d3, pgb3 = vec(gla_b_gate), vec(gla_g_out), vec(dn_g_out), vec(ple_gate_b)
    pe_p = p_prompt.reshape(nl, mp, D_PLE)
    pe_s = p_sample.reshape(nl, ms, D_PLE)

    z_s5 = jnp.zeros((1, bp, 1, S5_CH), F32)
    z_lru = jnp.zeros((1, bp, 1, W_MIX), F32)
    z_lconv = jnp.zeros((1, bp, CONV_W - 1, W_MIX), F32)
    z_gla = jnp.zeros((1, bp, GLA_H, GLA_DK, GLA_DV), F32)
    z_dn = jnp.zeros((1, bp, DN_H, DN_DK, DN_DV), F32)
    z_dconv = jnp.zeros((1, bp, CONV_W - 1, DN_QKV), F32)
    c_s5r = state_s5_re.reshape(nl, bs, 1, S5_CH)
    c_s5i = state_s5_im.reshape(nl, bs, 1, S5_CH)
    c_lru = state_lru_h.reshape(nl, bs, 1, W_MIX)

    def mixers(proj, tail, layer, *, batch, seq, t_valid, st, sl, nb_scan, tt, c, nb_mat, stacks):
        s5r0, s5i0, lru0, lconv0, gla0, dn0, dconv0 = st
        gla_stack, dn_stack = stacks
        mix_a, s5r, s5i = s5_mixer(proj, s5r0, s5i0, p["s5_coef"], p["s5_wur"], p["s5_wui"], p["s5_wyr"],
                                   p["s5_wyi"], p["s5_d"], p["w_glu"], b_glu3, layer, sl,
                                   batch=batch, seq=seq, t_valid=t_valid, nb=nb_scan, tt=tt)
        mix_b, lruh = lru_mixer(proj, lru0, lconv0, lru_conv_w, cb3, p["lru_wri"], br3, bi3, lam3, layer, sl,
                                batch=batch, seq=seq, t_valid=t_valid, nb=nb_scan, tt=tt)
        mix_c, glas = gla_mixer(proj, tail, gla0, p["gla_wg"], bg3, go_c3, layer, sl,
                                batch=batch, seq=seq, c=c, cv=min(c, t_valid), nb=nb_mat, n_layers=nl,
                                stack=gla_stack)
        mix_d, dns = gdn_mixer(proj, tail, dn0, dconv0, dn_conv_w, p["dn_alog"], p["dn_dtb"], go_d3, layer, sl,
                               batch=batch, seq=seq, c=c, cv=min(c, t_valid), nb=nb_mat, n_layers=nl,
                               stack=dn_stack)
        return (mix_a, mix_b, mix_c, mix_d), (s5r, s5i, lruh, glas, dns)

    hp = x_prompt.reshape(mp, D_MODEL)
    hs = x_sample.reshape(ms, D_MODEL)
    new_p, new_s = [], []
    stacks_p = stacks_s = (None, None)
    hgp, ssqp = prenorm(hp, g_norm3, 0, tm=DENSE_TILES_SAMPLE["tm"])
    hgs, ssqs = prenorm(hs, g_norm3, 0, tm=DENSE_TILES_SAMPLE["tm"])
    for i in range(nl):
        nxt = min(i + 1, nl - 1)
        proj, tail = in_proj(hgp, ssqp, p["w_main"], p["w_tail"], i, **DENSE_TILES_PROMPT)
        mixes, (s5r, s5i, lruh, glas, dns) = mixers(
            proj, tail, i, batch=bp, seq=tp, t_valid=tp,
            st=(z_s5, z_s5, z_lru, z_lconv, z_gla, z_dn, z_dconv), sl=0,
            nb_scan=1, tt=256, c=GLA_CHUNK, nb_mat=SEQ_GROUP, stacks=stacks_p)
        stacks_p = (glas, dns)
        proj3 = proj.reshape(bp, tp, N_MAIN)
        new_p.append((s5r.reshape(bp, S5_GROUPS, S5_STATE), s5i.reshape(bp, S5_GROUPS, S5_STATE),
                      lruh.reshape(bp, W_MIX), proj3[:, tp - (CONV_W - 1):, COL_XB:COL_XB + W_MIX],
                      None, None, proj3[:, tp - (CONV_W - 1):, COL_QKVD:COL_QKVD + DN_QKV]))
        hp, hpb = out_proj(mixes, p["w_out"], hp, i, **DENSE_TILES_PROMPT)
        hp, hgp, ssqp = ple(hp, hpb, pe_p, p["ple_gate_w"], pgb3, p["ple_w"], g_norm3, i, nxt, **DENSE_TILES_PROMPT)

        proj, tail = in_proj(hgs, ssqs, p["w_main"], p["w_tail"], i, **DENSE_TILES_SAMPLE)
        proj3 = proj.reshape(bs, ts, N_MAIN)
        proj_pad = jnp.pad(proj3, ((0, 0), (0, tsp - ts), (0, 0))).reshape(bs * tsp, N_MAIN)
        tail_pad = jnp.pad(tail.reshape(bs, ts, TAIL_W), ((0, 0), (0, tsp - ts), (0, 0))).reshape(bs * tsp, TAIL_W)
        mixes, (s5r, s5i, lruh, glas, dns) = mixers(
            proj_pad, tail_pad, i, batch=bs, seq=tsp, t_valid=ts,
            st=(c_s5r, c_s5i, c_lru, state_lru_conv, state_gla, state_delta, state_delta_conv), sl=i,
            nb_scan=32, tt=tsp, c=tsp, nb_mat=8, stacks=stacks_s)
        stacks_s = (glas, dns)
        mixes = tuple(m.reshape(bs, tsp, W_MIX)[:, :ts].reshape(ms, W_MIX) for m in mixes)
        new_s.append((s5r.reshape(bs, S5_GROUPS, S5_STATE), s5i.reshape(bs, S5_GROUPS, S5_STATE),
                      lruh.reshape(bs, W_MIX), proj3[:, ts - (CONV_W - 1):, COL_XB:COL_XB + W_MIX],
                      None, None, proj3[:, ts - (CONV_W - 1):, COL_QKVD:COL_QKVD + DN_QKV]))
        hs, hsb = out_proj(mixes, p["w_out"], hs, i, **DENSE_TILES_SAMPLE)
        hs, hgs, ssqs = ple(hs, hsb, pe_s, p["ple_gate_w"], pgb3, p["ple_w"], g_norm3, i, nxt, **DENSE_TILES_SAMPLE)

    g_fin = g_final.reshape(1, D_MODEL)
    y_prompt = final_norm(hp, g_fin, tm=256).reshape(bp, tp, D_MODEL)
    y_sample = final_norm(hs, g_fin, tm=256).reshape(bs, ts, D_MODEL)

    def stk(lst, j):
        return jnp.stack([s[j] for s in lst], axis=0)

    return (y_prompt, y_sample,
            stk(new_p, 0), stk(new_p, 1), stk(new_p, 2), stk(new_p, 3), stacks_p[0], stacks_p[1], stk(new_p, 6),
            stk(new_s, 0), stk(new_s, 1), stk(new_s, 2), stk(new_s, 3), stacks_s[0], stacks_s[1], stk(new_s, 6))
```

```python
import functools
import math

import jax
import jax.numpy as jnp
import numpy as np
from jax import lax
from jax.experimental import pallas as pl
from jax.experimental.pallas import tpu as pltpu

F32 = jnp.float32
BF16 = jnp.bfloat16
EPS = 1e-6
NEG_INF = float("-inf")

D_MODEL = 4096
DEPTH = 4
W_MIX = 1024
S5_GROUPS = 64
S5_GROUP = 16
S5_STATE = 64
S5_CH = S5_GROUPS * S5_STATE
LRU_BLOCKS = 8
LRU_BLK = 128
LRU_C = 8.0
CONV_W = 4
GLA_H = 4
GLA_DK = 128
GLA_DV = 256
GLA_RANK = 16
GLA_TAU = 16.0
GLA_CHUNK = 64
DN_H = 8
DN_DK = 128
DN_DV = 128
DN_QKV = 3072
DN_CHUNK = 64
D_PLE = 256
N_MAIN = 11264
COL_XA, COL_ZA, COL_XB, COL_ZB = 0, 1024, 2048, 3072
COL_QC, COL_KC, COL_VC = 4096, 4608, 5120
COL_QKVD, COL_ZC, COL_ZD = 6144, 9216, 10240
ORIG_ZC, ORIG_GC, ORIG_QKVD, ORIG_ZD, ORIG_AB = 6144, 7168, 7184, 10256, 11280
TAIL_W = 128
TAIL_G, TAIL_A, TAIL_B = 0, 16, 24
SAMPLE_T_PAD = 8

V7X_LANES = 128
V7X_SUBLANES = 8
V7X_MXU_DEPTH = 256
VMEM_LIMIT = 52 * 1024 * 1024
DENSE_TILES_PROMPT = dict(tm=1024, tn=512)
IN_PROJ_TILES_PROMPT = dict(tm=1024, tn=1024)
DENSE_TILES_SAMPLE = dict(tm=512, tn=1024)


def _cparams(n_axes):
    return pltpu.CompilerParams(dimension_semantics=("arbitrary",) * n_axes,
                                vmem_limit_bytes=VMEM_LIMIT)


def _sigmoid(x):
    return jax.nn.sigmoid(x)


def _silu(x):
    return x * jax.nn.sigmoid(x)


def _softplus(x):
    return jnp.maximum(x, 0.0) + jnp.log1p(jnp.exp(-jnp.abs(x)))


def _neg_expm1_nonpos(x):
    t = jnp.tanh(0.5 * x)
    return (-2.0 * t) / (1.0 - t)


def _log_sigmoid(x):
    return jnp.minimum(x, 0.0) - jnp.log1p(jnp.exp(-jnp.abs(x)))


def _gelu_tanh(x):
    c = math.sqrt(2.0 / math.pi)
    return x * (0.5 * (1.0 + jnp.tanh(c * (x + 0.044715 * (x * x * x)))))


def _dot(a, b):
    return jnp.dot(a, b, preferred_element_type=F32)


def _dot_nt(a, b):
    return lax.dot_general(a, b, (((1,), (1,)), ((), ())), preferred_element_type=F32)


def _dot_tn(a, b):
    return lax.dot_general(a, b, (((0,), (0,)), ((), ())), preferred_element_type=F32)


def _split3(x):
    h = x.astype(BF16)
    r = x - h.astype(F32)
    m = r.astype(BF16)
    l = (r - m.astype(F32)).astype(BF16)
    return h, m, l


def _cumsum_rows(x):
    c = x.shape[0]
    if c <= V7X_SUBLANES:
        row = lax.broadcasted_iota(jnp.int32, x.shape, 0)
        k = 1
        while k < c:
            x = x + jnp.where(row >= k, pltpu.roll(x, k, 0), 0.0)
            k *= 2
        return x
    h, m, l = _split3(x)
    if 3 * c <= V7X_MXU_DEPTH:
        col = lax.broadcasted_iota(jnp.int32, (c, 3 * c), 1)
        col = jnp.where(col >= 2 * c, col - 2 * c, jnp.where(col >= c, col - c, col))
        tri3 = (lax.broadcasted_iota(jnp.int32, (c, 3 * c), 0) >= col).astype(BF16)
        return _dot(tri3, jnp.concatenate([h, m, l], axis=0))
    tri = (lax.broadcasted_iota(jnp.int32, (c, c), 0) >= lax.broadcasted_iota(jnp.int32, (c, c), 1)).astype(BF16)
    return _dot(tri, h) + _dot(tri, m) + _dot(tri, l)


def _mm(a, b, dot=_dot, contract=1):
    if a.shape[contract] < 2 * V7X_SUBLANES:
        return dot(a, b)
    return dot(a.astype(BF16), b.astype(BF16))


def _dot_x3(a, b):
    ah32 = a.astype(BF16).astype(F32)
    al32 = a - ah32
    bh32 = b.astype(BF16).astype(F32)
    bl32 = b - bh32
    k = a.shape[1]
    if 3 * k <= V7X_MXU_DEPTH:
        lhs = jnp.concatenate([ah32, ah32, al32], axis=1).astype(BF16)
        rhs = jnp.concatenate([bh32, bl32, bh32], axis=0).astype(BF16)
        return _dot(lhs, rhs)
    ah, al, bh, bl = (t.astype(BF16) for t in (ah32, al32, bh32, bl32))
    return _dot(ah, bh) + _dot(ah, bl) + _dot(al, bh)


def _prenorm_kernel(x_ref, g_ref, hg_ref, ssq_ref):
    x = x_ref[...]
    hg_ref[...] = (x * g_ref[...]).astype(BF16)
    ssq_ref[...] = jnp.sum(x * x, axis=-1, keepdims=True)


def prenorm(x, g_norm, layer, *, tm):
    m = x.shape[0]
    return pl.pallas_call(
        _prenorm_kernel,
        grid=(m // tm,),
        in_specs=[pl.BlockSpec((tm, D_MODEL), lambda i: (i, 0)),
                  pl.BlockSpec((None, 1, D_MODEL), lambda i: (layer, 0, 0))],
        out_specs=[pl.BlockSpec((tm, D_MODEL), lambda i: (i, 0)), pl.BlockSpec((tm, 1), lambda i: (i, 0))],
        out_shape=[jax.ShapeDtypeStruct((m, D_MODEL), BF16), jax.ShapeDtypeStruct((m, 1), F32)],
        compiler_params=_cparams(1),
        name="prenorm",
    )(x, g_norm)


def _in_proj_kernel(hg_ref, ssq_ref, w_ref, wt_ref, o_ref, ot_ref, *, seq_rows, seq_pad):
    r = lax.rsqrt(ssq_ref[...] * (1.0 / D_MODEL) + EPS)

    def put(dst_ref, val):
        if seq_pad == seq_rows:
            dst_ref[...] = val
        else:
            n_seq = val.shape[0] // seq_rows
            dst_ref[:, 0:seq_rows, :] = val.reshape(n_seq, seq_rows, val.shape[1])
            dst_ref[:, seq_rows:seq_pad, :] = jnp.zeros((n_seq, seq_pad - seq_rows, val.shape[1]), F32)

    @pl.when(pl.program_id(1) == 0)
    def _():
        put(ot_ref, _dot_nt(hg_ref[...], wt_ref[...]) * r)

    put(o_ref, _dot_nt(hg_ref[...], w_ref[...]) * r)


def in_proj(hg, ssq, w_main, w_tail, layer, *, tm, tn, seq_rows=None, seq_pad=None):
    m = hg.shape[0]
    grid = (m // tm, N_MAIN // tn)
    if seq_rows is None:
        seq_rows = seq_pad = 1
        out_specs = [pl.BlockSpec((tm, tn), lambda i, j: (i, j)), pl.BlockSpec((tm, TAIL_W), lambda i, j: (i, 0))]
        out_shape = [jax.ShapeDtypeStruct((m, N_MAIN), F32), jax.ShapeDtypeStruct((m, TAIL_W), F32)]
    else:
        n_seq = tm // seq_rows
        out_specs = [pl.BlockSpec((n_seq, seq_pad, tn), lambda i, j: (i, 0, j)),
                     pl.BlockSpec((n_seq, seq_pad, TAIL_W), lambda i, j: (i, 0, 0))]
        out_shape = [jax.ShapeDtypeStruct((m // seq_rows, seq_pad, N_MAIN), F32),
                     jax.ShapeDtypeStruct((m // seq_rows, seq_pad, TAIL_W), F32)]
    proj, tail = pl.pallas_call(
        functools.partial(_in_proj_kernel, seq_rows=seq_rows, seq_pad=seq_pad),
        grid=grid,
        in_specs=[
            pl.BlockSpec((tm, D_MODEL), lambda i, j: (i, 0)),
            pl.BlockSpec((tm, 1), lambda i, j: (i, 0)),
            pl.BlockSpec((None, tn, D_MODEL), lambda i, j: (layer, j, 0)),
            pl.BlockSpec((None, TAIL_W, D_MODEL), lambda i, j: (layer, 0, 0)),
        ],
        out_specs=out_specs,
        out_shape=out_shape,
        compiler_params=_cparams(2),
        name="in_proj",
    )(hg, ssq, w_main, w_tail)
    return proj.reshape(-1, N_MAIN), tail.reshape(-1, TAIL_W)


def _out_proj_kernel(ma_ref, mb_ref, mc_ref, md_ref, w_ref, h_ref, o_ref, ob_ref):
    acc = h_ref[...]
    for k, m_ref in enumerate((ma_ref, mb_ref, mc_ref, md_ref)):
        acc = acc + _dot(m_ref[...], w_ref[k * W_MIX:(k + 1) * W_MIX, :])
    o_ref[...] = acc
    ob_ref[...] = acc.astype(BF16)


def out_proj(mixes, w_out, h, layer, *, tm, tn):
    m = h.shape[0]
    grid = (m // tm, D_MODEL // tn)
    mix_spec = pl.BlockSpec((tm, W_MIX), lambda i, j: (i, 0))
    tile = pl.BlockSpec((tm, tn), lambda i, j: (i, j))
    return pl.pallas_call(
        _out_proj_kernel,
        grid=grid,
        in_specs=[mix_spec, mix_spec, mix_spec, mix_spec,
                  pl.BlockSpec((None, 4 * W_MIX, tn), lambda i, j: (layer, 0, j)), tile],
        out_specs=[tile, tile],
        out_shape=[jax.ShapeDtypeStruct((m, D_MODEL), F32), jax.ShapeDtypeStruct((m, D_MODEL), BF16)],
        compiler_params=_cparams(2),
        name="out_proj",
    )(*mixes, w_out, h)


def _ple_kernel(hrow_ref, htile_ref, pe_ref, wg_ref, bg_ref, wp_ref, gn_ref, o_ref, hg_ref, ssq_ref):
    gate = _sigmoid(_dot(hrow_ref[...], wg_ref[...]) + bg_ref[...])
    pv = _dot(pe_ref[...].astype(BF16), wp_ref[...])
    h = htile_ref[...] + gate * pv
    o_ref[...] = h
    hg_ref[...] = (h * gn_ref[...]).astype(BF16)
    part = jnp.sum(h * h, axis=-1, keepdims=True)

    @pl.when(pl.program_id(1) == 0)
    def _():
        ssq_ref[...] = part

    @pl.when(pl.program_id(1) != 0)
    def _():
        ssq_ref[...] = ssq_ref[...] + part


def ple(h, hb, pe, w_gate, b_gate, w_ple, g_next, layer, next_layer, *, tm, tn):
    m = h.shape[0]
    grid = (m // tm, D_MODEL // tn)
    tile = pl.BlockSpec((tm, tn), lambda i, j: (i, j))
    return pl.pallas_call(
        _ple_kernel,
        grid=grid,
        in_specs=[
            pl.BlockSpec((tm, D_MODEL), lambda i, j: (i, 0)),
            tile,
            pl.BlockSpec((None, tm, D_PLE), lambda i, j: (layer, i, 0)),
            pl.BlockSpec((None, D_MODEL, tn), lambda i, j: (layer, 0, j)),
            pl.BlockSpec((None, 1, tn), lambda i, j: (layer, 0, j)),
            pl.BlockSpec((None, D_PLE, tn), lambda i, j: (layer, 0, j)),
            pl.BlockSpec((None, 1, tn), lambda i, j: (next_layer, 0, j)),
        ],
        out_specs=[tile, tile, pl.BlockSpec((tm, 1), lambda i, j: (i, 0))],
        out_shape=[jax.ShapeDtypeStruct((m, D_MODEL), F32), jax.ShapeDtypeStruct((m, D_MODEL), BF16),
                   jax.ShapeDtypeStruct((m, 1), F32)],
        compiler_params=_cparams(2),
        name="ple",
    )(hb, h, pe, w_gate, b_gate, w_ple, g_next)


REPACK_ROWS = 512


def _repack_kernel(a_ref, g_ref, ab_ref, o_ref, t_ref):
    o_ref[...] = a_ref[...].astype(BF16)

    @pl.when(pl.program_id(1) == 0)
    def _():
        t_ref[...] = jnp.zeros(t_ref.shape, BF16)
        t_ref[TAIL_G:TAIL_G + GLA_RANK, :] = g_ref[...].astype(BF16)
        t_ref[TAIL_A:TAIL_A + 2 * DN_H, :] = ab_ref[...].astype(BF16)


def repack_w_in(w_in_t):
    nl = w_in_t.shape[0]
    n_tiles = N_MAIN // REPACK_ROWS
    t_qkvd, t_zc, t_zd = COL_QKVD // REPACK_ROWS, COL_ZC // REPACK_ROWS, COL_ZD // REPACK_ROWS

    def src_row(j):
        return jnp.where(j < t_qkvd, j * REPACK_ROWS,
                         jnp.where(j < t_zc, ORIG_QKVD + (j - t_qkvd) * REPACK_ROWS,
                                   jnp.where(j < t_zd, ORIG_ZC + (j - t_zc) * REPACK_ROWS,
                                             ORIG_ZD + (j - t_zd) * REPACK_ROWS)))

    row_align = 2 * V7X_SUBLANES
    assert all(o % row_align == 0 for o in (ORIG_QKVD, ORIG_ZC, ORIG_ZD, ORIG_GC, ORIG_AB))

    def rows(n, start):
        return pl.BlockSpec((None, pl.Element(n), pl.Element(D_MODEL)),
                            lambda l, j: (l, pl.multiple_of(start(j), row_align), 0))

    return pl.pallas_call(
        _repack_kernel,
        grid=(nl, n_tiles),
        in_specs=[rows(REPACK_ROWS, src_row), rows(GLA_RANK, lambda j: ORIG_GC), rows(2 * DN_H, lambda j: ORIG_AB)],
        out_specs=[pl.BlockSpec((None, REPACK_ROWS, D_MODEL), lambda l, j: (l, j, 0)),
                   pl.BlockSpec((None, TAIL_W, D_MODEL), lambda l, j: (l, 0, 0))],
        out_shape=[jax.ShapeDtypeStruct((nl, N_MAIN, D_MODEL), BF16),
                   jax.ShapeDtypeStruct((nl, TAIL_W, D_MODEL), BF16)],
        compiler_params=_cparams(2),
        name="repack_w_in",
    )(w_in_t, w_in_t, w_in_t)


def _final_norm_kernel(x_ref, g_ref, o_ref):
    x = x_ref[...]
    ms = jnp.mean(x * x, axis=-1, keepdims=True)
    o_ref[...] = (x * lax.rsqrt(ms + EPS)) * g_ref[...]


def final_norm(h, g, *, tm):
    m = h.shape[0]
    return pl.pallas_call(
        _final_norm_kernel,
        grid=(m // tm,),
        in_specs=[pl.BlockSpec((tm, D_MODEL), lambda i: (i, 0)),
                  pl.BlockSpec((1, D_MODEL), lambda i: (0, 0))],
        out_specs=pl.BlockSpec((tm, D_MODEL), lambda i: (i, 0)),
        out_shape=jax.ShapeDtypeStruct((m, D_MODEL), F32),
        compiler_params=_cparams(1),
        name="final_norm",
    )(h, g)


S5_SEG = 32


def _s5_prep_kernel(lre_ref, lim_ref, ldt_ref, coef_ref, cre_ref, cim_ref):
    lre = lre_ref[...]
    lim = lim_ref[...]
    dt = jnp.exp(ldt_ref[...])
    ai = lim * dt
    mag = jnp.exp(lre * dt)
    ar = mag * jnp.cos(ai)
    aim = mag * jnp.sin(ai)
    den = lre * lre + lim * lim
    nr = ar - 1.0
    cre_ref[...] = (nr * lre + aim * lim) / den
    cim_ref[...] = (aim * lre - nr * lim) / den
    pr, pim = [ar], [aim]
    for _ in range(S5_SEG - 1):
        nr_, ni_ = pr[-1] * ar - pim[-1] * aim, pr[-1] * aim + pim[-1] * ar
        pr.append(nr_)
        pim.append(ni_)
    shape = (V7X_SUBLANES, S5_CH)
    coef_ref[0] = jnp.broadcast_to(ar, shape)
    coef_ref[1] = jnp.broadcast_to(aim, shape)
    coef_ref[2] = jnp.broadcast_to(pr[S5_SEG - 1], shape)
    coef_ref[3] = jnp.broadcast_to(pim[S5_SEG - 1], shape)


def s5_prep(lam_re, lam_im, log_dt_rep):
    nl = lam_re.shape[0]
    vec = pl.BlockSpec((None, 1, S5_CH), lambda l: (l, 0, 0))
    return pl.pallas_call(
        _s5_prep_kernel,
        grid=(nl,),
        in_specs=[vec, vec, vec],
        out_specs=[pl.BlockSpec((None, 4, V7X_SUBLANES, S5_CH), lambda l: (l, 0, 0, 0)), vec, vec],
        out_shape=[jax.ShapeDtypeStruct((nl, 4, V7X_SUBLANES, S5_CH), F32),
                   jax.ShapeDtypeStruct((nl, 1, S5_CH), F32),
                   jax.ShapeDtypeStruct((nl, 1, S5_CH), F32)],
        compiler_params=_cparams(1),
        name="s5_prep",
    )(lam_re, lam_im, log_dt_rep)


def _s5_bb_kernel(cr_ref, ci_ref, br_ref, bi_ref, or_ref, oi_ref):
    cr = cr_ref[...]
    ci = ci_ref[...]
    br = br_ref[...]
    bi = bi_ref[...]
    or_ref[...] = cr * br - ci * bi
    oi_ref[...] = cr * bi + ci * br


def s5_bb(coef_re_col, coef_im_col, b_re, b_im):
    nl = b_re.shape[0]
    rows = 1024
    col = pl.BlockSpec((None, rows, 1), lambda l, r: (l, r, 0))
    mat = pl.BlockSpec((None, rows, S5_GROUP), lambda l, r: (l, r, 0))
    return pl.pallas_call(
        _s5_bb_kernel,
        grid=(nl, S5_CH // rows),
        in_specs=[col, col, mat, mat],
        out_specs=[mat, mat],
        out_shape=[jax.ShapeDtypeStruct((nl, S5_CH, S5_GROUP), F32)] * 2,
        compiler_params=_cparams(2),
        name="s5_bb",
    )(coef_re_col, coef_im_col, b_re, b_im)


S5_STRIP = 512


def _s5_kernel(xa_ref, za_ref, h0r_ref, h0i_ref, coef_ref, perm_ref, permt_ref,
               wur_ref, wui_ref, wyr_ref, wyi_ref, dsk_ref, wglu_ref, bglu_ref,
               mix_ref, hro_ref, hio_ref,
               ur_scr, ui_scr, cr_scr, ci_scr, y_scr, *, nb, tt, t_last):
    tc = pl.program_id(1)
    ntc = pl.num_programs(1)
    split = nb == 1
    lc = tt // V7X_SUBLANES if split else tt
    n_sets = 1 if split else nb // V7X_SUBLANES
    set_rows = V7X_SUBLANES * lc

    xb16 = _dot(perm_ref[...], xa_ref[...].astype(BF16)).astype(BF16)
    for k in range(4):
        xk = xb16[:, 256 * k:256 * (k + 1)]
        ur_scr[:, 1024 * k:1024 * (k + 1)] = _dot(xk, wur_ref[k])
        ui_scr[:, 1024 * k:1024 * (k + 1)] = _dot(xk, wui_ref[k])

    @pl.when(tc == 0)
    def _():
        cr_scr[...] = h0r_ref[...]
        ci_scr[...] = h0i_ref[...]

    row8 = lax.broadcasted_iota(jnp.int32, (V7X_SUBLANES, S5_STRIP), 0)
    for st in range(S5_CH // S5_STRIP):
        ls = slice(st * S5_STRIP, (st + 1) * S5_STRIP)
        ar = coef_ref[0, :, ls]
        ai = coef_ref[1, :, ls]
        for s in range(n_sets):
            base = s * set_rows
            if split:
                h0r = jnp.zeros((V7X_SUBLANES, S5_STRIP), F32)
                h0i = jnp.zeros((V7X_SUBLANES, S5_STRIP), F32)
            else:
                h0r = jnp.zeros((V7X_SUBLANES, S5_STRIP), F32)
                h0i = jnp.zeros((V7X_SUBLANES, S5_STRIP), F32)
                for k in range(V7X_SUBLANES):
                    h0r = jnp.where(row8 == k, cr_scr[s * V7X_SUBLANES + k, :, ls], h0r)
                    h0i = jnp.where(row8 == k, ci_scr[s * V7X_SUBLANES + k, :, ls], h0i)

            def step(j, h, base=base, ls=ls, ar=ar, ai=ai):
                hr, hi = h
                rows = pl.ds(pl.multiple_of(base + j * V7X_SUBLANES, V7X_SUBLANES), V7X_SUBLANES)
                nr = (ar * hr - ai * hi) + ur_scr[rows, ls]
                ni = (ar * hi + ai * hr) + ui_scr[rows, ls]
                ur_scr[rows, ls] = nr
                ui_scr[rows, ls] = ni
                return nr, ni

            fr, fi = lax.fori_loop(0, lc, step, (h0r, h0i), unroll=True)
            if split:
                alr = coef_ref[2, 0:1, ls]
                ali = coef_ref[3, 0:1, ls]
                cr = cr_scr[0, :, ls]
                ci = ci_scr[0, :, ls]
                init_r = jnp.zeros((V7X_SUBLANES, S5_STRIP), F32)
                init_i = jnp.zeros((V7X_SUBLANES, S5_STRIP), F32)
                for k in range(V7X_SUBLANES):
                    init_r = jnp.where(row8 == k, cr, init_r)
                    init_i = jnp.where(row8 == k, ci, init_i)
                    cr, ci = (alr * cr - ali * ci) + fr[k:k + 1], (alr * ci + ali * cr) + fi[k:k + 1]
                cr_scr[0, :, ls] = cr
                ci_scr[0, :, ls] = ci

                def fix(j, c, base=base, ls=ls, ar=ar, ai=ai):
                    rows = pl.ds(pl.multiple_of(base + j * V7X_SUBLANES, V7X_SUBLANES), V7X_SUBLANES)
                    c_r = ar * c[0] - ai * c[1]
                    c_i = ar * c[1] + ai * c[0]
                    ur_scr[rows, ls] = ur_scr[rows, ls] + c_r
                    ui_scr[rows, ls] = ui_scr[rows, ls] + c_i
                    return c_r, c_i

                lax.fori_loop(0, lc, fix, (init_r, init_i), unroll=True)

        y_scr[:, V7X_LANES * st:V7X_LANES * (st + 1)] = (_dot(ur_scr[:, ls].astype(BF16), wyr_ref[st])
                                                           - _dot(ui_scr[:, ls].astype(BF16), wyi_ref[st]))
    yh, ym, yl = _split3(y_scr[...])
    pt = permt_ref[...]
    y = (_dot(pt, yh) + _dot(pt, ym) + _dot(pt, yl)) + dsk_ref[...] * xa_ref[...]
    ga = _gelu_tanh(y)
    ya = ga * _sigmoid(_dot(ga.astype(BF16), wglu_ref[...]) + bglu_ref[...])
    mix_ref[...] = (ya * _silu(za_ref[...])).astype(BF16)

    @pl.when(tc == ntc - 1)
    def _():
        if split:
            hro_ref[...] = cr_scr[...]
            hio_ref[...] = ci_scr[...]
        else:
            for s in range(n_sets):
                for k in range(V7X_SUBLANES):
                    r = s * set_rows + V7X_SUBLANES * t_last + k
                    hro_ref[s * V7X_SUBLANES + k] = ur_scr[r:r + 1, :]
                    hio_ref[s * V7X_SUBLANES + k] = ui_scr[r:r + 1, :]


def s5_mixer(proj, h0_re, h0_im, coef, wur, wui, wyr, wyi, dskip, wglu, bglu, layer, state_layer,
             *, batch, seq, t_valid, nb, tt):
    ntc = seq // tt
    assert nb == 1 or ntc == 1
    rows = nb * tt
    grid = (batch // nb, ntc)
    t_last = (t_valid - 1) - (ntc - 1) * tt
    assert 0 <= t_last < tt
    if nb == 1:
        assert tt == V7X_SUBLANES * S5_SEG and t_last == tt - 1
        lc = S5_SEG
    else:
        assert nb % V7X_SUBLANES == 0
        lc = tt
    idx = np.arange(rows)
    set_rows = V7X_SUBLANES * lc
    src = (idx // set_rows) * set_rows + (idx % V7X_SUBLANES) * lc + (idx % set_rows) // V7X_SUBLANES
    perm_np = np.zeros((rows, rows), np.float32)
    perm_np[idx, src] = 1.0
    perm = jnp.asarray(perm_np, BF16)
    perm_t = jnp.asarray(perm_np.T, BF16)
    sq = pl.BlockSpec((rows, rows), lambda b, t: (0, 0))

    def tok(cb):
        return pl.BlockSpec((rows, W_MIX), lambda b, t: (b * ntc + t, cb))

    st_in = pl.BlockSpec((None, nb, 1, S5_CH), lambda b, t: (state_layer, b, 0, 0))
    st_out = pl.BlockSpec((nb, 1, S5_CH), lambda b, t: (b, 0, 0))

    def lw(shape):
        nd = len(shape)
        return pl.BlockSpec((None,) + shape, lambda b, t: (layer,) + (0,) * nd)

    return pl.pallas_call(
        functools.partial(_s5_kernel, nb=nb, tt=tt, t_last=t_last),
        grid=grid,
        in_specs=[tok(COL_XA // W_MIX), tok(COL_ZA // W_MIX), st_in, st_in,
                  lw((4, V7X_SUBLANES, S5_CH)), sq, sq, lw((4, 256, 1024)), lw((4, 256, 1024)),
                  lw((8, 512, 128)), lw((8, 512, 128)), lw((1, W_MIX)), lw((W_MIX, W_MIX)), lw((1, W_MIX))],
        out_specs=[pl.BlockSpec((rows, W_MIX), lambda b, t: (b * ntc + t, 0)), st_out, st_out],
        out_shape=[jax.ShapeDtypeStruct((batch * seq, W_MIX), BF16),
                   jax.ShapeDtypeStruct((batch, 1, S5_CH), F32),
                   jax.ShapeDtypeStruct((batch, 1, S5_CH), F32)],
        scratch_shapes=[pltpu.VMEM((rows, S5_CH), F32), pltpu.VMEM((rows, S5_CH), F32),
                        pltpu.VMEM((nb, 1, S5_CH), F32), pltpu.VMEM((nb, 1, S5_CH), F32),
                        pltpu.VMEM((rows, W_MIX), F32)],
        compiler_params=_cparams(2),
        name="s5_mixer",
    )(proj, proj, h0_re, h0_im, coef, perm, perm_t, wur, wui, wyr, wyi, dskip, wglu, bglu)


LRU_STRIP = 512
CONV_PAD = V7X_SUBLANES


def _causal_conv(xp_scr, tail_scr, x, cw_ref, bi):
    del xp_scr
    rows_n = x.shape[0]
    width = x.shape[1]
    n_tiles = rows_n // V7X_SUBLANES
    xe = jnp.concatenate([tail_scr[bi], x], axis=0).reshape(n_tiles + 1, V7X_SUBLANES, width)
    row = lax.broadcasted_iota(jnp.int32, (n_tiles, V7X_SUBLANES, width), 1)
    acc = cw_ref[CONV_W - 1:CONV_W, :] * x
    for jj in range(CONV_W - 1):
        shift = CONV_W - 1 - jj
        rot = pltpu.roll(xe, shift, 1)
        shifted = jnp.where(row < shift, rot[:-1], rot[1:]).reshape(rows_n, width)
        acc = acc + cw_ref[jj:jj + 1, :] * shifted
    tail_scr[bi] = xe[n_tiles]
    return acc


def _init_conv_tail(tail_scr, cv0_ref, nb):
    tail_scr[...] = jnp.zeros(tail_scr.shape, F32)
    for bi in range(nb):
        tail_scr[bi, CONV_PAD - (CONV_W - 1):CONV_PAD, :] = cv0_ref[bi]


def _lru_kernel(xb_ref, zb_ref, h0_ref, cv0_ref, cw_ref, cb_ref, wri_ref, br_ref, bi_ref, lam_ref,
                mix_ref, ho_ref,
                xp_scr, tail_scr, xc_scr, a_scr, b_scr, c_scr, *, nb, tt, t_last):
    tc = pl.program_id(1)
    ntc = pl.num_programs(1)

    @pl.when(tc == 0)
    def _():
        c_scr[...] = h0_ref[...]
        _init_conv_tail(tail_scr, cv0_ref, nb)

    def conv_body(bi, carry):
        rows = pl.ds(pl.multiple_of(bi * tt, V7X_SUBLANES), tt)
        acc = _causal_conv(xp_scr, tail_scr, xb_ref[rows, :], cw_ref, bi)
        xc_scr[rows, :] = acc + cb_ref[...]
        return carry

    lax.fori_loop(0, nb, conv_body, 0)

    xc = xc_scr[...]
    xc16 = xc.astype(BF16)
    sp = _softplus(-lam_ref[...])
    for blk in range(LRU_BLOCKS):
        cs = slice(blk * LRU_BLK, (blk + 1) * LRU_BLK)
        pre = _dot(xc16[:, cs], wri_ref[blk])
        r = _sigmoid(pre[:, :LRU_BLK] + br_ref[:, cs])
        ig = _sigmoid(pre[:, LRU_BLK:] + bi_ref[:, cs])
        log_a = (-LRU_C) * r * sp[:, cs]
        a_scr[:, cs] = jnp.exp(log_a)
        b_scr[:, cs] = jnp.sqrt(_neg_expm1_nonpos(2.0 * log_a)) * (ig * xc[:, cs])

    ng = tt // V7X_SUBLANES
    for s in range(W_MIX // LRU_STRIP):
        ls = slice(s * LRU_STRIP, (s + 1) * LRU_STRIP)
        row = lax.broadcasted_iota(jnp.int32, (V7X_SUBLANES, LRU_STRIP), 0)

        def seq_body(bi, carry, ls=ls, row=row):
            def tile_body(g, c):
                rows = pl.ds(pl.multiple_of(bi * tt + g * V7X_SUBLANES, V7X_SUBLANES), V7X_SUBLANES)
                a = a_scr[rows, ls]
                b = b_scr[rows, ls]
                for k in (1, 2, 4):
                    a_sh = jnp.where(row >= k, pltpu.roll(a, k, 0), 1.0)
                    b_sh = jnp.where(row >= k, pltpu.roll(b, k, 0), 0.0)
                    b = b + a * b_sh
                    a = a * a_sh
                h = b + a * c
                b_scr[rows, ls] = h
                return h[V7X_SUBLANES - 1:, :]

            c = lax.fori_loop(0, ng, tile_body, c_scr[bi, :, ls])
            c_scr[bi, :, ls] = c
            return carry

        lax.fori_loop(0, nb, seq_body, 0)

    mix_ref[...] = (b_scr[...] * _silu(zb_ref[...])).astype(BF16)

    @pl.when(tc == ntc - 1)
    def _():
        for bi in range(nb):
            r = bi * tt + t_last
            ho_ref[bi] = b_scr[r:r + 1, :]


def lru_mixer(proj, h0, conv0, cw, cb, wri, br, bi_, lam, layer, state_layer, *, batch, seq, t_valid, nb, tt):
    ntc = seq // tt
    assert nb == 1 or ntc == 1
    rows = nb * tt
    grid = (batch // nb, ntc)
    t_last = (t_valid - 1) - (ntc - 1) * tt

    def tok(cb_):
        return pl.BlockSpec((rows, W_MIX), lambda b, t: (b * ntc + t, cb_))

    def lw(shape):
        nd = len(shape)
        return pl.BlockSpec((None,) + shape, lambda b, t: (layer,) + (0,) * nd)

    return pl.pallas_call(
        functools.partial(_lru_kernel, nb=nb, tt=tt, t_last=t_last),
        grid=grid,
        in_specs=[tok(COL_XB // W_MIX), tok(COL_ZB // W_MIX),
                  pl.BlockSpec((None, nb, 1, W_MIX), lambda b, t: (state_layer, b, 0, 0)),
                  pl.BlockSpec((None, nb, CONV_W - 1, W_MIX), lambda b, t: (state_layer, b, 0, 0)),
                  lw((CONV_W, W_MIX)), lw((1, W_MIX)), lw((LRU_BLOCKS, LRU_BLK, 2 * LRU_BLK)),
                  lw((1, W_MIX)), lw((1, W_MIX)), lw((1, W_MIX))],
        out_specs=[pl.BlockSpec((rows, W_MIX), lambda b, t: (b * ntc + t, 0)),
                   pl.BlockSpec((nb, 1, W_MIX), lambda b, t: (b, 0, 0))],
        out_shape=[jax.ShapeDtypeStruct((batch * seq, W_MIX), BF16),
                   jax.ShapeDtypeStruct((batch, 1, W_MIX), F32)],
        scratch_shapes=[pltpu.VMEM((CONV_PAD + tt, W_MIX), F32),
                        pltpu.VMEM((nb, CONV_PAD, W_MIX), F32),
                        pltpu.VMEM((rows, W_MIX), F32), pltpu.VMEM((rows, W_MIX), F32),
                        pltpu.VMEM((rows, W_MIX), F32), pltpu.VMEM((nb, 1, W_MIX), F32)],
        compiler_params=_cparams(2),
        name="lru_mixer",
    )(proj, proj, h0, conv0, cw, cb, wri, br, bi_, lam)


def _for_each_group(nb, g, group_body):
    assert nb % g == 0
    if nb == g:
        group_body(list(range(nb)))
    else:
        def body(i, carry):
            group_body([i * g + j for j in range(g)])
            return carry
        lax.fori_loop(0, nb // g, body, 0)


def _gla_kernel(q_ref, k_ref, v_ref, z_ref, tail_ref, s0_ref, wg_ref, bg_ref, go_ref, stack_ref,
                mix_ref, so_ref,
                s_scr, mix_scr, *, nb, group, c, cv, single_chunk):
    del stack_ref
    tc = pl.program_id(1)
    ntc = pl.num_programs(1)
    sb = min(16, c)
    nblk = c // sb

    s_src, s_dst = (s0_ref, so_ref) if single_chunk else (s_scr, s_scr)
    if not single_chunk:
        @pl.when(tc == 0)
        def _():
            s_scr[...] = s0_ref[...]

    row_c = lax.broadcasted_iota(jnp.int32, (c, 1), 0)
    lane_c = lax.broadcasted_iota(jnp.int32, (sb, c), 1)
    row_sb = lax.broadcasted_iota(jnp.int32, (sb, 1), 0)

    def group_body(bis):
        pairs = [(g, h) for g in range(len(bis)) for h in range(GLA_H)]
        heads = range(len(pairs))
        b_all, q_all, k_all, v_all = [], [], [], []
        for bi in bis:
            x = _dot(tail_ref[bi].astype(BF16), wg_ref[...]) + bg_ref[...]
            b_all.append(_cumsum_rows(_log_sigmoid(x) * (1.0 / GLA_TAU)))
            q_all.append(q_ref[bi])
            k_all.append(k_ref[bi])
            v_all.append(v_ref[bi])
        s_old = [s_src[bis[g], h] for g, h in pairs]
        q = [q_all[g][:, h * GLA_DK:(h + 1) * GLA_DK] * (GLA_DK ** -0.5) for g, h in pairs]
        k = [k_all[g][:, h * GLA_DK:(h + 1) * GLA_DK] for g, h in pairs]
        v = [v_all[g][:, h * GLA_DV:(h + 1) * GLA_DV] for g, h in pairs]
        b = [b_all[g][:, h * GLA_DK:(h + 1) * GLA_DK] for g, h in pairs]
        o_state = [_dot((q[h] * jnp.exp(b[h])).astype(BF16), s_old[h].astype(BF16)) for h in heads]
        s_new = []
        for h in heads:
            b_last = b[h][cv - 1:cv]
            dec = b_last - b[h]
            if cv < c:
                dec = jnp.where(row_c < cv, dec, NEG_INF)
            kd = k[h] * jnp.exp(dec)
            d_col = jnp.transpose(jnp.broadcast_to(jnp.exp(b_last), (V7X_SUBLANES, GLA_DK)))[:, 0:1]
            s_new.append(d_col * s_old[h] + _mm(kd, v[h], dot=_dot_tn, contract=0))
        att_off = []
        for h in heads:
            per_blk = [jnp.zeros((sb, c), F32)]
            for blk in range(1, nblk):
                r0 = blk * sb
                b_ref_row = b[h][r0 - 1:r0]
                qs = (q[h][r0:r0 + sb] * jnp.exp(b[h][r0:r0 + sb] - b_ref_row)).astype(BF16)
                kd = (k[h] * jnp.exp(jnp.where(row_c < r0, b_ref_row - b[h], NEG_INF))).astype(BF16)
                per_blk.append(_dot_nt(qs, kd))
            att_off.append(per_blk)
        att = []
        for h in heads:
            att_rows = []
            for blk in range(nblk):
                r0 = blk * sb
                q_i = q[h][r0:r0 + sb]
                b_i = b[h][r0:r0 + sb]
                a = att_off[h][blk]
                for sl in range(sb):
                    s_abs = r0 + sl
                    e = jnp.exp(jnp.where(row_sb >= sl, b_i - b[h][s_abs:s_abs + 1], NEG_INF))
                    col = jnp.sum(q_i * (k[h][s_abs:s_abs + 1] * e), axis=-1, keepdims=True)
                    a = jnp.where(lane_c == s_abs, col, a)
                att_rows.append(a)
            att.append(att_rows[0] if nblk == 1 else jnp.concatenate(att_rows, axis=0))
        o = [_mm(att[h], v[h]) + o_state[h] for h in heads]
        for p, (g, h) in enumerate(pairs):
            vs = slice(h * GLA_DV, (h + 1) * GLA_DV)
            ms = jnp.mean(o[p] * o[p], axis=-1, keepdims=True)
            on = (o[p] * lax.rsqrt(ms + EPS)) * go_ref[...]
            mix_scr[bis[g], :, vs] = on * _silu(z_ref[bis[g], :, vs])
            s_dst[bis[g], h] = s_new[p]

    _for_each_group(nb, group, group_body)
    mix_ref[...] = mix_scr[...].astype(BF16)

    if not single_chunk:
        @pl.when(tc == ntc - 1)
        def _():
            so_ref[...] = s_scr[...]


def _stack_alias(stack, n_inputs_before):
    spec = pl.BlockSpec(memory_space=pl.ANY)
    if stack is None:
        return jnp.zeros((V7X_SUBLANES, V7X_LANES), F32), spec, {}
    return stack, spec, {n_inputs_before: 1}


def gla_mixer(proj, tail, s0, wg, bg, go, layer, state_layer, *, batch, seq, c, cv, nb, group, n_layers, stack=None):
    ntc = seq // c
    assert batch % nb == 0 and seq % c == 0
    grid = (batch // nb, ntc)
    proj3 = proj.reshape(batch, seq, proj.shape[-1])
    tail3 = tail.reshape(batch, seq, TAIL_W)

    def tok(width, col):
        return pl.BlockSpec((nb, c, width), lambda b, t: (b, t, col // width))

    def lw(shape):
        nd = len(shape)
        return pl.BlockSpec((None,) + shape, lambda b, t: (layer,) + (0,) * nd)

    st_shape = (nb, GLA_H, GLA_DK, GLA_DV)
    stack_arg, stack_spec, aliases = _stack_alias(stack, 9)
    mix, state = pl.pallas_call(
        functools.partial(_gla_kernel, nb=nb, group=group, c=c, cv=cv, single_chunk=ntc == 1),
        grid=grid,
        in_specs=[tok(GLA_H * GLA_DK, COL_QC), tok(GLA_H * GLA_DK, COL_KC), tok(W_MIX, COL_VC), tok(W_MIX, COL_ZC),
                  tok(TAIL_W, 0),
                  pl.BlockSpec((None,) + st_shape, lambda b, t: (state_layer, b, 0, 0, 0)),
                  lw((TAIL_W, GLA_H * GLA_DK)), lw((1, GLA_H * GLA_DK)), lw((1, GLA_DV)), stack_spec],
        out_specs=[tok(W_MIX, 0),
                   pl.BlockSpec((None,) + st_shape, lambda b, t: (layer, b, 0, 0, 0))],
        out_shape=[jax.ShapeDtypeStruct((batch, seq, W_MIX), BF16),
                   jax.ShapeDtypeStruct((n_layers, batch, GLA_H, GLA_DK, GLA_DV), F32)],
        scratch_shapes=[pltpu.VMEM(st_shape, F32), pltpu.VMEM((nb, c, W_MIX), F32)],
        input_output_aliases=aliases,
        compiler_params=_cparams(2),
        name="gla_mixer",
    )(proj3, proj3, proj3, proj3, tail3, s0, wg, bg, go, stack_arg)
    return mix.reshape(batch * seq, W_MIX), state


def _l2norm(x):
    return x * lax.rsqrt(jnp.sum(x * x, axis=-1, keepdims=True) + EPS)


def _unit_lower_solve(ms, rhss, c, cv):
    if c <= V7X_SUBLANES:
        row = lax.broadcasted_iota(jnp.int32, (c, 1), 0)
        us = []
        for m, rhs in zip(ms, rhss):
            u = jnp.zeros_like(rhs)
            u_rows = []
            for t in range(cv):
                ut = rhs[t:t + 1]
                for s in range(t):
                    ut = ut - m[t:t + 1, s:s + 1] * u_rows[s]
                u_rows.append(ut)
                u = jnp.where(row == t, ut, u)
            us.append(u)
        return us
    xs = [-m for m in ms]
    ys = list(rhss)
    dv = rhss[0].shape[1]
    levels = int(math.log2(c))
    assert 3 * c <= V7X_MXU_DEPTH

    def hi_lo(t):
        hi = t.astype(BF16)
        return hi, (t - hi.astype(F32)).astype(BF16)

    for lvl in range(levels):
        last = lvl == levels - 1
        ps = []
        for x, y in zip(xs, ys):
            xh, xl = hi_lo(x)
            yh, yl = hi_lo(y)
            bh = yh if last else jnp.concatenate([yh, xh], axis=1)
            bl = yl if last else jnp.concatenate([yl, xl], axis=1)
            lhs = jnp.concatenate([xh.astype(F32), xh.astype(F32), xl.astype(F32)], axis=1).astype(BF16)
            ps.append(_dot(lhs, jnp.concatenate([bh, bl, bh], axis=0)))
        ys = [y + p[:, :dv] for y, p in zip(ys, ps)]
        if not last:
            xs = [p[:, dv:] for p in ps]
    return ys


def _gdn_kernel(qkv_ref, z_ref, tail_ref, s0_ref, cv0_ref, cw_ref, alog_ref, dtb_ref, go_ref, stack_ref,
                mix_ref, so_ref,
                s_scr, xp_scr, tail_scr, mix_scr, *, nb, group, c, cv, single_chunk):
    del stack_ref
    tc = pl.program_id(1)
    ntc = pl.num_programs(1)

    s_src, s_dst = (s0_ref, so_ref) if single_chunk else (s_scr, s_scr)

    @pl.when(tc == 0)
    def _():
        if not single_chunk:
            s_scr[...] = s0_ref[...]
        _init_conv_tail(tail_scr, cv0_ref, nb)

    row_c = lax.broadcasted_iota(jnp.int32, (c, 1), 0)
    ri = lax.broadcasted_iota(jnp.int32, (c, c), 0)
    ci = lax.broadcasted_iota(jnp.int32, (c, c), 1)
    lane_t = lax.broadcasted_iota(jnp.int32, (c, TAIL_W), 1)
    a_lanes = (lane_t >= TAIL_A) & (lane_t < TAIL_A + DN_H)

    def group_body(bis):
        pairs = [(g, h) for g in range(len(bis)) for h in range(DN_H)]
        heads = range(len(pairs))
        qkv, gam, gam_t, beta = [], [], [], []
        for gi, bi in enumerate(bis):
            qkv.append(_silu(_causal_conv(xp_scr.at[gi], tail_scr, qkv_ref[bi], cw_ref, bi)))
            tl = tail_ref[bi]
            gg = _cumsum_rows(jnp.where(a_lanes, -jnp.exp(alog_ref[...]) * _softplus(tl + dtb_ref[...]), 0.0))
            gam.append(gg)
            gam_t.append(jnp.transpose(gg))
            beta.append(_sigmoid(tl))
        s_old = [s_src[bis[g], h] for g, h in pairs]
        s16 = [s.astype(BF16) for s in s_old]
        q16, k16, kf, v = [], [], [], []
        for g, h in pairs:
            q = _l2norm(qkv[g][:, h * DN_DK:(h + 1) * DN_DK]) * (DN_DK ** -0.5)
            k = _l2norm(qkv[g][:, DN_H * DN_DK + h * DN_DK:DN_H * DN_DK + (h + 1) * DN_DK])
            q16.append(q.astype(BF16))
            k16.append(k.astype(BF16))
            kf.append(k)
            v.append(qkv[g][:, 2 * DN_H * DN_DK + h * DN_DV:2 * DN_H * DN_DK + (h + 1) * DN_DV])
        g_col = [gam[g][:, TAIL_A + h:TAIL_A + h + 1] for g, h in pairs]
        g_row = [gam_t[g][TAIL_A + h:TAIL_A + h + 1, :] for g, h in pairs]
        b_col = [beta[g][:, TAIL_B + h:TAIL_B + h + 1] for g, h in pairs]
        decay = [jnp.exp(jnp.where(ri >= ci, g_col[h] - g_row[h], NEG_INF)) for h in heads]
        eg = [jnp.exp(g_col[h]) for h in heads]
        kk = [_dot_nt(k16[h], k16[h]) for h in heads]
        k_s = [_dot(k16[h], s16[h]) for h in heads]
        q_s = [_dot(q16[h], s16[h]) for h in heads]
        qk = [_dot_nt(q16[h], k16[h]) for h in heads]
        m = [jnp.where(ri > ci, b_col[h] * kk[h] * decay[h], 0.0) for h in heads]
        rhs = [b_col[h] * (v[h] - eg[h] * k_s[h]) for h in heads]
        u = _unit_lower_solve(m, rhs, c, cv)
        o = [eg[h] * q_s[h] + _mm(qk[h] * decay[h], u[h]) for h in heads]
        kd = []
        for h in heads:
            dec = g_col[h][cv - 1:cv] - g_col[h]
            if cv < c:
                dec = jnp.where(row_c < cv, dec, NEG_INF)
            kd.append(kf[h] * jnp.exp(dec))
        s_new = [jnp.exp(g_col[h][cv - 1:cv]) * s_old[h] + _mm(kd[h], u[h], dot=_dot_tn, contract=0) for h in heads]
        for p, (g, h) in enumerate(pairs):
            vs = slice(h * DN_DV, (h + 1) * DN_DV)
            ms = jnp.mean(o[p] * o[p], axis=-1, keepdims=True)
            on = (o[p] * lax.rsqrt(ms + EPS)) * go_ref[...]
            mix_scr[bis[g], :, vs] = on * _silu(z_ref[bis[g], :, vs])
            s_dst[bis[g], h] = s_new[p]

    _for_each_group(nb, group, group_body)
    mix_ref[...] = mix_scr[...].astype(BF16)

    if not single_chunk:
        @pl.when(tc == ntc - 1)
        def _():
            so_ref[...] = s_scr[...]


def gdn_mixer(proj, tail, s0, conv0, cw, alog, dtb, go, layer, state_layer, *, batch, seq, c, cv, nb, group, n_layers,
              stack=None):
    ntc = seq // c
    assert batch % nb == 0 and seq % c == 0
    grid = (batch // nb, ntc)
    proj3 = proj.reshape(batch, seq, proj.shape[-1])
    tail3 = tail.reshape(batch, seq, TAIL_W)

    def tok(width, col):
        return pl.BlockSpec((nb, c, width), lambda b, t: (b, t, col // width))

    def lw(shape):
        nd = len(shape)
        return pl.BlockSpec((None,) + shape, lambda b, t: (layer,) + (0,) * nd)

    st_shape = (nb, DN_H, DN_DK, DN_DV)
    assert COL_QKVD % DN_QKV == 0
    stack_arg, stack_spec, aliases = _stack_alias(stack, 9)
    mix, state = pl.pallas_call(
        functools.partial(_gdn_kernel, nb=nb, group=group, c=c, cv=cv, single_chunk=ntc == 1),
        grid=grid,
        in_specs=[tok(DN_QKV, COL_QKVD),
                  tok(W_MIX, COL_ZD),
                  tok(TAIL_W, 0),
                  pl.BlockSpec((None,) + st_shape, lambda b, t: (state_layer, b, 0, 0, 0)),
                  pl.BlockSpec((None, nb, CONV_W - 1, DN_QKV), lambda b, t: (state_layer, b, 0, 0)),
                  lw((CONV_W, DN_QKV)), lw((1, TAIL_W)), lw((1, TAIL_W)), lw((1, DN_DV)), stack_spec],
        out_specs=[tok(W_MIX, 0),
                   pl.BlockSpec((None,) + st_shape, lambda b, t: (layer, b, 0, 0, 0))],
        out_shape=[jax.ShapeDtypeStruct((batch, seq, W_MIX), BF16),
                   jax.ShapeDtypeStruct((n_layers, batch, DN_H, DN_DK, DN_DV), F32)],
        input_output_aliases=aliases,
        scratch_shapes=[pltpu.VMEM(st_shape, F32),
                        pltpu.VMEM((group, CONV_PAD + c, DN_QKV), F32),
                        pltpu.VMEM((nb, CONV_PAD, DN_QKV), F32),
                        pltpu.VMEM((nb, c, W_MIX), F32)],
        compiler_params=_cparams(2),
        name="gdn_mixer",
    )(proj3, proj3, tail3, s0, conv0, cw, alog, dtb, go, stack_arg)
    return mix.reshape(batch * seq, W_MIX), state


def _prepare_weights(w_in, s5_lam_re, s5_lam_im, s5_log_dt, s5_b_re, s5_b_im, s5_c_re, s5_c_im, s5_d,
                     s5_w_glu, lru_w_r, lru_w_i, gla_w_gate, dn_a_log, dn_dt_bias, w_out, ple_w, ple_gate_w):
    nl = w_in.shape[0]
    p = {}
    p["w_main"], p["w_tail"] = repack_w_in(jnp.swapaxes(w_in, 1, 2))
    p["w_out"] = w_out.astype(BF16)
    p["ple_w"] = ple_w.astype(BF16)
    p["ple_gate_w"] = ple_gate_w.astype(BF16)
    p["w_glu"] = s5_w_glu.astype(BF16)
    ldt = jnp.repeat(s5_log_dt, S5_STATE, axis=1).reshape(nl, 1, S5_CH)
    coef, cre, cim = s5_prep(s5_lam_re.reshape(nl, 1, S5_CH), s5_lam_im.reshape(nl, 1, S5_CH), ldt)
    bb_re, bb_im = s5_bb(cre.reshape(nl, S5_CH, 1), cim.reshape(nl, S5_CH, 1),
                         s5_b_re.reshape(nl, S5_CH, S5_GROUP), s5_b_im.reshape(nl, S5_CH, S5_GROUP))
    eye16 = jnp.eye(16, dtype=F32)
    eye8 = jnp.eye(8, dtype=F32)

    def pack_u(bb):
        t = bb.reshape(nl, 4, 16, S5_STATE, S5_GROUP)
        return jnp.einsum("lkgph,gG->lkghGp", t, eye16).reshape(nl, 4, 256, 1024).astype(BF16)

    def pack_y(cc):
        t = cc.reshape(nl, 8, 8, S5_GROUP, S5_STATE)
        return jnp.einsum("ljghp,gG->ljgpGh", t, eye8).reshape(nl, 8, 512, 128).astype(BF16)

    p["s5_coef"] = coef
    p["s5_wur"], p["s5_wui"] = pack_u(bb_re), pack_u(bb_im)
    p["s5_wyr"], p["s5_wyi"] = pack_y(s5_c_re), pack_y(s5_c_im)
    p["s5_d"] = s5_d.reshape(nl, 1, W_MIX)
    p["lru_wri"] = jnp.concatenate([lru_w_r, lru_w_i], axis=-1).astype(BF16)
    p["gla_wg"] = jnp.concatenate([gla_w_gate, jnp.zeros((nl, TAIL_W - GLA_RANK, GLA_H * GLA_DK), F32)],
                                  axis=1).astype(BF16)

    def tail_row(x):
        return jnp.pad(x, ((0, 0), (TAIL_A, TAIL_W - TAIL_A - DN_H))).reshape(nl, 1, TAIL_W)

    p["dn_alog"] = tail_row(dn_a_log)
    p["dn_dtb"] = tail_row(dn_dt_bias)
    return p


def kernel(x_prompt, x_sample, state_s5_re, state_s5_im, state_lru_h, state_lru_conv, state_gla, state_delta, state_delta_conv, p_prompt, p_sample, g_norm, w_in, s5_lam_re, s5_lam_im, s5_log_dt, s5_b_re, s5_b_im, s5_c_re, s5_c_im, s5_d, s5_w_glu, s5_b_glu, lru_conv_w, lru_conv_b, lru_w_r, lru_b_r, lru_w_i, lru_b_i, lru_lam, gla_w_gate, gla_b_gate, gla_g_out, dn_conv_w, dn_a_log, dn_dt_bias, dn_g_out, w_out, ple_w, ple_gate_w, ple_gate_b, g_final):
    nl = w_in.shape[0]
    bp, tp, _ = x_prompt.shape
    bs, ts, _ = x_sample.shape
    tsp = SAMPLE_T_PAD
    mp, ms = bp * tp, bs * ts

    p = _prepare_weights(w_in, s5_lam_re, s5_lam_im, s5_log_dt, s5_b_re, s5_b_im, s5_c_re, s5_c_im, s5_d,
                         s5_w_glu, lru_w_r, lru_w_i, gla_w_gate, dn_a_log, dn_dt_bias, w_out, ple_w, ple_gate_w)

    def vec(x):
        return x.reshape(nl, 1, x.shape[-1])

    g_norm3, b_glu3, cb3 = vec(g_norm), vec(s5_b_glu), vec(lru_conv_b)
    br3, bi3, lam3 = vec(lru_b_r), vec(lru_b_i), vec(lru_lam)
    bg3, go_c3, go_d3, pgb3 = vec(gla_b_gate), vec(gla_g_out), vec(dn_g_out), vec(ple_gate_b)
    pe_p = p_prompt.reshape(nl, mp, D_PLE)
    pe_s = p_sample.reshape(nl, ms, D_PLE)

    z_s5 = jnp.zeros((1, bp, 1, S5_CH), F32)
    z_lru = jnp.zeros((1, bp, 1, W_MIX), F32)
    z_lconv = jnp.zeros((1, bp, CONV_W - 1, W_MIX), F32)
    z_gla = jnp.zeros((1, bp, GLA_H, GLA_DK, GLA_DV), F32)
    z_dn = jnp.zeros((1, bp, DN_H, DN_DK, DN_DV), F32)
    z_dconv = jnp.zeros((1, bp, CONV_W - 1, DN_QKV), F32)
    c_s5r = state_s5_re.reshape(nl, bs, 1, S5_CH)
    c_s5i = state_s5_im.reshape(nl, bs, 1, S5_CH)
    c_lru = state_lru_h.reshape(nl, bs, 1, W_MIX)

    def mixers(proj, tail, layer, *, batch, seq, t_valid, st, sl, nb_scan, tt, c, gla_grp, nb_gdn, gdn_grp, stacks):
        s5r0, s5i0, lru0, lconv0, gla0, dn0, dconv0 = st
        gla_stack, dn_stack = stacks
        mix_a, s5r, s5i = s5_mixer(proj, s5r0, s5i0, p["s5_coef"], p["s5_wur"], p["s5_wui"], p["s5_wyr"],
                                   p["s5_wyi"], p["s5_d"], p["w_glu"], b_glu3, layer, sl,
                                   batch=batch, seq=seq, t_valid=t_valid, nb=nb_scan, tt=tt)
        mix_b, lruh = lru_mixer(proj, lru0, lconv0, lru_conv_w, cb3, p["lru_wri"], br3, bi3, lam3, layer, sl,
                                batch=batch, seq=seq, t_valid=t_valid, nb=nb_scan, tt=tt)
        mix_c, glas = gla_mixer(proj, tail, gla0, p["gla_wg"], bg3, go_c3, layer, sl,
                                batch=batch, seq=seq, c=c, cv=min(c, t_valid), nb=gla_grp, group=gla_grp, n_layers=nl,
                                stack=gla_stack)
        mix_d, dns = gdn_mixer(proj, tail, dn0, dconv0, dn_conv_w, p["dn_alog"], p["dn_dtb"], go_d3, layer, sl,
                               batch=batch, seq=seq, c=c, cv=min(c, t_valid), nb=nb_gdn, group=gdn_grp, n_layers=nl,
                               stack=dn_stack)
        return (mix_a, mix_b, mix_c, mix_d), (s5r, s5i, lruh, glas, dns)

    hp = x_prompt.reshape(mp, D_MODEL)
    hs = x_sample.reshape(ms, D_MODEL)
    new_p, new_s = [], []
    stacks_p = stacks_s = (None, None)
    hgp, ssqp = prenorm(hp, g_norm3, 0, tm=DENSE_TILES_SAMPLE["tm"])
    hgs, ssqs = prenorm(hs, g_norm3, 0, tm=DENSE_TILES_SAMPLE["tm"])
    for i in range(nl):
        nxt = min(i + 1, nl - 1)
        proj, tail = in_proj(hgp, ssqp, p["w_main"], p["w_tail"], i, **IN_PROJ_TILES_PROMPT)
        mixes, (s5r, s5i, lruh, glas, dns) = mixers(
            proj, tail, i, batch=bp, seq=tp, t_valid=tp,
            st=(z_s5, z_s5, z_lru, z_lconv, z_gla, z_dn, z_dconv), sl=0,
            nb_scan=1, tt=256, c=GLA_CHUNK, gla_grp=4, nb_gdn=2, gdn_grp=2, stacks=stacks_p)
        stacks_p = (glas, dns)
        proj3 = proj.reshape(bp, tp, N_MAIN)
        new_p.append((s5r.reshape(bp, S5_GROUPS, S5_STATE), s5i.reshape(bp, S5_GROUPS, S5_STATE),
                      lruh.reshape(bp, W_MIX), proj3[:, tp - (CONV_W - 1):, COL_XB:COL_XB + W_MIX],
                      None, None, proj3[:, tp - (CONV_W - 1):, COL_QKVD:COL_QKVD + DN_QKV]))
        hp, hpb = out_proj(mixes, p["w_out"], hp, i, **DENSE_TILES_PROMPT)
        hp, hgp, ssqp = ple(hp, hpb, pe_p, p["ple_gate_w"], pgb3, p["ple_w"], g_norm3, i, nxt, **DENSE_TILES_PROMPT)

        proj_pad, tail_pad = in_proj(hgs, ssqs, p["w_main"], p["w_tail"], i, seq_rows=ts, seq_pad=tsp,
                                     **DENSE_TILES_SAMPLE)
        proj3 = proj_pad.reshape(bs, tsp, N_MAIN)[:, :ts]
        mixes, (s5r, s5i, lruh, glas, dns) = mixers(
            proj_pad, tail_pad, i, batch=bs, seq=tsp, t_valid=ts,
            st=(c_s5r, c_s5i, c_lru, state_lru_conv, state_gla, state_delta, state_delta_conv), sl=i,
            nb_scan=32, tt=tsp, c=tsp, gla_grp=8, nb_gdn=8, gdn_grp=4, stacks=stacks_s)
        stacks_s = (glas, dns)
        mixes = tuple(m.reshape(bs, tsp, W_MIX)[:, :ts].reshape(ms, W_MIX) for m in mixes)
        new_s.append((s5r.reshape(bs, S5_GROUPS, S5_STATE), s5i.reshape(bs, S5_GROUPS, S5_STATE),
                      lruh.reshape(bs, W_MIX), proj3[:, ts - (CONV_W - 1):, COL_XB:COL_XB + W_MIX],
                      None, None, proj3[:, ts - (CONV_W - 1):, COL_QKVD:COL_QKVD + DN_QKV]))
        hs, hsb = out_proj(mixes, p["w_out"], hs, i, **DENSE_TILES_SAMPLE)
        hs, hgs, ssqs = ple(hs, hsb, pe_s, p["ple_gate_w"], pgb3, p["ple_w"], g_norm3, i, nxt, **DENSE_TILES_SAMPLE)

    g_fin = g_final.reshape(1, D_MODEL)
    y_prompt = final_norm(hp, g_fin, tm=256).reshape(bp, tp, D_MODEL)
    y_sample = final_norm(hs, g_fin, tm=256).reshape(bs, ts, D_MODEL)

    def stk(lst, j):
        return jnp.stack([s[j] for s in lst], axis=0)

    return (y_prompt, y_sample,
            stk(new_p, 0), stk(new_p, 1), stk(new_p, 2), stk(new_p, 3), stacks_p[0], stacks_p[1], stk(new_p, 6),
            stk(new_s, 0), stk(new_s, 1), stk(new_s, 2), stk(new_s, 3), stacks_s[0], stacks_s[1], stk(new_s, 6))
```

```python
import functools
import math

import jax
import jax.numpy as jnp
import numpy as np
from jax import lax
from jax.experimental import pallas as pl
from jax.experimental.pallas import tpu as pltpu

F32 = jnp.float32
BF16 = jnp.bfloat16
EPS = 1e-6
NEG_INF = float("-inf")

D_MODEL = 4096
DEPTH = 4
W_MIX = 1024
S5_GROUPS = 64
S5_GROUP = 16
S5_STATE = 64
S5_CH = S5_GROUPS * S5_STATE
LRU_BLOCKS = 8
LRU_BLK = 128
LRU_C = 8.0
CONV_W = 4
GLA_H = 4
GLA_DK = 128
GLA_DV = 256
GLA_RANK = 16
GLA_TAU = 16.0
GLA_CHUNK = 64
GLA_SUB = 8
DN_H = 8
DN_DK = 128
DN_DV = 128
DN_QKV = 3072
DN_CHUNK = 64
D_PLE = 256
N_MAIN = 11264
COL_XA, COL_ZA, COL_XB, COL_ZB = 0, 1024, 2048, 3072
COL_QC, COL_KC, COL_VC = 4096, 4608, 5120
COL_QKVD, COL_ZC, COL_ZD = 6144, 9216, 10240
ORIG_ZC, ORIG_GC, ORIG_QKVD, ORIG_ZD, ORIG_AB = 6144, 7168, 7184, 10256, 11280
TAIL_W = 128
TAIL_G, TAIL_A, TAIL_B = 0, 16, 24
SAMPLE_T_PAD = 8

V7X_LANES = 128
V7X_SUBLANES = 8
V7X_MXU_DEPTH = 256
VMEM_LIMIT = 52 * 1024 * 1024
DENSE_TILES_PROMPT = dict(tm=1024, tn=512)
IN_PROJ_TILES_PROMPT = dict(tm=1024, tn=1024)
OUT_PROJ_TILES_PROMPT = dict(tm=1024, tn=1024)
VMEM_HEADROOM = 6 * 1024 * 1024
DENSE_TILES_SAMPLE = dict(tm=512, tn=1024)


def _cparams(n_axes, vmem_limit=VMEM_LIMIT):
    return pltpu.CompilerParams(dimension_semantics=("arbitrary",) * n_axes,
                                vmem_limit_bytes=vmem_limit)


def _sigmoid(x):
    return jax.nn.sigmoid(x)


def _silu(x):
    return x * jax.nn.sigmoid(x)


def _softplus(x):
    return jnp.maximum(x, 0.0) + jnp.log1p(jnp.exp(-jnp.abs(x)))


def _neg_expm1_nonpos(x):
    t = jnp.tanh(0.5 * x)
    return (-2.0 * t) / (1.0 - t)


def _log_sigmoid(x):
    return jnp.minimum(x, 0.0) - jnp.log1p(jnp.exp(-jnp.abs(x)))


def _gelu_tanh(x):
    c = math.sqrt(2.0 / math.pi)
    return x * (0.5 * (1.0 + jnp.tanh(c * (x + 0.044715 * (x * x * x)))))


def _dot(a, b):
    return jnp.dot(a, b, preferred_element_type=F32)


def _dot_nt(a, b):
    return lax.dot_general(a, b, (((1,), (1,)), ((), ())), preferred_element_type=F32)


def _dot_tn(a, b):
    return lax.dot_general(a, b, (((0,), (0,)), ((), ())), preferred_element_type=F32)


def _split3(x):
    h = x.astype(BF16)
    r = x - h.astype(F32)
    m = r.astype(BF16)
    l = (r - m.astype(F32)).astype(BF16)
    return h, m, l


def _cumsum_rows(x):
    c = x.shape[0]
    if c <= V7X_SUBLANES:
        row = lax.broadcasted_iota(jnp.int32, x.shape, 0)
        k = 1
        while k < c:
            x = x + jnp.where(row >= k, pltpu.roll(x, k, 0), 0.0)
            k *= 2
        return x
    h, m, l = _split3(x)
    if 3 * c <= V7X_MXU_DEPTH:
        col = lax.broadcasted_iota(jnp.int32, (c, 3 * c), 1)
        col = jnp.where(col >= 2 * c, col - 2 * c, jnp.where(col >= c, col - c, col))
        tri3 = (lax.broadcasted_iota(jnp.int32, (c, 3 * c), 0) >= col).astype(BF16)
        return _dot(tri3, jnp.concatenate([h, m, l], axis=0))
    tri = (lax.broadcasted_iota(jnp.int32, (c, c), 0) >= lax.broadcasted_iota(jnp.int32, (c, c), 1)).astype(BF16)
    return _dot(tri, h) + _dot(tri, m) + _dot(tri, l)


def _mm(a, b, dot=_dot, contract=1):
    if a.shape[contract] < 2 * V7X_SUBLANES:
        return dot(a, b)
    return dot(a.astype(BF16), b.astype(BF16))


def _dot_x3(a, b):
    ah32 = a.astype(BF16).astype(F32)
    al32 = a - ah32
    bh32 = b.astype(BF16).astype(F32)
    bl32 = b - bh32
    k = a.shape[1]
    if 3 * k <= V7X_MXU_DEPTH:
        lhs = jnp.concatenate([ah32, ah32, al32], axis=1).astype(BF16)
        rhs = jnp.concatenate([bh32, bl32, bh32], axis=0).astype(BF16)
        return _dot(lhs, rhs)
    ah, al, bh, bl = (t.astype(BF16) for t in (ah32, al32, bh32, bl32))
    return _dot(ah, bh) + _dot(ah, bl) + _dot(al, bh)


def _prenorm_kernel(x_ref, g_ref, hg_ref, ssq_ref):
    x = x_ref[...]
    hg_ref[...] = (x * g_ref[...]).astype(BF16)
    ssq_ref[...] = jnp.sum(x * x, axis=-1, keepdims=True)


def prenorm(x, g_norm, layer, *, tm):
    m = x.shape[0]
    return pl.pallas_call(
        _prenorm_kernel,
        grid=(m // tm,),
        in_specs=[pl.BlockSpec((tm, D_MODEL), lambda i: (i, 0)),
                  pl.BlockSpec((None, 1, D_MODEL), lambda i: (layer, 0, 0))],
        out_specs=[pl.BlockSpec((tm, D_MODEL), lambda i: (i, 0)), pl.BlockSpec((tm, 1), lambda i: (i, 0))],
        out_shape=[jax.ShapeDtypeStruct((m, D_MODEL), BF16), jax.ShapeDtypeStruct((m, 1), F32)],
        compiler_params=_cparams(1),
        name="prenorm",
    )(x, g_norm)


def _in_proj_kernel(hg_ref, ssq_ref, w_ref, wt_ref, o_ref, ot_ref, *, seq_rows, seq_pad):
    r = lax.rsqrt(ssq_ref[...] * (1.0 / D_MODEL) + EPS)

    def put(dst_ref, val):
        if seq_pad == seq_rows:
            dst_ref[...] = val
        else:
            n_seq = val.shape[0] // seq_rows
            dst_ref[:, 0:seq_rows, :] = val.reshape(n_seq, seq_rows, val.shape[1])
            dst_ref[:, seq_rows:seq_pad, :] = jnp.zeros((n_seq, seq_pad - seq_rows, val.shape[1]), F32)

    @pl.when(pl.program_id(1) == 0)
    def _():
        put(ot_ref, _dot_nt(hg_ref[...], wt_ref[...]) * r)

    put(o_ref, _dot_nt(hg_ref[...], w_ref[...]) * r)


def in_proj(hg, ssq, w_main, w_tail, layer, *, tm, tn, seq_rows=None, seq_pad=None):
    m = hg.shape[0]
    grid = (m // tm, N_MAIN // tn)
    if seq_rows is None:
        seq_rows = seq_pad = 1
        out_specs = [pl.BlockSpec((tm, tn), lambda i, j: (i, j)), pl.BlockSpec((tm, TAIL_W), lambda i, j: (i, 0))]
        out_shape = [jax.ShapeDtypeStruct((m, N_MAIN), F32), jax.ShapeDtypeStruct((m, TAIL_W), F32)]
    else:
        n_seq = tm // seq_rows
        out_specs = [pl.BlockSpec((n_seq, seq_pad, tn), lambda i, j: (i, 0, j)),
                     pl.BlockSpec((n_seq, seq_pad, TAIL_W), lambda i, j: (i, 0, 0))]
        out_shape = [jax.ShapeDtypeStruct((m // seq_rows, seq_pad, N_MAIN), F32),
                     jax.ShapeDtypeStruct((m // seq_rows, seq_pad, TAIL_W), F32)]
    proj, tail = pl.pallas_call(
        functools.partial(_in_proj_kernel, seq_rows=seq_rows, seq_pad=seq_pad),
        grid=grid,
        in_specs=[
            pl.BlockSpec((tm, D_MODEL), lambda i, j: (i, 0)),
            pl.BlockSpec((tm, 1), lambda i, j: (i, 0)),
            pl.BlockSpec((None, tn, D_MODEL), lambda i, j: (layer, j, 0)),
            pl.BlockSpec((None, TAIL_W, D_MODEL), lambda i, j: (layer, 0, 0)),
        ],
        out_specs=out_specs,
        out_shape=out_shape,
        compiler_params=_cparams(2),
        name="in_proj",
    )(hg, ssq, w_main, w_tail)
    return proj.reshape(-1, N_MAIN), tail.reshape(-1, TAIL_W)


def _out_proj_kernel(ma_ref, mb_ref, mc_ref, md_ref, w_ref, h_ref, o_ref, ob_ref):
    acc = h_ref[...]
    for k, m_ref in enumerate((ma_ref, mb_ref, mc_ref, md_ref)):
        acc = acc + _dot(m_ref[...], w_ref[k * W_MIX:(k + 1) * W_MIX, :])
    o_ref[...] = acc
    ob_ref[...] = acc.astype(BF16)


def out_proj(mixes, w_out, h, layer, *, tm, tn):
    m = h.shape[0]
    grid = (m // tm, D_MODEL // tn)
    mix_spec = pl.BlockSpec((tm, W_MIX), lambda i, j: (i, 0))
    tile = pl.BlockSpec((tm, tn), lambda i, j: (i, j))
    return pl.pallas_call(
        _out_proj_kernel,
        grid=grid,
        in_specs=[mix_spec, mix_spec, mix_spec, mix_spec,
                  pl.BlockSpec((None, 4 * W_MIX, tn), lambda i, j: (layer, 0, j)), tile],
        out_specs=[tile, tile],
        out_shape=[jax.ShapeDtypeStruct((m, D_MODEL), F32), jax.ShapeDtypeStruct((m, D_MODEL), BF16)],
        compiler_params=_cparams(2, max(VMEM_LIMIT, 2 * (tm * D_MODEL * 2 + 4 * W_MIX * tn * 2 + tm * tn * 10)
                                        + VMEM_HEADROOM)),
        name="out_proj",
    )(*mixes, w_out, h)


def _ple_kernel(hrow_ref, htile_ref, pe_ref, wg_ref, bg_ref, wp_ref, gn_ref, o_ref, hg_ref, ssq_ref):
    gate = _sigmoid(_dot(hrow_ref[...], wg_ref[...]) + bg_ref[...])
    pv = _dot(pe_ref[...].astype(BF16), wp_ref[...])
    h = htile_ref[...] + gate * pv
    o_ref[...] = h
    hg_ref[...] = (h * gn_ref[...]).astype(BF16)
    part = jnp.sum(h * h, axis=-1, keepdims=True)

    @pl.when(pl.program_id(1) == 0)
    def _():
        ssq_ref[...] = part

    @pl.when(pl.program_id(1) != 0)
    def _():
        ssq_ref[...] = ssq_ref[...] + part


def ple(h, hb, pe, w_gate, b_gate, w_ple, g_next, layer, next_layer, *, tm, tn):
    m = h.shape[0]
    grid = (m // tm, D_MODEL // tn)
    tile = pl.BlockSpec((tm, tn), lambda i, j: (i, j))
    return pl.pallas_call(
        _ple_kernel,
        grid=grid,
        in_specs=[
            pl.BlockSpec((tm, D_MODEL), lambda i, j: (i, 0)),
            tile,
            pl.BlockSpec((None, tm, D_PLE), lambda i, j: (layer, i, 0)),
            pl.BlockSpec((None, D_MODEL, tn), lambda i, j: (layer, 0, j)),
            pl.BlockSpec((None, 1, tn), lambda i, j: (layer, 0, j)),
            pl.BlockSpec((None, D_PLE, tn), lambda i, j: (layer, 0, j)),
            pl.BlockSpec((None, 1, tn), lambda i, j: (next_layer, 0, j)),
        ],
        out_specs=[tile, tile, pl.BlockSpec((tm, 1), lambda i, j: (i, 0))],
        out_shape=[jax.ShapeDtypeStruct((m, D_MODEL), F32), jax.ShapeDtypeStruct((m, D_MODEL), BF16),
                   jax.ShapeDtypeStruct((m, 1), F32)],
        compiler_params=_cparams(2),
        name="ple",
    )(hb, h, pe, w_gate, b_gate, w_ple, g_next)


REPACK_ROWS = 512


def _repack_kernel(a_ref, g_ref, ab_ref, o_ref, t_ref):
    o_ref[...] = a_ref[...].astype(BF16)

    @pl.when(pl.program_id(1) == 0)
    def _():
        t_ref[...] = jnp.zeros(t_ref.shape, BF16)
        t_ref[TAIL_G:TAIL_G + GLA_RANK, :] = g_ref[...].astype(BF16)
        t_ref[TAIL_A:TAIL_A + 2 * DN_H, :] = ab_ref[...].astype(BF16)


def repack_w_in(w_in_t):
    nl = w_in_t.shape[0]
    n_tiles = N_MAIN // REPACK_ROWS
    t_qkvd, t_zc, t_zd = COL_QKVD // REPACK_ROWS, COL_ZC // REPACK_ROWS, COL_ZD // REPACK_ROWS

    def src_row(j):
        return jnp.where(j < t_qkvd, j * REPACK_ROWS,
                         jnp.where(j < t_zc, ORIG_QKVD + (j - t_qkvd) * REPACK_ROWS,
                                   jnp.where(j < t_zd, ORIG_ZC + (j - t_zc) * REPACK_ROWS,
                                             ORIG_ZD + (j - t_zd) * REPACK_ROWS)))

    row_align = 2 * V7X_SUBLANES
    assert all(o % row_align == 0 for o in (ORIG_QKVD, ORIG_ZC, ORIG_ZD, ORIG_GC, ORIG_AB))

    def rows(n, start):
        return pl.BlockSpec((None, pl.Element(n), pl.Element(D_MODEL)),
                            lambda l, j: (l, pl.multiple_of(start(j), row_align), 0))

    return pl.pallas_call(
        _repack_kernel,
        grid=(nl, n_tiles),
        in_specs=[rows(REPACK_ROWS, src_row), rows(GLA_RANK, lambda j: ORIG_GC), rows(2 * DN_H, lambda j: ORIG_AB)],
        out_specs=[pl.BlockSpec((None, REPACK_ROWS, D_MODEL), lambda l, j: (l, j, 0)),
                   pl.BlockSpec((None, TAIL_W, D_MODEL), lambda l, j: (l, 0, 0))],
        out_shape=[jax.ShapeDtypeStruct((nl, N_MAIN, D_MODEL), BF16),
                   jax.ShapeDtypeStruct((nl, TAIL_W, D_MODEL), BF16)],
        compiler_params=_cparams(2),
        name="repack_w_in",
    )(w_in_t, w_in_t, w_in_t)


def _final_norm_kernel(x_ref, g_ref, o_ref):
    x = x_ref[...]
    ms = jnp.mean(x * x, axis=-1, keepdims=True)
    o_ref[...] = (x * lax.rsqrt(ms + EPS)) * g_ref[...]


def final_norm(h, g, *, tm):
    m = h.shape[0]
    return pl.pallas_call(
        _final_norm_kernel,
        grid=(m // tm,),
        in_specs=[pl.BlockSpec((tm, D_MODEL), lambda i: (i, 0)),
                  pl.BlockSpec((1, D_MODEL), lambda i: (0, 0))],
        out_specs=pl.BlockSpec((tm, D_MODEL), lambda i: (i, 0)),
        out_shape=jax.ShapeDtypeStruct((m, D_MODEL), F32),
        compiler_params=_cparams(1),
        name="final_norm",
    )(h, g)


S5_SEG = 32


def _s5_prep_kernel(lre_ref, lim_ref, ldt_ref, coef_ref, cre_ref, cim_ref):
    lre = lre_ref[...]
    lim = lim_ref[...]
    dt = jnp.exp(ldt_ref[...])
    ai = lim * dt
    mag = jnp.exp(lre * dt)
    ar = mag * jnp.cos(ai)
    aim = mag * jnp.sin(ai)
    den = lre * lre + lim * lim
    nr = ar - 1.0
    cre_ref[...] = (nr * lre + aim * lim) / den
    cim_ref[...] = (aim * lre - nr * lim) / den
    pr, pim = [ar], [aim]
    for _ in range(S5_SEG - 1):
        nr_, ni_ = pr[-1] * ar - pim[-1] * aim, pr[-1] * aim + pim[-1] * ar
        pr.append(nr_)
        pim.append(ni_)
    shape = (V7X_SUBLANES, S5_CH)
    coef_ref[0] = jnp.broadcast_to(ar, shape)
    coef_ref[1] = jnp.broadcast_to(aim, shape)
    coef_ref[2] = jnp.broadcast_to(pr[S5_SEG - 1], shape)
    coef_ref[3] = jnp.broadcast_to(pim[S5_SEG - 1], shape)


def s5_prep(lam_re, lam_im, log_dt_rep):
    nl = lam_re.shape[0]
    vec = pl.BlockSpec((None, 1, S5_CH), lambda l: (l, 0, 0))
    return pl.pallas_call(
        _s5_prep_kernel,
        grid=(nl,),
        in_specs=[vec, vec, vec],
        out_specs=[pl.BlockSpec((None, 4, V7X_SUBLANES, S5_CH), lambda l: (l, 0, 0, 0)), vec, vec],
        out_shape=[jax.ShapeDtypeStruct((nl, 4, V7X_SUBLANES, S5_CH), F32),
                   jax.ShapeDtypeStruct((nl, 1, S5_CH), F32),
                   jax.ShapeDtypeStruct((nl, 1, S5_CH), F32)],
        compiler_params=_cparams(1),
        name="s5_prep",
    )(lam_re, lam_im, log_dt_rep)


def _s5_bb_kernel(cr_ref, ci_ref, br_ref, bi_ref, or_ref, oi_ref):
    cr = cr_ref[...]
    ci = ci_ref[...]
    br = br_ref[...]
    bi = bi_ref[...]
    or_ref[...] = cr * br - ci * bi
    oi_ref[...] = cr * bi + ci * br


def s5_bb(coef_re_col, coef_im_col, b_re, b_im):
    nl = b_re.shape[0]
    rows = 1024
    col = pl.BlockSpec((None, rows, 1), lambda l, r: (l, r, 0))
    mat = pl.BlockSpec((None, rows, S5_GROUP), lambda l, r: (l, r, 0))
    return pl.pallas_call(
        _s5_bb_kernel,
        grid=(nl, S5_CH // rows),
        in_specs=[col, col, mat, mat],
        out_specs=[mat, mat],
        out_shape=[jax.ShapeDtypeStruct((nl, S5_CH, S5_GROUP), F32)] * 2,
        compiler_params=_cparams(2),
        name="s5_bb",
    )(coef_re_col, coef_im_col, b_re, b_im)


S5_STRIP = 512


def _s5_kernel(xa_ref, za_ref, h0r_ref, h0i_ref, coef_ref, perm_ref, permt_ref,
               wur_ref, wui_ref, wyr_ref, wyi_ref, dsk_ref, wglu_ref, bglu_ref,
               mix_ref, hro_ref, hio_ref,
               ur_scr, ui_scr, cr_scr, ci_scr, y_scr, *, nb, tt, t_last):
    tc = pl.program_id(1)
    ntc = pl.num_programs(1)
    split = nb == 1
    lc = tt // V7X_SUBLANES if split else tt
    n_sets = 1 if split else nb // V7X_SUBLANES
    set_rows = V7X_SUBLANES * lc

    xb16 = _dot(perm_ref[...], xa_ref[...].astype(BF16)).astype(BF16)
    for k in range(4):
        xk = xb16[:, 256 * k:256 * (k + 1)]
        ur_scr[:, 1024 * k:1024 * (k + 1)] = _dot(xk, wur_ref[k])
        ui_scr[:, 1024 * k:1024 * (k + 1)] = _dot(xk, wui_ref[k])

    @pl.when(tc == 0)
    def _():
        cr_scr[...] = h0r_ref[...]
        ci_scr[...] = h0i_ref[...]

    row8 = lax.broadcasted_iota(jnp.int32, (V7X_SUBLANES, S5_STRIP), 0)
    for st in range(S5_CH // S5_STRIP):
        ls = slice(st * S5_STRIP, (st + 1) * S5_STRIP)
        ar = coef_ref[0, :, ls]
        ai = coef_ref[1, :, ls]
        for s in range(n_sets):
            base = s * set_rows
            if split:
                h0r = jnp.zeros((V7X_SUBLANES, S5_STRIP), F32)
                h0i = jnp.zeros((V7X_SUBLANES, S5_STRIP), F32)
            else:
                h0r = jnp.zeros((V7X_SUBLANES, S5_STRIP), F32)
                h0i = jnp.zeros((V7X_SUBLANES, S5_STRIP), F32)
                for k in range(V7X_SUBLANES):
                    h0r = jnp.where(row8 == k, cr_scr[s * V7X_SUBLANES + k, :, ls], h0r)
                    h0i = jnp.where(row8 == k, ci_scr[s * V7X_SUBLANES + k, :, ls], h0i)

            def step(j, h, base=base, ls=ls, ar=ar, ai=ai):
                hr, hi = h
                rows = pl.ds(pl.multiple_of(base + j * V7X_SUBLANES, V7X_SUBLANES), V7X_SUBLANES)
                nr = (ar * hr - ai * hi) + ur_scr[rows, ls]
                ni = (ar * hi + ai * hr) + ui_scr[rows, ls]
                ur_scr[rows, ls] = nr
                ui_scr[rows, ls] = ni
                return nr, ni

            fr, fi = lax.fori_loop(0, lc, step, (h0r, h0i), unroll=True)
            if split:
                alr = coef_ref[2, 0:1, ls]
                ali = coef_ref[3, 0:1, ls]
                cr = cr_scr[0, :, ls]
                ci = ci_scr[0, :, ls]
                init_r = jnp.zeros((V7X_SUBLANES, S5_STRIP), F32)
                init_i = jnp.zeros((V7X_SUBLANES, S5_STRIP), F32)
                for k in range(V7X_SUBLANES):
                    init_r = jnp.where(row8 == k, cr, init_r)
                    init_i = jnp.where(row8 == k, ci, init_i)
                    cr, ci = (alr * cr - ali * ci) + fr[k:k + 1], (alr * ci + ali * cr) + fi[k:k + 1]
                cr_scr[0, :, ls] = cr
                ci_scr[0, :, ls] = ci

                def fix(j, c, base=base, ls=ls, ar=ar, ai=ai):
                    rows = pl.ds(pl.multiple_of(base + j * V7X_SUBLANES, V7X_SUBLANES), V7X_SUBLANES)
                    c_r = ar * c[0] - ai * c[1]
                    c_i = ar * c[1] + ai * c[0]
                    ur_scr[rows, ls] = ur_scr[rows, ls] + c_r
                    ui_scr[rows, ls] = ui_scr[rows, ls] + c_i
                    return c_r, c_i

                lax.fori_loop(0, lc, fix, (init_r, init_i), unroll=True)

        y_scr[:, V7X_LANES * st:V7X_LANES * (st + 1)] = (_dot(ur_scr[:, ls].astype(BF16), wyr_ref[st])
                                                           - _dot(ui_scr[:, ls].astype(BF16), wyi_ref[st]))
    yh, ym, yl = _split3(y_scr[...])
    pt = permt_ref[...]
    y = (_dot(pt, yh) + _dot(pt, ym) + _dot(pt, yl)) + dsk_ref[...] * xa_ref[...]
    ga = _gelu_tanh(y)
    ya = ga * _sigmoid(_dot(ga.astype(BF16), wglu_ref[...]) + bglu_ref[...])
    mix_ref[...] = (ya * _silu(za_ref[...])).astype(BF16)

    @pl.when(tc == ntc - 1)
    def _():
        if split:
            hro_ref[...] = cr_scr[...]
            hio_ref[...] = ci_scr[...]
        else:
            for s in range(n_sets):
                for k in range(V7X_SUBLANES):
                    r = s * set_rows + V7X_SUBLANES * t_last + k
                    hro_ref[s * V7X_SUBLANES + k] = ur_scr[r:r + 1, :]
                    hio_ref[s * V7X_SUBLANES + k] = ui_scr[r:r + 1, :]


def s5_mixer(proj, h0_re, h0_im, coef, wur, wui, wyr, wyi, dskip, wglu, bglu, layer, state_layer,
             *, batch, seq, t_valid, nb, tt):
    ntc = seq // tt
    assert nb == 1 or ntc == 1
    rows = nb * tt
    grid = (batch // nb, ntc)
    t_last = (t_valid - 1) - (ntc - 1) * tt
    assert 0 <= t_last < tt
    if nb == 1:
        assert tt == V7X_SUBLANES * S5_SEG and t_last == tt - 1
        lc = S5_SEG
    else:
        assert nb % V7X_SUBLANES == 0
        lc = tt
    idx = np.arange(rows)
    set_rows = V7X_SUBLANES * lc
    src = (idx // set_rows) * set_rows + (idx % V7X_SUBLANES) * lc + (idx % set_rows) // V7X_SUBLANES
    perm_np = np.zeros((rows, rows), np.float32)
    perm_np[idx, src] = 1.0
    perm = jnp.asarray(perm_np, BF16)
    perm_t = jnp.asarray(perm_np.T, BF16)
    sq = pl.BlockSpec((rows, rows), lambda b, t: (0, 0))

    def tok(cb):
        return pl.BlockSpec((rows, W_MIX), lambda b, t: (b * ntc + t, cb))

    st_in = pl.BlockSpec((None, nb, 1, S5_CH), lambda b, t: (state_layer, b, 0, 0))
    st_out = pl.BlockSpec((nb, 1, S5_CH), lambda b, t: (b, 0, 0))

    def lw(shape):
        nd = len(shape)
        return pl.BlockSpec((None,) + shape, lambda b, t: (layer,) + (0,) * nd)

    return pl.pallas_call(
        functools.partial(_s5_kernel, nb=nb, tt=tt, t_last=t_last),
        grid=grid,
        in_specs=[tok(COL_XA // W_MIX), tok(COL_ZA // W_MIX), st_in, st_in,
                  lw((4, V7X_SUBLANES, S5_CH)), sq, sq, lw((4, 256, 1024)), lw((4, 256, 1024)),
                  lw((8, 512, 128)), lw((8, 512, 128)), lw((1, W_MIX)), lw((W_MIX, W_MIX)), lw((1, W_MIX))],
        out_specs=[pl.BlockSpec((rows, W_MIX), lambda b, t: (b * ntc + t, 0)), st_out, st_out],
        out_shape=[jax.ShapeDtypeStruct((batch * seq, W_MIX), BF16),
                   jax.ShapeDtypeStruct((batch, 1, S5_CH), F32),
                   jax.ShapeDtypeStruct((batch, 1, S5_CH), F32)],
        scratch_shapes=[pltpu.VMEM((rows, S5_CH), F32), pltpu.VMEM((rows, S5_CH), F32),
                        pltpu.VMEM((nb, 1, S5_CH), F32), pltpu.VMEM((nb, 1, S5_CH), F32),
                        pltpu.VMEM((rows, W_MIX), F32)],
        compiler_params=_cparams(2),
        name="s5_mixer",
    )(proj, proj, h0_re, h0_im, coef, perm, perm_t, wur, wui, wyr, wyi, dskip, wglu, bglu)


LRU_STRIP = 512
CONV_PAD = V7X_SUBLANES


def _causal_conv(xp_scr, tail_scr, x, cw_ref, bi):
    del xp_scr
    rows_n = x.shape[0]
    width = x.shape[1]
    n_tiles = rows_n // V7X_SUBLANES
    xe = jnp.concatenate([tail_scr[bi], x], axis=0).reshape(n_tiles + 1, V7X_SUBLANES, width)
    row = lax.broadcasted_iota(jnp.int32, (n_tiles, V7X_SUBLANES, width), 1)
    acc = cw_ref[CONV_W - 1:CONV_W, :] * x
    for jj in range(CONV_W - 1):
        shift = CONV_W - 1 - jj
        rot = pltpu.roll(xe, shift, 1)
        shifted = jnp.where(row < shift, rot[:-1], rot[1:]).reshape(rows_n, width)
        acc = acc + cw_ref[jj:jj + 1, :] * shifted
    tail_scr[bi] = xe[n_tiles]
    return acc


def _init_conv_tail(tail_scr, cv0_ref, nb):
    tail_scr[...] = jnp.zeros(tail_scr.shape, F32)
    for bi in range(nb):
        tail_scr[bi, CONV_PAD - (CONV_W - 1):CONV_PAD, :] = cv0_ref[bi]


def _lru_kernel(xb_ref, zb_ref, h0_ref, cv0_ref, cw_ref, cb_ref, wri_ref, br_ref, bi_ref, lam_ref,
                mix_ref, ho_ref,
                xp_scr, tail_scr, xc_scr, a_scr, b_scr, c_scr, *, nb, tt, t_last):
    tc = pl.program_id(1)
    ntc = pl.num_programs(1)

    @pl.when(tc == 0)
    def _():
        c_scr[...] = h0_ref[...]
        _init_conv_tail(tail_scr, cv0_ref, nb)

    def conv_body(bi, carry):
        rows = pl.ds(pl.multiple_of(bi * tt, V7X_SUBLANES), tt)
        acc = _causal_conv(xp_scr, tail_scr, xb_ref[rows, :], cw_ref, bi)
        xc_scr[rows, :] = acc + cb_ref[...]
        return carry

    lax.fori_loop(0, nb, conv_body, 0)

    xc = xc_scr[...]
    xc16 = xc.astype(BF16)
    sp = _softplus(-lam_ref[...])
    for blk in range(LRU_BLOCKS):
        cs = slice(blk * LRU_BLK, (blk + 1) * LRU_BLK)
        pre = _dot(xc16[:, cs], wri_ref[blk])
        r = _sigmoid(pre[:, :LRU_BLK] + br_ref[:, cs])
        ig = _sigmoid(pre[:, LRU_BLK:] + bi_ref[:, cs])
        log_a = (-LRU_C) * r * sp[:, cs]
        a_scr[:, cs] = jnp.exp(log_a)
        b_scr[:, cs] = jnp.sqrt(_neg_expm1_nonpos(2.0 * log_a)) * (ig * xc[:, cs])

    ng = tt // V7X_SUBLANES
    for s in range(W_MIX // LRU_STRIP):
        ls = slice(s * LRU_STRIP, (s + 1) * LRU_STRIP)
        row = lax.broadcasted_iota(jnp.int32, (V7X_SUBLANES, LRU_STRIP), 0)

        def seq_body(bi, carry, ls=ls, row=row):
            def tile_body(g, c):
                rows = pl.ds(pl.multiple_of(bi * tt + g * V7X_SUBLANES, V7X_SUBLANES), V7X_SUBLANES)
                a = a_scr[rows, ls]
                b = b_scr[rows, ls]
                for k in (1, 2, 4):
                    a_sh = jnp.where(row >= k, pltpu.roll(a, k, 0), 1.0)
                    b_sh = jnp.where(row >= k, pltpu.roll(b, k, 0), 0.0)
                    b = b + a * b_sh
                    a = a * a_sh
                h = b + a * c
                b_scr[rows, ls] = h
                return h[V7X_SUBLANES - 1:, :]

            c = lax.fori_loop(0, ng, tile_body, c_scr[bi, :, ls])
            c_scr[bi, :, ls] = c
            return carry

        lax.fori_loop(0, nb, seq_body, 0)

    mix_ref[...] = (b_scr[...] * _silu(zb_ref[...])).astype(BF16)

    @pl.when(tc == ntc - 1)
    def _():
        for bi in range(nb):
            r = bi * tt + t_last
            ho_ref[bi] = b_scr[r:r + 1, :]


def lru_mixer(proj, h0, conv0, cw, cb, wri, br, bi_, lam, layer, state_layer, *, batch, seq, t_valid, nb, tt):
    ntc = seq // tt
    assert nb == 1 or ntc == 1
    rows = nb * tt
    grid = (batch // nb, ntc)
    t_last = (t_valid - 1) - (ntc - 1) * tt

    def tok(cb_):
        return pl.BlockSpec((rows, W_MIX), lambda b, t: (b * ntc + t, cb_))

    def lw(shape):
        nd = len(shape)
        return pl.BlockSpec((None,) + shape, lambda b, t: (layer,) + (0,) * nd)

    return pl.pallas_call(
        functools.partial(_lru_kernel, nb=nb, tt=tt, t_last=t_last),
        grid=grid,
        in_specs=[tok(COL_XB // W_MIX), tok(COL_ZB // W_MIX),
                  pl.BlockSpec((None, nb, 1, W_MIX), lambda b, t: (state_layer, b, 0, 0)),
                  pl.BlockSpec((None, nb, CONV_W - 1, W_MIX), lambda b, t: (state_layer, b, 0, 0)),
                  lw((CONV_W, W_MIX)), lw((1, W_MIX)), lw((LRU_BLOCKS, LRU_BLK, 2 * LRU_BLK)),
                  lw((1, W_MIX)), lw((1, W_MIX)), lw((1, W_MIX))],
        out_specs=[pl.BlockSpec((rows, W_MIX), lambda b, t: (b * ntc + t, 0)),
                   pl.BlockSpec((nb, 1, W_MIX), lambda b, t: (b, 0, 0))],
        out_shape=[jax.ShapeDtypeStruct((batch * seq, W_MIX), BF16),
                   jax.ShapeDtypeStruct((batch, 1, W_MIX), F32)],
        scratch_shapes=[pltpu.VMEM((CONV_PAD + tt, W_MIX), F32),
                        pltpu.VMEM((nb, CONV_PAD, W_MIX), F32),
                        pltpu.VMEM((rows, W_MIX), F32), pltpu.VMEM((rows, W_MIX), F32),
                        pltpu.VMEM((rows, W_MIX), F32), pltpu.VMEM((nb, 1, W_MIX), F32)],
        compiler_params=_cparams(2),
        name="lru_mixer",
    )(proj, proj, h0, conv0, cw, cb, wri, br, bi_, lam)


def _for_each_group(nb, g, group_body):
    assert nb % g == 0
    if nb == g:
        group_body(list(range(nb)))
    else:
        def body(i, carry):
            group_body([i * g + j for j in range(g)])
            return carry
        lax.fori_loop(0, nb // g, body, 0)


def _gla_kernel(q_ref, k_ref, v_ref, z_ref, tail_ref, s0_ref, wg_ref, bg_ref, go_ref, stack_ref,
                mix_ref, so_ref,
                s_scr, mix_scr, *, nb, group, c, cv, single_chunk):
    del stack_ref
    tc = pl.program_id(1)
    ntc = pl.num_programs(1)
    sb = min(GLA_SUB, c)
    nblk = c // sb

    s_src, s_dst = (s0_ref, so_ref) if single_chunk else (s_scr, s_scr)
    if not single_chunk:
        @pl.when(tc == 0)
        def _():
            s_scr[...] = s0_ref[...]

    row_c = lax.broadcasted_iota(jnp.int32, (c, 1), 0)
    lane_c = lax.broadcasted_iota(jnp.int32, (sb, c), 1)
    row_sb = lax.broadcasted_iota(jnp.int32, (sb, 1), 0)

    def group_body(bis):
        pairs = [(g, h) for g in range(len(bis)) for h in range(GLA_H)]
        heads = range(len(pairs))
        b_all, q_all, k_all, v_all = [], [], [], []
        for bi in bis:
            x = _dot(tail_ref[bi].astype(BF16), wg_ref[...]) + bg_ref[...]
            b_all.append(_cumsum_rows(_log_sigmoid(x) * (1.0 / GLA_TAU)))
            q_all.append(q_ref[bi])
            k_all.append(k_ref[bi])
            v_all.append(v_ref[bi])
        s_old = [s_src[bis[g], h] for g, h in pairs]
        q = [q_all[g][:, h * GLA_DK:(h + 1) * GLA_DK] * (GLA_DK ** -0.5) for g, h in pairs]
        k = [k_all[g][:, h * GLA_DK:(h + 1) * GLA_DK] for g, h in pairs]
        v = [v_all[g][:, h * GLA_DV:(h + 1) * GLA_DV] for g, h in pairs]
        b = [b_all[g][:, h * GLA_DK:(h + 1) * GLA_DK] for g, h in pairs]
        o_state = [_dot((q[h] * jnp.exp(b[h])).astype(BF16), s_old[h].astype(BF16)) for h in heads]
        s_new = []
        for h in heads:
            b_last = b[h][cv - 1:cv]
            dec = b_last - b[h]
            if cv < c:
                dec = jnp.where(row_c < cv, dec, NEG_INF)
            kd = k[h] * jnp.exp(dec)
            d_col = jnp.transpose(jnp.broadcast_to(jnp.exp(b_last), (V7X_SUBLANES, GLA_DK)))[:, 0:1]
            s_new.append(d_col * s_old[h] + _mm(kd, v[h], dot=_dot_tn, contract=0))
        att_off = []
        for h in heads:
            per_blk = [jnp.zeros((sb, c), F32)]
            for blk in range(1, nblk):
                r0 = blk * sb
                b_ref_row = b[h][r0 - 1:r0]
                qs = (q[h][r0:r0 + sb] * jnp.exp(b[h][r0:r0 + sb] - b_ref_row)).astype(BF16)
                kd = (k[h] * jnp.exp(jnp.where(row_c < r0, b_ref_row - b[h], NEG_INF))).astype(BF16)
                per_blk.append(_dot_nt(qs, kd))
            att_off.append(per_blk)
        att = []
        for h in heads:
            att_rows = []
            for blk in range(nblk):
                r0 = blk * sb
                q_i = q[h][r0:r0 + sb]
                b_i = b[h][r0:r0 + sb]
                a = att_off[h][blk]
                for sl in range(sb):
                    s_abs = r0 + sl
                    e = jnp.exp(jnp.where(row_sb >= sl, b_i - b[h][s_abs:s_abs + 1], NEG_INF))
                    col = jnp.sum(q_i * (k[h][s_abs:s_abs + 1] * e), axis=-1, keepdims=True)
                    a = jnp.where(lane_c == s_abs, col, a)
                att_rows.append(a)
            att.append(att_rows[0] if nblk == 1 else jnp.concatenate(att_rows, axis=0))
        o = [_mm(att[h], v[h]) + o_state[h] for h in heads]
        for p, (g, h) in enumerate(pairs):
            vs = slice(h * GLA_DV, (h + 1) * GLA_DV)
            ms = jnp.mean(o[p] * o[p], axis=-1, keepdims=True)
            on = (o[p] * lax.rsqrt(ms + EPS)) * go_ref[...]
            mix_scr[bis[g], :, vs] = on * _silu(z_ref[bis[g], :, vs])
            s_dst[bis[g], h] = s_new[p]

    _for_each_group(nb, group, group_body)
    mix_ref[...] = mix_scr[...].astype(BF16)

    if not single_chunk:
        @pl.when(tc == ntc - 1)
        def _():
            so_ref[...] = s_scr[...]


def _stack_alias(stack, n_inputs_before):
    spec = pl.BlockSpec(memory_space=pl.ANY)
    if stack is None:
        return jnp.zeros((V7X_SUBLANES, V7X_LANES), F32), spec, {}
    return stack, spec, {n_inputs_before: 1}


def gla_mixer(proj, tail, s0, wg, bg, go, layer, state_layer, *, batch, seq, c, cv, nb, group, n_layers, stack=None):
    ntc = seq // c
    assert batch % nb == 0 and seq % c == 0
    grid = (batch // nb, ntc)
    proj3 = proj.reshape(batch, seq, proj.shape[-1])
    tail3 = tail.reshape(batch, seq, TAIL_W)

    def tok(width, col):
        return pl.BlockSpec((nb, c, width), lambda b, t: (b, t, col // width))

    def lw(shape):
        nd = len(shape)
        return pl.BlockSpec((None,) + shape, lambda b, t: (layer,) + (0,) * nd)

    st_shape = (nb, GLA_H, GLA_DK, GLA_DV)
    stack_arg, stack_spec, aliases = _stack_alias(stack, 9)
    mix, state = pl.pallas_call(
        functools.partial(_gla_kernel, nb=nb, group=group, c=c, cv=cv, single_chunk=ntc == 1),
        grid=grid,
        in_specs=[tok(GLA_H * GLA_DK, COL_QC), tok(GLA_H * GLA_DK, COL_KC), tok(W_MIX, COL_VC), tok(W_MIX, COL_ZC),
                  tok(TAIL_W, 0),
                  pl.BlockSpec((None,) + st_shape, lambda b, t: (state_layer, b, 0, 0, 0)),
                  lw((TAIL_W, GLA_H * GLA_DK)), lw((1, GLA_H * GLA_DK)), lw((1, GLA_DV)), stack_spec],
        out_specs=[tok(W_MIX, 0),
                   pl.BlockSpec((None,) + st_shape, lambda b, t: (layer, b, 0, 0, 0))],
        out_shape=[jax.ShapeDtypeStruct((batch, seq, W_MIX), BF16),
                   jax.ShapeDtypeStruct((n_layers, batch, GLA_H, GLA_DK, GLA_DV), F32)],
        scratch_shapes=[pltpu.VMEM(st_shape, F32), pltpu.VMEM((nb, c, W_MIX), F32)],
        input_output_aliases=aliases,
        compiler_params=_cparams(2),
        name="gla_mixer",
    )(proj3, proj3, proj3, proj3, tail3, s0, wg, bg, go, stack_arg)
    return mix.reshape(batch * seq, W_MIX), state


def _l2norm(x):
    return x * lax.rsqrt(jnp.sum(x * x, axis=-1, keepdims=True) + EPS)


def _unit_lower_solve(ms, rhss, c, cv):
    if c <= V7X_SUBLANES:
        row = lax.broadcasted_iota(jnp.int32, (c, 1), 0)
        us = []
        for m, rhs in zip(ms, rhss):
            u = jnp.zeros_like(rhs)
            u_rows = []
            for t in range(cv):
                ut = rhs[t:t + 1]
                for s in range(t):
                    ut = ut - m[t:t + 1, s:s + 1] * u_rows[s]
                u_rows.append(ut)
                u = jnp.where(row == t, ut, u)
            us.append(u)
        return us
    xs = [-m for m in ms]
    ys = list(rhss)
    dv = rhss[0].shape[1]
    levels = int(math.log2(c))
    assert 3 * c <= V7X_MXU_DEPTH

    def hi_lo(t):
        hi = t.astype(BF16)
        return hi, (t - hi.astype(F32)).astype(BF16)

    for lvl in range(levels):
        last = lvl == levels - 1
        ps = []
        for x, y in zip(xs, ys):
            xh, xl = hi_lo(x)
            yh, yl = hi_lo(y)
            bh = yh if last else jnp.concatenate([yh, xh], axis=1)
            bl = yl if last else jnp.concatenate([yl, xl], axis=1)
            lhs = jnp.concatenate([xh.astype(F32), xh.astype(F32), xl.astype(F32)], axis=1).astype(BF16)
            ps.append(_dot(lhs, jnp.concatenate([bh, bl, bh], axis=0)))
        ys = [y + p[:, :dv] for y, p in zip(ys, ps)]
        if not last:
            xs = [p[:, dv:] for p in ps]
    return ys


def _gdn_kernel(qkv_ref, z_ref, tail_ref, s0_ref, cv0_ref, cw_ref, alog_ref, dtb_ref, go_ref, stack_ref,
                mix_ref, so_ref,
                s_scr, xp_scr, tail_scr, mix_scr, *, nb, group, c, cv, single_chunk):
    del stack_ref
    tc = pl.program_id(1)
    ntc = pl.num_programs(1)

    s_src, s_dst = (s0_ref, so_ref) if single_chunk else (s_scr, s_scr)

    @pl.when(tc == 0)
    def _():
        if not single_chunk:
            s_scr[...] = s0_ref[...]
        _init_conv_tail(tail_scr, cv0_ref, nb)

    row_c = lax.broadcasted_iota(jnp.int32, (c, 1), 0)
    ri = lax.broadcasted_iota(jnp.int32, (c, c), 0)
    ci = lax.broadcasted_iota(jnp.int32, (c, c), 1)
    lane_t = lax.broadcasted_iota(jnp.int32, (c, TAIL_W), 1)
    a_lanes = (lane_t >= TAIL_A) & (lane_t < TAIL_A + DN_H)

    def group_body(bis):
        pairs = [(g, h) for g in range(len(bis)) for h in range(DN_H)]
        heads = range(len(pairs))
        qkv, gam, gam_t, beta = [], [], [], []
        for gi, bi in enumerate(bis):
            qkv.append(_silu(_causal_conv(xp_scr.at[gi], tail_scr, qkv_ref[bi], cw_ref, bi)))
            tl = tail_ref[bi]
            gg = _cumsum_rows(jnp.where(a_lanes, -jnp.exp(alog_ref[...]) * _softplus(tl + dtb_ref[...]), 0.0))
            gam.append(gg)
            gam_t.append(jnp.transpose(gg))
            beta.append(_sigmoid(tl))
        s_old = [s_src[bis[g], h] for g, h in pairs]
        s16 = [s.astype(BF16) for s in s_old]
        q16, k16, kf, v = [], [], [], []
        for g, h in pairs:
            q = _l2norm(qkv[g][:, h * DN_DK:(h + 1) * DN_DK]) * (DN_DK ** -0.5)
            k = _l2norm(qkv[g][:, DN_H * DN_DK + h * DN_DK:DN_H * DN_DK + (h + 1) * DN_DK])
            q16.append(q.astype(BF16))
            k16.append(k.astype(BF16))
            kf.append(k)
            v.append(qkv[g][:, 2 * DN_H * DN_DK + h * DN_DV:2 * DN_H * DN_DK + (h + 1) * DN_DV])
        g_col = [gam[g][:, TAIL_A + h:TAIL_A + h + 1] for g, h in pairs]
        g_row = [gam_t[g][TAIL_A + h:TAIL_A + h + 1, :] for g, h in pairs]
        b_col = [beta[g][:, TAIL_B + h:TAIL_B + h + 1] for g, h in pairs]
        decay = [jnp.exp(jnp.where(ri >= ci, g_col[h] - g_row[h], NEG_INF)) for h in heads]
        eg = [jnp.exp(g_col[h]) for h in heads]
        kk = [_dot_nt(k16[h], k16[h]) for h in heads]
        k_s = [_dot(k16[h], s16[h]) for h in heads]
        q_s = [_dot(q16[h], s16[h]) for h in heads]
        qk = [_dot_nt(q16[h], k16[h]) for h in heads]
        m = [jnp.where(ri > ci, b_col[h] * kk[h] * decay[h], 0.0) for h in heads]
        rhs = [b_col[h] * (v[h] - eg[h] * k_s[h]) for h in heads]
        u = _unit_lower_solve(m, rhs, c, cv)
        o = [eg[h] * q_s[h] + _mm(qk[h] * decay[h], u[h]) for h in heads]
        kd = []
        for h in heads:
            dec = g_col[h][cv - 1:cv] - g_col[h]
            if cv < c:
                dec = jnp.where(row_c < cv, dec, NEG_INF)
            kd.append(kf[h] * jnp.exp(dec))
        s_new = [jnp.exp(g_col[h][cv - 1:cv]) * s_old[h] + _mm(kd[h], u[h], dot=_dot_tn, contract=0) for h in heads]
        for p, (g, h) in enumerate(pairs):
            vs = slice(h * DN_DV, (h + 1) * DN_DV)
            ms = jnp.mean(o[p] * o[p], axis=-1, keepdims=True)
            on = (o[p] * lax.rsqrt(ms + EPS)) * go_ref[...]
            mix_scr[bis[g], :, vs] = on * _silu(z_ref[bis[g], :, vs])
            s_dst[bis[g], h] = s_new[p]

    _for_each_group(nb, group, group_body)
    mix_ref[...] = mix_scr[...].astype(BF16)

    if not single_chunk:
        @pl.when(tc == ntc - 1)
        def _():
            so_ref[...] = s_scr[...]


def gdn_mixer(proj, tail, s0, conv0, cw, alog, dtb, go, layer, state_layer, *, batch, seq, c, cv, nb, group, n_layers,
              stack=None):
    ntc = seq // c
    assert batch % nb == 0 and seq % c == 0
    grid = (batch // nb, ntc)
    proj3 = proj.reshape(batch, seq, proj.shape[-1])
    tail3 = tail.reshape(batch, seq, TAIL_W)

    def tok(width, col):
        return pl.BlockSpec((nb, c, width), lambda b, t: (b, t, col // width))

    def lw(shape):
        nd = len(shape)
        return pl.BlockSpec((None,) + shape, lambda b, t: (layer,) + (0,) * nd)

    st_shape = (nb, DN_H, DN_DK, DN_DV)
    assert COL_QKVD % DN_QKV == 0
    stack_arg, stack_spec, aliases = _stack_alias(stack, 9)
    mix, state = pl.pallas_call(
        functools.partial(_gdn_kernel, nb=nb, group=group, c=c, cv=cv, single_chunk=ntc == 1),
        grid=grid,
        in_specs=[tok(DN_QKV, COL_QKVD),
                  tok(W_MIX, COL_ZD),
                  tok(TAIL_W, 0),
                  pl.BlockSpec((None,) + st_shape, lambda b, t: (state_layer, b, 0, 0, 0)),
                  pl.BlockSpec((None, nb, CONV_W - 1, DN_QKV), lambda b, t: (state_layer, b, 0, 0)),
                  lw((CONV_W, DN_QKV)), lw((1, TAIL_W)), lw((1, TAIL_W)), lw((1, DN_DV)), stack_spec],
        out_specs=[tok(W_MIX, 0),
                   pl.BlockSpec((None,) + st_shape, lambda b, t: (layer, b, 0, 0, 0))],
        out_shape=[jax.ShapeDtypeStruct((batch, seq, W_MIX), BF16),
                   jax.ShapeDtypeStruct((n_layers, batch, DN_H, DN_DK, DN_DV), F32)],
        input_output_aliases=aliases,
        scratch_shapes=[pltpu.VMEM(st_shape, F32),
                        pltpu.VMEM((group, CONV_PAD + c, DN_QKV), F32),
                        pltpu.VMEM((nb, CONV_PAD, DN_QKV), F32),
                        pltpu.VMEM((nb, c, W_MIX), F32)],
        compiler_params=_cparams(2),
        name="gdn_mixer",
    )(proj3, proj3, tail3, s0, conv0, cw, alog, dtb, go, stack_arg)
    return mix.reshape(batch * seq, W_MIX), state


def _prepare_weights(w_in, s5_lam_re, s5_lam_im, s5_log_dt, s5_b_re, s5_b_im, s5_c_re, s5_c_im, s5_d,
                     s5_w_glu, lru_w_r, lru_w_i, gla_w_gate, dn_a_log, dn_dt_bias, w_out, ple_w, ple_gate_w):
    nl = w_in.shape[0]
    p = {}
    p["w_main"], p["w_tail"] = repack_w_in(jnp.swapaxes(w_in, 1, 2))
    p["w_out"] = w_out.astype(BF16)
    p["ple_w"] = ple_w.astype(BF16)
    p["ple_gate_w"] = ple_gate_w.astype(BF16)
    p["w_glu"] = s5_w_glu.astype(BF16)
    ldt = jnp.repeat(s5_log_dt, S5_STATE, axis=1).reshape(nl, 1, S5_CH)
    coef, cre, cim = s5_prep(s5_lam_re.reshape(nl, 1, S5_CH), s5_lam_im.reshape(nl, 1, S5_CH), ldt)
    bb_re, bb_im = s5_bb(cre.reshape(nl, S5_CH, 1), cim.reshape(nl, S5_CH, 1),
                         s5_b_re.reshape(nl, S5_CH, S5_GROUP), s5_b_im.reshape(nl, S5_CH, S5_GROUP))
    eye16 = jnp.eye(16, dtype=F32)
    eye8 = jnp.eye(8, dtype=F32)

    def pack_u(bb):
        t = bb.reshape(nl, 4, 16, S5_STATE, S5_GROUP)
        return jnp.einsum("lkgph,gG->lkghGp", t, eye16).reshape(nl, 4, 256, 1024).astype(BF16)

    def pack_y(cc):
        t = cc.reshape(nl, 8, 8, S5_GROUP, S5_STATE)
        return jnp.einsum("ljghp,gG->ljgpGh", t, eye8).reshape(nl, 8, 512, 128).astype(BF16)

    p["s5_coef"] = coef
    p["s5_wur"], p["s5_wui"] = pack_u(bb_re), pack_u(bb_im)
    p["s5_wyr"], p["s5_wyi"] = pack_y(s5_c_re), pack_y(s5_c_im)
    p["s5_d"] = s5_d.reshape(nl, 1, W_MIX)
    p["lru_wri"] = jnp.concatenate([lru_w_r, lru_w_i], axis=-1).astype(BF16)
    p["gla_wg"] = jnp.concatenate([gla_w_gate, jnp.zeros((nl, TAIL_W - GLA_RANK, GLA_H * GLA_DK), F32)],
                                  axis=1).astype(BF16)

    def tail_row(x):
        return jnp.pad(x, ((0, 0), (TAIL_A, TAIL_W - TAIL_A - DN_H))).reshape(nl, 1, TAIL_W)

    p["dn_alog"] = tail_row(dn_a_log)
    p["dn_dtb"] = tail_row(dn_dt_bias)
    return p


def kernel(x_prompt, x_sample, state_s5_re, state_s5_im, state_lru_h, state_lru_conv, state_gla, state_delta, state_delta_conv, p_prompt, p_sample, g_norm, w_in, s5_lam_re, s5_lam_im, s5_log_dt, s5_b_re, s5_b_im, s5_c_re, s5_c_im, s5_d, s5_w_glu, s5_b_glu, lru_conv_w, lru_conv_b, lru_w_r, lru_b_r, lru_w_i, lru_b_i, lru_lam, gla_w_gate, gla_b_gate, gla_g_out, dn_conv_w, dn_a_log, dn_dt_bias, dn_g_out, w_out, ple_w, ple_gate_w, ple_gate_b, g_final):
    nl = w_in.shape[0]
    bp, tp, _ = x_prompt.shape
    bs, ts, _ = x_sample.shape
    tsp = SAMPLE_T_PAD
    mp, ms = bp * tp, bs * ts

    p = _prepare_weights(w_in, s5_lam_re, s5_lam_im, s5_log_dt, s5_b_re, s5_b_im, s5_c_re, s5_c_im, s5_d,
                         s5_w_glu, lru_w_r, lru_w_i, gla_w_gate, dn_a_log, dn_dt_bias, w_out, ple_w, ple_gate_w)

    def vec(x):
        return x.reshape(nl, 1, x.shape[-1])

    g_norm3, b_glu3, cb3 = vec(g_norm), vec(s5_b_glu), vec(lru_conv_b)
    br3, bi3, lam3 = vec(lru_b_r), vec(lru_b_i), vec(lru_lam)
    bg3, go_c3, go_d3, pgb3 = vec(gla_b_gate), vec(gla_g_out), vec(dn_g_out), vec(ple_gate_b)
    pe_p = p_prompt.reshape(nl, mp, D_PLE)
    pe_s = p_sample.reshape(nl, ms, D_PLE)

    z_s5 = jnp.zeros((1, bp, 1, S5_CH), F32)
    z_lru = jnp.zeros((1, bp, 1, W_MIX), F32)
    z_lconv = jnp.zeros((1, bp, CONV_W - 1, W_MIX), F32)
    z_gla = jnp.zeros((1, bp, GLA_H, GLA_DK, GLA_DV), F32)
    z_dn = jnp.zeros((1, bp, DN_H, DN_DK, DN_DV), F32)
    z_dconv = jnp.zeros((1, bp, CONV_W - 1, DN_QKV), F32)
    c_s5r = state_s5_re.reshape(nl, bs, 1, S5_CH)
    c_s5i = state_s5_im.reshape(nl, bs, 1, S5_CH)
    c_lru = state_lru_h.reshape(nl, bs, 1, W_MIX)

    def mixers(proj, tail, layer, *, batch, seq, t_valid, st, sl, nb_scan, tt, c, gla_grp, nb_gdn, gdn_grp, stacks):
        s5r0, s5i0, lru0, lconv0, gla0, dn0, dconv0 = st
        gla_stack, dn_stack = stacks
        mix_a, s5r, s5i = s5_mixer(proj, s5r0, s5i0, p["s5_coef"], p["s5_wur"], p["s5_wui"], p["s5_wyr"],
                                   p["s5_wyi"], p["s5_d"], p["w_glu"], b_glu3, layer, sl,
                                   batch=batch, seq=seq, t_valid=t_valid, nb=nb_scan, tt=tt)
        mix_b, lruh = lru_mixer(proj, lru0, lconv0, lru_conv_w, cb3, p["lru_wri"], br3, bi3, lam3, layer, sl,
                                batch=batch, seq=seq, t_valid=t_valid, nb=nb_scan, tt=tt)
        mix_c, glas = gla_mixer(proj, tail, gla0, p["gla_wg"], bg3, go_c3, layer, sl,
                                batch=batch, seq=seq, c=c, cv=min(c, t_valid), nb=gla_grp, group=gla_grp, n_layers=nl,
                                stack=gla_stack)
        mix_d, dns = gdn_mixer(proj, tail, dn0, dconv0, dn_conv_w, p["dn_alog"], p["dn_dtb"], go_d3, layer, sl,
                               batch=batch, seq=seq, c=c, cv=min(c, t_valid), nb=nb_gdn, group=gdn_grp, n_layers=nl,
                               stack=dn_stack)
        return (mix_a, mix_b, mix_c, mix_d), (s5r, s5i, lruh, glas, dns)

    hp = x_prompt.reshape(mp, D_MODEL)
    hs = x_sample.reshape(ms, D_MODEL)
    new_p, new_s = [], []
    stacks_p = stacks_s = (None, None)
    hgp, ssqp = prenorm(hp, g_norm3, 0, tm=DENSE_TILES_SAMPLE["tm"])
    hgs, ssqs = prenorm(hs, g_norm3, 0, tm=DENSE_TILES_SAMPLE["tm"])
    for i in range(nl):
        nxt = min(i + 1, nl - 1)
        proj, tail = in_proj(hgp, ssqp, p["w_main"], p["w_tail"], i, **IN_PROJ_TILES_PROMPT)
        mixes, (s5r, s5i, lruh, glas, dns) = mixers(
            proj, tail, i, batch=bp, seq=tp, t_valid=tp,
            st=(z_s5, z_s5, z_lru, z_lconv, z_gla, z_dn, z_dconv), sl=0,
            nb_scan=1, tt=256, c=GLA_CHUNK, gla_grp=4, nb_gdn=2, gdn_grp=2, stacks=stacks_p)
        stacks_p = (glas, dns)
        proj3 = proj.reshape(bp, tp, N_MAIN)
        new_p.append((s5r.reshape(bp, S5_GROUPS, S5_STATE), s5i.reshape(bp, S5_GROUPS, S5_STATE),
                      lruh.reshape(bp, W_MIX), proj3[:, tp - (CONV_W - 1):, COL_XB:COL_XB + W_MIX],
                      None, None, proj3[:, tp - (CONV_W - 1):, COL_QKVD:COL_QKVD + DN_QKV]))
        hp, hpb = out_proj(mixes, p["w_out"], hp, i, **OUT_PROJ_TILES_PROMPT)
        hp, hgp, ssqp = ple(hp, hpb, pe_p, p["ple_gate_w"], pgb3, p["ple_w"], g_norm3, i, nxt, **DENSE_TILES_PROMPT)

        proj_pad, tail_pad = in_proj(hgs, ssqs, p["w_main"], p["w_tail"], i, seq_rows=ts, seq_pad=tsp,
                                     **DENSE_TILES_SAMPLE)
        proj3 = proj_pad.reshape(bs, tsp, N_MAIN)[:, :ts]
        mixes, (s5r, s5i, lruh, glas, dns) = mixers(
            proj_pad, tail_pad, i, batch=bs, seq=tsp, t_valid=ts,
            st=(c_s5r, c_s5i, c_lru, state_lru_conv, state_gla, state_delta, state_delta_conv), sl=i,
            nb_scan=32, tt=tsp, c=tsp, gla_grp=8, nb_gdn=8, gdn_grp=4, stacks=stacks_s)
        stacks_s = (glas, dns)
        mixes = tuple(m.reshape(bs, tsp, W_MIX)[:, :ts].reshape(ms, W_MIX) for m in mixes)
        new_s.append((s5r.reshape(bs, S5_GROUPS, S5_STATE), s5i.reshape(bs, S5_GROUPS, S5_STATE),
                      lruh.reshape(bs, W_MIX), proj3[:, ts - (CONV_W - 1):, COL_XB:COL_XB + W_MIX],
                      None, None, proj3[:, ts - (CONV_W - 1):, COL_QKVD:COL_QKVD + DN_QKV]))
        hs, hsb = out_proj(mixes, p["w_out"], hs, i, **DENSE_TILES_SAMPLE)
        hs, hgs, ssqs = ple(hs, hsb, pe_s, p["ple_gate_w"], pgb3, p["ple_w"], g_norm3, i, nxt, **DENSE_TILES_SAMPLE)

    g_fin = g_final.reshape(1, D_MODEL)
    y_prompt = final_norm(hp, g_fin, tm=256).reshape(bp, tp, D_MODEL)
    y_sample = final_norm(hs, g_fin, tm=256).reshape(bs, ts, D_MODEL)

    def stk(lst, j):
        return jnp.stack([s[j] for s in lst], axis=0)

    return (y_prompt, y_sample,
            stk(new_p, 0), stk(new_p, 1), stk(new_p, 2), stk(new_p, 3), stacks_p[0], stacks_p[1], stk(new_p, 6),
            stk(new_s, 0), stk(new_s, 1), stk(new_s, 2), stk(new_s, 3), stacks_s[0], stacks_s[1], stk(new_s, 6))
```

```python
import functools
import math

import jax
import jax.numpy as jnp
import numpy as np
from jax import lax
from jax.experimental import pallas as pl
from jax.experimental.pallas import tpu as pltpu

F32 = jnp.float32
BF16 = jnp.bfloat16
EPS = 1e-6
NEG_INF = float("-inf")

D_MODEL = 4096
DEPTH = 4
W_MIX = 1024
S5_GROUPS = 64
S5_GROUP = 16
S5_STATE = 64
S5_CH = S5_GROUPS * S5_STATE
LRU_BLOCKS = 8
LRU_BLK = 128
LRU_C = 8.0
CONV_W = 4
GLA_H = 4
GLA_DK = 128
GLA_DV = 256
GLA_RANK = 16
GLA_TAU = 16.0
GLA_CHUNK = 64
GLA_SUB = 8
DN_H = 8
DN_DK = 128
DN_DV = 128
DN_QKV = 3072
DN_CHUNK = 64
D_PLE = 256
N_MAIN = 11264
COL_XA, COL_ZA, COL_XB, COL_ZB = 0, 1024, 2048, 3072
COL_QC, COL_KC, COL_VC = 4096, 4608, 5120
COL_QKVD, COL_ZC, COL_ZD = 6144, 9216, 10240
ORIG_ZC, ORIG_GC, ORIG_QKVD, ORIG_ZD, ORIG_AB = 6144, 7168, 7184, 10256, 11280
TAIL_W = 128
TAIL_G, TAIL_A, TAIL_B = 0, 16, 24
SAMPLE_T_PAD = 8

V7X_LANES = 128
V7X_SUBLANES = 8
V7X_MXU_DEPTH = 256
VMEM_LIMIT = 52 * 1024 * 1024
DENSE_TILES_PROMPT = dict(tm=1024, tn=512)
IN_PROJ_TILES_PROMPT = dict(tm=1024, tn=1024)
OUT_PROJ_TILES_PROMPT = dict(tm=1024, tn=1024)
VMEM_HEADROOM = 6 * 1024 * 1024
DENSE_TILES_SAMPLE = dict(tm=512, tn=1024)


def _cparams(n_axes, vmem_limit=VMEM_LIMIT):
    return pltpu.CompilerParams(dimension_semantics=("arbitrary",) * n_axes,
                                vmem_limit_bytes=vmem_limit)


def _sigmoid(x):
    return jax.nn.sigmoid(x)


def _silu(x):
    return x * jax.nn.sigmoid(x)


def _softplus(x):
    return jnp.maximum(x, 0.0) + jnp.log1p(jnp.exp(-jnp.abs(x)))


def _neg_expm1_nonpos(x):
    t = jnp.tanh(0.5 * x)
    return (-2.0 * t) / (1.0 - t)


def _log_sigmoid(x):
    return jnp.minimum(x, 0.0) - jnp.log1p(jnp.exp(-jnp.abs(x)))


def _gelu_tanh(x):
    c = math.sqrt(2.0 / math.pi)
    return x * (0.5 * (1.0 + jnp.tanh(c * (x + 0.044715 * (x * x * x)))))


def _dot(a, b):
    return jnp.dot(a, b, preferred_element_type=F32)


def _dot_nt(a, b):
    return lax.dot_general(a, b, (((1,), (1,)), ((), ())), preferred_element_type=F32)


def _dot_tn(a, b):
    return lax.dot_general(a, b, (((0,), (0,)), ((), ())), preferred_element_type=F32)


def _split3(x):
    h = x.astype(BF16)
    r = x - h.astype(F32)
    m = r.astype(BF16)
    l = (r - m.astype(F32)).astype(BF16)
    return h, m, l


def _cumsum_rows(x):
    c = x.shape[0]
    if c <= V7X_SUBLANES:
        row = lax.broadcasted_iota(jnp.int32, x.shape, 0)
        k = 1
        while k < c:
            x = x + jnp.where(row >= k, pltpu.roll(x, k, 0), 0.0)
            k *= 2
        return x
    h, m, l = _split3(x)
    if 3 * c <= V7X_MXU_DEPTH:
        col = lax.broadcasted_iota(jnp.int32, (c, 3 * c), 1)
        col = jnp.where(col >= 2 * c, col - 2 * c, jnp.where(col >= c, col - c, col))
        tri3 = (lax.broadcasted_iota(jnp.int32, (c, 3 * c), 0) >= col).astype(BF16)
        return _dot(tri3, jnp.concatenate([h, m, l], axis=0))
    tri = (lax.broadcasted_iota(jnp.int32, (c, c), 0) >= lax.broadcasted_iota(jnp.int32, (c, c), 1)).astype(BF16)
    return _dot(tri, h) + _dot(tri, m) + _dot(tri, l)


def _mm(a, b, dot=_dot, contract=1):
    if a.shape[contract] < 2 * V7X_SUBLANES:
        return dot(a, b)
    return dot(a.astype(BF16), b.astype(BF16))


def _prenorm_kernel(x_ref, g_ref, hg_ref, ssq_ref):
    x = x_ref[...]
    hg_ref[...] = (x * g_ref[...]).astype(BF16)
    ssq_ref[...] = jnp.sum(x * x, axis=-1, keepdims=True)


def prenorm(x, g_norm, layer, *, tm):
    m = x.shape[0]
    return pl.pallas_call(
        _prenorm_kernel,
        grid=(m // tm,),
        in_specs=[pl.BlockSpec((tm, D_MODEL), lambda i: (i, 0)),
                  pl.BlockSpec((None, 1, D_MODEL), lambda i: (layer, 0, 0))],
        out_specs=[pl.BlockSpec((tm, D_MODEL), lambda i: (i, 0)), pl.BlockSpec((tm, 1), lambda i: (i, 0))],
        out_shape=[jax.ShapeDtypeStruct((m, D_MODEL), BF16), jax.ShapeDtypeStruct((m, 1), F32)],
        compiler_params=_cparams(1),
        name="prenorm",
    )(x, g_norm)


def _in_proj_kernel(hg_ref, ssq_ref, w_ref, wt_ref, o_ref, ot_ref, *, seq_rows, seq_pad):
    r = lax.rsqrt(ssq_ref[...] * (1.0 / D_MODEL) + EPS)

    def put(dst_ref, val):
        if seq_pad == seq_rows:
            dst_ref[...] = val
        else:
            n_seq = val.shape[0] // seq_rows
            dst_ref[:, 0:seq_rows, :] = val.reshape(n_seq, seq_rows, val.shape[1])
            dst_ref[:, seq_rows:seq_pad, :] = jnp.zeros((n_seq, seq_pad - seq_rows, val.shape[1]), F32)

    @pl.when(pl.program_id(1) == 0)
    def _():
        put(ot_ref, _dot_nt(hg_ref[...], wt_ref[...]) * r)

    put(o_ref, _dot_nt(hg_ref[...], w_ref[...]) * r)


def in_proj(hg, ssq, w_main, w_tail, layer, *, tm, tn, seq_rows=None, seq_pad=None):
    m = hg.shape[0]
    grid = (m // tm, N_MAIN // tn)
    if seq_rows is None:
        seq_rows = seq_pad = 1
        out_specs = [pl.BlockSpec((tm, tn), lambda i, j: (i, j)), pl.BlockSpec((tm, TAIL_W), lambda i, j: (i, 0))]
        out_shape = [jax.ShapeDtypeStruct((m, N_MAIN), F32), jax.ShapeDtypeStruct((m, TAIL_W), F32)]
    else:
        n_seq = tm // seq_rows
        out_specs = [pl.BlockSpec((n_seq, seq_pad, tn), lambda i, j: (i, 0, j)),
                     pl.BlockSpec((n_seq, seq_pad, TAIL_W), lambda i, j: (i, 0, 0))]
        out_shape = [jax.ShapeDtypeStruct((m // seq_rows, seq_pad, N_MAIN), F32),
                     jax.ShapeDtypeStruct((m // seq_rows, seq_pad, TAIL_W), F32)]
    proj, tail = pl.pallas_call(
        functools.partial(_in_proj_kernel, seq_rows=seq_rows, seq_pad=seq_pad),
        grid=grid,
        in_specs=[
            pl.BlockSpec((tm, D_MODEL), lambda i, j: (i, 0)),
            pl.BlockSpec((tm, 1), lambda i, j: (i, 0)),
            pl.BlockSpec((None, tn, D_MODEL), lambda i, j: (layer, j, 0)),
            pl.BlockSpec((None, TAIL_W, D_MODEL), lambda i, j: (layer, 0, 0)),
        ],
        out_specs=out_specs,
        out_shape=out_shape,
        compiler_params=_cparams(2),
        name="in_proj",
    )(hg, ssq, w_main, w_tail)
    return proj.reshape(-1, N_MAIN), tail.reshape(-1, TAIL_W)


def _out_proj_kernel(ma_ref, mb_ref, mc_ref, md_ref, w_ref, h_ref, o_ref, ob_ref):
    acc = h_ref[...]
    for k, m_ref in enumerate((ma_ref, mb_ref, mc_ref, md_ref)):
        acc = acc + _dot(m_ref[...], w_ref[k * W_MIX:(k + 1) * W_MIX, :])
    o_ref[...] = acc
    ob_ref[...] = acc.astype(BF16)


def out_proj(mixes, w_out, h, layer, *, tm, tn):
    m = h.shape[0]
    grid = (m // tm, D_MODEL // tn)
    mix_spec = pl.BlockSpec((tm, W_MIX), lambda i, j: (i, 0))
    tile = pl.BlockSpec((tm, tn), lambda i, j: (i, j))
    return pl.pallas_call(
        _out_proj_kernel,
        grid=grid,
        in_specs=[mix_spec, mix_spec, mix_spec, mix_spec,
                  pl.BlockSpec((None, 4 * W_MIX, tn), lambda i, j: (layer, 0, j)), tile],
        out_specs=[tile, tile],
        out_shape=[jax.ShapeDtypeStruct((m, D_MODEL), F32), jax.ShapeDtypeStruct((m, D_MODEL), BF16)],
        compiler_params=_cparams(2, max(VMEM_LIMIT, 2 * (tm * D_MODEL * 2 + 4 * W_MIX * tn * 2 + tm * tn * 10)
                                        + VMEM_HEADROOM)),
        name="out_proj",
    )(*mixes, w_out, h)


def _ple_kernel(hrow_ref, htile_ref, pe_ref, wg_ref, bg_ref, wp_ref, gn_ref, o_ref, hg_ref, ssq_ref):
    gate = _sigmoid(_dot(hrow_ref[...], wg_ref[...]) + bg_ref[...])
    pv = _dot(pe_ref[...], wp_ref[...])
    h = htile_ref[...] + gate * pv
    o_ref[...] = h
    hg_ref[...] = (h * gn_ref[...]).astype(BF16)
    part = jnp.sum(h * h, axis=-1, keepdims=True)

    @pl.when(pl.program_id(1) == 0)
    def _():
        ssq_ref[...] = part

    @pl.when(pl.program_id(1) != 0)
    def _():
        ssq_ref[...] = ssq_ref[...] + part


def ple(h, hb, pe, w_gate, b_gate, w_ple, g_next, layer, next_layer, *, tm, tn):
    m = h.shape[0]
    grid = (m // tm, D_MODEL // tn)
    tile = pl.BlockSpec((tm, tn), lambda i, j: (i, j))
    return pl.pallas_call(
        _ple_kernel,
        grid=grid,
        in_specs=[
            pl.BlockSpec((tm, D_MODEL), lambda i, j: (i, 0)),
            tile,
            pl.BlockSpec((None, tm, D_PLE), lambda i, j: (layer, i, 0)),
            pl.BlockSpec((None, D_MODEL, tn), lambda i, j: (layer, 0, j)),
            pl.BlockSpec((None, 1, tn), lambda i, j: (layer, 0, j)),
            pl.BlockSpec((None, D_PLE, tn), lambda i, j: (layer, 0, j)),
            pl.BlockSpec((None, 1, tn), lambda i, j: (next_layer, 0, j)),
        ],
        out_specs=[tile, tile, pl.BlockSpec((tm, 1), lambda i, j: (i, 0))],
        out_shape=[jax.ShapeDtypeStruct((m, D_MODEL), F32), jax.ShapeDtypeStruct((m, D_MODEL), BF16),
                   jax.ShapeDtypeStruct((m, 1), F32)],
        compiler_params=_cparams(2),
        name="ple",
    )(hb, h, pe, w_gate, b_gate, w_ple, g_next)


REPACK_ROWS = 512


def _repack_kernel(a_ref, g_ref, ab_ref, o_ref, t_ref):
    o_ref[...] = a_ref[...].astype(BF16)

    @pl.when(pl.program_id(1) == 0)
    def _():
        t_ref[...] = jnp.zeros(t_ref.shape, BF16)
        t_ref[TAIL_G:TAIL_G + GLA_RANK, :] = g_ref[...].astype(BF16)
        t_ref[TAIL_A:TAIL_A + 2 * DN_H, :] = ab_ref[...].astype(BF16)


def repack_w_in(w_in_t):
    nl = w_in_t.shape[0]
    n_tiles = N_MAIN // REPACK_ROWS
    t_qkvd, t_zc, t_zd = COL_QKVD // REPACK_ROWS, COL_ZC // REPACK_ROWS, COL_ZD // REPACK_ROWS

    def src_row(j):
        return jnp.where(j < t_qkvd, j * REPACK_ROWS,
                         jnp.where(j < t_zc, ORIG_QKVD + (j - t_qkvd) * REPACK_ROWS,
                                   jnp.where(j < t_zd, ORIG_ZC + (j - t_zc) * REPACK_ROWS,
                                             ORIG_ZD + (j - t_zd) * REPACK_ROWS)))

    row_align = 2 * V7X_SUBLANES
    assert all(o % row_align == 0 for o in (ORIG_QKVD, ORIG_ZC, ORIG_ZD, ORIG_GC, ORIG_AB))

    def rows(n, start):
        return pl.BlockSpec((None, pl.Element(n), pl.Element(D_MODEL)),
                            lambda l, j: (l, pl.multiple_of(start(j), row_align), 0))

    return pl.pallas_call(
        _repack_kernel,
        grid=(nl, n_tiles),
        in_specs=[rows(REPACK_ROWS, src_row), rows(GLA_RANK, lambda j: ORIG_GC), rows(2 * DN_H, lambda j: ORIG_AB)],
        out_specs=[pl.BlockSpec((None, REPACK_ROWS, D_MODEL), lambda l, j: (l, j, 0)),
                   pl.BlockSpec((None, TAIL_W, D_MODEL), lambda l, j: (l, 0, 0))],
        out_shape=[jax.ShapeDtypeStruct((nl, N_MAIN, D_MODEL), BF16),
                   jax.ShapeDtypeStruct((nl, TAIL_W, D_MODEL), BF16)],
        compiler_params=_cparams(2),
        name="repack_w_in",
    )(w_in_t, w_in_t, w_in_t)


def _final_norm_kernel(x_ref, g_ref, o_ref):
    x = x_ref[...]
    ms = jnp.mean(x * x, axis=-1, keepdims=True)
    o_ref[...] = (x * lax.rsqrt(ms + EPS)) * g_ref[...]


def final_norm(h, g, *, tm):
    m = h.shape[0]
    return pl.pallas_call(
        _final_norm_kernel,
        grid=(m // tm,),
        in_specs=[pl.BlockSpec((tm, D_MODEL), lambda i: (i, 0)),
                  pl.BlockSpec((1, D_MODEL), lambda i: (0, 0))],
        out_specs=pl.BlockSpec((tm, D_MODEL), lambda i: (i, 0)),
        out_shape=jax.ShapeDtypeStruct((m, D_MODEL), F32),
        compiler_params=_cparams(1),
        name="final_norm",
    )(h, g)


S5_SEG = 32


def _s5_prep_kernel(lre_ref, lim_ref, ldt_ref, coef_ref, cre_ref, cim_ref):
    lre = lre_ref[...]
    lim = lim_ref[...]
    dt = jnp.exp(ldt_ref[...])
    ai = lim * dt
    mag = jnp.exp(lre * dt)
    ar = mag * jnp.cos(ai)
    aim = mag * jnp.sin(ai)
    den = lre * lre + lim * lim
    nr = ar - 1.0
    cre_ref[...] = (nr * lre + aim * lim) / den
    cim_ref[...] = (aim * lre - nr * lim) / den
    pr, pim = [ar], [aim]
    for _ in range(S5_SEG - 1):
        nr_, ni_ = pr[-1] * ar - pim[-1] * aim, pr[-1] * aim + pim[-1] * ar
        pr.append(nr_)
        pim.append(ni_)
    shape = (V7X_SUBLANES, S5_CH)
    coef_ref[0] = jnp.broadcast_to(ar, shape)
    coef_ref[1] = jnp.broadcast_to(aim, shape)
    coef_ref[2] = jnp.broadcast_to(pr[S5_SEG - 1], shape)
    coef_ref[3] = jnp.broadcast_to(pim[S5_SEG - 1], shape)


def s5_prep(lam_re, lam_im, log_dt_rep):
    nl = lam_re.shape[0]
    vec = pl.BlockSpec((None, 1, S5_CH), lambda l: (l, 0, 0))
    return pl.pallas_call(
        _s5_prep_kernel,
        grid=(nl,),
        in_specs=[vec, vec, vec],
        out_specs=[pl.BlockSpec((None, 4, V7X_SUBLANES, S5_CH), lambda l: (l, 0, 0, 0)), vec, vec],
        out_shape=[jax.ShapeDtypeStruct((nl, 4, V7X_SUBLANES, S5_CH), F32),
                   jax.ShapeDtypeStruct((nl, 1, S5_CH), F32),
                   jax.ShapeDtypeStruct((nl, 1, S5_CH), F32)],
        compiler_params=_cparams(1),
        name="s5_prep",
    )(lam_re, lam_im, log_dt_rep)


def _s5_bb_kernel(cr_ref, ci_ref, br_ref, bi_ref, or_ref, oi_ref):
    cr = cr_ref[...]
    ci = ci_ref[...]
    br = br_ref[...]
    bi = bi_ref[...]
    or_ref[...] = cr * br - ci * bi
    oi_ref[...] = cr * bi + ci * br


def s5_bb(coef_re_col, coef_im_col, b_re, b_im):
    nl = b_re.shape[0]
    rows = 1024
    col = pl.BlockSpec((None, rows, 1), lambda l, r: (l, r, 0))
    mat = pl.BlockSpec((None, rows, S5_GROUP), lambda l, r: (l, r, 0))
    return pl.pallas_call(
        _s5_bb_kernel,
        grid=(nl, S5_CH // rows),
        in_specs=[col, col, mat, mat],
        out_specs=[mat, mat],
        out_shape=[jax.ShapeDtypeStruct((nl, S5_CH, S5_GROUP), F32)] * 2,
        compiler_params=_cparams(2),
        name="s5_bb",
    )(coef_re_col, coef_im_col, b_re, b_im)


S5_STRIP = 512


def _s5_kernel(xa_ref, za_ref, h0r_ref, h0i_ref, coef_ref, perm_ref, permt_ref,
               wur_ref, wui_ref, wyr_ref, wyi_ref, dsk_ref, wglu_ref, bglu_ref,
               mix_ref, hro_ref, hio_ref,
               ur_scr, ui_scr, cr_scr, ci_scr, y_scr, *, nb, tt, t_last):
    tc = pl.program_id(1)
    ntc = pl.num_programs(1)
    split = nb == 1
    lc = tt // V7X_SUBLANES if split else tt
    n_sets = 1 if split else nb // V7X_SUBLANES
    set_rows = V7X_SUBLANES * lc

    xb16 = _dot(perm_ref[...], xa_ref[...].astype(BF16)).astype(BF16)
    for k in range(4):
        xk = xb16[:, 256 * k:256 * (k + 1)]
        ur_scr[:, 1024 * k:1024 * (k + 1)] = _dot(xk, wur_ref[k])
        ui_scr[:, 1024 * k:1024 * (k + 1)] = _dot(xk, wui_ref[k])

    @pl.when(tc == 0)
    def _():
        cr_scr[...] = h0r_ref[...]
        ci_scr[...] = h0i_ref[...]

    row8 = lax.broadcasted_iota(jnp.int32, (V7X_SUBLANES, S5_STRIP), 0)
    for st in range(S5_CH // S5_STRIP):
        ls = slice(st * S5_STRIP, (st + 1) * S5_STRIP)
        ar = coef_ref[0, :, ls]
        ai = coef_ref[1, :, ls]
        for s in range(n_sets):
            base = s * set_rows
            if split:
                h0r = jnp.zeros((V7X_SUBLANES, S5_STRIP), F32)
                h0i = jnp.zeros((V7X_SUBLANES, S5_STRIP), F32)
            else:
                h0r = jnp.zeros((V7X_SUBLANES, S5_STRIP), F32)
                h0i = jnp.zeros((V7X_SUBLANES, S5_STRIP), F32)
                for k in range(V7X_SUBLANES):
                    h0r = jnp.where(row8 == k, cr_scr[s * V7X_SUBLANES + k, :, ls], h0r)
                    h0i = jnp.where(row8 == k, ci_scr[s * V7X_SUBLANES + k, :, ls], h0i)

            def step(j, h, base=base, ls=ls, ar=ar, ai=ai):
                hr, hi = h
                rows = pl.ds(pl.multiple_of(base + j * V7X_SUBLANES, V7X_SUBLANES), V7X_SUBLANES)
                nr = (ar * hr - ai * hi) + ur_scr[rows, ls]
                ni = (ar * hi + ai * hr) + ui_scr[rows, ls]
                ur_scr[rows, ls] = nr
                ui_scr[rows, ls] = ni
                return nr, ni

            fr, fi = lax.fori_loop(0, lc, step, (h0r, h0i), unroll=True)
            if split:
                alr = coef_ref[2, 0:1, ls]
                ali = coef_ref[3, 0:1, ls]
                cr = cr_scr[0, :, ls]
                ci = ci_scr[0, :, ls]
                init_r = jnp.zeros((V7X_SUBLANES, S5_STRIP), F32)
                init_i = jnp.zeros((V7X_SUBLANES, S5_STRIP), F32)
                for k in range(V7X_SUBLANES):
                    init_r = jnp.where(row8 == k, cr, init_r)
                    init_i = jnp.where(row8 == k, ci, init_i)
                    cr, ci = (alr * cr - ali * ci) + fr[k:k + 1], (alr * ci + ali * cr) + fi[k:k + 1]
                cr_scr[0, :, ls] = cr
                ci_scr[0, :, ls] = ci

                def fix(j, c, base=base, ls=ls, ar=ar, ai=ai):
                    rows = pl.ds(pl.multiple_of(base + j * V7X_SUBLANES, V7X_SUBLANES), V7X_SUBLANES)
                    c_r = ar * c[0] - ai * c[1]
                    c_i = ar * c[1] + ai * c[0]
                    ur_scr[rows, ls] = ur_scr[rows, ls] + c_r
                    ui_scr[rows, ls] = ui_scr[rows, ls] + c_i
                    return c_r, c_i

                lax.fori_loop(0, lc, fix, (init_r, init_i), unroll=True)

        y_scr[:, V7X_LANES * st:V7X_LANES * (st + 1)] = (_dot(ur_scr[:, ls].astype(BF16), wyr_ref[st])
                                                           - _dot(ui_scr[:, ls].astype(BF16), wyi_ref[st]))
    yh, ym, yl = _split3(y_scr[...])
    pt = permt_ref[...]
    y = (_dot(pt, yh) + _dot(pt, ym) + _dot(pt, yl)) + dsk_ref[...] * xa_ref[...]
    ga = _gelu_tanh(y)
    ya = ga * _sigmoid(_dot(ga.astype(BF16), wglu_ref[...]) + bglu_ref[...])
    mix_ref[...] = (ya * _silu(za_ref[...])).astype(BF16)

    @pl.when(tc == ntc - 1)
    def _():
        if split:
            hro_ref[...] = cr_scr[...]
            hio_ref[...] = ci_scr[...]
        else:
            for s in range(n_sets):
                for k in range(V7X_SUBLANES):
                    r = s * set_rows + V7X_SUBLANES * t_last + k
                    hro_ref[s * V7X_SUBLANES + k] = ur_scr[r:r + 1, :]
                    hio_ref[s * V7X_SUBLANES + k] = ui_scr[r:r + 1, :]


def s5_mixer(proj, h0_re, h0_im, coef, wur, wui, wyr, wyi, dskip, wglu, bglu, layer, state_layer,
             *, batch, seq, t_valid, nb, tt):
    ntc = seq // tt
    assert nb == 1 or ntc == 1
    rows = nb * tt
    grid = (batch // nb, ntc)
    t_last = (t_valid - 1) - (ntc - 1) * tt
    assert 0 <= t_last < tt
    if nb == 1:
        assert tt == V7X_SUBLANES * S5_SEG and t_last == tt - 1
        lc = S5_SEG
    else:
        assert nb % V7X_SUBLANES == 0
        lc = tt
    idx = np.arange(rows)
    set_rows = V7X_SUBLANES * lc
    src = (idx // set_rows) * set_rows + (idx % V7X_SUBLANES) * lc + (idx % set_rows) // V7X_SUBLANES
    perm_np = np.zeros((rows, rows), np.float32)
    perm_np[idx, src] = 1.0
    perm = jnp.asarray(perm_np, BF16)
    perm_t = jnp.asarray(perm_np.T, BF16)
    sq = pl.BlockSpec((rows, rows), lambda b, t: (0, 0))

    def tok(cb):
        return pl.BlockSpec((rows, W_MIX), lambda b, t: (b * ntc + t, cb))

    st_in = pl.BlockSpec((None, nb, 1, S5_CH), lambda b, t: (state_layer, b, 0, 0))
    st_out = pl.BlockSpec((nb, 1, S5_CH), lambda b, t: (b, 0, 0))

    def lw(shape):
        nd = len(shape)
        return pl.BlockSpec((None,) + shape, lambda b, t: (layer,) + (0,) * nd)

    return pl.pallas_call(
        functools.partial(_s5_kernel, nb=nb, tt=tt, t_last=t_last),
        grid=grid,
        in_specs=[tok(COL_XA // W_MIX), tok(COL_ZA // W_MIX), st_in, st_in,
                  lw((4, V7X_SUBLANES, S5_CH)), sq, sq, lw((4, 256, 1024)), lw((4, 256, 1024)),
                  lw((8, 512, 128)), lw((8, 512, 128)), lw((1, W_MIX)), lw((W_MIX, W_MIX)), lw((1, W_MIX))],
        out_specs=[pl.BlockSpec((rows, W_MIX), lambda b, t: (b * ntc + t, 0)), st_out, st_out],
        out_shape=[jax.ShapeDtypeStruct((batch * seq, W_MIX), BF16),
                   jax.ShapeDtypeStruct((batch, 1, S5_CH), F32),
                   jax.ShapeDtypeStruct((batch, 1, S5_CH), F32)],
        scratch_shapes=[pltpu.VMEM((rows, S5_CH), F32), pltpu.VMEM((rows, S5_CH), F32),
                        pltpu.VMEM((nb, 1, S5_CH), F32), pltpu.VMEM((nb, 1, S5_CH), F32),
                        pltpu.VMEM((rows, W_MIX), F32)],
        compiler_params=_cparams(2),
        name="s5_mixer",
    )(proj, proj, h0_re, h0_im, coef, perm, perm_t, wur, wui, wyr, wyi, dskip, wglu, bglu)


LRU_STRIP = 512
CONV_PAD = V7X_SUBLANES


def _causal_conv(tail_scr, x, cw_ref, bi):
    rows_n = x.shape[0]
    width = x.shape[1]
    n_tiles = rows_n // V7X_SUBLANES
    xe = jnp.concatenate([tail_scr[bi], x], axis=0).reshape(n_tiles + 1, V7X_SUBLANES, width)
    row = lax.broadcasted_iota(jnp.int32, (n_tiles, V7X_SUBLANES, width), 1)
    acc = cw_ref[CONV_W - 1:CONV_W, :] * x
    for jj in range(CONV_W - 1):
        shift = CONV_W - 1 - jj
        rot = pltpu.roll(xe, shift, 1)
        shifted = jnp.where(row < shift, rot[:-1], rot[1:]).reshape(rows_n, width)
        acc = acc + cw_ref[jj:jj + 1, :] * shifted
    tail_scr[bi] = xe[n_tiles]
    return acc


def _init_conv_tail(tail_scr, cv0_ref, nb):
    tail_scr[...] = jnp.zeros(tail_scr.shape, F32)
    for bi in range(nb):
        tail_scr[bi, CONV_PAD - (CONV_W - 1):CONV_PAD, :] = cv0_ref[bi]


def _lru_kernel(xb_ref, zb_ref, h0_ref, cv0_ref, cw_ref, cb_ref, wri_ref, br_ref, bi_ref, lam_ref,
                mix_ref, ho_ref,
                tail_scr, xc_scr, a_scr, b_scr, c_scr, *, nb, tt, t_last):
    tc = pl.program_id(1)
    ntc = pl.num_programs(1)

    @pl.when(tc == 0)
    def _():
        c_scr[...] = h0_ref[...]
        _init_conv_tail(tail_scr, cv0_ref, nb)

    def conv_body(bi, carry):
        rows = pl.ds(pl.multiple_of(bi * tt, V7X_SUBLANES), tt)
        acc = _causal_conv(tail_scr, xb_ref[rows, :], cw_ref, bi)
        xc_scr[rows, :] = acc + cb_ref[...]
        return carry

    lax.fori_loop(0, nb, conv_body, 0)

    xc = xc_scr[...]
    xc16 = xc.astype(BF16)
    sp = _softplus(-lam_ref[...])
    for blk in range(LRU_BLOCKS):
        cs = slice(blk * LRU_BLK, (blk + 1) * LRU_BLK)
        pre = _dot(xc16[:, cs], wri_ref[blk])
        r = _sigmoid(pre[:, :LRU_BLK] + br_ref[:, cs])
        ig = _sigmoid(pre[:, LRU_BLK:] + bi_ref[:, cs])
        log_a = (-LRU_C) * r * sp[:, cs]
        a_scr[:, cs] = jnp.exp(log_a)
        b_scr[:, cs] = jnp.sqrt(_neg_expm1_nonpos(2.0 * log_a)) * (ig * xc[:, cs])

    ng = tt // V7X_SUBLANES
    for s in range(W_MIX // LRU_STRIP):
        ls = slice(s * LRU_STRIP, (s + 1) * LRU_STRIP)
        row = lax.broadcasted_iota(jnp.int32, (V7X_SUBLANES, LRU_STRIP), 0)

        def seq_body(bi, carry, ls=ls, row=row):
            def tile_body(g, c):
                rows = pl.ds(pl.multiple_of(bi * tt + g * V7X_SUBLANES, V7X_SUBLANES), V7X_SUBLANES)
                a = a_scr[rows, ls]
                b = b_scr[rows, ls]
                for k in (1, 2, 4):
                    a_sh = jnp.where(row >= k, pltpu.roll(a, k, 0), 1.0)
                    b_sh = jnp.where(row >= k, pltpu.roll(b, k, 0), 0.0)
                    b = b + a * b_sh
                    a = a * a_sh
                h = b + a * c
                b_scr[rows, ls] = h
                return h[V7X_SUBLANES - 1:, :]

            c = lax.fori_loop(0, ng, tile_body, c_scr[bi, :, ls])
            c_scr[bi, :, ls] = c
            return carry

        lax.fori_loop(0, nb, seq_body, 0)

    mix_ref[...] = (b_scr[...] * _silu(zb_ref[...])).astype(BF16)

    @pl.when(tc == ntc - 1)
    def _():
        for bi in range(nb):
            r = bi * tt + t_last
            ho_ref[bi] = b_scr[r:r + 1, :]


def lru_mixer(proj, h0, conv0, cw, cb, wri, br, bi_, lam, layer, state_layer, *, batch, seq, t_valid, nb, tt):
    ntc = seq // tt
    assert nb == 1 or ntc == 1
    rows = nb * tt
    grid = (batch // nb, ntc)
    t_last = (t_valid - 1) - (ntc - 1) * tt

    def tok(cb_):
        return pl.BlockSpec((rows, W_MIX), lambda b, t: (b * ntc + t, cb_))

    def lw(shape):
        nd = len(shape)
        return pl.BlockSpec((None,) + shape, lambda b, t: (layer,) + (0,) * nd)

    return pl.pallas_call(
        functools.partial(_lru_kernel, nb=nb, tt=tt, t_last=t_last),
        grid=grid,
        in_specs=[tok(COL_XB // W_MIX), tok(COL_ZB // W_MIX),
                  pl.BlockSpec((None, nb, 1, W_MIX), lambda b, t: (state_layer, b, 0, 0)),
                  pl.BlockSpec((None, nb, CONV_W - 1, W_MIX), lambda b, t: (state_layer, b, 0, 0)),
                  lw((CONV_W, W_MIX)), lw((1, W_MIX)), lw((LRU_BLOCKS, LRU_BLK, 2 * LRU_BLK)),
                  lw((1, W_MIX)), lw((1, W_MIX)), lw((1, W_MIX))],
        out_specs=[pl.BlockSpec((rows, W_MIX), lambda b, t: (b * ntc + t, 0)),
                   pl.BlockSpec((nb, 1, W_MIX), lambda b, t: (b, 0, 0))],
        out_shape=[jax.ShapeDtypeStruct((batch * seq, W_MIX), BF16),
                   jax.ShapeDtypeStruct((batch, 1, W_MIX), F32)],
        scratch_shapes=[pltpu.VMEM((nb, CONV_PAD, W_MIX), F32),
                        pltpu.VMEM((rows, W_MIX), F32), pltpu.VMEM((rows, W_MIX), F32),
                        pltpu.VMEM((rows, W_MIX), F32), pltpu.VMEM((nb, 1, W_MIX), F32)],
        compiler_params=_cparams(2),
        name="lru_mixer",
    )(proj, proj, h0, conv0, cw, cb, wri, br, bi_, lam)


def _for_each_group(nb, g, group_body):
    assert nb % g == 0
    if nb == g:
        group_body(list(range(nb)))
    else:
        def body(i, carry):
            group_body([i * g + j for j in range(g)])
            return carry
        lax.fori_loop(0, nb // g, body, 0)


def _gla_kernel(q_ref, k_ref, v_ref, z_ref, tail_ref, s0_ref, wg_ref, bg_ref, go_ref, stack_ref,
                mix_ref, so_ref,
                s_scr, mix_scr, *, nb, group, c, cv, single_chunk):
    del stack_ref
    tc = pl.program_id(1)
    ntc = pl.num_programs(1)
    sb = min(GLA_SUB, c)
    nblk = c // sb

    s_src, s_dst = (s0_ref, so_ref) if single_chunk else (s_scr, s_scr)
    if not single_chunk:
        @pl.when(tc == 0)
        def _():
            s_scr[...] = s0_ref[...]

    row_c = lax.broadcasted_iota(jnp.int32, (c, 1), 0)
    lane_c = lax.broadcasted_iota(jnp.int32, (sb, c), 1)
    row_sb = lax.broadcasted_iota(jnp.int32, (sb, 1), 0)

    def group_body(bis):
        pairs = [(g, h) for g in range(len(bis)) for h in range(GLA_H)]
        heads = range(len(pairs))
        b_all, q_all, k_all, v_all = [], [], [], []
        for bi in bis:
            x = _dot(tail_ref[bi].astype(BF16), wg_ref[...]) + bg_ref[...]
            b_all.append(_cumsum_rows(_log_sigmoid(x) * (1.0 / GLA_TAU)))
            q_all.append(q_ref[bi])
            k_all.append(k_ref[bi])
            v_all.append(v_ref[bi])
        s_old = [s_src[bis[g], h] for g, h in pairs]
        q = [q_all[g][:, h * GLA_DK:(h + 1) * GLA_DK] * (GLA_DK ** -0.5) for g, h in pairs]
        k = [k_all[g][:, h * GLA_DK:(h + 1) * GLA_DK] for g, h in pairs]
        v = [v_all[g][:, h * GLA_DV:(h + 1) * GLA_DV] for g, h in pairs]
        b = [b_all[g][:, h * GLA_DK:(h + 1) * GLA_DK] for g, h in pairs]
        o_state = [_dot((q[h] * jnp.exp(b[h])).astype(BF16), s_old[h].astype(BF16)) for h in heads]
        s_new = []
        for h in heads:
            b_last = b[h][cv - 1:cv]
            dec = b_last - b[h]
            if cv < c:
                dec = jnp.where(row_c < cv, dec, NEG_INF)
            kd = k[h] * jnp.exp(dec)
            d_col = jnp.transpose(jnp.broadcast_to(jnp.exp(b_last), (V7X_SUBLANES, GLA_DK)))[:, 0:1]
            s_new.append(d_col * s_old[h] + _mm(kd, v[h], dot=_dot_tn, contract=0))
        att_off = []
        for h in heads:
            per_blk = [jnp.zeros((sb, c), F32)]
            for blk in range(1, nblk):
                r0 = blk * sb
                b_ref_row = b[h][r0 - 1:r0]
                qs = (q[h][r0:r0 + sb] * jnp.exp(b[h][r0:r0 + sb] - b_ref_row)).astype(BF16)
                kd = (k[h] * jnp.exp(jnp.where(row_c < r0, b_ref_row - b[h], NEG_INF))).astype(BF16)
                per_blk.append(_dot_nt(qs, kd))
            att_off.append(per_blk)
        att = []
        for h in heads:
            att_rows = []
            for blk in range(nblk):
                r0 = blk * sb
                q_i = q[h][r0:r0 + sb]
                b_i = b[h][r0:r0 + sb]
                a = att_off[h][blk]
                for sl in range(sb):
                    s_abs = r0 + sl
                    e = jnp.exp(jnp.where(row_sb >= sl, b_i - b[h][s_abs:s_abs + 1], NEG_INF))
                    col = jnp.sum(q_i * (k[h][s_abs:s_abs + 1] * e), axis=-1, keepdims=True)
                    a = jnp.where(lane_c == s_abs, col, a)
                att_rows.append(a)
            att.append(att_rows[0] if nblk == 1 else jnp.concatenate(att_rows, axis=0))
        o = [_mm(att[h], v[h]) + o_state[h] for h in heads]
        for p, (g, h) in enumerate(pairs):
            vs = slice(h * GLA_DV, (h + 1) * GLA_DV)
            ms = jnp.mean(o[p] * o[p], axis=-1, keepdims=True)
            on = (o[p] * lax.rsqrt(ms + EPS)) * go_ref[...]
            mix_scr[bis[g], :, vs] = on * _silu(z_ref[bis[g], :, vs])
            s_dst[bis[g], h] = s_new[p]

    _for_each_group(nb, group, group_body)
    mix_ref[...] = mix_scr[...].astype(BF16)

    if not single_chunk:
        @pl.when(tc == ntc - 1)
        def _():
            so_ref[...] = s_scr[...]


def _stack_alias(stack, n_inputs_before):
    spec = pl.BlockSpec(memory_space=pl.ANY)
    if stack is None:
        return jnp.zeros((V7X_SUBLANES, V7X_LANES), F32), spec, {}
    return stack, spec, {n_inputs_before: 1}


def gla_mixer(proj, tail, s0, wg, bg, go, layer, state_layer, *, batch, seq, c, cv, nb, group, n_layers, stack=None):
    ntc = seq // c
    assert batch % nb == 0 and seq % c == 0
    grid = (batch // nb, ntc)
    proj3 = proj.reshape(batch, seq, proj.shape[-1])
    tail3 = tail.reshape(batch, seq, TAIL_W)

    def tok(width, col):
        return pl.BlockSpec((nb, c, width), lambda b, t: (b, t, col // width))

    def lw(shape):
        nd = len(shape)
        return pl.BlockSpec((None,) + shape, lambda b, t: (layer,) + (0,) * nd)

    st_shape = (nb, GLA_H, GLA_DK, GLA_DV)
    stack_arg, stack_spec, aliases = _stack_alias(stack, 9)
    mix, state = pl.pallas_call(
        functools.partial(_gla_kernel, nb=nb, group=group, c=c, cv=cv, single_chunk=ntc == 1),
        grid=grid,
        in_specs=[tok(GLA_H * GLA_DK, COL_QC), tok(GLA_H * GLA_DK, COL_KC), tok(W_MIX, COL_VC), tok(W_MIX, COL_ZC),
                  tok(TAIL_W, 0),
                  pl.BlockSpec((None,) + st_shape, lambda b, t: (state_layer, b, 0, 0, 0)),
                  lw((TAIL_W, GLA_H * GLA_DK)), lw((1, GLA_H * GLA_DK)), lw((1, GLA_DV)), stack_spec],
        out_specs=[tok(W_MIX, 0),
                   pl.BlockSpec((None,) + st_shape, lambda b, t: (layer, b, 0, 0, 0))],
        out_shape=[jax.ShapeDtypeStruct((batch, seq, W_MIX), BF16),
                   jax.ShapeDtypeStruct((n_layers, batch, GLA_H, GLA_DK, GLA_DV), F32)],
        scratch_shapes=[pltpu.VMEM(st_shape, F32), pltpu.VMEM((nb, c, W_MIX), F32)],
        input_output_aliases=aliases,
        compiler_params=_cparams(2),
        name="gla_mixer",
    )(proj3, proj3, proj3, proj3, tail3, s0, wg, bg, go, stack_arg)
    return mix.reshape(batch * seq, W_MIX), state


def _l2norm(x):
    return x * lax.rsqrt(jnp.sum(x * x, axis=-1, keepdims=True) + EPS)


def _unit_lower_solve(ms, rhss, c, cv):
    if c <= V7X_SUBLANES:
        row = lax.broadcasted_iota(jnp.int32, (c, 1), 0)
        us = []
        for m, rhs in zip(ms, rhss):
            u = jnp.zeros_like(rhs)
            u_rows = []
            for t in range(cv):
                ut = rhs[t:t + 1]
                for s in range(t):
                    ut = ut - m[t:t + 1, s:s + 1] * u_rows[s]
                u_rows.append(ut)
                u = jnp.where(row == t, ut, u)
            us.append(u)
        return us
    xs = [-m for m in ms]
    ys = list(rhss)
    dv = rhss[0].shape[1]
    levels = int(math.log2(c))
    assert 3 * c <= V7X_MXU_DEPTH

    def hi_lo(t):
        hi = t.astype(BF16)
        return hi, (t - hi.astype(F32)).astype(BF16)

    for lvl in range(levels):
        last = lvl == levels - 1
        ps = []
        for x, y in zip(xs, ys):
            xh, xl = hi_lo(x)
            yh, yl = hi_lo(y)
            bh = yh if last else jnp.concatenate([yh, xh], axis=1)
            bl = yl if last else jnp.concatenate([yl, xl], axis=1)
            lhs = jnp.concatenate([xh.astype(F32), xh.astype(F32), xl.astype(F32)], axis=1).astype(BF16)
            ps.append(_dot(lhs, jnp.concatenate([bh, bl, bh], axis=0)))
        ys = [y + p[:, :dv] for y, p in zip(ys, ps)]
        if not last:
            xs = [p[:, dv:] for p in ps]
    return ys


def _gdn_kernel(qkv_ref, z_ref, tail_ref, s0_ref, cv0_ref, cw_ref, alog_ref, dtb_ref, go_ref, stack_ref,
                mix_ref, so_ref,
                s_scr, tail_scr, mix_scr, *, nb, group, c, cv, single_chunk):
    del stack_ref
    tc = pl.program_id(1)
    ntc = pl.num_programs(1)

    s_src, s_dst = (s0_ref, so_ref) if single_chunk else (s_scr, s_scr)

    @pl.when(tc == 0)
    def _():
        if not single_chunk:
            s_scr[...] = s0_ref[...]
        _init_conv_tail(tail_scr, cv0_ref, nb)

    row_c = lax.broadcasted_iota(jnp.int32, (c, 1), 0)
    ri = lax.broadcasted_iota(jnp.int32, (c, c), 0)
    ci = lax.broadcasted_iota(jnp.int32, (c, c), 1)
    lane_t = lax.broadcasted_iota(jnp.int32, (c, TAIL_W), 1)
    a_lanes = (lane_t >= TAIL_A) & (lane_t < TAIL_A + DN_H)

    def group_body(bis):
        pairs = [(g, h) for g in range(len(bis)) for h in range(DN_H)]
        heads = range(len(pairs))
        qkv, gam, gam_t, beta = [], [], [], []
        for gi, bi in enumerate(bis):
            qkv.append(_silu(_causal_conv(tail_scr, qkv_ref[bi], cw_ref, bi)))
            tl = tail_ref[bi]
            gg = _cumsum_rows(jnp.where(a_lanes, -jnp.exp(alog_ref[...]) * _softplus(tl + dtb_ref[...]), 0.0))
            gam.append(gg)
            gam_t.append(jnp.transpose(gg))
            beta.append(_sigmoid(tl))
        s_old = [s_src[bis[g], h] for g, h in pairs]
        s16 = [s.astype(BF16) for s in s_old]
        q16, k16, kf, v = [], [], [], []
        for g, h in pairs:
            q = _l2norm(qkv[g][:, h * DN_DK:(h + 1) * DN_DK]) * (DN_DK ** -0.5)
            k = _l2norm(qkv[g][:, DN_H * DN_DK + h * DN_DK:DN_H * DN_DK + (h + 1) * DN_DK])
            q16.append(q.astype(BF16))
            k16.append(k.astype(BF16))
            kf.append(k)
            v.append(qkv[g][:, 2 * DN_H * DN_DK + h * DN_DV:2 * DN_H * DN_DK + (h + 1) * DN_DV])
        g_col = [gam[g][:, TAIL_A + h:TAIL_A + h + 1] for g, h in pairs]
        g_row = [gam_t[g][TAIL_A + h:TAIL_A + h + 1, :] for g, h in pairs]
        b_col = [beta[g][:, TAIL_B + h:TAIL_B + h + 1] for g, h in pairs]
        decay = [jnp.exp(jnp.where(ri >= ci, g_col[h] - g_row[h], NEG_INF)) for h in heads]
        eg = [jnp.exp(g_col[h]) for h in heads]
        kk = [_dot_nt(k16[h], k16[h]) for h in heads]
        k_s = [_dot(k16[h], s16[h]) for h in heads]
        q_s = [_dot(q16[h], s16[h]) for h in heads]
        qk = [_dot_nt(q16[h], k16[h]) for h in heads]
        m = [jnp.where(ri > ci, b_col[h] * kk[h] * decay[h], 0.0) for h in heads]
        rhs = [b_col[h] * (v[h] - eg[h] * k_s[h]) for h in heads]
        u = _unit_lower_solve(m, rhs, c, cv)
        o = [eg[h] * q_s[h] + _mm(qk[h] * decay[h], u[h]) for h in heads]
        kd = []
        for h in heads:
            dec = g_col[h][cv - 1:cv] - g_col[h]
            if cv < c:
                dec = jnp.where(row_c < cv, dec, NEG_INF)
            kd.append(kf[h] * jnp.exp(dec))
        s_new = [jnp.exp(g_col[h][cv - 1:cv]) * s_old[h] + _mm(kd[h], u[h], dot=_dot_tn, contract=0) for h in heads]
        for p, (g, h) in enumerate(pairs):
            vs = slice(h * DN_DV, (h + 1) * DN_DV)
            ms = jnp.mean(o[p] * o[p], axis=-1, keepdims=True)
            on = (o[p] * lax.rsqrt(ms + EPS)) * go_ref[...]
            mix_scr[bis[g], :, vs] = on * _silu(z_ref[bis[g], :, vs])
            s_dst[bis[g], h] = s_new[p]

    _for_each_group(nb, group, group_body)
    mix_ref[...] = mix_scr[...].astype(BF16)

    if not single_chunk:
        @pl.when(tc == ntc - 1)
        def _():
            so_ref[...] = s_scr[...]


def gdn_mixer(proj, tail, s0, conv0, cw, alog, dtb, go, layer, state_layer, *, batch, seq, c, cv, nb, group, n_layers,
              stack=None):
    ntc = seq // c
    assert batch % nb == 0 and seq % c == 0
    grid = (batch // nb, ntc)
    proj3 = proj.reshape(batch, seq, proj.shape[-1])
    tail3 = tail.reshape(batch, seq, TAIL_W)

    def tok(width, col):
        return pl.BlockSpec((nb, c, width), lambda b, t: (b, t, col // width))

    def lw(shape):
        nd = len(shape)
        return pl.BlockSpec((None,) + shape, lambda b, t: (layer,) + (0,) * nd)

    st_shape = (nb, DN_H, DN_DK, DN_DV)
    assert COL_QKVD % DN_QKV == 0
    stack_arg, stack_spec, aliases = _stack_alias(stack, 9)
    mix, state = pl.pallas_call(
        functools.partial(_gdn_kernel, nb=nb, group=group, c=c, cv=cv, single_chunk=ntc == 1),
        grid=grid,
        in_specs=[tok(DN_QKV, COL_QKVD),
                  tok(W_MIX, COL_ZD),
                  tok(TAIL_W, 0),
                  pl.BlockSpec((None,) + st_shape, lambda b, t: (state_layer, b, 0, 0, 0)),
                  pl.BlockSpec((None, nb, CONV_W - 1, DN_QKV), lambda b, t: (state_layer, b, 0, 0)),
                  lw((CONV_W, DN_QKV)), lw((1, TAIL_W)), lw((1, TAIL_W)), lw((1, DN_DV)), stack_spec],
        out_specs=[tok(W_MIX, 0),
                   pl.BlockSpec((None,) + st_shape, lambda b, t: (layer, b, 0, 0, 0))],
        out_shape=[jax.ShapeDtypeStruct((batch, seq, W_MIX), BF16),
                   jax.ShapeDtypeStruct((n_layers, batch, DN_H, DN_DK, DN_DV), F32)],
        input_output_aliases=aliases,
        scratch_shapes=[pltpu.VMEM(st_shape, F32),
                        pltpu.VMEM((nb, CONV_PAD, DN_QKV), F32),
                        pltpu.VMEM((nb, c, W_MIX), F32)],
        compiler_params=_cparams(2),
        name="gdn_mixer",
    )(proj3, proj3, tail3, s0, conv0, cw, alog, dtb, go, stack_arg)
    return mix.reshape(batch * seq, W_MIX), state


def _prepare_weights(w_in, s5_lam_re, s5_lam_im, s5_log_dt, s5_b_re, s5_b_im, s5_c_re, s5_c_im, s5_d,
                     s5_w_glu, lru_w_r, lru_w_i, gla_w_gate, dn_a_log, dn_dt_bias, w_out, ple_w, ple_gate_w):
    nl = w_in.shape[0]
    p = {}
    p["w_main"], p["w_tail"] = repack_w_in(jnp.swapaxes(w_in, 1, 2))
    p["w_out"] = w_out.astype(BF16)
    p["ple_w"] = ple_w.astype(BF16)
    p["ple_gate_w"] = ple_gate_w.astype(BF16)
    p["w_glu"] = s5_w_glu.astype(BF16)
    ldt = jnp.repeat(s5_log_dt, S5_STATE, axis=1).reshape(nl, 1, S5_CH)
    coef, cre, cim = s5_prep(s5_lam_re.reshape(nl, 1, S5_CH), s5_lam_im.reshape(nl, 1, S5_CH), ldt)
    bb_re, bb_im = s5_bb(cre.reshape(nl, S5_CH, 1), cim.reshape(nl, S5_CH, 1),
                         s5_b_re.reshape(nl, S5_CH, S5_GROUP), s5_b_im.reshape(nl, S5_CH, S5_GROUP))
    eye16 = jnp.eye(16, dtype=F32)
    eye8 = jnp.eye(8, dtype=F32)

    def pack_u(bb):
        t = bb.reshape(nl, 4, 16, S5_STATE, S5_GROUP)
        return jnp.einsum("lkgph,gG->lkghGp", t, eye16).reshape(nl, 4, 256, 1024).astype(BF16)

    def pack_y(cc):
        t = cc.reshape(nl, 8, 8, S5_GROUP, S5_STATE)
        return jnp.einsum("ljghp,gG->ljgpGh", t, eye8).reshape(nl, 8, 512, 128).astype(BF16)

    p["s5_coef"] = coef
    p["s5_wur"], p["s5_wui"] = pack_u(bb_re), pack_u(bb_im)
    p["s5_wyr"], p["s5_wyi"] = pack_y(s5_c_re), pack_y(s5_c_im)
    p["s5_d"] = s5_d.reshape(nl, 1, W_MIX)
    p["lru_wri"] = jnp.concatenate([lru_w_r, lru_w_i], axis=-1).astype(BF16)
    p["gla_wg"] = jnp.concatenate([gla_w_gate, jnp.zeros((nl, TAIL_W - GLA_RANK, GLA_H * GLA_DK), F32)],
                                  axis=1).astype(BF16)

    def tail_row(x):
        return jnp.pad(x, ((0, 0), (TAIL_A, TAIL_W - TAIL_A - DN_H))).reshape(nl, 1, TAIL_W)

    p["dn_alog"] = tail_row(dn_a_log)
    p["dn_dtb"] = tail_row(dn_dt_bias)
    return p


def kernel(x_prompt, x_sample, state_s5_re, state_s5_im, state_lru_h, state_lru_conv, state_gla, state_delta, state_delta_conv, p_prompt, p_sample, g_norm, w_in, s5_lam_re, s5_lam_im, s5_log_dt, s5_b_re, s5_b_im, s5_c_re, s5_c_im, s5_d, s5_w_glu, s5_b_glu, lru_conv_w, lru_conv_b, lru_w_r, lru_b_r, lru_w_i, lru_b_i, lru_lam, gla_w_gate, gla_b_gate, gla_g_out, dn_conv_w, dn_a_log, dn_dt_bias, dn_g_out, w_out, ple_w, ple_gate_w, ple_gate_b, g_final):
    nl = w_in.shape[0]
    bp, tp, _ = x_prompt.shape
    bs, ts, _ = x_sample.shape
    tsp = SAMPLE_T_PAD
    mp, ms = bp * tp, bs * ts

    p = _prepare_weights(w_in, s5_lam_re, s5_lam_im, s5_log_dt, s5_b_re, s5_b_im, s5_c_re, s5_c_im, s5_d,
                         s5_w_glu, lru_w_r, lru_w_i, gla_w_gate, dn_a_log, dn_dt_bias, w_out, ple_w, ple_gate_w)

    def vec(x):
        return x.reshape(nl, 1, x.shape[-1])

    g_norm3, b_glu3, cb3 = vec(g_norm), vec(s5_b_glu), vec(lru_conv_b)
    br3, bi3, lam3 = vec(lru_b_r), vec(lru_b_i), vec(lru_lam)
    bg3, go_c3, go_d3, pgb3 = vec(gla_b_gate), vec(gla_g_out), vec(dn_g_out), vec(ple_gate_b)
    pe_p = p_prompt.reshape(nl, mp, D_PLE).astype(BF16)
    pe_s = p_sample.reshape(nl, ms, D_PLE).astype(BF16)

    z_s5 = jnp.zeros((1, bp, 1, S5_CH), F32)
    z_lru = jnp.zeros((1, bp, 1, W_MIX), F32)
    z_lconv = jnp.zeros((1, bp, CONV_W - 1, W_MIX), F32)
    z_gla = jnp.zeros((1, bp, GLA_H, GLA_DK, GLA_DV), F32)
    z_dn = jnp.zeros((1, bp, DN_H, DN_DK, DN_DV), F32)
    z_dconv = jnp.zeros((1, bp, CONV_W - 1, DN_QKV), F32)
    c_s5r = state_s5_re.reshape(nl, bs, 1, S5_CH)
    c_s5i = state_s5_im.reshape(nl, bs, 1, S5_CH)
    c_lru = state_lru_h.reshape(nl, bs, 1, W_MIX)

    def mixers(proj, tail, layer, *, batch, seq, t_valid, st, sl, nb_scan, tt, c, gla_grp, nb_gdn, gdn_grp, stacks):
        s5r0, s5i0, lru0, lconv0, gla0, dn0, dconv0 = st
        gla_stack, dn_stack = stacks
        mix_a, s5r, s5i = s5_mixer(proj, s5r0, s5i0, p["s5_coef"], p["s5_wur"], p["s5_wui"], p["s5_wyr"],
                                   p["s5_wyi"], p["s5_d"], p["w_glu"], b_glu3, layer, sl,
                                   batch=batch, seq=seq, t_valid=t_valid, nb=nb_scan, tt=tt)
        mix_b, lruh = lru_mixer(proj, lru0, lconv0, lru_conv_w, cb3, p["lru_wri"], br3, bi3, lam3, layer, sl,
                                batch=batch, seq=seq, t_valid=t_valid, nb=nb_scan, tt=tt)
        mix_c, glas = gla_mixer(proj, tail, gla0, p["gla_wg"], bg3, go_c3, layer, sl,
                                batch=batch, seq=seq, c=c, cv=min(c, t_valid), nb=gla_grp, group=gla_grp, n_layers=nl,
                                stack=gla_stack)
        mix_d, dns = gdn_mixer(proj, tail, dn0, dconv0, dn_conv_w, p["dn_alog"], p["dn_dtb"], go_d3, layer, sl,
                               batch=batch, seq=seq, c=c, cv=min(c, t_valid), nb=nb_gdn, group=gdn_grp, n_layers=nl,
                               stack=dn_stack)
        return (mix_a, mix_b, mix_c, mix_d), (s5r, s5i, lruh, glas, dns)

    hp = x_prompt.reshape(mp, D_MODEL)
    hs = x_sample.reshape(ms, D_MODEL)
    new_p, new_s = [], []
    stacks_p = stacks_s = (None, None)
    hgp, ssqp = prenorm(hp, g_norm3, 0, tm=DENSE_TILES_SAMPLE["tm"])
    hgs, ssqs = prenorm(hs, g_norm3, 0, tm=DENSE_TILES_SAMPLE["tm"])
    for i in range(nl):
        nxt = min(i + 1, nl - 1)
        proj, tail = in_proj(hgp, ssqp, p["w_main"], p["w_tail"], i, **IN_PROJ_TILES_PROMPT)
        mixes, (s5r, s5i, lruh, glas, dns) = mixers(
            proj, tail, i, batch=bp, seq=tp, t_valid=tp,
            st=(z_s5, z_s5, z_lru, z_lconv, z_gla, z_dn, z_dconv), sl=0,
            nb_scan=1, tt=256, c=GLA_CHUNK, gla_grp=4, nb_gdn=4, gdn_grp=2, stacks=stacks_p)
        stacks_p = (glas, dns)
        proj3 = proj.reshape(bp, tp, N_MAIN)
        new_p.append((s5r.reshape(bp, S5_GROUPS, S5_STATE), s5i.reshape(bp, S5_GROUPS, S5_STATE),
                      lruh.reshape(bp, W_MIX), proj3[:, tp - (CONV_W - 1):, COL_XB:COL_XB + W_MIX],
                      None, None, proj3[:, tp - (CONV_W - 1):, COL_QKVD:COL_QKVD + DN_QKV]))
        hp, hpb = out_proj(mixes, p["w_out"], hp, i, **OUT_PROJ_TILES_PROMPT)
        hp, hgp, ssqp = ple(hp, hpb, pe_p, p["ple_gate_w"], pgb3, p["ple_w"], g_norm3, i, nxt, **DENSE_TILES_PROMPT)

        proj_pad, tail_pad = in_proj(hgs, ssqs, p["w_main"], p["w_tail"], i, seq_rows=ts, seq_pad=tsp,
                                     **DENSE_TILES_SAMPLE)
        proj3 = proj_pad.reshape(bs, tsp, N_MAIN)[:, :ts]
        mixes, (s5r, s5i, lruh, glas, dns) = mixers(
            proj_pad, tail_pad, i, batch=bs, seq=tsp, t_valid=ts,
            st=(c_s5r, c_s5i, c_lru, state_lru_conv, state_gla, state_delta, state_delta_conv), sl=i,
            nb_scan=32, tt=tsp, c=tsp, gla_grp=8, nb_gdn=8, gdn_grp=4, stacks=stacks_s)
        stacks_s = (glas, dns)
        mixes = tuple(m.reshape(bs, tsp, W_MIX)[:, :ts].reshape(ms, W_MIX) for m in mixes)
        new_s.append((s5r.reshape(bs, S5_GROUPS, S5_STATE), s5i.reshape(bs, S5_GROUPS, S5_STATE),
                      lruh.reshape(bs, W_MIX), proj3[:, ts - (CONV_W - 1):, COL_XB:COL_XB + W_MIX],
                      None, None, proj3[:, ts - (CONV_W - 1):, COL_QKVD:COL_QKVD + DN_QKV]))
        hs, hsb = out_proj(mixes, p["w_out"], hs, i, **DENSE_TILES_SAMPLE)
        hs, hgs, ssqs = ple(hs, hsb, pe_s, p["ple_gate_w"], pgb3, p["ple_w"], g_norm3, i, nxt, **DENSE_TILES_SAMPLE)

    g_fin = g_final.reshape(1, D_MODEL)
    y_prompt = final_norm(hp, g_fin, tm=256).reshape(bp, tp, D_MODEL)
    y_sample = final_norm(hs, g_fin, tm=256).reshape(bs, ts, D_MODEL)

    def stk(lst, j):
        return jnp.stack([s[j] for s in lst], axis=0)

    return (y_prompt, y_sample,
            stk(new_p, 0), stk(new_p, 1), stk(new_p, 2), stk(new_p, 3), stacks_p[0], stacks_p[1], stk(new_p, 6),
            stk(new_s, 0), stk(new_s, 1), stk(new_s, 2), stk(new_s, 3), stacks_s[0], stacks_s[1], stk(new_s, 6))
```

```python
import functools
import math

import jax
import jax.numpy as jnp
import numpy as np
from jax import lax
from jax.experimental import pallas as pl
from jax.experimental.pallas import tpu as pltpu

F32 = jnp.float32
BF16 = jnp.bfloat16
EPS = 1e-6
NEG_INF = float("-inf")

D_MODEL = 4096
DEPTH = 4
W_MIX = 1024
S5_GROUPS = 64
S5_GROUP = 16
S5_STATE = 64
S5_CH = S5_GROUPS * S5_STATE
LRU_BLOCKS = 8
LRU_BLK = 128
LRU_C = 8.0
CONV_W = 4
GLA_H = 4
GLA_DK = 128
GLA_DV = 256
GLA_RANK = 16
GLA_TAU = 16.0
GLA_CHUNK = 64
GLA_SUB = 8
DN_H = 8
DN_DK = 128
DN_DV = 128
DN_QKV = 3072
DN_CHUNK = 64
D_PLE = 256
N_MAIN = 11264
COL_XA, COL_ZA, COL_XB, COL_ZB = 0, 1024, 2048, 3072
COL_QC, COL_KC, COL_VC = 4096, 4608, 5120
COL_QKVD, COL_ZC, COL_ZD = 6144, 9216, 10240
ORIG_ZC, ORIG_GC, ORIG_QKVD, ORIG_ZD, ORIG_AB = 6144, 7168, 7184, 10256, 11280
TAIL_W = 128
TAIL_G, TAIL_A, TAIL_B = 0, 16, 24
SAMPLE_T_PAD = 8

V7X_LANES = 128
V7X_SUBLANES = 8
V7X_MXU_DEPTH = 256
VMEM_LIMIT = 52 * 1024 * 1024
DENSE_TILES_PROMPT = dict(tm=1024, tn=512)
IN_PROJ_TILES_PROMPT = dict(tm=1024, tn=1024)
OUT_PROJ_TILES_PROMPT = dict(tm=1024, tn=1024)
VMEM_HEADROOM = 6 * 1024 * 1024
DENSE_TILES_SAMPLE = dict(tm=512, tn=1024)


def _cparams(n_axes, vmem_limit=VMEM_LIMIT):
    return pltpu.CompilerParams(dimension_semantics=("arbitrary",) * n_axes,
                                vmem_limit_bytes=vmem_limit)


def _sigmoid(x):
    return jax.nn.sigmoid(x)


def _silu(x):
    return x * jax.nn.sigmoid(x)


def _softplus(x):
    return jnp.maximum(x, 0.0) + jnp.log1p(jnp.exp(-jnp.abs(x)))


def _neg_expm1_nonpos(x):
    t = jnp.tanh(0.5 * x)
    return (-2.0 * t) / (1.0 - t)


def _log_sigmoid(x):
    return jnp.minimum(x, 0.0) - jnp.log1p(jnp.exp(-jnp.abs(x)))


def _gelu_tanh(x):
    c = math.sqrt(2.0 / math.pi)
    return x * (0.5 * (1.0 + jnp.tanh(c * (x + 0.044715 * (x * x * x)))))


def _dot(a, b):
    return jnp.dot(a, b, preferred_element_type=F32)


def _dot_nt(a, b):
    return lax.dot_general(a, b, (((1,), (1,)), ((), ())), preferred_element_type=F32)


def _dot_tn(a, b):
    return lax.dot_general(a, b, (((0,), (0,)), ((), ())), preferred_element_type=F32)


def _split3(x):
    h = x.astype(BF16)
    r = x - h.astype(F32)
    m = r.astype(BF16)
    l = (r - m.astype(F32)).astype(BF16)
    return h, m, l


def _cumsum_rows(x):
    c = x.shape[0]
    if c <= V7X_SUBLANES:
        row = lax.broadcasted_iota(jnp.int32, x.shape, 0)
        k = 1
        while k < c:
            x = x + jnp.where(row >= k, pltpu.roll(x, k, 0), 0.0)
            k *= 2
        return x
    h, m, l = _split3(x)
    if 3 * c <= V7X_MXU_DEPTH:
        col = lax.broadcasted_iota(jnp.int32, (c, 3 * c), 1)
        col = jnp.where(col >= 2 * c, col - 2 * c, jnp.where(col >= c, col - c, col))
        tri3 = (lax.broadcasted_iota(jnp.int32, (c, 3 * c), 0) >= col).astype(BF16)
        return _dot(tri3, jnp.concatenate([h, m, l], axis=0))
    tri = (lax.broadcasted_iota(jnp.int32, (c, c), 0) >= lax.broadcasted_iota(jnp.int32, (c, c), 1)).astype(BF16)
    return _dot(tri, h) + _dot(tri, m) + _dot(tri, l)


def _mm(a, b, dot=_dot, contract=1):
    if a.shape[contract] < 2 * V7X_SUBLANES:
        return dot(a, b)
    return dot(a.astype(BF16), b.astype(BF16))


def _prenorm_kernel(x_ref, g_ref, hg_ref, ssq_ref):
    x = x_ref[...]
    hg_ref[...] = (x * g_ref[...]).astype(BF16)
    ssq_ref[...] = jnp.sum(x * x, axis=-1, keepdims=True)


def prenorm(x, g_norm, layer, *, tm):
    m = x.shape[0]
    return pl.pallas_call(
        _prenorm_kernel,
        grid=(m // tm,),
        in_specs=[pl.BlockSpec((tm, D_MODEL), lambda i: (i, 0)),
                  pl.BlockSpec((None, 1, D_MODEL), lambda i: (layer, 0, 0))],
        out_specs=[pl.BlockSpec((tm, D_MODEL), lambda i: (i, 0)), pl.BlockSpec((tm, 1), lambda i: (i, 0))],
        out_shape=[jax.ShapeDtypeStruct((m, D_MODEL), BF16), jax.ShapeDtypeStruct((m, 1), F32)],
        compiler_params=_cparams(1),
        name="prenorm",
    )(x, g_norm)


def _in_proj_kernel(hg_ref, ssq_ref, w_ref, wt_ref, o_ref, ot_ref, *, seq_rows, seq_pad):
    r = lax.rsqrt(ssq_ref[...] * (1.0 / D_MODEL) + EPS)

    def put(dst_ref, val):
        if seq_pad == seq_rows:
            dst_ref[...] = val
        else:
            n_seq = val.shape[0] // seq_rows
            dst_ref[:, 0:seq_rows, :] = val.reshape(n_seq, seq_rows, val.shape[1])
            dst_ref[:, seq_rows:seq_pad, :] = jnp.zeros((n_seq, seq_pad - seq_rows, val.shape[1]), F32)

    @pl.when(pl.program_id(1) == 0)
    def _():
        put(ot_ref, _dot_nt(hg_ref[...], wt_ref[...]) * r)

    put(o_ref, _dot_nt(hg_ref[...], w_ref[...]) * r)


def in_proj(hg, ssq, w_main, w_tail, layer, *, tm, tn, seq_rows=None, seq_pad=None):
    m = hg.shape[0]
    grid = (m // tm, N_MAIN // tn)
    if seq_rows is None:
        seq_rows = seq_pad = 1
        out_specs = [pl.BlockSpec((tm, tn), lambda i, j: (i, j)), pl.BlockSpec((tm, TAIL_W), lambda i, j: (i, 0))]
        out_shape = [jax.ShapeDtypeStruct((m, N_MAIN), F32), jax.ShapeDtypeStruct((m, TAIL_W), F32)]
    else:
        n_seq = tm // seq_rows
        out_specs = [pl.BlockSpec((n_seq, seq_pad, tn), lambda i, j: (i, 0, j)),
                     pl.BlockSpec((n_seq, seq_pad, TAIL_W), lambda i, j: (i, 0, 0))]
        out_shape = [jax.ShapeDtypeStruct((m // seq_rows, seq_pad, N_MAIN), F32),
                     jax.ShapeDtypeStruct((m // seq_rows, seq_pad, TAIL_W), F32)]
    proj, tail = pl.pallas_call(
        functools.partial(_in_proj_kernel, seq_rows=seq_rows, seq_pad=seq_pad),
        grid=grid,
        in_specs=[
            pl.BlockSpec((tm, D_MODEL), lambda i, j: (i, 0)),
            pl.BlockSpec((tm, 1), lambda i, j: (i, 0)),
            pl.BlockSpec((None, tn, D_MODEL), lambda i, j: (layer, j, 0)),
            pl.BlockSpec((None, TAIL_W, D_MODEL), lambda i, j: (layer, 0, 0)),
        ],
        out_specs=out_specs,
        out_shape=out_shape,
        compiler_params=_cparams(2),
        name="in_proj",
    )(hg, ssq, w_main, w_tail)
    return proj.reshape(-1, N_MAIN), tail.reshape(-1, TAIL_W)


def _out_proj_kernel(ma_ref, mb_ref, mc_ref, md_ref, w_ref, h_ref, o_ref, ob_ref):
    acc = h_ref[...]
    for k, m_ref in enumerate((ma_ref, mb_ref, mc_ref, md_ref)):
        acc = acc + _dot(m_ref[...], w_ref[k * W_MIX:(k + 1) * W_MIX, :])
    o_ref[...] = acc
    ob_ref[...] = acc.astype(BF16)


def out_proj(mixes, w_out, h, layer, *, tm, tn):
    m = h.shape[0]
    grid = (m // tm, D_MODEL // tn)
    mix_spec = pl.BlockSpec((tm, W_MIX), lambda i, j: (i, 0))
    tile = pl.BlockSpec((tm, tn), lambda i, j: (i, j))
    return pl.pallas_call(
        _out_proj_kernel,
        grid=grid,
        in_specs=[mix_spec, mix_spec, mix_spec, mix_spec,
                  pl.BlockSpec((None, 4 * W_MIX, tn), lambda i, j: (layer, 0, j)), tile],
        out_specs=[tile, tile],
        out_shape=[jax.ShapeDtypeStruct((m, D_MODEL), F32), jax.ShapeDtypeStruct((m, D_MODEL), BF16)],
        compiler_params=_cparams(2, max(VMEM_LIMIT, 2 * (tm * D_MODEL * 2 + 4 * W_MIX * tn * 2 + tm * tn * 10)
                                        + VMEM_HEADROOM)),
        name="out_proj",
    )(*mixes, w_out, h)


def _ple_kernel(hrow_ref, htile_ref, pe_ref, wg_ref, bg_ref, wp_ref, gn_ref, o_ref, hg_ref, ssq_ref):
    gate = _sigmoid(_dot(hrow_ref[...], wg_ref[...]) + bg_ref[...])
    pv = _dot(pe_ref[...], wp_ref[...])
    h = htile_ref[...] + gate * pv
    o_ref[...] = h
    hg_ref[...] = (h * gn_ref[...]).astype(BF16)
    part = jnp.sum(h * h, axis=-1, keepdims=True)

    @pl.when(pl.program_id(1) == 0)
    def _():
        ssq_ref[...] = part

    @pl.when(pl.program_id(1) != 0)
    def _():
        ssq_ref[...] = ssq_ref[...] + part


def ple(h, hb, pe, w_gate, b_gate, w_ple, g_next, layer, next_layer, *, tm, tn):
    m = h.shape[0]
    grid = (m // tm, D_MODEL // tn)
    tile = pl.BlockSpec((tm, tn), lambda i, j: (i, j))
    return pl.pallas_call(
        _ple_kernel,
        grid=grid,
        in_specs=[
            pl.BlockSpec((tm, D_MODEL), lambda i, j: (i, 0)),
            tile,
            pl.BlockSpec((None, tm, D_PLE), lambda i, j: (layer, i, 0)),
            pl.BlockSpec((None, D_MODEL, tn), lambda i, j: (layer, 0, j)),
            pl.BlockSpec((None, 1, tn), lambda i, j: (layer, 0, j)),
            pl.BlockSpec((None, D_PLE, tn), lambda i, j: (layer, 0, j)),
            pl.BlockSpec((None, 1, tn), lambda i, j: (next_layer, 0, j)),
        ],
        out_specs=[tile, tile, pl.BlockSpec((tm, 1), lambda i, j: (i, 0))],
        out_shape=[jax.ShapeDtypeStruct((m, D_MODEL), F32), jax.ShapeDtypeStruct((m, D_MODEL), BF16),
                   jax.ShapeDtypeStruct((m, 1), F32)],
        compiler_params=_cparams(2),
        name="ple",
    )(hb, h, pe, w_gate, b_gate, w_ple, g_next)


REPACK_ROWS = 512


def _repack_kernel(a_ref, g_ref, ab_ref, o_ref, t_ref):
    o_ref[...] = a_ref[...].astype(BF16)

    @pl.when(pl.program_id(1) == 0)
    def _():
        t_ref[...] = jnp.zeros(t_ref.shape, BF16)
        t_ref[TAIL_G:TAIL_G + GLA_RANK, :] = g_ref[...].astype(BF16)
        t_ref[TAIL_A:TAIL_A + 2 * DN_H, :] = ab_ref[...].astype(BF16)


def repack_w_in(w_in_t):
    nl = w_in_t.shape[0]
    n_tiles = N_MAIN // REPACK_ROWS
    t_qkvd, t_zc, t_zd = COL_QKVD // REPACK_ROWS, COL_ZC // REPACK_ROWS, COL_ZD // REPACK_ROWS

    def src_row(j):
        return jnp.where(j < t_qkvd, j * REPACK_ROWS,
                         jnp.where(j < t_zc, ORIG_QKVD + (j - t_qkvd) * REPACK_ROWS,
                                   jnp.where(j < t_zd, ORIG_ZC + (j - t_zc) * REPACK_ROWS,
                                             ORIG_ZD + (j - t_zd) * REPACK_ROWS)))

    row_align = 2 * V7X_SUBLANES
    assert all(o % row_align == 0 for o in (ORIG_QKVD, ORIG_ZC, ORIG_ZD, ORIG_GC, ORIG_AB))

    def rows(n, start):
        return pl.BlockSpec((None, pl.Element(n), pl.Element(D_MODEL)),
                            lambda l, j: (l, pl.multiple_of(start(j), row_align), 0))

    return pl.pallas_call(
        _repack_kernel,
        grid=(nl, n_tiles),
        in_specs=[rows(REPACK_ROWS, src_row), rows(GLA_RANK, lambda j: ORIG_GC), rows(2 * DN_H, lambda j: ORIG_AB)],
        out_specs=[pl.BlockSpec((None, REPACK_ROWS, D_MODEL), lambda l, j: (l, j, 0)),
                   pl.BlockSpec((None, TAIL_W, D_MODEL), lambda l, j: (l, 0, 0))],
        out_shape=[jax.ShapeDtypeStruct((nl, N_MAIN, D_MODEL), BF16),
                   jax.ShapeDtypeStruct((nl, TAIL_W, D_MODEL), BF16)],
        compiler_params=_cparams(2),
        name="repack_w_in",
    )(w_in_t, w_in_t, w_in_t)


def _final_norm_kernel(x_ref, g_ref, o_ref):
    x = x_ref[...]
    ms = jnp.mean(x * x, axis=-1, keepdims=True)
    o_ref[...] = (x * lax.rsqrt(ms + EPS)) * g_ref[...]


def final_norm(h, g, *, tm):
    m = h.shape[0]
    return pl.pallas_call(
        _final_norm_kernel,
        grid=(m // tm,),
        in_specs=[pl.BlockSpec((tm, D_MODEL), lambda i: (i, 0)),
                  pl.BlockSpec((1, D_MODEL), lambda i: (0, 0))],
        out_specs=pl.BlockSpec((tm, D_MODEL), lambda i: (i, 0)),
        out_shape=jax.ShapeDtypeStruct((m, D_MODEL), F32),
        compiler_params=_cparams(1),
        name="final_norm",
    )(h, g)


S5_SEG = 32


def _s5_prep_kernel(lre_ref, lim_ref, ldt_ref, coef_ref, cre_ref, cim_ref):
    lre = lre_ref[...]
    lim = lim_ref[...]
    dt = jnp.exp(ldt_ref[...])
    ai = lim * dt
    mag = jnp.exp(lre * dt)
    ar = mag * jnp.cos(ai)
    aim = mag * jnp.sin(ai)
    den = lre * lre + lim * lim
    nr = ar - 1.0
    cre_ref[...] = (nr * lre + aim * lim) / den
    cim_ref[...] = (aim * lre - nr * lim) / den
    pr, pim = [ar], [aim]
    for _ in range(S5_SEG - 1):
        nr_, ni_ = pr[-1] * ar - pim[-1] * aim, pr[-1] * aim + pim[-1] * ar
        pr.append(nr_)
        pim.append(ni_)
    shape = (V7X_SUBLANES, S5_CH)
    coef_ref[0] = jnp.broadcast_to(ar, shape)
    coef_ref[1] = jnp.broadcast_to(aim, shape)
    coef_ref[2] = jnp.broadcast_to(pr[S5_SEG - 1], shape)
    coef_ref[3] = jnp.broadcast_to(pim[S5_SEG - 1], shape)


def s5_prep(lam_re, lam_im, log_dt_rep):
    nl = lam_re.shape[0]
    vec = pl.BlockSpec((None, 1, S5_CH), lambda l: (l, 0, 0))
    return pl.pallas_call(
        _s5_prep_kernel,
        grid=(nl,),
        in_specs=[vec, vec, vec],
        out_specs=[pl.BlockSpec((None, 4, V7X_SUBLANES, S5_CH), lambda l: (l, 0, 0, 0)), vec, vec],
        out_shape=[jax.ShapeDtypeStruct((nl, 4, V7X_SUBLANES, S5_CH), F32),
                   jax.ShapeDtypeStruct((nl, 1, S5_CH), F32),
                   jax.ShapeDtypeStruct((nl, 1, S5_CH), F32)],
        compiler_params=_cparams(1),
        name="s5_prep",
    )(lam_re, lam_im, log_dt_rep)


def _s5_bb_kernel(cr_ref, ci_ref, br_ref, bi_ref, or_ref, oi_ref):
    cr = cr_ref[...]
    ci = ci_ref[...]
    br = br_ref[...]
    bi = bi_ref[...]
    or_ref[...] = cr * br - ci * bi
    oi_ref[...] = cr * bi + ci * br


def s5_bb(coef_re_col, coef_im_col, b_re, b_im):
    nl = b_re.shape[0]
    rows = 1024
    col = pl.BlockSpec((None, rows, 1), lambda l, r: (l, r, 0))
    mat = pl.BlockSpec((None, rows, S5_GROUP), lambda l, r: (l, r, 0))
    return pl.pallas_call(
        _s5_bb_kernel,
        grid=(nl, S5_CH // rows),
        in_specs=[col, col, mat, mat],
        out_specs=[mat, mat],
        out_shape=[jax.ShapeDtypeStruct((nl, S5_CH, S5_GROUP), F32)] * 2,
        compiler_params=_cparams(2),
        name="s5_bb",
    )(coef_re_col, coef_im_col, b_re, b_im)


S5_STRIP = 512
S5_U_GROUPS = V7X_MXU_DEPTH // S5_GROUP
S5_U_IN = S5_U_GROUPS * S5_GROUP
S5_U_OUT = S5_U_GROUPS * S5_STATE
S5_Y_GROUPS = S5_STRIP // S5_STATE
S5_WU_SHAPE = (S5_GROUPS // S5_U_GROUPS, S5_U_IN, S5_U_OUT)
S5_WY_SHAPE = (S5_CH // S5_STRIP, S5_STRIP, S5_Y_GROUPS * S5_GROUP)


def _s5_kernel(xa_ref, za_ref, h0r_ref, h0i_ref, coef_ref, perm_ref, permt_ref,
               wur_ref, wui_ref, wyr_ref, wyi_ref, dsk_ref, wglu_ref, bglu_ref,
               mix_ref, hro_ref, hio_ref,
               ur_scr, ui_scr, cr_scr, ci_scr, y_scr, *, nb, tt, t_last):
    tc = pl.program_id(1)
    ntc = pl.num_programs(1)
    split = nb == 1
    lc = tt // V7X_SUBLANES if split else tt
    n_sets = 1 if split else nb // V7X_SUBLANES
    set_rows = V7X_SUBLANES * lc

    xb16 = _dot(perm_ref[...], xa_ref[...].astype(BF16)).astype(BF16)
    for k in range(S5_GROUPS // S5_U_GROUPS):
        xk = xb16[:, S5_U_IN * k:S5_U_IN * (k + 1)]
        ur_scr[:, S5_U_OUT * k:S5_U_OUT * (k + 1)] = _dot(xk, wur_ref[k])
        ui_scr[:, S5_U_OUT * k:S5_U_OUT * (k + 1)] = _dot(xk, wui_ref[k])

    @pl.when(tc == 0)
    def _():
        cr_scr[...] = h0r_ref[...]
        ci_scr[...] = h0i_ref[...]

    row8 = lax.broadcasted_iota(jnp.int32, (V7X_SUBLANES, S5_STRIP), 0)
    for st in range(S5_CH // S5_STRIP):
        ls = slice(st * S5_STRIP, (st + 1) * S5_STRIP)
        ar = coef_ref[0, :, ls]
        ai = coef_ref[1, :, ls]
        for s in range(n_sets):
            base = s * set_rows
            if split:
                h0r = jnp.zeros((V7X_SUBLANES, S5_STRIP), F32)
                h0i = jnp.zeros((V7X_SUBLANES, S5_STRIP), F32)
            else:
                h0r = jnp.zeros((V7X_SUBLANES, S5_STRIP), F32)
                h0i = jnp.zeros((V7X_SUBLANES, S5_STRIP), F32)
                for k in range(V7X_SUBLANES):
                    h0r = jnp.where(row8 == k, cr_scr[s * V7X_SUBLANES + k, :, ls], h0r)
                    h0i = jnp.where(row8 == k, ci_scr[s * V7X_SUBLANES + k, :, ls], h0i)

            def step(j, h, base=base, ls=ls, ar=ar, ai=ai):
                hr, hi = h
                rows = pl.ds(pl.multiple_of(base + j * V7X_SUBLANES, V7X_SUBLANES), V7X_SUBLANES)
                nr = (ar * hr - ai * hi) + ur_scr[rows, ls]
                ni = (ar * hi + ai * hr) + ui_scr[rows, ls]
                ur_scr[rows, ls] = nr
                ui_scr[rows, ls] = ni
                return nr, ni

            fr, fi = lax.fori_loop(0, lc, step, (h0r, h0i), unroll=True)
            if split:
                alr = coef_ref[2, 0:1, ls]
                ali = coef_ref[3, 0:1, ls]
                cr = cr_scr[0, :, ls]
                ci = ci_scr[0, :, ls]
                init_r = jnp.zeros((V7X_SUBLANES, S5_STRIP), F32)
                init_i = jnp.zeros((V7X_SUBLANES, S5_STRIP), F32)
                for k in range(V7X_SUBLANES):
                    init_r = jnp.where(row8 == k, cr, init_r)
                    init_i = jnp.where(row8 == k, ci, init_i)
                    cr, ci = (alr * cr - ali * ci) + fr[k:k + 1], (alr * ci + ali * cr) + fi[k:k + 1]
                cr_scr[0, :, ls] = cr
                ci_scr[0, :, ls] = ci

                def fix(j, c, base=base, ls=ls, ar=ar, ai=ai):
                    rows = pl.ds(pl.multiple_of(base + j * V7X_SUBLANES, V7X_SUBLANES), V7X_SUBLANES)
                    c_r = ar * c[0] - ai * c[1]
                    c_i = ar * c[1] + ai * c[0]
                    ur_scr[rows, ls] = ur_scr[rows, ls] + c_r
                    ui_scr[rows, ls] = ui_scr[rows, ls] + c_i
                    return c_r, c_i

                lax.fori_loop(0, lc, fix, (init_r, init_i), unroll=True)

        y_scr[:, V7X_LANES * st:V7X_LANES * (st + 1)] = (_dot(ur_scr[:, ls].astype(BF16), wyr_ref[st])
                                                           - _dot(ui_scr[:, ls].astype(BF16), wyi_ref[st]))
    yh, ym, yl = _split3(y_scr[...])
    pt = permt_ref[...]
    y = (_dot(pt, yh) + _dot(pt, ym) + _dot(pt, yl)) + dsk_ref[...] * xa_ref[...]
    ga = _gelu_tanh(y)
    ya = ga * _sigmoid(_dot(ga.astype(BF16), wglu_ref[...]) + bglu_ref[...])
    mix_ref[...] = (ya * _silu(za_ref[...])).astype(BF16)

    @pl.when(tc == ntc - 1)
    def _():
        if split:
            hro_ref[...] = cr_scr[...]
            hio_ref[...] = ci_scr[...]
        else:
            for s in range(n_sets):
                for k in range(V7X_SUBLANES):
                    r = s * set_rows + V7X_SUBLANES * t_last + k
                    hro_ref[s * V7X_SUBLANES + k] = ur_scr[r:r + 1, :]
                    hio_ref[s * V7X_SUBLANES + k] = ui_scr[r:r + 1, :]


def s5_mixer(proj, h0_re, h0_im, coef, wur, wui, wyr, wyi, dskip, wglu, bglu, layer, state_layer,
             *, batch, seq, t_valid, nb, tt):
    ntc = seq // tt
    assert nb == 1 or ntc == 1
    rows = nb * tt
    grid = (batch // nb, ntc)
    t_last = (t_valid - 1) - (ntc - 1) * tt
    assert 0 <= t_last < tt
    if nb == 1:
        assert tt == V7X_SUBLANES * S5_SEG and t_last == tt - 1
        lc = S5_SEG
    else:
        assert nb % V7X_SUBLANES == 0
        lc = tt
    idx = np.arange(rows)
    set_rows = V7X_SUBLANES * lc
    src = (idx // set_rows) * set_rows + (idx % V7X_SUBLANES) * lc + (idx % set_rows) // V7X_SUBLANES
    perm_np = np.zeros((rows, rows), np.float32)
    perm_np[idx, src] = 1.0
    perm = jnp.asarray(perm_np, BF16)
    perm_t = jnp.asarray(perm_np.T, BF16)
    sq = pl.BlockSpec((rows, rows), lambda b, t: (0, 0))

    def tok(cb):
        return pl.BlockSpec((rows, W_MIX), lambda b, t: (b * ntc + t, cb))

    st_in = pl.BlockSpec((None, nb, 1, S5_CH), lambda b, t: (state_layer, b, 0, 0))
    st_out = pl.BlockSpec((nb, 1, S5_CH), lambda b, t: (b, 0, 0))

    def lw(shape):
        nd = len(shape)
        return pl.BlockSpec((None,) + shape, lambda b, t: (layer,) + (0,) * nd)

    return pl.pallas_call(
        functools.partial(_s5_kernel, nb=nb, tt=tt, t_last=t_last),
        grid=grid,
        in_specs=[tok(COL_XA // W_MIX), tok(COL_ZA // W_MIX), st_in, st_in,
                  lw((4, V7X_SUBLANES, S5_CH)), sq, sq, lw(S5_WU_SHAPE), lw(S5_WU_SHAPE),
                  lw(S5_WY_SHAPE), lw(S5_WY_SHAPE), lw((1, W_MIX)), lw((W_MIX, W_MIX)), lw((1, W_MIX))],
        out_specs=[pl.BlockSpec((rows, W_MIX), lambda b, t: (b * ntc + t, 0)), st_out, st_out],
        out_shape=[jax.ShapeDtypeStruct((batch * seq, W_MIX), BF16),
                   jax.ShapeDtypeStruct((batch, 1, S5_CH), F32),
                   jax.ShapeDtypeStruct((batch, 1, S5_CH), F32)],
        scratch_shapes=[pltpu.VMEM((rows, S5_CH), F32), pltpu.VMEM((rows, S5_CH), F32),
                        pltpu.VMEM((nb, 1, S5_CH), F32), pltpu.VMEM((nb, 1, S5_CH), F32),
                        pltpu.VMEM((rows, W_MIX), F32)],
        compiler_params=_cparams(2),
        name="s5_mixer",
    )(proj, proj, h0_re, h0_im, coef, perm, perm_t, wur, wui, wyr, wyi, dskip, wglu, bglu)


LRU_STRIP = 512
CONV_PAD = V7X_SUBLANES


def _causal_conv(tail_scr, x, cw_ref, bi):
    rows_n = x.shape[0]
    width = x.shape[1]
    n_tiles = rows_n // V7X_SUBLANES
    xe = jnp.concatenate([tail_scr[bi], x], axis=0).reshape(n_tiles + 1, V7X_SUBLANES, width)
    row = lax.broadcasted_iota(jnp.int32, (n_tiles, V7X_SUBLANES, width), 1)
    acc = cw_ref[CONV_W - 1:CONV_W, :] * x
    for jj in range(CONV_W - 1):
        shift = CONV_W - 1 - jj
        rot = pltpu.roll(xe, shift, 1)
        shifted = jnp.where(row < shift, rot[:-1], rot[1:]).reshape(rows_n, width)
        acc = acc + cw_ref[jj:jj + 1, :] * shifted
    tail_scr[bi] = xe[n_tiles]
    return acc


def _init_conv_tail(tail_scr, cv0_ref, nb):
    tail_scr[...] = jnp.zeros(tail_scr.shape, F32)
    for bi in range(nb):
        tail_scr[bi, CONV_PAD - (CONV_W - 1):CONV_PAD, :] = cv0_ref[bi]


def _lru_kernel(xb_ref, zb_ref, h0_ref, cv0_ref, cw_ref, cb_ref, wri_ref, br_ref, bi_ref, lam_ref,
                mix_ref, ho_ref,
                tail_scr, xc_scr, a_scr, b_scr, c_scr, *, nb, tt, t_last):
    tc = pl.program_id(1)
    ntc = pl.num_programs(1)

    @pl.when(tc == 0)
    def _():
        c_scr[...] = h0_ref[...]
        _init_conv_tail(tail_scr, cv0_ref, nb)

    def conv_body(bi, carry):
        rows = pl.ds(pl.multiple_of(bi * tt, V7X_SUBLANES), tt)
        acc = _causal_conv(tail_scr, xb_ref[rows, :], cw_ref, bi)
        xc_scr[rows, :] = acc + cb_ref[...]
        return carry

    lax.fori_loop(0, nb, conv_body, 0)

    xc = xc_scr[...]
    xc16 = xc.astype(BF16)
    sp = _softplus(-lam_ref[...])
    for blk in range(LRU_BLOCKS):
        cs = slice(blk * LRU_BLK, (blk + 1) * LRU_BLK)
        pre = _dot(xc16[:, cs], wri_ref[blk])
        r = _sigmoid(pre[:, :LRU_BLK] + br_ref[:, cs])
        ig = _sigmoid(pre[:, LRU_BLK:] + bi_ref[:, cs])
        log_a = (-LRU_C) * r * sp[:, cs]
        a_scr[:, cs] = jnp.exp(log_a)
        b_scr[:, cs] = jnp.sqrt(_neg_expm1_nonpos(2.0 * log_a)) * (ig * xc[:, cs])

    ng = tt // V7X_SUBLANES
    for s in range(W_MIX // LRU_STRIP):
        ls = slice(s * LRU_STRIP, (s + 1) * LRU_STRIP)
        row = lax.broadcasted_iota(jnp.int32, (V7X_SUBLANES, LRU_STRIP), 0)

        def seq_body(bi, carry, ls=ls, row=row):
            def tile_body(g, c):
                rows = pl.ds(pl.multiple_of(bi * tt + g * V7X_SUBLANES, V7X_SUBLANES), V7X_SUBLANES)
                a = a_scr[rows, ls]
                b = b_scr[rows, ls]
                for k in (1, 2, 4):
                    a_sh = jnp.where(row >= k, pltpu.roll(a, k, 0), 1.0)
                    b_sh = jnp.where(row >= k, pltpu.roll(b, k, 0), 0.0)
                    b = b + a * b_sh
                    a = a * a_sh
                h = b + a * c
                b_scr[rows, ls] = h
                return h[V7X_SUBLANES - 1:, :]

            c = lax.fori_loop(0, ng, tile_body, c_scr[bi, :, ls])
            c_scr[bi, :, ls] = c
            return carry

        lax.fori_loop(0, nb, seq_body, 0)

    mix_ref[...] = (b_scr[...] * _silu(zb_ref[...])).astype(BF16)

    @pl.when(tc == ntc - 1)
    def _():
        for bi in range(nb):
            r = bi * tt + t_last
            ho_ref[bi] = b_scr[r:r + 1, :]


def lru_mixer(proj, h0, conv0, cw, cb, wri, br, bi_, lam, layer, state_layer, *, batch, seq, t_valid, nb, tt):
    ntc = seq // tt
    assert nb == 1 or ntc == 1
    rows = nb * tt
    grid = (batch // nb, ntc)
    t_last = (t_valid - 1) - (ntc - 1) * tt

    def tok(cb_):
        return pl.BlockSpec((rows, W_MIX), lambda b, t: (b * ntc + t, cb_))

    def lw(shape):
        nd = len(shape)
        return pl.BlockSpec((None,) + shape, lambda b, t: (layer,) + (0,) * nd)

    return pl.pallas_call(
        functools.partial(_lru_kernel, nb=nb, tt=tt, t_last=t_last),
        grid=grid,
        in_specs=[tok(COL_XB // W_MIX), tok(COL_ZB // W_MIX),
                  pl.BlockSpec((None, nb, 1, W_MIX), lambda b, t: (state_layer, b, 0, 0)),
                  pl.BlockSpec((None, nb, CONV_W - 1, W_MIX), lambda b, t: (state_layer, b, 0, 0)),
                  lw((CONV_W, W_MIX)), lw((1, W_MIX)), lw((LRU_BLOCKS, LRU_BLK, 2 * LRU_BLK)),
                  lw((1, W_MIX)), lw((1, W_MIX)), lw((1, W_MIX))],
        out_specs=[pl.BlockSpec((rows, W_MIX), lambda b, t: (b * ntc + t, 0)),
                   pl.BlockSpec((nb, 1, W_MIX), lambda b, t: (b, 0, 0))],
        out_shape=[jax.ShapeDtypeStruct((batch * seq, W_MIX), BF16),
                   jax.ShapeDtypeStruct((batch, 1, W_MIX), F32)],
        scratch_shapes=[pltpu.VMEM((nb, CONV_PAD, W_MIX), F32),
                        pltpu.VMEM((rows, W_MIX), F32), pltpu.VMEM((rows, W_MIX), F32),
                        pltpu.VMEM((rows, W_MIX), F32), pltpu.VMEM((nb, 1, W_MIX), F32)],
        compiler_params=_cparams(2),
        name="lru_mixer",
    )(proj, proj, h0, conv0, cw, cb, wri, br, bi_, lam)


def _for_each_group(nb, g, group_body):
    assert nb % g == 0
    if nb == g:
        group_body(list(range(nb)))
    else:
        def body(i, carry):
            group_body([i * g + j for j in range(g)])
            return carry
        lax.fori_loop(0, nb // g, body, 0)


def _gla_kernel(q_ref, k_ref, v_ref, z_ref, tail_ref, s0_ref, wg_ref, bg_ref, go_ref, stack_ref,
                mix_ref, so_ref,
                s_scr, mix_scr, *, nb, group, c, cv, single_chunk):
    del stack_ref
    tc = pl.program_id(1)
    ntc = pl.num_programs(1)
    sb = min(GLA_SUB, c)
    nblk = c // sb

    s_src, s_dst = (s0_ref, so_ref) if single_chunk else (s_scr, s_scr)
    if not single_chunk:
        @pl.when(tc == 0)
        def _():
            s_scr[...] = s0_ref[...]

    row_c = lax.broadcasted_iota(jnp.int32, (c, 1), 0)
    lane_c = lax.broadcasted_iota(jnp.int32, (sb, c), 1)
    row_sb = lax.broadcasted_iota(jnp.int32, (sb, 1), 0)

    def group_body(bis):
        pairs = [(g, h) for g in range(len(bis)) for h in range(GLA_H)]
        heads = range(len(pairs))
        b_all, q_all, k_all, v_all = [], [], [], []
        for bi in bis:
            x = _dot(tail_ref[bi].astype(BF16), wg_ref[...]) + bg_ref[...]
            b_all.append(_cumsum_rows(_log_sigmoid(x) * (1.0 / GLA_TAU)))
            q_all.append(q_ref[bi])
            k_all.append(k_ref[bi])
            v_all.append(v_ref[bi])
        s_old = [s_src[bis[g], h] for g, h in pairs]
        q = [q_all[g][:, h * GLA_DK:(h + 1) * GLA_DK] * (GLA_DK ** -0.5) for g, h in pairs]
        k = [k_all[g][:, h * GLA_DK:(h + 1) * GLA_DK] for g, h in pairs]
        v = [v_all[g][:, h * GLA_DV:(h + 1) * GLA_DV] for g, h in pairs]
        b = [b_all[g][:, h * GLA_DK:(h + 1) * GLA_DK] for g, h in pairs]
        o_state = [_dot((q[h] * jnp.exp(b[h])).astype(BF16), s_old[h].astype(BF16)) for h in heads]
        s_new = []
        for h in heads:
            b_last = b[h][cv - 1:cv]
            dec = b_last - b[h]
            if cv < c:
                dec = jnp.where(row_c < cv, dec, NEG_INF)
            kd = k[h] * jnp.exp(dec)
            d_col = jnp.transpose(jnp.broadcast_to(jnp.exp(b_last), (V7X_SUBLANES, GLA_DK)))[:, 0:1]
            s_new.append(d_col * s_old[h] + _mm(kd, v[h], dot=_dot_tn, contract=0))
        att_off = []
        for h in heads:
            per_blk = [jnp.zeros((sb, c), F32)]
            for blk in range(1, nblk):
                r0 = blk * sb
                b_ref_row = b[h][r0 - 1:r0]
                qs = (q[h][r0:r0 + sb] * jnp.exp(b[h][r0:r0 + sb] - b_ref_row)).astype(BF16)
                kd = (k[h] * jnp.exp(jnp.where(row_c < r0, b_ref_row - b[h], NEG_INF))).astype(BF16)
                per_blk.append(_dot_nt(qs, kd))
            att_off.append(per_blk)
        att = []
        for h in heads:
            att_rows = []
            for blk in range(nblk):
                r0 = blk * sb
                q_i = q[h][r0:r0 + sb]
                b_i = b[h][r0:r0 + sb]
                a = att_off[h][blk]
                for sl in range(sb):
                    s_abs = r0 + sl
                    e = jnp.exp(jnp.where(row_sb >= sl, b_i - b[h][s_abs:s_abs + 1], NEG_INF))
                    col = jnp.sum(q_i * (k[h][s_abs:s_abs + 1] * e), axis=-1, keepdims=True)
                    a = jnp.where(lane_c == s_abs, col, a)
                att_rows.append(a)
            att.append(att_rows[0] if nblk == 1 else jnp.concatenate(att_rows, axis=0))
        o = [_mm(att[h], v[h]) + o_state[h] for h in heads]
        for p, (g, h) in enumerate(pairs):
            vs = slice(h * GLA_DV, (h + 1) * GLA_DV)
            ms = jnp.mean(o[p] * o[p], axis=-1, keepdims=True)
            on = (o[p] * lax.rsqrt(ms + EPS)) * go_ref[...]
            mix_scr[bis[g], :, vs] = on * _silu(z_ref[bis[g], :, vs])
            s_dst[bis[g], h] = s_new[p]

    _for_each_group(nb, group, group_body)
    mix_ref[...] = mix_scr[...].astype(BF16)

    if not single_chunk:
        @pl.when(tc == ntc - 1)
        def _():
            so_ref[...] = s_scr[...]


def _stack_alias(stack, n_inputs_before):
    spec = pl.BlockSpec(memory_space=pl.ANY)
    if stack is None:
        return jnp.zeros((V7X_SUBLANES, V7X_LANES), F32), spec, {}
    return stack, spec, {n_inputs_before: 1}


def gla_mixer(proj, tail, s0, wg, bg, go, layer, state_layer, *, batch, seq, c, cv, nb, group, n_layers, stack=None):
    ntc = seq // c
    assert batch % nb == 0 and seq % c == 0
    grid = (batch // nb, ntc)
    proj3 = proj.reshape(batch, seq, proj.shape[-1])
    tail3 = tail.reshape(batch, seq, TAIL_W)

    def tok(width, col):
        return pl.BlockSpec((nb, c, width), lambda b, t: (b, t, col // width))

    def lw(shape):
        nd = len(shape)
        return pl.BlockSpec((None,) + shape, lambda b, t: (layer,) + (0,) * nd)

    st_shape = (nb, GLA_H, GLA_DK, GLA_DV)
    stack_arg, stack_spec, aliases = _stack_alias(stack, 9)
    mix, state = pl.pallas_call(
        functools.partial(_gla_kernel, nb=nb, group=group, c=c, cv=cv, single_chunk=ntc == 1),
        grid=grid,
        in_specs=[tok(GLA_H * GLA_DK, COL_QC), tok(GLA_H * GLA_DK, COL_KC), tok(W_MIX, COL_VC), tok(W_MIX, COL_ZC),
                  tok(TAIL_W, 0),
                  pl.BlockSpec((None,) + st_shape, lambda b, t: (state_layer, b, 0, 0, 0)),
                  lw((TAIL_W, GLA_H * GLA_DK)), lw((1, GLA_H * GLA_DK)), lw((1, GLA_DV)), stack_spec],
        out_specs=[tok(W_MIX, 0),
                   pl.BlockSpec((None,) + st_shape, lambda b, t: (layer, b, 0, 0, 0))],
        out_shape=[jax.ShapeDtypeStruct((batch, seq, W_MIX), BF16),
                   jax.ShapeDtypeStruct((n_layers, batch, GLA_H, GLA_DK, GLA_DV), F32)],
        scratch_shapes=[pltpu.VMEM(st_shape, F32), pltpu.VMEM((nb, c, W_MIX), F32)],
        input_output_aliases=aliases,
        compiler_params=_cparams(2),
        name="gla_mixer",
    )(proj3, proj3, proj3, proj3, tail3, s0, wg, bg, go, stack_arg)
    return mix.reshape(batch * seq, W_MIX), state


def _l2norm(x):
    return x * lax.rsqrt(jnp.sum(x * x, axis=-1, keepdims=True) + EPS)


def _unit_lower_solve(ms, rhss, c, cv):
    if c <= V7X_SUBLANES:
        row = lax.broadcasted_iota(jnp.int32, (c, 1), 0)
        us = []
        for m, rhs in zip(ms, rhss):
            u = jnp.zeros_like(rhs)
            u_rows = []
            for t in range(cv):
                ut = rhs[t:t + 1]
                for s in range(t):
                    ut = ut - m[t:t + 1, s:s + 1] * u_rows[s]
                u_rows.append(ut)
                u = jnp.where(row == t, ut, u)
            us.append(u)
        return us
    xs = [-m for m in ms]
    ys = list(rhss)
    dv = rhss[0].shape[1]
    levels = int(math.log2(c))
    assert 3 * c <= V7X_MXU_DEPTH

    def hi_lo(t):
        hi = t.astype(BF16)
        return hi, (t - hi.astype(F32)).astype(BF16)

    for lvl in range(levels):
        last = lvl == levels - 1
        ps = []
        for x, y in zip(xs, ys):
            xh, xl = hi_lo(x)
            yh, yl = hi_lo(y)
            bh = yh if last else jnp.concatenate([yh, xh], axis=1)
            bl = yl if last else jnp.concatenate([yl, xl], axis=1)
            lhs = jnp.concatenate([xh.astype(F32), xh.astype(F32), xl.astype(F32)], axis=1).astype(BF16)
            ps.append(_dot(lhs, jnp.concatenate([bh, bl, bh], axis=0)))
        ys = [y + p[:, :dv] for y, p in zip(ys, ps)]
        if not last:
            xs = [p[:, dv:] for p in ps]
    return ys


def _gdn_kernel(qkv_ref, z_ref, tail_ref, s0_ref, cv0_ref, cw_ref, alog_ref, dtb_ref, go_ref, stack_ref,
                mix_ref, so_ref,
                s_scr, tail_scr, mix_scr, *, nb, group, c, cv, single_chunk):
    del stack_ref
    tc = pl.program_id(1)
    ntc = pl.num_programs(1)

    s_src, s_dst = (s0_ref, so_ref) if single_chunk else (s_scr, s_scr)

    @pl.when(tc == 0)
    def _():
        if not single_chunk:
            s_scr[...] = s0_ref[...]
        _init_conv_tail(tail_scr, cv0_ref, nb)

    row_c = lax.broadcasted_iota(jnp.int32, (c, 1), 0)
    ri = lax.broadcasted_iota(jnp.int32, (c, c), 0)
    ci = lax.broadcasted_iota(jnp.int32, (c, c), 1)
    lane_t = lax.broadcasted_iota(jnp.int32, (c, TAIL_W), 1)
    a_lanes = (lane_t >= TAIL_A) & (lane_t < TAIL_A + DN_H)

    def group_body(bis):
        pairs = [(g, h) for g in range(len(bis)) for h in range(DN_H)]
        heads = range(len(pairs))
        qkv, gam, gam_t, beta = [], [], [], []
        for gi, bi in enumerate(bis):
            qkv.append(_silu(_causal_conv(tail_scr, qkv_ref[bi], cw_ref, bi)))
            tl = tail_ref[bi]
            gg = _cumsum_rows(jnp.where(a_lanes, -jnp.exp(alog_ref[...]) * _softplus(tl + dtb_ref[...]), 0.0))
            gam.append(gg)
            gam_t.append(jnp.transpose(gg))
            beta.append(_sigmoid(tl))
        s_old = [s_src[bis[g], h] for g, h in pairs]
        s16 = [s.astype(BF16) for s in s_old]
        q16, k16, kf, v = [], [], [], []
        for g, h in pairs:
            q = _l2norm(qkv[g][:, h * DN_DK:(h + 1) * DN_DK]) * (DN_DK ** -0.5)
            k = _l2norm(qkv[g][:, DN_H * DN_DK + h * DN_DK:DN_H * DN_DK + (h + 1) * DN_DK])
            q16.append(q.astype(BF16))
            k16.append(k.astype(BF16))
            kf.append(k)
            v.append(qkv[g][:, 2 * DN_H * DN_DK + h * DN_DV:2 * DN_H * DN_DK + (h + 1) * DN_DV])
        g_col = [gam[g][:, TAIL_A + h:TAIL_A + h + 1] for g, h in pairs]
        g_row = [gam_t[g][TAIL_A + h:TAIL_A + h + 1, :] for g, h in pairs]
        b_col = [beta[g][:, TAIL_B + h:TAIL_B + h + 1] for g, h in pairs]
        decay = [jnp.exp(jnp.where(ri >= ci, g_col[h] - g_row[h], NEG_INF)) for h in heads]
        eg = [jnp.exp(g_col[h]) for h in heads]
        kk = [_dot_nt(k16[h], k16[h]) for h in heads]
        k_s = [_dot(k16[h], s16[h]) for h in heads]
        q_s = [_dot(q16[h], s16[h]) for h in heads]
        qk = [_dot_nt(q16[h], k16[h]) for h in heads]
        m = [jnp.where(ri > ci, b_col[h] * kk[h] * decay[h], 0.0) for h in heads]
        rhs = [b_col[h] * (v[h] - eg[h] * k_s[h]) for h in heads]
        u = _unit_lower_solve(m, rhs, c, cv)
        o = [eg[h] * q_s[h] + _mm(qk[h] * decay[h], u[h]) for h in heads]
        kd = []
        for h in heads:
            dec = g_col[h][cv - 1:cv] - g_col[h]
            if cv < c:
                dec = jnp.where(row_c < cv, dec, NEG_INF)
            kd.append(kf[h] * jnp.exp(dec))
        s_new = [jnp.exp(g_col[h][cv - 1:cv]) * s_old[h] + _mm(kd[h], u[h], dot=_dot_tn, contract=0) for h in heads]
        for p, (g, h) in enumerate(pairs):
            vs = slice(h * DN_DV, (h + 1) * DN_DV)
            ms = jnp.mean(o[p] * o[p], axis=-1, keepdims=True)
            on = (o[p] * lax.rsqrt(ms + EPS)) * go_ref[...]
            mix_scr[bis[g], :, vs] = on * _silu(z_ref[bis[g], :, vs])
            s_dst[bis[g], h] = s_new[p]

    _for_each_group(nb, group, group_body)
    mix_ref[...] = mix_scr[...].astype(BF16)

    if not single_chunk:
        @pl.when(tc == ntc - 1)
        def _():
            so_ref[...] = s_scr[...]


def gdn_mixer(proj, tail, s0, conv0, cw, alog, dtb, go, layer, state_layer, *, batch, seq, c, cv, nb, group, n_layers,
              stack=None):
    ntc = seq // c
    assert batch % nb == 0 and seq % c == 0
    grid = (batch // nb, ntc)
    proj3 = proj.reshape(batch, seq, proj.shape[-1])
    tail3 = tail.reshape(batch, seq, TAIL_W)

    def tok(width, col):
        return pl.BlockSpec((nb, c, width), lambda b, t: (b, t, col // width))

    def lw(shape):
        nd = len(shape)
        return pl.BlockSpec((None,) + shape, lambda b, t: (layer,) + (0,) * nd)

    st_shape = (nb, DN_H, DN_DK, DN_DV)
    assert COL_QKVD % DN_QKV == 0
    stack_arg, stack_spec, aliases = _stack_alias(stack, 9)
    mix, state = pl.pallas_call(
        functools.partial(_gdn_kernel, nb=nb, group=group, c=c, cv=cv, single_chunk=ntc == 1),
        grid=grid,
        in_specs=[tok(DN_QKV, COL_QKVD),
                  tok(W_MIX, COL_ZD),
                  tok(TAIL_W, 0),
                  pl.BlockSpec((None,) + st_shape, lambda b, t: (state_layer, b, 0, 0, 0)),
                  pl.BlockSpec((None, nb, CONV_W - 1, DN_QKV), lambda b, t: (state_layer, b, 0, 0)),
                  lw((CONV_W, DN_QKV)), lw((1, TAIL_W)), lw((1, TAIL_W)), lw((1, DN_DV)), stack_spec],
        out_specs=[tok(W_MIX, 0),
                   pl.BlockSpec((None,) + st_shape, lambda b, t: (layer, b, 0, 0, 0))],
        out_shape=[jax.ShapeDtypeStruct((batch, seq, W_MIX), BF16),
                   jax.ShapeDtypeStruct((n_layers, batch, DN_H, DN_DK, DN_DV), F32)],
        input_output_aliases=aliases,
        scratch_shapes=[pltpu.VMEM(st_shape, F32),
                        pltpu.VMEM((nb, CONV_PAD, DN_QKV), F32),
                        pltpu.VMEM((nb, c, W_MIX), F32)],
        compiler_params=_cparams(2),
        name="gdn_mixer",
    )(proj3, proj3, tail3, s0, conv0, cw, alog, dtb, go, stack_arg)
    return mix.reshape(batch * seq, W_MIX), state


def _prepare_weights(w_in, s5_lam_re, s5_lam_im, s5_log_dt, s5_b_re, s5_b_im, s5_c_re, s5_c_im, s5_d,
                     s5_w_glu, lru_w_r, lru_w_i, gla_w_gate, dn_a_log, dn_dt_bias, w_out, ple_w, ple_gate_w):
    nl = w_in.shape[0]
    p = {}
    p["w_main"], p["w_tail"] = repack_w_in(jnp.swapaxes(w_in, 1, 2))
    p["w_out"] = w_out.astype(BF16)
    p["ple_w"] = ple_w.astype(BF16)
    p["ple_gate_w"] = ple_gate_w.astype(BF16)
    p["w_glu"] = s5_w_glu.astype(BF16)
    ldt = jnp.repeat(s5_log_dt, S5_STATE, axis=1).reshape(nl, 1, S5_CH)
    coef, cre, cim = s5_prep(s5_lam_re.reshape(nl, 1, S5_CH), s5_lam_im.reshape(nl, 1, S5_CH), ldt)
    bb_re, bb_im = s5_bb(cre.reshape(nl, S5_CH, 1), cim.reshape(nl, S5_CH, 1),
                         s5_b_re.reshape(nl, S5_CH, S5_GROUP), s5_b_im.reshape(nl, S5_CH, S5_GROUP))
    eye16 = jnp.eye(S5_U_GROUPS, dtype=F32)
    eye8 = jnp.eye(S5_Y_GROUPS, dtype=F32)

    def pack_u(bb):
        t = bb.reshape(nl, S5_GROUPS // S5_U_GROUPS, S5_U_GROUPS, S5_STATE, S5_GROUP)
        return jnp.einsum("lkgph,gG->lkghGp", t, eye16).reshape((nl,) + S5_WU_SHAPE).astype(BF16)

    def pack_y(cc):
        t = cc.reshape(nl, S5_GROUPS // S5_Y_GROUPS, S5_Y_GROUPS, S5_GROUP, S5_STATE)
        return jnp.einsum("ljghp,gG->ljgpGh", t, eye8).reshape((nl,) + S5_WY_SHAPE).astype(BF16)

    p["s5_coef"] = coef
    p["s5_wur"], p["s5_wui"] = pack_u(bb_re), pack_u(bb_im)
    p["s5_wyr"], p["s5_wyi"] = pack_y(s5_c_re), pack_y(s5_c_im)
    p["s5_d"] = s5_d.reshape(nl, 1, W_MIX)
    p["lru_wri"] = jnp.concatenate([lru_w_r, lru_w_i], axis=-1).astype(BF16)
    p["gla_wg"] = jnp.concatenate([gla_w_gate, jnp.zeros((nl, TAIL_W - GLA_RANK, GLA_H * GLA_DK), F32)],
                                  axis=1).astype(BF16)

    def tail_row(x):
        return jnp.pad(x, ((0, 0), (TAIL_A, TAIL_W - TAIL_A - DN_H))).reshape(nl, 1, TAIL_W)

    p["dn_alog"] = tail_row(dn_a_log)
    p["dn_dtb"] = tail_row(dn_dt_bias)
    return p


def kernel(x_prompt, x_sample, state_s5_re, state_s5_im, state_lru_h, state_lru_conv, state_gla, state_delta, state_delta_conv, p_prompt, p_sample, g_norm, w_in, s5_lam_re, s5_lam_im, s5_log_dt, s5_b_re, s5_b_im, s5_c_re, s5_c_im, s5_d, s5_w_glu, s5_b_glu, lru_conv_w, lru_conv_b, lru_w_r, lru_b_r, lru_w_i, lru_b_i, lru_lam, gla_w_gate, gla_b_gate, gla_g_out, dn_conv_w, dn_a_log, dn_dt_bias, dn_g_out, w_out, ple_w, ple_gate_w, ple_gate_b, g_final):
    nl = w_in.shape[0]
    bp, tp, _ = x_prompt.shape
    bs, ts, _ = x_sample.shape
    tsp = SAMPLE_T_PAD
    mp, ms = bp * tp, bs * ts

    p = _prepare_weights(w_in, s5_lam_re, s5_lam_im, s5_log_dt, s5_b_re, s5_b_im, s5_c_re, s5_c_im, s5_d,
                         s5_w_glu, lru_w_r, lru_w_i, gla_w_gate, dn_a_log, dn_dt_bias, w_out, ple_w, ple_gate_w)

    def vec(x):
        return x.reshape(nl, 1, x.shape[-1])

    g_norm3, b_glu3, cb3 = vec(g_norm), vec(s5_b_glu), vec(lru_conv_b)
    br3, bi3, lam3 = vec(lru_b_r), vec(lru_b_i), vec(lru_lam)
    bg3, go_c3, go_d3, pgb3 = vec(gla_b_gate), vec(gla_g_out), vec(dn_g_out), vec(ple_gate_b)
    pe_p = p_prompt.reshape(nl, mp, D_PLE).astype(BF16)
    pe_s = p_sample.reshape(nl, ms, D_PLE).astype(BF16)

    z_s5 = jnp.zeros((1, bp, 1, S5_CH), F32)
    z_lru = jnp.zeros((1, bp, 1, W_MIX), F32)
    z_lconv = jnp.zeros((1, bp, CONV_W - 1, W_MIX), F32)
    z_gla = jnp.zeros((1, bp, GLA_H, GLA_DK, GLA_DV), F32)
    z_dn = jnp.zeros((1, bp, DN_H, DN_DK, DN_DV), F32)
    z_dconv = jnp.zeros((1, bp, CONV_W - 1, DN_QKV), F32)
    c_s5r = state_s5_re.reshape(nl, bs, 1, S5_CH)
    c_s5i = state_s5_im.reshape(nl, bs, 1, S5_CH)
    c_lru = state_lru_h.reshape(nl, bs, 1, W_MIX)

    def mixers(proj, tail, layer, *, batch, seq, t_valid, st, sl, nb_scan, tt, c, gla_grp, nb_gdn, gdn_grp, stacks):
        s5r0, s5i0, lru0, lconv0, gla0, dn0, dconv0 = st
        gla_stack, dn_stack = stacks
        mix_a, s5r, s5i = s5_mixer(proj, s5r0, s5i0, p["s5_coef"], p["s5_wur"], p["s5_wui"], p["s5_wyr"],
                                   p["s5_wyi"], p["s5_d"], p["w_glu"], b_glu3, layer, sl,
                                   batch=batch, seq=seq, t_valid=t_valid, nb=nb_scan, tt=tt)
        mix_b, lruh = lru_mixer(proj, lru0, lconv0, lru_conv_w, cb3, p["lru_wri"], br3, bi3, lam3, layer, sl,
                                batch=batch, seq=seq, t_valid=t_valid, nb=nb_scan, tt=tt)
        mix_c, glas = gla_mixer(proj, tail, gla0, p["gla_wg"], bg3, go_c3, layer, sl,
                                batch=batch, seq=seq, c=c, cv=min(c, t_valid), nb=gla_grp, group=gla_grp, n_layers=nl,
                                stack=gla_stack)
        mix_d, dns = gdn_mixer(proj, tail, dn0, dconv0, dn_conv_w, p["dn_alog"], p["dn_dtb"], go_d3, layer, sl,
                               batch=batch, seq=seq, c=c, cv=min(c, t_valid), nb=nb_gdn, group=gdn_grp, n_layers=nl,
                               stack=dn_stack)
        return (mix_a, mix_b, mix_c, mix_d), (s5r, s5i, lruh, glas, dns)

    hp = x_prompt.reshape(mp, D_MODEL)
    hs = x_sample.reshape(ms, D_MODEL)
    new_p, new_s = [], []
    stacks_p = stacks_s = (None, None)
    hgp, ssqp = prenorm(hp, g_norm3, 0, tm=DENSE_TILES_SAMPLE["tm"])
    hgs, ssqs = prenorm(hs, g_norm3, 0, tm=DENSE_TILES_SAMPLE["tm"])
    for i in range(nl):
        nxt = min(i + 1, nl - 1)
        proj, tail = in_proj(hgp, ssqp, p["w_main"], p["w_tail"], i, **IN_PROJ_TILES_PROMPT)
        mixes, (s5r, s5i, lruh, glas, dns) = mixers(
            proj, tail, i, batch=bp, seq=tp, t_valid=tp,
            st=(z_s5, z_s5, z_lru, z_lconv, z_gla, z_dn, z_dconv), sl=0,
            nb_scan=1, tt=256, c=GLA_CHUNK, gla_grp=4, nb_gdn=4, gdn_grp=2, stacks=stacks_p)
        stacks_p = (glas, dns)
        proj3 = proj.reshape(bp, tp, N_MAIN)
        new_p.append((s5r.reshape(bp, S5_GROUPS, S5_STATE), s5i.reshape(bp, S5_GROUPS, S5_STATE),
                      lruh.reshape(bp, W_MIX), proj3[:, tp - (CONV_W - 1):, COL_XB:COL_XB + W_MIX],
                      None, None, proj3[:, tp - (CONV_W - 1):, COL_QKVD:COL_QKVD + DN_QKV]))
        hp, hpb = out_proj(mixes, p["w_out"], hp, i, **OUT_PROJ_TILES_PROMPT)
        hp, hgp, ssqp = ple(hp, hpb, pe_p, p["ple_gate_w"], pgb3, p["ple_w"], g_norm3, i, nxt, **DENSE_TILES_PROMPT)

        proj_pad, tail_pad = in_proj(hgs, ssqs, p["w_main"], p["w_tail"], i, seq_rows=ts, seq_pad=tsp,
                                     **DENSE_TILES_SAMPLE)
        proj3 = proj_pad.reshape(bs, tsp, N_MAIN)[:, :ts]
        mixes, (s5r, s5i, lruh, glas, dns) = mixers(
            proj_pad, tail_pad, i, batch=bs, seq=tsp, t_valid=ts,
            st=(c_s5r, c_s5i, c_lru, state_lru_conv, state_gla, state_delta, state_delta_conv), sl=i,
            nb_scan=32, tt=tsp, c=tsp, gla_grp=8, nb_gdn=8, gdn_grp=4, stacks=stacks_s)
        stacks_s = (glas, dns)
        mixes = tuple(m.reshape(bs, tsp, W_MIX)[:, :ts].reshape(ms, W_MIX) for m in mixes)
        new_s.append((s5r.reshape(bs, S5_GROUPS, S5_STATE), s5i.reshape(bs, S5_GROUPS, S5_STATE),
                      lruh.reshape(bs, W_MIX), proj3[:, ts - (CONV_W - 1):, COL_XB:COL_XB + W_MIX],
                      None, None, proj3[:, ts - (CONV_W - 1):, COL_QKVD:COL_QKVD + DN_QKV]))
        hs, hsb = out_proj(mixes, p["w_out"], hs, i, **DENSE_TILES_SAMPLE)
        hs, hgs, ssqs = ple(hs, hsb, pe_s, p["ple_gate_w"], pgb3, p["ple_w"], g_norm3, i, nxt, **DENSE_TILES_SAMPLE)

    g_fin = g_final.reshape(1, D_MODEL)
    y_prompt = final_norm(hp, g_fin, tm=256).reshape(bp, tp, D_MODEL)
    y_sample = final_norm(hs, g_fin, tm=256).reshape(bs, ts, D_MODEL)

    def stk(lst, j):
        return jnp.stack([s[j] for s in lst], axis=0)

    return (y_prompt, y_sample,
            stk(new_p, 0), stk(new_p, 1), stk(new_p, 2), stk(new_p, 3), stacks_p[0], stacks_p[1], stk(new_p, 6),
            stk(new_s, 0), stk(new_s, 1), stk(new_s, 2), stk(new_s, 3), stacks_s[0], stacks_s[1], stk(new_s, 6))
```

```python
import functools
import math

import jax
import jax.numpy as jnp
import numpy as np
from jax import lax
from jax.experimental import pallas as pl
from jax.experimental.pallas import tpu as pltpu

F32 = jnp.float32
BF16 = jnp.bfloat16
EPS = 1e-6
NEG_INF = float("-inf")

D_MODEL = 4096
DEPTH = 4
W_MIX = 1024
S5_GROUPS = 64
S5_GROUP = 16
S5_STATE = 64
S5_CH = S5_GROUPS * S5_STATE
LRU_BLOCKS = 8
LRU_BLK = 128
LRU_C = 8.0
CONV_W = 4
GLA_H = 4
GLA_DK = 128
GLA_DV = 256
GLA_RANK = 16
GLA_TAU = 16.0
GLA_CHUNK = 64
GLA_SUB = 8
DN_H = 8
DN_DK = 128
DN_DV = 128
DN_QKV = 3072
DN_CHUNK = 64
D_PLE = 256
N_MAIN = 11264
COL_XA, COL_ZA, COL_XB, COL_ZB = 0, 1024, 2048, 3072
COL_QC, COL_KC, COL_VC = 4096, 4608, 5120
COL_QKVD, COL_ZC, COL_ZD = 6144, 9216, 10240
ORIG_ZC, ORIG_GC, ORIG_QKVD, ORIG_ZD, ORIG_AB = 6144, 7168, 7184, 10256, 11280
TAIL_W = 128
TAIL_G, TAIL_A, TAIL_B = 0, 16, 24
SAMPLE_T_PAD = 8

V7X_LANES = 128
V7X_SUBLANES = 8
V7X_MXU_DEPTH = 256
VMEM_LIMIT = 52 * 1024 * 1024
DENSE_TILES_PROMPT = dict(tm=1024, tn=512)
IN_PROJ_TILES_PROMPT = dict(tm=1024, tn=1024)
OUT_PROJ_TILES_PROMPT = dict(tm=1024, tn=1024)
PLE_ROW_SPLIT = 8
VMEM_HEADROOM = 6 * 1024 * 1024
DENSE_TILES_SAMPLE = dict(tm=512, tn=1024)


def _cparams(n_axes, vmem_limit=VMEM_LIMIT):
    return pltpu.CompilerParams(dimension_semantics=("arbitrary",) * n_axes,
                                vmem_limit_bytes=vmem_limit)


def _sigmoid(x):
    return jax.nn.sigmoid(x)


def _silu(x):
    return x * jax.nn.sigmoid(x)


def _softplus(x):
    return jnp.maximum(x, 0.0) + jnp.log1p(jnp.exp(-jnp.abs(x)))


def _neg_expm1_nonpos(x):
    t = jnp.tanh(0.5 * x)
    return (-2.0 * t) / (1.0 - t)


def _log_sigmoid(x):
    return jnp.minimum(x, 0.0) - jnp.log1p(jnp.exp(-jnp.abs(x)))


def _gelu_tanh(x):
    c = math.sqrt(2.0 / math.pi)
    return x * (0.5 * (1.0 + jnp.tanh(c * (x + 0.044715 * (x * x * x)))))


def _dot(a, b):
    return jnp.dot(a, b, preferred_element_type=F32)


def _dot_nt(a, b):
    return lax.dot_general(a, b, (((1,), (1,)), ((), ())), preferred_element_type=F32)


def _dot_tn(a, b):
    return lax.dot_general(a, b, (((0,), (0,)), ((), ())), preferred_element_type=F32)


def _split3(x):
    h = x.astype(BF16)
    r = x - h.astype(F32)
    m = r.astype(BF16)
    l = (r - m.astype(F32)).astype(BF16)
    return h, m, l


def _cumsum_rows(x):
    c = x.shape[0]
    if c <= V7X_SUBLANES:
        row = lax.broadcasted_iota(jnp.int32, x.shape, 0)
        k = 1
        while k < c:
            x = x + jnp.where(row >= k, pltpu.roll(x, k, 0), 0.0)
            k *= 2
        return x
    h, m, l = _split3(x)
    if 3 * c <= V7X_MXU_DEPTH:
        col = lax.broadcasted_iota(jnp.int32, (c, 3 * c), 1)
        col = jnp.where(col >= 2 * c, col - 2 * c, jnp.where(col >= c, col - c, col))
        tri3 = (lax.broadcasted_iota(jnp.int32, (c, 3 * c), 0) >= col).astype(BF16)
        return _dot(tri3, jnp.concatenate([h, m, l], axis=0))
    tri = (lax.broadcasted_iota(jnp.int32, (c, c), 0) >= lax.broadcasted_iota(jnp.int32, (c, c), 1)).astype(BF16)
    return _dot(tri, h) + _dot(tri, m) + _dot(tri, l)


def _mm(a, b, dot=_dot, contract=1):
    if a.shape[contract] < 2 * V7X_SUBLANES:
        return dot(a, b)
    return dot(a.astype(BF16), b.astype(BF16))


def _prenorm_kernel(x_ref, g_ref, hg_ref, ssq_ref):
    x = x_ref[...]
    hg_ref[...] = (x * g_ref[...]).astype(BF16)
    ssq_ref[...] = jnp.sum(x * x, axis=-1, keepdims=True)


def prenorm(x, g_norm, layer, *, tm):
    m = x.shape[0]
    return pl.pallas_call(
        _prenorm_kernel,
        grid=(m // tm,),
        in_specs=[pl.BlockSpec((tm, D_MODEL), lambda i: (i, 0)),
                  pl.BlockSpec((None, 1, D_MODEL), lambda i: (layer, 0, 0))],
        out_specs=[pl.BlockSpec((tm, D_MODEL), lambda i: (i, 0)), pl.BlockSpec((tm, 1), lambda i: (i, 0))],
        out_shape=[jax.ShapeDtypeStruct((m, D_MODEL), BF16), jax.ShapeDtypeStruct((m, 1), F32)],
        compiler_params=_cparams(1),
        name="prenorm",
    )(x, g_norm)


def _in_proj_kernel(hg_ref, ssq_ref, w_ref, wt_ref, o_ref, ot_ref, *, seq_rows, seq_pad):
    r = lax.rsqrt(ssq_ref[...] * (1.0 / D_MODEL) + EPS)

    def put(dst_ref, val):
        if seq_pad == seq_rows:
            dst_ref[...] = val
        else:
            n_seq = val.shape[0] // seq_rows
            dst_ref[:, 0:seq_rows, :] = val.reshape(n_seq, seq_rows, val.shape[1])
            dst_ref[:, seq_rows:seq_pad, :] = jnp.zeros((n_seq, seq_pad - seq_rows, val.shape[1]), F32)

    @pl.when(pl.program_id(1) == 0)
    def _():
        put(ot_ref, _dot_nt(hg_ref[...], wt_ref[...]) * r)

    put(o_ref, _dot_nt(hg_ref[...], w_ref[...]) * r)


def in_proj(hg, ssq, w_main, w_tail, layer, *, tm, tn, seq_rows=None, seq_pad=None):
    m = hg.shape[0]
    grid = (m // tm, N_MAIN // tn)
    if seq_rows is None:
        seq_rows = seq_pad = 1
        out_specs = [pl.BlockSpec((tm, tn), lambda i, j: (i, j)), pl.BlockSpec((tm, TAIL_W), lambda i, j: (i, 0))]
        out_shape = [jax.ShapeDtypeStruct((m, N_MAIN), F32), jax.ShapeDtypeStruct((m, TAIL_W), F32)]
    else:
        n_seq = tm // seq_rows
        out_specs = [pl.BlockSpec((n_seq, seq_pad, tn), lambda i, j: (i, 0, j)),
                     pl.BlockSpec((n_seq, seq_pad, TAIL_W), lambda i, j: (i, 0, 0))]
        out_shape = [jax.ShapeDtypeStruct((m // seq_rows, seq_pad, N_MAIN), F32),
                     jax.ShapeDtypeStruct((m // seq_rows, seq_pad, TAIL_W), F32)]
    proj, tail = pl.pallas_call(
        functools.partial(_in_proj_kernel, seq_rows=seq_rows, seq_pad=seq_pad),
        grid=grid,
        in_specs=[
            pl.BlockSpec((tm, D_MODEL), lambda i, j: (i, 0)),
            pl.BlockSpec((tm, 1), lambda i, j: (i, 0)),
            pl.BlockSpec((None, tn, D_MODEL), lambda i, j: (layer, j, 0)),
            pl.BlockSpec((None, TAIL_W, D_MODEL), lambda i, j: (layer, 0, 0)),
        ],
        out_specs=out_specs,
        out_shape=out_shape,
        compiler_params=_cparams(2),
        name="in_proj",
    )(hg, ssq, w_main, w_tail)
    return proj.reshape(-1, N_MAIN), tail.reshape(-1, TAIL_W)


def _out_proj_kernel(ma_ref, mb_ref, mc_ref, md_ref, w_ref, h_ref, o_ref, ob_ref):
    acc = h_ref[...]
    for k, m_ref in enumerate((ma_ref, mb_ref, mc_ref, md_ref)):
        acc = acc + _dot(m_ref[...], w_ref[k * W_MIX:(k + 1) * W_MIX, :])
    o_ref[...] = acc
    ob_ref[...] = acc.astype(BF16)


def out_proj(mixes, w_out, h, layer, *, tm, tn):
    m = h.shape[0]
    grid = (m // tm, D_MODEL // tn)
    mix_spec = pl.BlockSpec((tm, W_MIX), lambda i, j: (i, 0))
    tile = pl.BlockSpec((tm, tn), lambda i, j: (i, j))
    return pl.pallas_call(
        _out_proj_kernel,
        grid=grid,
        in_specs=[mix_spec, mix_spec, mix_spec, mix_spec,
                  pl.BlockSpec((None, 4 * W_MIX, tn), lambda i, j: (layer, 0, j)), tile],
        out_specs=[tile, tile],
        out_shape=[jax.ShapeDtypeStruct((m, D_MODEL), F32), jax.ShapeDtypeStruct((m, D_MODEL), BF16)],
        compiler_params=_cparams(2, max(VMEM_LIMIT, 2 * (tm * D_MODEL * 2 + 4 * W_MIX * tn * 2 + tm * tn * 10)
                                        + VMEM_HEADROOM)),
        name="out_proj",
    )(*mixes, w_out, h)


def _ple_kernel(hrow_ref, htile_ref, pe_ref, wg_ref, bg_ref, wp_ref, gn_ref, o_ref, hg_ref, ssq_ref):
    tm = hrow_ref.shape[0]
    n_sub = PLE_ROW_SPLIT if tm % (PLE_ROW_SPLIT * 2 * V7X_SUBLANES) == 0 else 1
    parts = []
    for s in range(n_sub):
        rows = slice(s * (tm // n_sub), (s + 1) * (tm // n_sub))
        gate = _sigmoid(_dot(hrow_ref[rows, :], wg_ref[...]) + bg_ref[...])
        pv = _dot(pe_ref[rows, :], wp_ref[...])
        h = htile_ref[rows, :] + gate * pv
        o_ref[rows, :] = h
        hg_ref[rows, :] = (h * gn_ref[...]).astype(BF16)
        parts.append(jnp.sum(h * h, axis=-1, keepdims=True))
    part = parts[0] if n_sub == 1 else jnp.concatenate(parts, axis=0)

    @pl.when(pl.program_id(1) == 0)
    def _():
        ssq_ref[...] = part

    @pl.when(pl.program_id(1) != 0)
    def _():
        ssq_ref[...] = ssq_ref[...] + part


def ple(h, hb, pe, w_gate, b_gate, w_ple, g_next, layer, next_layer, *, tm, tn):
    m = h.shape[0]
    grid = (m // tm, D_MODEL // tn)
    tile = pl.BlockSpec((tm, tn), lambda i, j: (i, j))
    return pl.pallas_call(
        _ple_kernel,
        grid=grid,
        in_specs=[
            pl.BlockSpec((tm, D_MODEL), lambda i, j: (i, 0)),
            tile,
            pl.BlockSpec((None, tm, D_PLE), lambda i, j: (layer, i, 0)),
            pl.BlockSpec((None, D_MODEL, tn), lambda i, j: (layer, 0, j)),
            pl.BlockSpec((None, 1, tn), lambda i, j: (layer, 0, j)),
            pl.BlockSpec((None, D_PLE, tn), lambda i, j: (layer, 0, j)),
            pl.BlockSpec((None, 1, tn), lambda i, j: (next_layer, 0, j)),
        ],
        out_specs=[tile, tile, pl.BlockSpec((tm, 1), lambda i, j: (i, 0))],
        out_shape=[jax.ShapeDtypeStruct((m, D_MODEL), F32), jax.ShapeDtypeStruct((m, D_MODEL), BF16),
                   jax.ShapeDtypeStruct((m, 1), F32)],
        compiler_params=_cparams(2),
        name="ple",
    )(hb, h, pe, w_gate, b_gate, w_ple, g_next)


REPACK_ROWS = 512


def _repack_kernel(a_ref, g_ref, ab_ref, o_ref, t_ref):
    o_ref[...] = a_ref[...].astype(BF16)

    @pl.when(pl.program_id(1) == 0)
    def _():
        t_ref[...] = jnp.zeros(t_ref.shape, BF16)
        t_ref[TAIL_G:TAIL_G + GLA_RANK, :] = g_ref[...].astype(BF16)
        t_ref[TAIL_A:TAIL_A + 2 * DN_H, :] = ab_ref[...].astype(BF16)


def repack_w_in(w_in_t):
    nl = w_in_t.shape[0]
    n_tiles = N_MAIN // REPACK_ROWS
    t_qkvd, t_zc, t_zd = COL_QKVD // REPACK_ROWS, COL_ZC // REPACK_ROWS, COL_ZD // REPACK_ROWS

    def src_row(j):
        return jnp.where(j < t_qkvd, j * REPACK_ROWS,
                         jnp.where(j < t_zc, ORIG_QKVD + (j - t_qkvd) * REPACK_ROWS,
                                   jnp.where(j < t_zd, ORIG_ZC + (j - t_zc) * REPACK_ROWS,
                                             ORIG_ZD + (j - t_zd) * REPACK_ROWS)))

    row_align = 2 * V7X_SUBLANES
    assert all(o % row_align == 0 for o in (ORIG_QKVD, ORIG_ZC, ORIG_ZD, ORIG_GC, ORIG_AB))

    def rows(n, start):
        return pl.BlockSpec((None, pl.Element(n), pl.Element(D_MODEL)),
                            lambda l, j: (l, pl.multiple_of(start(j), row_align), 0))

    return pl.pallas_call(
        _repack_kernel,
        grid=(nl, n_tiles),
        in_specs=[rows(REPACK_ROWS, src_row), rows(GLA_RANK, lambda j: ORIG_GC), rows(2 * DN_H, lambda j: ORIG_AB)],
        out_specs=[pl.BlockSpec((None, REPACK_ROWS, D_MODEL), lambda l, j: (l, j, 0)),
                   pl.BlockSpec((None, TAIL_W, D_MODEL), lambda l, j: (l, 0, 0))],
        out_shape=[jax.ShapeDtypeStruct((nl, N_MAIN, D_MODEL), BF16),
                   jax.ShapeDtypeStruct((nl, TAIL_W, D_MODEL), BF16)],
        compiler_params=_cparams(2),
        name="repack_w_in",
    )(w_in_t, w_in_t, w_in_t)


def _final_norm_kernel(x_ref, g_ref, o_ref):
    x = x_ref[...]
    ms = jnp.mean(x * x, axis=-1, keepdims=True)
    o_ref[...] = (x * lax.rsqrt(ms + EPS)) * g_ref[...]


def final_norm(h, g, *, tm):
    m = h.shape[0]
    return pl.pallas_call(
        _final_norm_kernel,
        grid=(m // tm,),
        in_specs=[pl.BlockSpec((tm, D_MODEL), lambda i: (i, 0)),
                  pl.BlockSpec((1, D_MODEL), lambda i: (0, 0))],
        out_specs=pl.BlockSpec((tm, D_MODEL), lambda i: (i, 0)),
        out_shape=jax.ShapeDtypeStruct((m, D_MODEL), F32),
        compiler_params=_cparams(1),
        name="final_norm",
    )(h, g)


S5_SEG = 32


def _s5_prep_kernel(lre_ref, lim_ref, ldt_ref, coef_ref, cre_ref, cim_ref):
    lre = lre_ref[...]
    lim = lim_ref[...]
    dt = jnp.exp(ldt_ref[...])
    ai = lim * dt
    mag = jnp.exp(lre * dt)
    ar = mag * jnp.cos(ai)
    aim = mag * jnp.sin(ai)
    den = lre * lre + lim * lim
    nr = ar - 1.0
    cre_ref[...] = (nr * lre + aim * lim) / den
    cim_ref[...] = (aim * lre - nr * lim) / den
    pr, pim = [ar], [aim]
    for _ in range(S5_SEG - 1):
        nr_, ni_ = pr[-1] * ar - pim[-1] * aim, pr[-1] * aim + pim[-1] * ar
        pr.append(nr_)
        pim.append(ni_)
    shape = (V7X_SUBLANES, S5_CH)
    coef_ref[0] = jnp.broadcast_to(ar, shape)
    coef_ref[1] = jnp.broadcast_to(aim, shape)
    coef_ref[2] = jnp.broadcast_to(pr[S5_SEG - 1], shape)
    coef_ref[3] = jnp.broadcast_to(pim[S5_SEG - 1], shape)


def s5_prep(lam_re, lam_im, log_dt_rep):
    nl = lam_re.shape[0]
    vec = pl.BlockSpec((None, 1, S5_CH), lambda l: (l, 0, 0))
    return pl.pallas_call(
        _s5_prep_kernel,
        grid=(nl,),
        in_specs=[vec, vec, vec],
        out_specs=[pl.BlockSpec((None, 4, V7X_SUBLANES, S5_CH), lambda l: (l, 0, 0, 0)), vec, vec],
        out_shape=[jax.ShapeDtypeStruct((nl, 4, V7X_SUBLANES, S5_CH), F32),
                   jax.ShapeDtypeStruct((nl, 1, S5_CH), F32),
                   jax.ShapeDtypeStruct((nl, 1, S5_CH), F32)],
        compiler_params=_cparams(1),
        name="s5_prep",
    )(lam_re, lam_im, log_dt_rep)


def _s5_bb_kernel(cr_ref, ci_ref, br_ref, bi_ref, or_ref, oi_ref):
    cr = cr_ref[...]
    ci = ci_ref[...]
    br = br_ref[...]
    bi = bi_ref[...]
    or_ref[...] = cr * br - ci * bi
    oi_ref[...] = cr * bi + ci * br


def s5_bb(coef_re_col, coef_im_col, b_re, b_im):
    nl = b_re.shape[0]
    rows = 1024
    col = pl.BlockSpec((None, rows, 1), lambda l, r: (l, r, 0))
    mat = pl.BlockSpec((None, rows, S5_GROUP), lambda l, r: (l, r, 0))
    return pl.pallas_call(
        _s5_bb_kernel,
        grid=(nl, S5_CH // rows),
        in_specs=[col, col, mat, mat],
        out_specs=[mat, mat],
        out_shape=[jax.ShapeDtypeStruct((nl, S5_CH, S5_GROUP), F32)] * 2,
        compiler_params=_cparams(2),
        name="s5_bb",
    )(coef_re_col, coef_im_col, b_re, b_im)


S5_STRIP = 512
S5_U_GROUPS = V7X_MXU_DEPTH // S5_GROUP
S5_U_IN = S5_U_GROUPS * S5_GROUP
S5_U_OUT = S5_U_GROUPS * S5_STATE
S5_Y_GROUPS = S5_STRIP // S5_STATE
S5_WU_SHAPE = (S5_GROUPS // S5_U_GROUPS, S5_U_IN, S5_U_OUT)
S5_WY_SHAPE = (S5_CH // S5_STRIP, S5_STRIP, S5_Y_GROUPS * S5_GROUP)


def _s5_kernel(xa_ref, za_ref, h0r_ref, h0i_ref, coef_ref, perm_ref, permt_ref,
               wur_ref, wui_ref, wyr_ref, wyi_ref, dsk_ref, wglu_ref, bglu_ref,
               mix_ref, hro_ref, hio_ref,
               ur_scr, ui_scr, cr_scr, ci_scr, y_scr, *, nb, tt, t_last):
    tc = pl.program_id(1)
    ntc = pl.num_programs(1)
    split = nb == 1
    lc = tt // V7X_SUBLANES if split else tt
    n_sets = 1 if split else nb // V7X_SUBLANES
    set_rows = V7X_SUBLANES * lc

    xb16 = _dot(perm_ref[...], xa_ref[...].astype(BF16)).astype(BF16)
    for k in range(S5_GROUPS // S5_U_GROUPS):
        xk = xb16[:, S5_U_IN * k:S5_U_IN * (k + 1)]
        ur_scr[:, S5_U_OUT * k:S5_U_OUT * (k + 1)] = _dot(xk, wur_ref[k])
        ui_scr[:, S5_U_OUT * k:S5_U_OUT * (k + 1)] = _dot(xk, wui_ref[k])

    @pl.when(tc == 0)
    def _():
        cr_scr[...] = h0r_ref[...]
        ci_scr[...] = h0i_ref[...]

    row8 = lax.broadcasted_iota(jnp.int32, (V7X_SUBLANES, S5_STRIP), 0)
    for st in range(S5_CH // S5_STRIP):
        ls = slice(st * S5_STRIP, (st + 1) * S5_STRIP)
        ar = coef_ref[0, :, ls]
        ai = coef_ref[1, :, ls]
        for s in range(n_sets):
            base = s * set_rows
            if split:
                h0r = jnp.zeros((V7X_SUBLANES, S5_STRIP), F32)
                h0i = jnp.zeros((V7X_SUBLANES, S5_STRIP), F32)
            else:
                h0r = jnp.zeros((V7X_SUBLANES, S5_STRIP), F32)
                h0i = jnp.zeros((V7X_SUBLANES, S5_STRIP), F32)
                for k in range(V7X_SUBLANES):
                    h0r = jnp.where(row8 == k, cr_scr[s * V7X_SUBLANES + k, :, ls], h0r)
                    h0i = jnp.where(row8 == k, ci_scr[s * V7X_SUBLANES + k, :, ls], h0i)

            def step(j, h, base=base, ls=ls, ar=ar, ai=ai):
                hr, hi = h
                rows = pl.ds(pl.multiple_of(base + j * V7X_SUBLANES, V7X_SUBLANES), V7X_SUBLANES)
                nr = (ar * hr - ai * hi) + ur_scr[rows, ls]
                ni = (ar * hi + ai * hr) + ui_scr[rows, ls]
                ur_scr[rows, ls] = nr
                ui_scr[rows, ls] = ni
                return nr, ni

            fr, fi = lax.fori_loop(0, lc, step, (h0r, h0i), unroll=True)
            if split:
                alr = coef_ref[2, 0:1, ls]
                ali = coef_ref[3, 0:1, ls]
                cr = cr_scr[0, :, ls]
                ci = ci_scr[0, :, ls]
                init_r = jnp.zeros((V7X_SUBLANES, S5_STRIP), F32)
                init_i = jnp.zeros((V7X_SUBLANES, S5_STRIP), F32)
                for k in range(V7X_SUBLANES):
                    init_r = jnp.where(row8 == k, cr, init_r)
                    init_i = jnp.where(row8 == k, ci, init_i)
                    cr, ci = (alr * cr - ali * ci) + fr[k:k + 1], (alr * ci + ali * cr) + fi[k:k + 1]
                cr_scr[0, :, ls] = cr
                ci_scr[0, :, ls] = ci

                def fix(j, c, base=base, ls=ls, ar=ar, ai=ai):
                    rows = pl.ds(pl.multiple_of(base + j * V7X_SUBLANES, V7X_SUBLANES), V7X_SUBLANES)
                    c_r = ar * c[0] - ai * c[1]
                    c_i = ar * c[1] + ai * c[0]
                    ur_scr[rows, ls] = ur_scr[rows, ls] + c_r
                    ui_scr[rows, ls] = ui_scr[rows, ls] + c_i
                    return c_r, c_i

                lax.fori_loop(0, lc, fix, (init_r, init_i), unroll=True)

        y_scr[:, V7X_LANES * st:V7X_LANES * (st + 1)] = (_dot(ur_scr[:, ls].astype(BF16), wyr_ref[st])
                                                           - _dot(ui_scr[:, ls].astype(BF16), wyi_ref[st]))
    yh, ym, yl = _split3(y_scr[...])
    pt = permt_ref[...]
    y = (_dot(pt, yh) + _dot(pt, ym) + _dot(pt, yl)) + dsk_ref[...] * xa_ref[...]
    ga = _gelu_tanh(y)
    ya = ga * _sigmoid(_dot(ga.astype(BF16), wglu_ref[...]) + bglu_ref[...])
    mix_ref[...] = (ya * _silu(za_ref[...])).astype(BF16)

    @pl.when(tc == ntc - 1)
    def _():
        if split:
            hro_ref[...] = cr_scr[...]
            hio_ref[...] = ci_scr[...]
        else:
            for s in range(n_sets):
                for k in range(V7X_SUBLANES):
                    r = s * set_rows + V7X_SUBLANES * t_last + k
                    hro_ref[s * V7X_SUBLANES + k] = ur_scr[r:r + 1, :]
                    hio_ref[s * V7X_SUBLANES + k] = ui_scr[r:r + 1, :]


def s5_mixer(proj, h0_re, h0_im, coef, wur, wui, wyr, wyi, dskip, wglu, bglu, layer, state_layer,
             *, batch, seq, t_valid, nb, tt):
    ntc = seq // tt
    assert nb == 1 or ntc == 1
    rows = nb * tt
    grid = (batch // nb, ntc)
    t_last = (t_valid - 1) - (ntc - 1) * tt
    assert 0 <= t_last < tt
    if nb == 1:
        assert tt == V7X_SUBLANES * S5_SEG and t_last == tt - 1
        lc = S5_SEG
    else:
        assert nb % V7X_SUBLANES == 0
        lc = tt
    idx = np.arange(rows)
    set_rows = V7X_SUBLANES * lc
    src = (idx // set_rows) * set_rows + (idx % V7X_SUBLANES) * lc + (idx % set_rows) // V7X_SUBLANES
    perm_np = np.zeros((rows, rows), np.float32)
    perm_np[idx, src] = 1.0
    perm = jnp.asarray(perm_np, BF16)
    perm_t = jnp.asarray(perm_np.T, BF16)
    sq = pl.BlockSpec((rows, rows), lambda b, t: (0, 0))

    def tok(cb):
        return pl.BlockSpec((rows, W_MIX), lambda b, t: (b * ntc + t, cb))

    st_in = pl.BlockSpec((None, nb, 1, S5_CH), lambda b, t: (state_layer, b, 0, 0))
    st_out = pl.BlockSpec((nb, 1, S5_CH), lambda b, t: (b, 0, 0))

    def lw(shape):
        nd = len(shape)
        return pl.BlockSpec((None,) + shape, lambda b, t: (layer,) + (0,) * nd)

    return pl.pallas_call(
        functools.partial(_s5_kernel, nb=nb, tt=tt, t_last=t_last),
        grid=grid,
        in_specs=[tok(COL_XA // W_MIX), tok(COL_ZA // W_MIX), st_in, st_in,
                  lw((4, V7X_SUBLANES, S5_CH)), sq, sq, lw(S5_WU_SHAPE), lw(S5_WU_SHAPE),
                  lw(S5_WY_SHAPE), lw(S5_WY_SHAPE), lw((1, W_MIX)), lw((W_MIX, W_MIX)), lw((1, W_MIX))],
        out_specs=[pl.BlockSpec((rows, W_MIX), lambda b, t: (b * ntc + t, 0)), st_out, st_out],
        out_shape=[jax.ShapeDtypeStruct((batch * seq, W_MIX), BF16),
                   jax.ShapeDtypeStruct((batch, 1, S5_CH), F32),
                   jax.ShapeDtypeStruct((batch, 1, S5_CH), F32)],
        scratch_shapes=[pltpu.VMEM((rows, S5_CH), F32), pltpu.VMEM((rows, S5_CH), F32),
                        pltpu.VMEM((nb, 1, S5_CH), F32), pltpu.VMEM((nb, 1, S5_CH), F32),
                        pltpu.VMEM((rows, W_MIX), F32)],
        compiler_params=_cparams(2),
        name="s5_mixer",
    )(proj, proj, h0_re, h0_im, coef, perm, perm_t, wur, wui, wyr, wyi, dskip, wglu, bglu)


LRU_STRIP = 512
CONV_PAD = V7X_SUBLANES


def _causal_conv(tail_scr, x, cw_ref, bi):
    rows_n = x.shape[0]
    width = x.shape[1]
    n_tiles = rows_n // V7X_SUBLANES
    xe = jnp.concatenate([tail_scr[bi], x], axis=0).reshape(n_tiles + 1, V7X_SUBLANES, width)
    row = lax.broadcasted_iota(jnp.int32, (n_tiles, V7X_SUBLANES, width), 1)
    acc = cw_ref[CONV_W - 1:CONV_W, :] * x
    for jj in range(CONV_W - 1):
        shift = CONV_W - 1 - jj
        rot = pltpu.roll(xe, shift, 1)
        shifted = jnp.where(row < shift, rot[:-1], rot[1:]).reshape(rows_n, width)
        acc = acc + cw_ref[jj:jj + 1, :] * shifted
    tail_scr[bi] = xe[n_tiles]
    return acc


def _init_conv_tail(tail_scr, cv0_ref, nb):
    tail_scr[...] = jnp.zeros(tail_scr.shape, F32)
    for bi in range(nb):
        tail_scr[bi, CONV_PAD - (CONV_W - 1):CONV_PAD, :] = cv0_ref[bi]


def _lru_kernel(xb_ref, zb_ref, h0_ref, cv0_ref, cw_ref, cb_ref, wri_ref, br_ref, bi_ref, lam_ref,
                mix_ref, ho_ref,
                tail_scr, xc_scr, a_scr, b_scr, c_scr, *, nb, tt, t_last):
    tc = pl.program_id(1)
    ntc = pl.num_programs(1)

    @pl.when(tc == 0)
    def _():
        c_scr[...] = h0_ref[...]
        _init_conv_tail(tail_scr, cv0_ref, nb)

    def conv_body(bi, carry):
        rows = pl.ds(pl.multiple_of(bi * tt, V7X_SUBLANES), tt)
        acc = _causal_conv(tail_scr, xb_ref[rows, :], cw_ref, bi)
        xc_scr[rows, :] = acc + cb_ref[...]
        return carry

    lax.fori_loop(0, nb, conv_body, 0)

    xc = xc_scr[...]
    xc16 = xc.astype(BF16)
    sp = _softplus(-lam_ref[...])
    for blk in range(LRU_BLOCKS):
        cs = slice(blk * LRU_BLK, (blk + 1) * LRU_BLK)
        pre = _dot(xc16[:, cs], wri_ref[blk])
        r = _sigmoid(pre[:, :LRU_BLK] + br_ref[:, cs])
        ig = _sigmoid(pre[:, LRU_BLK:] + bi_ref[:, cs])
        log_a = (-LRU_C) * r * sp[:, cs]
        a_scr[:, cs] = jnp.exp(log_a)
        b_scr[:, cs] = jnp.sqrt(_neg_expm1_nonpos(2.0 * log_a)) * (ig * xc[:, cs])

    ng = tt // V7X_SUBLANES
    for s in range(W_MIX // LRU_STRIP):
        ls = slice(s * LRU_STRIP, (s + 1) * LRU_STRIP)
        row = lax.broadcasted_iota(jnp.int32, (V7X_SUBLANES, LRU_STRIP), 0)

        def seq_body(bi, carry, ls=ls, row=row):
            def tile_body(g, c):
                rows = pl.ds(pl.multiple_of(bi * tt + g * V7X_SUBLANES, V7X_SUBLANES), V7X_SUBLANES)
                a = a_scr[rows, ls]
                b = b_scr[rows, ls]
                for k in (1, 2, 4):
                    a_sh = jnp.where(row >= k, pltpu.roll(a, k, 0), 1.0)
                    b_sh = jnp.where(row >= k, pltpu.roll(b, k, 0), 0.0)
                    b = b + a * b_sh
                    a = a * a_sh
                h = b + a * c
                b_scr[rows, ls] = h
                return h[V7X_SUBLANES - 1:, :]

            c = lax.fori_loop(0, ng, tile_body, c_scr[bi, :, ls])
            c_scr[bi, :, ls] = c
            return carry

        lax.fori_loop(0, nb, seq_body, 0)

    mix_ref[...] = (b_scr[...] * _silu(zb_ref[...])).astype(BF16)

    @pl.when(tc == ntc - 1)
    def _():
        for bi in range(nb):
            r = bi * tt + t_last
            ho_ref[bi] = b_scr[r:r + 1, :]


def lru_mixer(proj, h0, conv0, cw, cb, wri, br, bi_, lam, layer, state_layer, *, batch, seq, t_valid, nb, tt):
    ntc = seq // tt
    assert nb == 1 or ntc == 1
    rows = nb * tt
    grid = (batch // nb, ntc)
    t_last = (t_valid - 1) - (ntc - 1) * tt

    def tok(cb_):
        return pl.BlockSpec((rows, W_MIX), lambda b, t: (b * ntc + t, cb_))

    def lw(shape):
        nd = len(shape)
        return pl.BlockSpec((None,) + shape, lambda b, t: (layer,) + (0,) * nd)

    return pl.pallas_call(
        functools.partial(_lru_kernel, nb=nb, tt=tt, t_last=t_last),
        grid=grid,
        in_specs=[tok(COL_XB // W_MIX), tok(COL_ZB // W_MIX),
                  pl.BlockSpec((None, nb, 1, W_MIX), lambda b, t: (state_layer, b, 0, 0)),
                  pl.BlockSpec((None, nb, CONV_W - 1, W_MIX), lambda b, t: (state_layer, b, 0, 0)),
                  lw((CONV_W, W_MIX)), lw((1, W_MIX)), lw((LRU_BLOCKS, LRU_BLK, 2 * LRU_BLK)),
                  lw((1, W_MIX)), lw((1, W_MIX)), lw((1, W_MIX))],
        out_specs=[pl.BlockSpec((rows, W_MIX), lambda b, t: (b * ntc + t, 0)),
                   pl.BlockSpec((nb, 1, W_MIX), lambda b, t: (b, 0, 0))],
        out_shape=[jax.ShapeDtypeStruct((batch * seq, W_MIX), BF16),
                   jax.ShapeDtypeStruct((batch, 1, W_MIX), F32)],
        scratch_shapes=[pltpu.VMEM((nb, CONV_PAD, W_MIX), F32),
                        pltpu.VMEM((rows, W_MIX), F32), pltpu.VMEM((rows, W_MIX), F32),
                        pltpu.VMEM((rows, W_MIX), F32), pltpu.VMEM((nb, 1, W_MIX), F32)],
        compiler_params=_cparams(2),
        name="lru_mixer",
    )(proj, proj, h0, conv0, cw, cb, wri, br, bi_, lam)


def _for_each_group(nb, g, group_body):
    assert nb % g == 0
    if nb == g:
        group_body(list(range(nb)))
    else:
        def body(i, carry):
            group_body([i * g + j for j in range(g)])
            return carry
        lax.fori_loop(0, nb // g, body, 0)


def _gla_kernel(q_ref, k_ref, v_ref, z_ref, tail_ref, s0_ref, wg_ref, bg_ref, go_ref, stack_ref,
                mix_ref, so_ref,
                s_scr, mix_scr, *, nb, group, c, cv, single_chunk):
    del stack_ref
    tc = pl.program_id(1)
    ntc = pl.num_programs(1)
    sb = min(GLA_SUB, c)
    nblk = c // sb

    s_src, s_dst = (s0_ref, so_ref) if single_chunk else (s_scr, s_scr)
    if not single_chunk:
        @pl.when(tc == 0)
        def _():
            s_scr[...] = s0_ref[...]

    row_c = lax.broadcasted_iota(jnp.int32, (c, 1), 0)
    lane_c = lax.broadcasted_iota(jnp.int32, (sb, c), 1)
    row_sb = lax.broadcasted_iota(jnp.int32, (sb, 1), 0)

    def group_body(bis):
        pairs = [(g, h) for g in range(len(bis)) for h in range(GLA_H)]
        heads = range(len(pairs))
        b_all, q_all, k_all, v_all = [], [], [], []
        for bi in bis:
            x = _dot(tail_ref[bi].astype(BF16), wg_ref[...]) + bg_ref[...]
            b_all.append(_cumsum_rows(_log_sigmoid(x) * (1.0 / GLA_TAU)))
            q_all.append(q_ref[bi])
            k_all.append(k_ref[bi])
            v_all.append(v_ref[bi])
        s_old = [s_src[bis[g], h] for g, h in pairs]
        q = [q_all[g][:, h * GLA_DK:(h + 1) * GLA_DK] * (GLA_DK ** -0.5) for g, h in pairs]
        k = [k_all[g][:, h * GLA_DK:(h + 1) * GLA_DK] for g, h in pairs]
        v = [v_all[g][:, h * GLA_DV:(h + 1) * GLA_DV] for g, h in pairs]
        b = [b_all[g][:, h * GLA_DK:(h + 1) * GLA_DK] for g, h in pairs]
        o_state = [_dot((q[h] * jnp.exp(b[h])).astype(BF16), s_old[h].astype(BF16)) for h in heads]
        s_new = []
        for h in heads:
            b_last = b[h][cv - 1:cv]
            dec = b_last - b[h]
            if cv < c:
                dec = jnp.where(row_c < cv, dec, NEG_INF)
            kd = k[h] * jnp.exp(dec)
            d_col = jnp.transpose(jnp.broadcast_to(jnp.exp(b_last), (V7X_SUBLANES, GLA_DK)))[:, 0:1]
            s_new.append(d_col * s_old[h] + _mm(kd, v[h], dot=_dot_tn, contract=0))
        att_off = []
        for h in heads:
            per_blk = [jnp.zeros((sb, c), F32)]
            for blk in range(1, nblk):
                r0 = blk * sb
                b_ref_row = b[h][r0 - 1:r0]
                qs = (q[h][r0:r0 + sb] * jnp.exp(b[h][r0:r0 + sb] - b_ref_row)).astype(BF16)
                kd = (k[h] * jnp.exp(jnp.where(row_c < r0, b_ref_row - b[h], NEG_INF))).astype(BF16)
                per_blk.append(_dot_nt(qs, kd))
            att_off.append(per_blk)
        att = []
        for h in heads:
            att_rows = []
            for blk in range(nblk):
                r0 = blk * sb
                q_i = q[h][r0:r0 + sb]
                b_i = b[h][r0:r0 + sb]
                a = att_off[h][blk]
                for sl in range(sb):
                    s_abs = r0 + sl
                    e = jnp.exp(jnp.where(row_sb >= sl, b_i - b[h][s_abs:s_abs + 1], NEG_INF))
                    col = jnp.sum(q_i * (k[h][s_abs:s_abs + 1] * e), axis=-1, keepdims=True)
                    a = jnp.where(lane_c == s_abs, col, a)
                att_rows.append(a)
            att.append(att_rows[0] if nblk == 1 else jnp.concatenate(att_rows, axis=0))
        o = [_mm(att[h], v[h]) + o_state[h] for h in heads]
        for p, (g, h) in enumerate(pairs):
            vs = slice(h * GLA_DV, (h + 1) * GLA_DV)
            ms = jnp.mean(o[p] * o[p], axis=-1, keepdims=True)
            on = (o[p] * lax.rsqrt(ms + EPS)) * go_ref[...]
            mix_scr[bis[g], :, vs] = on * _silu(z_ref[bis[g], :, vs])
            s_dst[bis[g], h] = s_new[p]

    _for_each_group(nb, group, group_body)
    mix_ref[...] = mix_scr[...].astype(BF16)

    if not single_chunk:
        @pl.when(tc == ntc - 1)
        def _():
            so_ref[...] = s_scr[...]


def _stack_alias(stack, n_inputs_before):
    spec = pl.BlockSpec(memory_space=pl.ANY)
    if stack is None:
        return jnp.zeros((V7X_SUBLANES, V7X_LANES), F32), spec, {}
    return stack, spec, {n_inputs_before: 1}


def gla_mixer(proj, tail, s0, wg, bg, go, layer, state_layer, *, batch, seq, c, cv, nb, group, n_layers, stack=None):
    ntc = seq // c
    assert batch % nb == 0 and seq % c == 0
    grid = (batch // nb, ntc)
    proj3 = proj.reshape(batch, seq, proj.shape[-1])
    tail3 = tail.reshape(batch, seq, TAIL_W)

    def tok(width, col):
        return pl.BlockSpec((nb, c, width), lambda b, t: (b, t, col // width))

    def lw(shape):
        nd = len(shape)
        return pl.BlockSpec((None,) + shape, lambda b, t: (layer,) + (0,) * nd)

    st_shape = (nb, GLA_H, GLA_DK, GLA_DV)
    stack_arg, stack_spec, aliases = _stack_alias(stack, 9)
    mix, state = pl.pallas_call(
        functools.partial(_gla_kernel, nb=nb, group=group, c=c, cv=cv, single_chunk=ntc == 1),
        grid=grid,
        in_specs=[tok(GLA_H * GLA_DK, COL_QC), tok(GLA_H * GLA_DK, COL_KC), tok(W_MIX, COL_VC), tok(W_MIX, COL_ZC),
                  tok(TAIL_W, 0),
                  pl.BlockSpec((None,) + st_shape, lambda b, t: (state_layer, b, 0, 0, 0)),
                  lw((TAIL_W, GLA_H * GLA_DK)), lw((1, GLA_H * GLA_DK)), lw((1, GLA_DV)), stack_spec],
        out_specs=[tok(W_MIX, 0),
                   pl.BlockSpec((None,) + st_shape, lambda b, t: (layer, b, 0, 0, 0))],
        out_shape=[jax.ShapeDtypeStruct((batch, seq, W_MIX), BF16),
                   jax.ShapeDtypeStruct((n_layers, batch, GLA_H, GLA_DK, GLA_DV), F32)],
        scratch_shapes=[pltpu.VMEM(st_shape, F32), pltpu.VMEM((nb, c, W_MIX), F32)],
        input_output_aliases=aliases,
        compiler_params=_cparams(2),
        name="gla_mixer",
    )(proj3, proj3, proj3, proj3, tail3, s0, wg, bg, go, stack_arg)
    return mix.reshape(batch * seq, W_MIX), state


def _l2norm(x):
    return x * lax.rsqrt(jnp.sum(x * x, axis=-1, keepdims=True) + EPS)


def _unit_lower_solve(ms, rhss, c, cv):
    if c <= V7X_SUBLANES:
        row = lax.broadcasted_iota(jnp.int32, (c, 1), 0)
        us = []
        for m, rhs in zip(ms, rhss):
            u = jnp.zeros_like(rhs)
            u_rows = []
            for t in range(cv):
                ut = rhs[t:t + 1]
                for s in range(t):
                    ut = ut - m[t:t + 1, s:s + 1] * u_rows[s]
                u_rows.append(ut)
                u = jnp.where(row == t, ut, u)
            us.append(u)
        return us
    xs = [-m for m in ms]
    ys = list(rhss)
    dv = rhss[0].shape[1]
    levels = int(math.log2(c))
    assert 3 * c <= V7X_MXU_DEPTH

    def hi_lo(t):
        hi = t.astype(BF16)
        return hi, (t - hi.astype(F32)).astype(BF16)

    for lvl in range(levels):
        last = lvl == levels - 1
        ps = []
        for x, y in zip(xs, ys):
            xh, xl = hi_lo(x)
            yh, yl = hi_lo(y)
            bh = yh if last else jnp.concatenate([yh, xh], axis=1)
            bl = yl if last else jnp.concatenate([yl, xl], axis=1)
            lhs = jnp.concatenate([xh.astype(F32), xh.astype(F32), xl.astype(F32)], axis=1).astype(BF16)
            ps.append(_dot(lhs, jnp.concatenate([bh, bl, bh], axis=0)))
        ys = [y + p[:, :dv] for y, p in zip(ys, ps)]
        if not last:
            xs = [p[:, dv:] for p in ps]
    return ys


def _gdn_kernel(qkv_ref, z_ref, tail_ref, s0_ref, cv0_ref, cw_ref, alog_ref, dtb_ref, go_ref, stack_ref,
                mix_ref, so_ref,
                s_scr, tail_scr, mix_scr, *, nb, group, c, cv, single_chunk):
    del stack_ref
    tc = pl.program_id(1)
    ntc = pl.num_programs(1)

    s_src, s_dst = (s0_ref, so_ref) if single_chunk else (s_scr, s_scr)

    @pl.when(tc == 0)
    def _():
        if not single_chunk:
            s_scr[...] = s0_ref[...]
        _init_conv_tail(tail_scr, cv0_ref, nb)

    row_c = lax.broadcasted_iota(jnp.int32, (c, 1), 0)
    ri = lax.broadcasted_iota(jnp.int32, (c, c), 0)
    ci = lax.broadcasted_iota(jnp.int32, (c, c), 1)
    lane_t = lax.broadcasted_iota(jnp.int32, (c, TAIL_W), 1)
    a_lanes = (lane_t >= TAIL_A) & (lane_t < TAIL_A + DN_H)

    def group_body(bis):
        pairs = [(g, h) for g in range(len(bis)) for h in range(DN_H)]
        heads = range(len(pairs))
        qkv, gam, gam_t, beta = [], [], [], []
        for gi, bi in enumerate(bis):
            qkv.append(_silu(_causal_conv(tail_scr, qkv_ref[bi], cw_ref, bi)))
            tl = tail_ref[bi]
            gg = _cumsum_rows(jnp.where(a_lanes, -jnp.exp(alog_ref[...]) * _softplus(tl + dtb_ref[...]), 0.0))
            gam.append(gg)
            gam_t.append(jnp.transpose(gg))
            beta.append(_sigmoid(tl))
        s_old = [s_src[bis[g], h] for g, h in pairs]
        s16 = [s.astype(BF16) for s in s_old]
        q16, k16, kf, v = [], [], [], []
        for g, h in pairs:
            q = _l2norm(qkv[g][:, h * DN_DK:(h + 1) * DN_DK]) * (DN_DK ** -0.5)
            k = _l2norm(qkv[g][:, DN_H * DN_DK + h * DN_DK:DN_H * DN_DK + (h + 1) * DN_DK])
            q16.append(q.astype(BF16))
            k16.append(k.astype(BF16))
            kf.append(k)
            v.append(qkv[g][:, 2 * DN_H * DN_DK + h * DN_DV:2 * DN_H * DN_DK + (h + 1) * DN_DV])
        g_col = [gam[g][:, TAIL_A + h:TAIL_A + h + 1] for g, h in pairs]
        g_row = [gam_t[g][TAIL_A + h:TAIL_A + h + 1, :] for g, h in pairs]
        b_col = [beta[g][:, TAIL_B + h:TAIL_B + h + 1] for g, h in pairs]
        decay = [jnp.exp(jnp.where(ri >= ci, g_col[h] - g_row[h], NEG_INF)) for h in heads]
        eg = [jnp.exp(g_col[h]) for h in heads]
        kk = [_dot_nt(k16[h], k16[h]) for h in heads]
        k_s = [_dot(k16[h], s16[h]) for h in heads]
        q_s = [_dot(q16[h], s16[h]) for h in heads]
        qk = [_dot_nt(q16[h], k16[h]) for h in heads]
        m = [jnp.where(ri > ci, b_col[h] * kk[h] * decay[h], 0.0) for h in heads]
        rhs = [b_col[h] * (v[h] - eg[h] * k_s[h]) for h in heads]
        u = _unit_lower_solve(m, rhs, c, cv)
        o = [eg[h] * q_s[h] + _mm(qk[h] * decay[h], u[h]) for h in heads]
        kd = []
        for h in heads:
            dec = g_col[h][cv - 1:cv] - g_col[h]
            if cv < c:
                dec = jnp.where(row_c < cv, dec, NEG_INF)
            kd.append(kf[h] * jnp.exp(dec))
        s_new = [jnp.exp(g_col[h][cv - 1:cv]) * s_old[h] + _mm(kd[h], u[h], dot=_dot_tn, contract=0) for h in heads]
        for p, (g, h) in enumerate(pairs):
            vs = slice(h * DN_DV, (h + 1) * DN_DV)
            ms = jnp.mean(o[p] * o[p], axis=-1, keepdims=True)
            on = (o[p] * lax.rsqrt(ms + EPS)) * go_ref[...]
            mix_scr[bis[g], :, vs] = on * _silu(z_ref[bis[g], :, vs])
            s_dst[bis[g], h] = s_new[p]

    _for_each_group(nb, group, group_body)
    mix_ref[...] = mix_scr[...].astype(BF16)

    if not single_chunk:
        @pl.when(tc == ntc - 1)
        def _():
            so_ref[...] = s_scr[...]


def gdn_mixer(proj, tail, s0, conv0, cw, alog, dtb, go, layer, state_layer, *, batch, seq, c, cv, nb, group, n_layers,
              stack=None):
    ntc = seq // c
    assert batch % nb == 0 and seq % c == 0
    grid = (batch // nb, ntc)
    proj3 = proj.reshape(batch, seq, proj.shape[-1])
    tail3 = tail.reshape(batch, seq, TAIL_W)

    def tok(width, col):
        return pl.BlockSpec((nb, c, width), lambda b, t: (b, t, col // width))

    def lw(shape):
        nd = len(shape)
        return pl.BlockSpec((None,) + shape, lambda b, t: (layer,) + (0,) * nd)

    st_shape = (nb, DN_H, DN_DK, DN_DV)
    assert COL_QKVD % DN_QKV == 0
    stack_arg, stack_spec, aliases = _stack_alias(stack, 9)
    mix, state = pl.pallas_call(
        functools.partial(_gdn_kernel, nb=nb, group=group, c=c, cv=cv, single_chunk=ntc == 1),
        grid=grid,
        in_specs=[tok(DN_QKV, COL_QKVD),
                  tok(W_MIX, COL_ZD),
                  tok(TAIL_W, 0),
                  pl.BlockSpec((None,) + st_shape, lambda b, t: (state_layer, b, 0, 0, 0)),
                  pl.BlockSpec((None, nb, CONV_W - 1, DN_QKV), lambda b, t: (state_layer, b, 0, 0)),
                  lw((CONV_W, DN_QKV)), lw((1, TAIL_W)), lw((1, TAIL_W)), lw((1, DN_DV)), stack_spec],
        out_specs=[tok(W_MIX, 0),
                   pl.BlockSpec((None,) + st_shape, lambda b, t: (layer, b, 0, 0, 0))],
        out_shape=[jax.ShapeDtypeStruct((batch, seq, W_MIX), BF16),
                   jax.ShapeDtypeStruct((n_layers, batch, DN_H, DN_DK, DN_DV), F32)],
        input_output_aliases=aliases,
        scratch_shapes=[pltpu.VMEM(st_shape, F32),
                        pltpu.VMEM((nb, CONV_PAD, DN_QKV), F32),
                        pltpu.VMEM((nb, c, W_MIX), F32)],
        compiler_params=_cparams(2),
        name="gdn_mixer",
    )(proj3, proj3, tail3, s0, conv0, cw, alog, dtb, go, stack_arg)
    return mix.reshape(batch * seq, W_MIX), state


def _prepare_weights(w_in, s5_lam_re, s5_lam_im, s5_log_dt, s5_b_re, s5_b_im, s5_c_re, s5_c_im, s5_d,
                     s5_w_glu, lru_w_r, lru_w_i, gla_w_gate, dn_a_log, dn_dt_bias, w_out, ple_w, ple_gate_w):
    nl = w_in.shape[0]
    p = {}
    p["w_main"], p["w_tail"] = repack_w_in(jnp.swapaxes(w_in, 1, 2))
    p["w_out"] = w_out.astype(BF16)
    p["ple_w"] = ple_w.astype(BF16)
    p["ple_gate_w"] = ple_gate_w.astype(BF16)
    p["w_glu"] = s5_w_glu.astype(BF16)
    ldt = jnp.repeat(s5_log_dt, S5_STATE, axis=1).reshape(nl, 1, S5_CH)
    coef, cre, cim = s5_prep(s5_lam_re.reshape(nl, 1, S5_CH), s5_lam_im.reshape(nl, 1, S5_CH), ldt)
    bb_re, bb_im = s5_bb(cre.reshape(nl, S5_CH, 1), cim.reshape(nl, S5_CH, 1),
                         s5_b_re.reshape(nl, S5_CH, S5_GROUP), s5_b_im.reshape(nl, S5_CH, S5_GROUP))
    eye16 = jnp.eye(S5_U_GROUPS, dtype=F32)
    eye8 = jnp.eye(S5_Y_GROUPS, dtype=F32)

    def pack_u(bb):
        t = bb.reshape(nl, S5_GROUPS // S5_U_GROUPS, S5_U_GROUPS, S5_STATE, S5_GROUP)
        return jnp.einsum("lkgph,gG->lkghGp", t, eye16).reshape((nl,) + S5_WU_SHAPE).astype(BF16)

    def pack_y(cc):
        t = cc.reshape(nl, S5_GROUPS // S5_Y_GROUPS, S5_Y_GROUPS, S5_GROUP, S5_STATE)
        return jnp.einsum("ljghp,gG->ljgpGh", t, eye8).reshape((nl,) + S5_WY_SHAPE).astype(BF16)

    p["s5_coef"] = coef
    p["s5_wur"], p["s5_wui"] = pack_u(bb_re), pack_u(bb_im)
    p["s5_wyr"], p["s5_wyi"] = pack_y(s5_c_re), pack_y(s5_c_im)
    p["s5_d"] = s5_d.reshape(nl, 1, W_MIX)
    p["lru_wri"] = jnp.concatenate([lru_w_r, lru_w_i], axis=-1).astype(BF16)
    p["gla_wg"] = jnp.concatenate([gla_w_gate, jnp.zeros((nl, TAIL_W - GLA_RANK, GLA_H * GLA_DK), F32)],
                                  axis=1).astype(BF16)

    def tail_row(x):
        return jnp.pad(x, ((0, 0), (TAIL_A, TAIL_W - TAIL_A - DN_H))).reshape(nl, 1, TAIL_W)

    p["dn_alog"] = tail_row(dn_a_log)
    p["dn_dtb"] = tail_row(dn_dt_bias)
    return p


def kernel(x_prompt, x_sample, state_s5_re, state_s5_im, state_lru_h, state_lru_conv, state_gla, state_delta, state_delta_conv, p_prompt, p_sample, g_norm, w_in, s5_lam_re, s5_lam_im, s5_log_dt, s5_b_re, s5_b_im, s5_c_re, s5_c_im, s5_d, s5_w_glu, s5_b_glu, lru_conv_w, lru_conv_b, lru_w_r, lru_b_r, lru_w_i, lru_b_i, lru_lam, gla_w_gate, gla_b_gate, gla_g_out, dn_conv_w, dn_a_log, dn_dt_bias, dn_g_out, w_out, ple_w, ple_gate_w, ple_gate_b, g_final):
    nl = w_in.shape[0]
    bp, tp, _ = x_prompt.shape
    bs, ts, _ = x_sample.shape
    tsp = SAMPLE_T_PAD
    mp, ms = bp * tp, bs * ts

    p = _prepare_weights(w_in, s5_lam_re, s5_lam_im, s5_log_dt, s5_b_re, s5_b_im, s5_c_re, s5_c_im, s5_d,
                         s5_w_glu, lru_w_r, lru_w_i, gla_w_gate, dn_a_log, dn_dt_bias, w_out, ple_w, ple_gate_w)

    def vec(x):
        return x.reshape(nl, 1, x.shape[-1])

    g_norm3, b_glu3, cb3 = vec(g_norm), vec(s5_b_glu), vec(lru_conv_b)
    br3, bi3, lam3 = vec(lru_b_r), vec(lru_b_i), vec(lru_lam)
    bg3, go_c3, go_d3, pgb3 = vec(gla_b_gate), vec(gla_g_out), vec(dn_g_out), vec(ple_gate_b)
    pe_p = p_prompt.reshape(nl, mp, D_PLE).astype(BF16)
    pe_s = p_sample.reshape(nl, ms, D_PLE).astype(BF16)

    z_s5 = jnp.zeros((1, bp, 1, S5_CH), F32)
    z_lru = jnp.zeros((1, bp, 1, W_MIX), F32)
    z_lconv = jnp.zeros((1, bp, CONV_W - 1, W_MIX), F32)
    z_gla = jnp.zeros((1, bp, GLA_H, GLA_DK, GLA_DV), F32)
    z_dn = jnp.zeros((1, bp, DN_H, DN_DK, DN_DV), F32)
    z_dconv = jnp.zeros((1, bp, CONV_W - 1, DN_QKV), F32)
    c_s5r = state_s5_re.reshape(nl, bs, 1, S5_CH)
    c_s5i = state_s5_im.reshape(nl, bs, 1, S5_CH)
    c_lru = state_lru_h.reshape(nl, bs, 1, W_MIX)

    def mixers(proj, tail, layer, *, batch, seq, t_valid, st, sl, nb_scan, tt, c, gla_grp, nb_gdn, gdn_grp, stacks):
        s5r0, s5i0, lru0, lconv0, gla0, dn0, dconv0 = st
        gla_stack, dn_stack = stacks
        mix_a, s5r, s5i = s5_mixer(proj, s5r0, s5i0, p["s5_coef"], p["s5_wur"], p["s5_wui"], p["s5_wyr"],
                                   p["s5_wyi"], p["s5_d"], p["w_glu"], b_glu3, layer, sl,
                                   batch=batch, seq=seq, t_valid=t_valid, nb=nb_scan, tt=tt)
        mix_b, lruh = lru_mixer(proj, lru0, lconv0, lru_conv_w, cb3, p["lru_wri"], br3, bi3, lam3, layer, sl,
                                batch=batch, seq=seq, t_valid=t_valid, nb=nb_scan, tt=tt)
        mix_c, glas = gla_mixer(proj, tail, gla0, p["gla_wg"], bg3, go_c3, layer, sl,
                                batch=batch, seq=seq, c=c, cv=min(c, t_valid), nb=gla_grp, group=gla_grp, n_layers=nl,
                                stack=gla_stack)
        mix_d, dns = gdn_mixer(proj, tail, dn0, dconv0, dn_conv_w, p["dn_alog"], p["dn_dtb"], go_d3, layer, sl,
                               batch=batch, seq=seq, c=c, cv=min(c, t_valid), nb=nb_gdn, group=gdn_grp, n_layers=nl,
                               stack=dn_stack)
        return (mix_a, mix_b, mix_c, mix_d), (s5r, s5i, lruh, glas, dns)

    hp = x_prompt.reshape(mp, D_MODEL)
    hs = x_sample.reshape(ms, D_MODEL)
    new_p, new_s = [], []
    stacks_p = stacks_s = (None, None)
    hgp, ssqp = prenorm(hp, g_norm3, 0, tm=DENSE_TILES_SAMPLE["tm"])
    hgs, ssqs = prenorm(hs, g_norm3, 0, tm=DENSE_TILES_SAMPLE["tm"])
    for i in range(nl):
        nxt = min(i + 1, nl - 1)
        proj, tail = in_proj(hgp, ssqp, p["w_main"], p["w_tail"], i, **IN_PROJ_TILES_PROMPT)
        mixes, (s5r, s5i, lruh, glas, dns) = mixers(
            proj, tail, i, batch=bp, seq=tp, t_valid=tp,
            st=(z_s5, z_s5, z_lru, z_lconv, z_gla, z_dn, z_dconv), sl=0,
            nb_scan=1, tt=256, c=GLA_CHUNK, gla_grp=4, nb_gdn=4, gdn_grp=2, stacks=stacks_p)
        stacks_p = (glas, dns)
        proj3 = proj.reshape(bp, tp, N_MAIN)
        new_p.append((s5r.reshape(bp, S5_GROUPS, S5_STATE), s5i.reshape(bp, S5_GROUPS, S5_STATE),
                      lruh.reshape(bp, W_MIX), proj3[:, tp - (CONV_W - 1):, COL_XB:COL_XB + W_MIX],
                      None, None, proj3[:, tp - (CONV_W - 1):, COL_QKVD:COL_QKVD + DN_QKV]))
        hp, hpb = out_proj(mixes, p["w_out"], hp, i, **OUT_PROJ_TILES_PROMPT)
        hp, hgp, ssqp = ple(hp, hpb, pe_p, p["ple_gate_w"], pgb3, p["ple_w"], g_norm3, i, nxt, **DENSE_TILES_PROMPT)

        proj_pad, tail_pad = in_proj(hgs, ssqs, p["w_main"], p["w_tail"], i, seq_rows=ts, seq_pad=tsp,
                                     **DENSE_TILES_SAMPLE)
        proj3 = proj_pad.reshape(bs, tsp, N_MAIN)[:, :ts]
        mixes, (s5r, s5i, lruh, glas, dns) = mixers(
            proj_pad, tail_pad, i, batch=bs, seq=tsp, t_valid=ts,
            st=(c_s5r, c_s5i, c_lru, state_lru_conv, state_gla, state_delta, state_delta_conv), sl=i,
            nb_scan=32, tt=tsp, c=tsp, gla_grp=8, nb_gdn=8, gdn_grp=4, stacks=stacks_s)
        stacks_s = (glas, dns)
        mixes = tuple(m.reshape(bs, tsp, W_MIX)[:, :ts].reshape(ms, W_MIX) for m in mixes)
        new_s.append((s5r.reshape(bs, S5_GROUPS, S5_STATE), s5i.reshape(bs, S5_GROUPS, S5_STATE),
                      lruh.reshape(bs, W_MIX), proj3[:, ts - (CONV_W - 1):, COL_XB:COL_XB + W_MIX],
                      None, None, proj3[:, ts - (CONV_W - 1):, COL_QKVD:COL_QKVD + DN_QKV]))
        hs, hsb = out_proj(mixes, p["w_out"], hs, i, **DENSE_TILES_SAMPLE)
        hs, hgs, ssqs = ple(hs, hsb, pe_s, p["ple_gate_w"], pgb3, p["ple_w"], g_norm3, i, nxt, **DENSE_TILES_SAMPLE)

    g_fin = g_final.reshape(1, D_MODEL)
    y_prompt = final_norm(hp, g_fin, tm=256).reshape(bp, tp, D_MODEL)
    y_sample = final_norm(hs, g_fin, tm=256).reshape(bs, ts, D_MODEL)

    def stk(lst, j):
        return jnp.stack([s[j] for s in lst], axis=0)

    return (y_prompt, y_sample,
            stk(new_p, 0), stk(new_p, 1), stk(new_p, 2), stk(new_p, 3), stacks_p[0], stacks_p[1], stk(new_p, 6),
            stk(new_s, 0), stk(new_s, 1), stk(new_s, 2), stk(new_s, 3), stacks_s[0], stacks_s[1], stk(new_s, 6))
```

```python
import functools
import math

import jax
import jax.numpy as jnp
import numpy as np
from jax import lax
from jax.experimental import pallas as pl
from jax.experimental.pallas import tpu as pltpu

F32 = jnp.float32
BF16 = jnp.bfloat16
EPS = 1e-6
NEG_INF = float("-inf")

D_MODEL = 4096
DEPTH = 4
W_MIX = 1024
S5_GROUPS = 64
S5_GROUP = 16
S5_STATE = 64
S5_CH = S5_GROUPS * S5_STATE
LRU_BLOCKS = 8
LRU_BLK = 128
LRU_C = 8.0
CONV_W = 4
GLA_H = 4
GLA_DK = 128
GLA_DV = 256
GLA_RANK = 16
GLA_TAU = 16.0
GLA_CHUNK = 64
GLA_SUB = 8
DN_H = 8
DN_DK = 128
DN_DV = 128
DN_QKV = 3072
DN_CHUNK = 64
D_PLE = 256
N_MAIN = 11264
COL_XA, COL_ZA, COL_XB, COL_ZB = 0, 1024, 2048, 3072
COL_QC, COL_KC, COL_VC = 4096, 4608, 5120
COL_QKVD, COL_ZC, COL_ZD = 6144, 9216, 10240
ORIG_ZC, ORIG_GC, ORIG_QKVD, ORIG_ZD, ORIG_AB = 6144, 7168, 7184, 10256, 11280
TAIL_W = 128
TAIL_G, TAIL_A, TAIL_B = 0, 16, 24
SAMPLE_T_PAD = 8

V7X_LANES = 128
V7X_SUBLANES = 8
V7X_MXU_DEPTH = 256
VMEM_LIMIT = 52 * 1024 * 1024
DENSE_TILES_PROMPT = dict(tm=1024, tn=512)
IN_PROJ_TILES_PROMPT = dict(tm=1024, tn=1024)
OUT_PROJ_TILES_PROMPT = dict(tm=1024, tn=1024)
VMEM_HEADROOM = 6 * 1024 * 1024
DENSE_TILES_SAMPLE = dict(tm=512, tn=1024)


def _cparams(n_axes, vmem_limit=VMEM_LIMIT):
    return pltpu.CompilerParams(dimension_semantics=("arbitrary",) * n_axes,
                                vmem_limit_bytes=vmem_limit)


def _sigmoid(x):
    return jax.nn.sigmoid(x)


def _silu(x):
    return x * jax.nn.sigmoid(x)


def _softplus(x):
    return jnp.maximum(x, 0.0) + jnp.log1p(jnp.exp(-jnp.abs(x)))


def _neg_expm1_nonpos(x):
    t = jnp.tanh(0.5 * x)
    return (-2.0 * t) / (1.0 - t)


def _log_sigmoid(x):
    return jnp.minimum(x, 0.0) - jnp.log1p(jnp.exp(-jnp.abs(x)))


def _gelu_tanh(x):
    c = math.sqrt(2.0 / math.pi)
    return x * (0.5 * (1.0 + jnp.tanh(c * (x + 0.044715 * (x * x * x)))))


def _dot(a, b):
    return jnp.dot(a, b, preferred_element_type=F32)


def _dot_nt(a, b):
    return lax.dot_general(a, b, (((1,), (1,)), ((), ())), preferred_element_type=F32)


def _dot_tn(a, b):
    return lax.dot_general(a, b, (((0,), (0,)), ((), ())), preferred_element_type=F32)


def _split3(x):
    h = x.astype(BF16)
    r = x - h.astype(F32)
    m = r.astype(BF16)
    l = (r - m.astype(F32)).astype(BF16)
    return h, m, l


def _cumsum_rows(x):
    c = x.shape[0]
    if c <= V7X_SUBLANES:
        row = lax.broadcasted_iota(jnp.int32, x.shape, 0)
        k = 1
        while k < c:
            x = x + jnp.where(row >= k, pltpu.roll(x, k, 0), 0.0)
            k *= 2
        return x
    h, m, l = _split3(x)
    if 3 * c <= V7X_MXU_DEPTH:
        col = lax.broadcasted_iota(jnp.int32, (c, 3 * c), 1)
        col = jnp.where(col >= 2 * c, col - 2 * c, jnp.where(col >= c, col - c, col))
        tri3 = (lax.broadcasted_iota(jnp.int32, (c, 3 * c), 0) >= col).astype(BF16)
        return _dot(tri3, jnp.concatenate([h, m, l], axis=0))
    tri = (lax.broadcasted_iota(jnp.int32, (c, c), 0) >= lax.broadcasted_iota(jnp.int32, (c, c), 1)).astype(BF16)
    return _dot(tri, h) + _dot(tri, m) + _dot(tri, l)


def _mm(a, b, dot=_dot, contract=1):
    if a.shape[contract] < 2 * V7X_SUBLANES:
        return dot(a, b)
    return dot(a.astype(BF16), b.astype(BF16))


def _prenorm_kernel(x_ref, g_ref, hg_ref, ssq_ref):
    x = x_ref[...]
    hg_ref[...] = (x * g_ref[...]).astype(BF16)
    ssq_ref[...] = jnp.sum(x * x, axis=-1, keepdims=True)


def prenorm(x, g_norm, layer, *, tm):
    m = x.shape[0]
    return pl.pallas_call(
        _prenorm_kernel,
        grid=(m // tm,),
        in_specs=[pl.BlockSpec((tm, D_MODEL), lambda i: (i, 0)),
                  pl.BlockSpec((None, 1, D_MODEL), lambda i: (layer, 0, 0))],
        out_specs=[pl.BlockSpec((tm, D_MODEL), lambda i: (i, 0)), pl.BlockSpec((tm, 1), lambda i: (i, 0))],
        out_shape=[jax.ShapeDtypeStruct((m, D_MODEL), BF16), jax.ShapeDtypeStruct((m, 1), F32)],
        compiler_params=_cparams(1),
        name="prenorm",
    )(x, g_norm)


def _in_proj_kernel(hg_ref, ssq_ref, w_ref, wt_ref, o_ref, ot_ref, *, seq_rows, seq_pad):
    r = lax.rsqrt(ssq_ref[...] * (1.0 / D_MODEL) + EPS)

    def put(dst_ref, val):
        if seq_pad == seq_rows:
            dst_ref[...] = val
        else:
            n_seq = val.shape[0] // seq_rows
            dst_ref[:, 0:seq_rows, :] = val.reshape(n_seq, seq_rows, val.shape[1])
            dst_ref[:, seq_rows:seq_pad, :] = jnp.zeros((n_seq, seq_pad - seq_rows, val.shape[1]), F32)

    @pl.when(pl.program_id(1) == 0)
    def _():
        put(ot_ref, _dot_nt(hg_ref[...], wt_ref[...]) * r)

    put(o_ref, _dot_nt(hg_ref[...], w_ref[...]) * r)


def in_proj(hg, ssq, w_main, w_tail, layer, *, tm, tn, seq_rows=None, seq_pad=None):
    m = hg.shape[0]
    grid = (m // tm, N_MAIN // tn)
    if seq_rows is None:
        seq_rows = seq_pad = 1
        out_specs = [pl.BlockSpec((tm, tn), lambda i, j: (i, j)), pl.BlockSpec((tm, TAIL_W), lambda i, j: (i, 0))]
        out_shape = [jax.ShapeDtypeStruct((m, N_MAIN), F32), jax.ShapeDtypeStruct((m, TAIL_W), F32)]
    else:
        n_seq = tm // seq_rows
        out_specs = [pl.BlockSpec((n_seq, seq_pad, tn), lambda i, j: (i, 0, j)),
                     pl.BlockSpec((n_seq, seq_pad, TAIL_W), lambda i, j: (i, 0, 0))]
        out_shape = [jax.ShapeDtypeStruct((m // seq_rows, seq_pad, N_MAIN), F32),
                     jax.ShapeDtypeStruct((m // seq_rows, seq_pad, TAIL_W), F32)]
    proj, tail = pl.pallas_call(
        functools.partial(_in_proj_kernel, seq_rows=seq_rows, seq_pad=seq_pad),
        grid=grid,
        in_specs=[
            pl.BlockSpec((tm, D_MODEL), lambda i, j: (i, 0)),
            pl.BlockSpec((tm, 1), lambda i, j: (i, 0)),
            pl.BlockSpec((None, tn, D_MODEL), lambda i, j: (layer, j, 0)),
            pl.BlockSpec((None, TAIL_W, D_MODEL), lambda i, j: (layer, 0, 0)),
        ],
        out_specs=out_specs,
        out_shape=out_shape,
        compiler_params=_cparams(2),
        name="in_proj",
    )(hg, ssq, w_main, w_tail)
    return proj.reshape(-1, N_MAIN), tail.reshape(-1, TAIL_W)


def _out_proj_kernel(ma_ref, mb_ref, mc_ref, md_ref, w_ref, h_ref, o_ref, ob_ref):
    acc = h_ref[...]
    for k, m_ref in enumerate((ma_ref, mb_ref, mc_ref, md_ref)):
        acc = acc + _dot(m_ref[...], w_ref[k * W_MIX:(k + 1) * W_MIX, :])
    o_ref[...] = acc
    ob_ref[...] = acc.astype(BF16)


def out_proj(mixes, w_out, h, layer, *, tm, tn):
    m = h.shape[0]
    grid = (m // tm, D_MODEL // tn)
    mix_spec = pl.BlockSpec((tm, W_MIX), lambda i, j: (i, 0))
    tile = pl.BlockSpec((tm, tn), lambda i, j: (i, j))
    return pl.pallas_call(
        _out_proj_kernel,
        grid=grid,
        in_specs=[mix_spec, mix_spec, mix_spec, mix_spec,
                  pl.BlockSpec((None, 4 * W_MIX, tn), lambda i, j: (layer, 0, j)), tile],
        out_specs=[tile, tile],
        out_shape=[jax.ShapeDtypeStruct((m, D_MODEL), F32), jax.ShapeDtypeStruct((m, D_MODEL), BF16)],
        compiler_params=_cparams(2, max(VMEM_LIMIT, 2 * (tm * D_MODEL * 2 + 4 * W_MIX * tn * 2 + tm * tn * 10)
                                        + VMEM_HEADROOM)),
        name="out_proj",
    )(*mixes, w_out, h)


def _ple_kernel(hrow_ref, htile_ref, pe_ref, wg_ref, bg_ref, wp_ref, gn_ref, o_ref, hg_ref, ssq_ref):
    gate = _sigmoid(_dot(hrow_ref[...], wg_ref[...]) + bg_ref[...])
    pv = _dot(pe_ref[...], wp_ref[...])
    h = htile_ref[...] + gate * pv
    o_ref[...] = h
    hg_ref[...] = (h * gn_ref[...]).astype(BF16)
    part = jnp.sum(h * h, axis=-1, keepdims=True)

    @pl.when(pl.program_id(1) == 0)
    def _():
        ssq_ref[...] = part

    @pl.when(pl.program_id(1) != 0)
    def _():
        ssq_ref[...] = ssq_ref[...] + part


def ple(h, hb, pe, w_gate, b_gate, w_ple, g_next, layer, next_layer, *, tm, tn):
    m = h.shape[0]
    grid = (m // tm, D_MODEL // tn)
    tile = pl.BlockSpec((tm, tn), lambda i, j: (i, j))
    return pl.pallas_call(
        _ple_kernel,
        grid=grid,
        in_specs=[
            pl.BlockSpec((tm, D_MODEL), lambda i, j: (i, 0)),
            tile,
            pl.BlockSpec((None, tm, D_PLE), lambda i, j: (layer, i, 0)),
            pl.BlockSpec((None, D_MODEL, tn), lambda i, j: (layer, 0, j)),
            pl.BlockSpec((None, 1, tn), lambda i, j: (layer, 0, j)),
            pl.BlockSpec((None, D_PLE, tn), lambda i, j: (layer, 0, j)),
            pl.BlockSpec((None, 1, tn), lambda i, j: (next_layer, 0, j)),
        ],
        out_specs=[tile, tile, pl.BlockSpec((tm, 1), lambda i, j: (i, 0))],
        out_shape=[jax.ShapeDtypeStruct((m, D_MODEL), F32), jax.ShapeDtypeStruct((m, D_MODEL), BF16),
                   jax.ShapeDtypeStruct((m, 1), F32)],
        compiler_params=_cparams(2),
        name="ple",
    )(hb, h, pe, w_gate, b_gate, w_ple, g_next)


REPACK_ROWS = 512


def _repack_kernel(a_ref, g_ref, ab_ref, o_ref, t_ref):
    o_ref[...] = a_ref[...].astype(BF16)

    @pl.when(pl.program_id(1) == 0)
    def _():
        t_ref[...] = jnp.zeros(t_ref.shape, BF16)
        t_ref[TAIL_G:TAIL_G + GLA_RANK, :] = g_ref[...].astype(BF16)
        t_ref[TAIL_A:TAIL_A + 2 * DN_H, :] = ab_ref[...].astype(BF16)


def repack_w_in(w_in_t):
    nl = w_in_t.shape[0]
    n_tiles = N_MAIN // REPACK_ROWS
    t_qkvd, t_zc, t_zd = COL_QKVD // REPACK_ROWS, COL_ZC // REPACK_ROWS, COL_ZD // REPACK_ROWS

    def src_row(j):
        return jnp.where(j < t_qkvd, j * REPACK_ROWS,
                         jnp.where(j < t_zc, ORIG_QKVD + (j - t_qkvd) * REPACK_ROWS,
                                   jnp.where(j < t_zd, ORIG_ZC + (j - t_zc) * REPACK_ROWS,
                                             ORIG_ZD + (j - t_zd) * REPACK_ROWS)))

    row_align = 2 * V7X_SUBLANES
    assert all(o % row_align == 0 for o in (ORIG_QKVD, ORIG_ZC, ORIG_ZD, ORIG_GC, ORIG_AB))

    def rows(n, start):
        return pl.BlockSpec((None, pl.Element(n), pl.Element(D_MODEL)),
                            lambda l, j: (l, pl.multiple_of(start(j), row_align), 0))

    return pl.pallas_call(
        _repack_kernel,
        grid=(nl, n_tiles),
        in_specs=[rows(REPACK_ROWS, src_row), rows(GLA_RANK, lambda j: ORIG_GC), rows(2 * DN_H, lambda j: ORIG_AB)],
        out_specs=[pl.BlockSpec((None, REPACK_ROWS, D_MODEL), lambda l, j: (l, j, 0)),
                   pl.BlockSpec((None, TAIL_W, D_MODEL), lambda l, j: (l, 0, 0))],
        out_shape=[jax.ShapeDtypeStruct((nl, N_MAIN, D_MODEL), BF16),
                   jax.ShapeDtypeStruct((nl, TAIL_W, D_MODEL), BF16)],
        compiler_params=_cparams(2),
        name="repack_w_in",
    )(w_in_t, w_in_t, w_in_t)


def _final_norm_kernel(x_ref, g_ref, o_ref):
    x = x_ref[...]
    ms = jnp.mean(x * x, axis=-1, keepdims=True)
    o_ref[...] = (x * lax.rsqrt(ms + EPS)) * g_ref[...]


def final_norm(h, g, *, tm):
    m = h.shape[0]
    return pl.pallas_call(
        _final_norm_kernel,
        grid=(m // tm,),
        in_specs=[pl.BlockSpec((tm, D_MODEL), lambda i: (i, 0)),
                  pl.BlockSpec((1, D_MODEL), lambda i: (0, 0))],
        out_specs=pl.BlockSpec((tm, D_MODEL), lambda i: (i, 0)),
        out_shape=jax.ShapeDtypeStruct((m, D_MODEL), F32),
        compiler_params=_cparams(1),
        name="final_norm",
    )(h, g)


S5_SEG = 32


def _s5_prep_kernel(lre_ref, lim_ref, ldt_ref, coef_ref, cre_ref, cim_ref):
    lre = lre_ref[...]
    lim = lim_ref[...]
    dt = jnp.exp(ldt_ref[...])
    ai = lim * dt
    mag = jnp.exp(lre * dt)
    ar = mag * jnp.cos(ai)
    aim = mag * jnp.sin(ai)
    den = lre * lre + lim * lim
    nr = ar - 1.0
    cre_ref[...] = (nr * lre + aim * lim) / den
    cim_ref[...] = (aim * lre - nr * lim) / den
    pr, pim = [ar], [aim]
    for _ in range(S5_SEG - 1):
        nr_, ni_ = pr[-1] * ar - pim[-1] * aim, pr[-1] * aim + pim[-1] * ar
        pr.append(nr_)
        pim.append(ni_)
    shape = (V7X_SUBLANES, S5_CH)
    coef_ref[0] = jnp.broadcast_to(ar, shape)
    coef_ref[1] = jnp.broadcast_to(aim, shape)
    coef_ref[2] = jnp.broadcast_to(pr[S5_SEG - 1], shape)
    coef_ref[3] = jnp.broadcast_to(pim[S5_SEG - 1], shape)


def s5_prep(lam_re, lam_im, log_dt_rep):
    nl = lam_re.shape[0]
    vec = pl.BlockSpec((None, 1, S5_CH), lambda l: (l, 0, 0))
    return pl.pallas_call(
        _s5_prep_kernel,
        grid=(nl,),
        in_specs=[vec, vec, vec],
        out_specs=[pl.BlockSpec((None, 4, V7X_SUBLANES, S5_CH), lambda l: (l, 0, 0, 0)), vec, vec],
        out_shape=[jax.ShapeDtypeStruct((nl, 4, V7X_SUBLANES, S5_CH), F32),
                   jax.ShapeDtypeStruct((nl, 1, S5_CH), F32),
                   jax.ShapeDtypeStruct((nl, 1, S5_CH), F32)],
        compiler_params=_cparams(1),
        name="s5_prep",
    )(lam_re, lam_im, log_dt_rep)


def _s5_bb_kernel(cr_ref, ci_ref, br_ref, bi_ref, or_ref, oi_ref):
    cr = cr_ref[...]
    ci = ci_ref[...]
    br = br_ref[...]
    bi = bi_ref[...]
    or_ref[...] = cr * br - ci * bi
    oi_ref[...] = cr * bi + ci * br


def s5_bb(coef_re_col, coef_im_col, b_re, b_im):
    nl = b_re.shape[0]
    rows = 1024
    col = pl.BlockSpec((None, rows, 1), lambda l, r: (l, r, 0))
    mat = pl.BlockSpec((None, rows, S5_GROUP), lambda l, r: (l, r, 0))
    return pl.pallas_call(
        _s5_bb_kernel,
        grid=(nl, S5_CH // rows),
        in_specs=[col, col, mat, mat],
        out_specs=[mat, mat],
        out_shape=[jax.ShapeDtypeStruct((nl, S5_CH, S5_GROUP), F32)] * 2,
        compiler_params=_cparams(2),
        name="s5_bb",
    )(coef_re_col, coef_im_col, b_re, b_im)


S5_STRIP = 512
S5_U_GROUPS = V7X_MXU_DEPTH // S5_GROUP
S5_U_IN = S5_U_GROUPS * S5_GROUP
S5_U_OUT = S5_U_GROUPS * S5_STATE
S5_Y_GROUPS = S5_STRIP // S5_STATE
S5_WU_SHAPE = (S5_GROUPS // S5_U_GROUPS, S5_U_IN, S5_U_OUT)
S5_WY_SHAPE = (S5_CH // S5_STRIP, S5_STRIP, S5_Y_GROUPS * S5_GROUP)


def _s5_kernel(xa_ref, za_ref, h0r_ref, h0i_ref, coef_ref, perm_ref, permt_ref,
               wur_ref, wui_ref, wyr_ref, wyi_ref, dsk_ref, wglu_ref, bglu_ref,
               mix_ref, hro_ref, hio_ref,
               ur_scr, ui_scr, cr_scr, ci_scr, y_scr, *, nb, tt, t_last):
    tc = pl.program_id(1)
    ntc = pl.num_programs(1)
    split = nb == 1
    lc = tt // V7X_SUBLANES if split else tt
    n_sets = 1 if split else nb // V7X_SUBLANES
    set_rows = V7X_SUBLANES * lc

    xb16 = _dot(perm_ref[...], xa_ref[...].astype(BF16)).astype(BF16)
    for k in range(S5_GROUPS // S5_U_GROUPS):
        xk = xb16[:, S5_U_IN * k:S5_U_IN * (k + 1)]
        ur_scr[:, S5_U_OUT * k:S5_U_OUT * (k + 1)] = _dot(xk, wur_ref[k])
        ui_scr[:, S5_U_OUT * k:S5_U_OUT * (k + 1)] = _dot(xk, wui_ref[k])

    @pl.when(tc == 0)
    def _():
        cr_scr[...] = h0r_ref[...]
        ci_scr[...] = h0i_ref[...]

    row8 = lax.broadcasted_iota(jnp.int32, (V7X_SUBLANES, S5_STRIP), 0)
    for st in range(S5_CH // S5_STRIP):
        ls = slice(st * S5_STRIP, (st + 1) * S5_STRIP)
        ar = coef_ref[0, :, ls]
        ai = coef_ref[1, :, ls]
        for s in range(n_sets):
            base = s * set_rows
            if split:
                h0r = jnp.zeros((V7X_SUBLANES, S5_STRIP), F32)
                h0i = jnp.zeros((V7X_SUBLANES, S5_STRIP), F32)
            else:
                h0r = jnp.zeros((V7X_SUBLANES, S5_STRIP), F32)
                h0i = jnp.zeros((V7X_SUBLANES, S5_STRIP), F32)
                for k in range(V7X_SUBLANES):
                    h0r = jnp.where(row8 == k, cr_scr[s * V7X_SUBLANES + k, :, ls], h0r)
                    h0i = jnp.where(row8 == k, ci_scr[s * V7X_SUBLANES + k, :, ls], h0i)

            def step(j, h, base=base, ls=ls, ar=ar, ai=ai):
                hr, hi = h
                rows = pl.ds(pl.multiple_of(base + j * V7X_SUBLANES, V7X_SUBLANES), V7X_SUBLANES)
                nr = (ar * hr - ai * hi) + ur_scr[rows, ls]
                ni = (ar * hi + ai * hr) + ui_scr[rows, ls]
                ur_scr[rows, ls] = nr
                ui_scr[rows, ls] = ni
                return nr, ni

            fr, fi = lax.fori_loop(0, lc, step, (h0r, h0i), unroll=True)
            if split:
                alr = coef_ref[2, 0:1, ls]
                ali = coef_ref[3, 0:1, ls]
                cr = cr_scr[0, :, ls]
                ci = ci_scr[0, :, ls]
                init_r = jnp.zeros((V7X_SUBLANES, S5_STRIP), F32)
                init_i = jnp.zeros((V7X_SUBLANES, S5_STRIP), F32)
                for k in range(V7X_SUBLANES):
                    init_r = jnp.where(row8 == k, cr, init_r)
                    init_i = jnp.where(row8 == k, ci, init_i)
                    cr, ci = (alr * cr - ali * ci) + fr[k:k + 1], (alr * ci + ali * cr) + fi[k:k + 1]
                cr_scr[0, :, ls] = cr
                ci_scr[0, :, ls] = ci

                def fix(j, c, base=base, ls=ls, ar=ar, ai=ai):
                    rows = pl.ds(pl.multiple_of(base + j * V7X_SUBLANES, V7X_SUBLANES), V7X_SUBLANES)
                    c_r = ar * c[0] - ai * c[1]
                    c_i = ar * c[1] + ai * c[0]
                    ur_scr[rows, ls] = ur_scr[rows, ls] + c_r
                    ui_scr[rows, ls] = ui_scr[rows, ls] + c_i
                    return c_r, c_i

                lax.fori_loop(0, lc, fix, (init_r, init_i), unroll=True)

        y_scr[:, V7X_LANES * st:V7X_LANES * (st + 1)] = (_dot(ur_scr[:, ls].astype(BF16), wyr_ref[st])
                                                           - _dot(ui_scr[:, ls].astype(BF16), wyi_ref[st]))
    yh, ym, yl = _split3(y_scr[...])
    pt = permt_ref[...]
    y = (_dot(pt, yh) + _dot(pt, ym) + _dot(pt, yl)) + dsk_ref[...] * xa_ref[...]
    ga = _gelu_tanh(y)
    ya = ga * _sigmoid(_dot(ga.astype(BF16), wglu_ref[...]) + bglu_ref[...])
    mix_ref[...] = (ya * _silu(za_ref[...])).astype(BF16)

    @pl.when(tc == ntc - 1)
    def _():
        if split:
            hro_ref[...] = cr_scr[...]
            hio_ref[...] = ci_scr[...]
        else:
            for s in range(n_sets):
                for k in range(V7X_SUBLANES):
                    r = s * set_rows + V7X_SUBLANES * t_last + k
                    hro_ref[s * V7X_SUBLANES + k] = ur_scr[r:r + 1, :]
                    hio_ref[s * V7X_SUBLANES + k] = ui_scr[r:r + 1, :]


def s5_mixer(proj, h0_re, h0_im, coef, wur, wui, wyr, wyi, dskip, wglu, bglu, layer, state_layer,
             *, batch, seq, t_valid, nb, tt):
    ntc = seq // tt
    assert nb == 1 or ntc == 1
    rows = nb * tt
    grid = (batch // nb, ntc)
    t_last = (t_valid - 1) - (ntc - 1) * tt
    assert 0 <= t_last < tt
    if nb == 1:
        assert tt == V7X_SUBLANES * S5_SEG and t_last == tt - 1
        lc = S5_SEG
    else:
        assert nb % V7X_SUBLANES == 0
        lc = tt
    idx = np.arange(rows)
    set_rows = V7X_SUBLANES * lc
    src = (idx // set_rows) * set_rows + (idx % V7X_SUBLANES) * lc + (idx % set_rows) // V7X_SUBLANES
    perm_np = np.zeros((rows, rows), np.float32)
    perm_np[idx, src] = 1.0
    perm = jnp.asarray(perm_np, BF16)
    perm_t = jnp.asarray(perm_np.T, BF16)
    sq = pl.BlockSpec((rows, rows), lambda b, t: (0, 0))

    def tok(cb):
        return pl.BlockSpec((rows, W_MIX), lambda b, t: (b * ntc + t, cb))

    st_in = pl.BlockSpec((None, nb, 1, S5_CH), lambda b, t: (state_layer, b, 0, 0))
    st_out = pl.BlockSpec((nb, 1, S5_CH), lambda b, t: (b, 0, 0))

    def lw(shape):
        nd = len(shape)
        return pl.BlockSpec((None,) + shape, lambda b, t: (layer,) + (0,) * nd)

    return pl.pallas_call(
        functools.partial(_s5_kernel, nb=nb, tt=tt, t_last=t_last),
        grid=grid,
        in_specs=[tok(COL_XA // W_MIX), tok(COL_ZA // W_MIX), st_in, st_in,
                  lw((4, V7X_SUBLANES, S5_CH)), sq, sq, lw(S5_WU_SHAPE), lw(S5_WU_SHAPE),
                  lw(S5_WY_SHAPE), lw(S5_WY_SHAPE), lw((1, W_MIX)), lw((W_MIX, W_MIX)), lw((1, W_MIX))],
        out_specs=[pl.BlockSpec((rows, W_MIX), lambda b, t: (b * ntc + t, 0)), st_out, st_out],
        out_shape=[jax.ShapeDtypeStruct((batch * seq, W_MIX), BF16),
                   jax.ShapeDtypeStruct((batch, 1, S5_CH), F32),
                   jax.ShapeDtypeStruct((batch, 1, S5_CH), F32)],
        scratch_shapes=[pltpu.VMEM((rows, S5_CH), F32), pltpu.VMEM((rows, S5_CH), F32),
                        pltpu.VMEM((nb, 1, S5_CH), F32), pltpu.VMEM((nb, 1, S5_CH), F32),
                        pltpu.VMEM((rows, W_MIX), F32)],
        compiler_params=_cparams(2),
        name="s5_mixer",
    )(proj, proj, h0_re, h0_im, coef, perm, perm_t, wur, wui, wyr, wyi, dskip, wglu, bglu)


LRU_STRIP = 512
CONV_PAD = V7X_SUBLANES


def _causal_conv(tail_scr, x, cw_ref, bi, cols=slice(None)):
    rows_n = x.shape[0]
    width = x.shape[1]
    n_tiles = rows_n // V7X_SUBLANES
    xe = jnp.concatenate([tail_scr[bi, :, cols], x], axis=0).reshape(n_tiles + 1, V7X_SUBLANES, width)
    row = lax.broadcasted_iota(jnp.int32, (n_tiles, V7X_SUBLANES, width), 1)
    acc = cw_ref[CONV_W - 1:CONV_W, cols] * x
    for jj in range(CONV_W - 1):
        shift = CONV_W - 1 - jj
        rot = pltpu.roll(xe, shift, 1)
        shifted = jnp.where(row < shift, rot[:-1], rot[1:]).reshape(rows_n, width)
        acc = acc + cw_ref[jj:jj + 1, cols] * shifted
    tail_scr[bi, :, cols] = xe[n_tiles]
    return acc


def _init_conv_tail(tail_scr, cv0_ref, nb):
    tail_scr[...] = jnp.zeros(tail_scr.shape, F32)
    for bi in range(nb):
        tail_scr[bi, CONV_PAD - (CONV_W - 1):CONV_PAD, :] = cv0_ref[bi]


def _lru_kernel(xb_ref, zb_ref, h0_ref, cv0_ref, cw_ref, cb_ref, wri_ref, br_ref, bi_ref, lam_ref,
                mix_ref, ho_ref,
                tail_scr, xc_scr, a_scr, b_scr, c_scr, *, nb, tt, t_last):
    tc = pl.program_id(1)
    ntc = pl.num_programs(1)

    @pl.when(tc == 0)
    def _():
        c_scr[...] = h0_ref[...]
        _init_conv_tail(tail_scr, cv0_ref, nb)

    def conv_body(bi, carry):
        rows = pl.ds(pl.multiple_of(bi * tt, V7X_SUBLANES), tt)
        acc = _causal_conv(tail_scr, xb_ref[rows, :], cw_ref, bi)
        xc_scr[rows, :] = acc + cb_ref[...]
        return carry

    lax.fori_loop(0, nb, conv_body, 0)

    xc = xc_scr[...]
    xc16 = xc.astype(BF16)
    sp = _softplus(-lam_ref[...])
    for blk in range(LRU_BLOCKS):
        cs = slice(blk * LRU_BLK, (blk + 1) * LRU_BLK)
        pre = _dot(xc16[:, cs], wri_ref[blk])
        r = _sigmoid(pre[:, :LRU_BLK] + br_ref[:, cs])
        ig = _sigmoid(pre[:, LRU_BLK:] + bi_ref[:, cs])
        log_a = (-LRU_C) * r * sp[:, cs]
        a_scr[:, cs] = jnp.exp(log_a)
        b_scr[:, cs] = jnp.sqrt(_neg_expm1_nonpos(2.0 * log_a)) * (ig * xc[:, cs])

    ng = tt // V7X_SUBLANES
    for s in range(W_MIX // LRU_STRIP):
        ls = slice(s * LRU_STRIP, (s + 1) * LRU_STRIP)
        row = lax.broadcasted_iota(jnp.int32, (V7X_SUBLANES, LRU_STRIP), 0)

        def seq_body(bi, carry, ls=ls, row=row):
            def tile_body(g, c):
                rows = pl.ds(pl.multiple_of(bi * tt + g * V7X_SUBLANES, V7X_SUBLANES), V7X_SUBLANES)
                a = a_scr[rows, ls]
                b = b_scr[rows, ls]
                for k in (1, 2, 4):
                    a_sh = jnp.where(row >= k, pltpu.roll(a, k, 0), 1.0)
                    b_sh = jnp.where(row >= k, pltpu.roll(b, k, 0), 0.0)
                    b = b + a * b_sh
                    a = a * a_sh
                h = b + a * c
                b_scr[rows, ls] = h
                return h[V7X_SUBLANES - 1:, :]

            c = lax.fori_loop(0, ng, tile_body, c_scr[bi, :, ls])
            c_scr[bi, :, ls] = c
            return carry

        lax.fori_loop(0, nb, seq_body, 0)

    mix_ref[...] = (b_scr[...] * _silu(zb_ref[...])).astype(BF16)

    @pl.when(tc == ntc - 1)
    def _():
        for bi in range(nb):
            r = bi * tt + t_last
            ho_ref[bi] = b_scr[r:r + 1, :]


def lru_mixer(proj, h0, conv0, cw, cb, wri, br, bi_, lam, layer, state_layer, *, batch, seq, t_valid, nb, tt):
    ntc = seq // tt
    assert nb == 1 or ntc == 1
    rows = nb * tt
    grid = (batch // nb, ntc)
    t_last = (t_valid - 1) - (ntc - 1) * tt

    def tok(cb_):
        return pl.BlockSpec((rows, W_MIX), lambda b, t: (b * ntc + t, cb_))

    def lw(shape):
        nd = len(shape)
        return pl.BlockSpec((None,) + shape, lambda b, t: (layer,) + (0,) * nd)

    return pl.pallas_call(
        functools.partial(_lru_kernel, nb=nb, tt=tt, t_last=t_last),
        grid=grid,
        in_specs=[tok(COL_XB // W_MIX), tok(COL_ZB // W_MIX),
                  pl.BlockSpec((None, nb, 1, W_MIX), lambda b, t: (state_layer, b, 0, 0)),
                  pl.BlockSpec((None, nb, CONV_W - 1, W_MIX), lambda b, t: (state_layer, b, 0, 0)),
                  lw((CONV_W, W_MIX)), lw((1, W_MIX)), lw((LRU_BLOCKS, LRU_BLK, 2 * LRU_BLK)),
                  lw((1, W_MIX)), lw((1, W_MIX)), lw((1, W_MIX))],
        out_specs=[pl.BlockSpec((rows, W_MIX), lambda b, t: (b * ntc + t, 0)),
                   pl.BlockSpec((nb, 1, W_MIX), lambda b, t: (b, 0, 0))],
        out_shape=[jax.ShapeDtypeStruct((batch * seq, W_MIX), BF16),
                   jax.ShapeDtypeStruct((batch, 1, W_MIX), F32)],
        scratch_shapes=[pltpu.VMEM((nb, CONV_PAD, W_MIX), F32),
                        pltpu.VMEM((rows, W_MIX), F32), pltpu.VMEM((rows, W_MIX), F32),
                        pltpu.VMEM((rows, W_MIX), F32), pltpu.VMEM((nb, 1, W_MIX), F32)],
        compiler_params=_cparams(2),
        name="lru_mixer",
    )(proj, proj, h0, conv0, cw, cb, wri, br, bi_, lam)


def _for_each_group(nb, g, group_body):
    assert nb % g == 0
    if nb == g:
        group_body(list(range(nb)))
    else:
        def body(i, carry):
            group_body([i * g + j for j in range(g)])
            return carry
        lax.fori_loop(0, nb // g, body, 0)


def _gla_kernel(q_ref, k_ref, v_ref, z_ref, tail_ref, s0_ref, wg_ref, bg_ref, go_ref, stack_ref,
                mix_ref, so_ref,
                s_scr, mix_scr, *, nb, group, c, cv, single_chunk):
    del stack_ref
    tc = pl.program_id(1)
    ntc = pl.num_programs(1)
    sb = min(GLA_SUB, c)
    nblk = c // sb

    s_src, s_dst = (s0_ref, so_ref) if single_chunk else (s_scr, s_scr)
    if not single_chunk:
        @pl.when(tc == 0)
        def _():
            s_scr[...] = s0_ref[...]

    row_c = lax.broadcasted_iota(jnp.int32, (c, 1), 0)
    lane_c = lax.broadcasted_iota(jnp.int32, (sb, c), 1)
    row_sb = lax.broadcasted_iota(jnp.int32, (sb, 1), 0)

    def group_body(bis):
        pairs = [(g, h) for g in range(len(bis)) for h in range(GLA_H)]
        heads = range(len(pairs))
        b_all, q_all, k_all, v_all = [], [], [], []
        for bi in bis:
            x = _dot(tail_ref[bi].astype(BF16), wg_ref[...]) + bg_ref[...]
            b_all.append(_cumsum_rows(_log_sigmoid(x) * (1.0 / GLA_TAU)))
            q_all.append(q_ref[bi])
            k_all.append(k_ref[bi])
            v_all.append(v_ref[bi])
        s_old = [s_src[bis[g], h] for g, h in pairs]
        q = [q_all[g][:, h * GLA_DK:(h + 1) * GLA_DK] * (GLA_DK ** -0.5) for g, h in pairs]
        k = [k_all[g][:, h * GLA_DK:(h + 1) * GLA_DK] for g, h in pairs]
        v = [v_all[g][:, h * GLA_DV:(h + 1) * GLA_DV] for g, h in pairs]
        b = [b_all[g][:, h * GLA_DK:(h + 1) * GLA_DK] for g, h in pairs]
        o_state = [_dot((q[h] * jnp.exp(b[h])).astype(BF16), s_old[h].astype(BF16)) for h in heads]
        s_new = []
        for h in heads:
            b_last = b[h][cv - 1:cv]
            dec = b_last - b[h]
            if cv < c:
                dec = jnp.where(row_c < cv, dec, NEG_INF)
            kd = k[h] * jnp.exp(dec)
            d_col = jnp.transpose(jnp.broadcast_to(jnp.exp(b_last), (V7X_SUBLANES, GLA_DK)))[:, 0:1]
            s_new.append(d_col * s_old[h] + _mm(kd, v[h], dot=_dot_tn, contract=0))
        att_off = []
        for h in heads:
            per_blk = [jnp.zeros((sb, c), F32)]
            for blk in range(1, nblk):
                r0 = blk * sb
                b_ref_row = b[h][r0 - 1:r0]
                qs = (q[h][r0:r0 + sb] * jnp.exp(b[h][r0:r0 + sb] - b_ref_row)).astype(BF16)
                kd = (k[h] * jnp.exp(jnp.where(row_c < r0, b_ref_row - b[h], NEG_INF))).astype(BF16)
                per_blk.append(_dot_nt(qs, kd))
            att_off.append(per_blk)
        att = []
        for h in heads:
            att_rows = []
            for blk in range(nblk):
                r0 = blk * sb
                q_i = q[h][r0:r0 + sb]
                b_i = b[h][r0:r0 + sb]
                a = att_off[h][blk]
                for sl in range(sb):
                    s_abs = r0 + sl
                    e = jnp.exp(jnp.where(row_sb >= sl, b_i - b[h][s_abs:s_abs + 1], NEG_INF))
                    col = jnp.sum(q_i * (k[h][s_abs:s_abs + 1] * e), axis=-1, keepdims=True)
                    a = jnp.where(lane_c == s_abs, col, a)
                att_rows.append(a)
            att.append(att_rows[0] if nblk == 1 else jnp.concatenate(att_rows, axis=0))
        o = [_mm(att[h], v[h]) + o_state[h] for h in heads]
        for p, (g, h) in enumerate(pairs):
            vs = slice(h * GLA_DV, (h + 1) * GLA_DV)
            ms = jnp.mean(o[p] * o[p], axis=-1, keepdims=True)
            on = (o[p] * lax.rsqrt(ms + EPS)) * go_ref[...]
            mix_scr[bis[g], :, vs] = on * _silu(z_ref[bis[g], :, vs])
            s_dst[bis[g], h] = s_new[p]

    _for_each_group(nb, group, group_body)
    mix_ref[...] = mix_scr[...].astype(BF16)

    if not single_chunk:
        @pl.when(tc == ntc - 1)
        def _():
            so_ref[...] = s_scr[...]


def _stack_alias(stack, n_inputs_before):
    spec = pl.BlockSpec(memory_space=pl.ANY)
    if stack is None:
        return jnp.zeros((V7X_SUBLANES, V7X_LANES), F32), spec, {}
    return stack, spec, {n_inputs_before: 1}


def gla_mixer(proj, tail, s0, wg, bg, go, layer, state_layer, *, batch, seq, c, cv, nb, group, n_layers, stack=None):
    ntc = seq // c
    assert batch % nb == 0 and seq % c == 0
    grid = (batch // nb, ntc)
    proj3 = proj.reshape(batch, seq, proj.shape[-1])
    tail3 = tail.reshape(batch, seq, TAIL_W)

    def tok(width, col):
        return pl.BlockSpec((nb, c, width), lambda b, t: (b, t, col // width))

    def lw(shape):
        nd = len(shape)
        return pl.BlockSpec((None,) + shape, lambda b, t: (layer,) + (0,) * nd)

    st_shape = (nb, GLA_H, GLA_DK, GLA_DV)
    stack_arg, stack_spec, aliases = _stack_alias(stack, 9)
    mix, state = pl.pallas_call(
        functools.partial(_gla_kernel, nb=nb, group=group, c=c, cv=cv, single_chunk=ntc == 1),
        grid=grid,
        in_specs=[tok(GLA_H * GLA_DK, COL_QC), tok(GLA_H * GLA_DK, COL_KC), tok(W_MIX, COL_VC), tok(W_MIX, COL_ZC),
                  tok(TAIL_W, 0),
                  pl.BlockSpec((None,) + st_shape, lambda b, t: (state_layer, b, 0, 0, 0)),
                  lw((TAIL_W, GLA_H * GLA_DK)), lw((1, GLA_H * GLA_DK)), lw((1, GLA_DV)), stack_spec],
        out_specs=[tok(W_MIX, 0),
                   pl.BlockSpec((None,) + st_shape, lambda b, t: (layer, b, 0, 0, 0))],
        out_shape=[jax.ShapeDtypeStruct((batch, seq, W_MIX), BF16),
                   jax.ShapeDtypeStruct((n_layers, batch, GLA_H, GLA_DK, GLA_DV), F32)],
        scratch_shapes=[pltpu.VMEM(st_shape, F32), pltpu.VMEM((nb, c, W_MIX), F32)],
        input_output_aliases=aliases,
        compiler_params=_cparams(2),
        name="gla_mixer",
    )(proj3, proj3, proj3, proj3, tail3, s0, wg, bg, go, stack_arg)
    return mix.reshape(batch * seq, W_MIX), state


def _l2norm(x):
    return x * lax.rsqrt(jnp.sum(x * x, axis=-1, keepdims=True) + EPS)


def _unit_lower_solve(ms, rhss, c, cv):
    if c <= V7X_SUBLANES:
        row = lax.broadcasted_iota(jnp.int32, (c, 1), 0)
        us = []
        for m, rhs in zip(ms, rhss):
            u = jnp.zeros_like(rhs)
            u_rows = []
            for t in range(cv):
                ut = rhs[t:t + 1]
                for s in range(t):
                    ut = ut - m[t:t + 1, s:s + 1] * u_rows[s]
                u_rows.append(ut)
                u = jnp.where(row == t, ut, u)
            us.append(u)
        return us
    xs = [-m for m in ms]
    ys = list(rhss)
    dv = rhss[0].shape[1]
    levels = int(math.log2(c))
    assert 3 * c <= V7X_MXU_DEPTH

    def hi_lo(t):
        hi = t.astype(BF16)
        return hi, (t - hi.astype(F32)).astype(BF16)

    for lvl in range(levels):
        last = lvl == levels - 1
        ps = []
        for x, y in zip(xs, ys):
            xh, xl = hi_lo(x)
            yh, yl = hi_lo(y)
            bh = yh if last else jnp.concatenate([yh, xh], axis=1)
            bl = yl if last else jnp.concatenate([yl, xl], axis=1)
            lhs = jnp.concatenate([xh.astype(F32), xh.astype(F32), xl.astype(F32)], axis=1).astype(BF16)
            ps.append(_dot(lhs, jnp.concatenate([bh, bl, bh], axis=0)))
        ys = [y + p[:, :dv] for y, p in zip(ys, ps)]
        if not last:
            xs = [p[:, dv:] for p in ps]
    return ys


def _gdn_kernel(qkv_ref, z_ref, tail_ref, s0_ref, cv0_ref, cw_ref, alog_ref, dtb_ref, go_ref, stack_ref,
                mix_ref, so_ref,
                s_scr, tail_scr, mix_scr, *, nb, group, c, cv, single_chunk):
    del stack_ref
    tc = pl.program_id(1)
    ntc = pl.num_programs(1)

    s_src, s_dst = (s0_ref, so_ref) if single_chunk else (s_scr, s_scr)

    @pl.when(tc == 0)
    def _():
        if not single_chunk:
            s_scr[...] = s0_ref[...]
        _init_conv_tail(tail_scr, cv0_ref, nb)

    row_c = lax.broadcasted_iota(jnp.int32, (c, 1), 0)
    ri = lax.broadcasted_iota(jnp.int32, (c, c), 0)
    ci = lax.broadcasted_iota(jnp.int32, (c, c), 1)
    lane_t = lax.broadcasted_iota(jnp.int32, (c, TAIL_W), 1)
    a_lanes = (lane_t >= TAIL_A) & (lane_t < TAIL_A + DN_H)

    def group_body(bis):
        pairs = [(g, h) for g in range(len(bis)) for h in range(DN_H)]
        heads = range(len(pairs))
        gam, gam_t, beta = [], [], []

        def conv_act(g, col0, width):
            cols = slice(col0, col0 + width)
            return _silu(_causal_conv(tail_scr, qkv_ref[bis[g], :, cols], cw_ref, bis[g], cols))

        for gi, bi in enumerate(bis):
            tl = tail_ref[bi]
            gg = _cumsum_rows(jnp.where(a_lanes, -jnp.exp(alog_ref[...]) * _softplus(tl + dtb_ref[...]), 0.0))
            gam.append(gg)
            gam_t.append(jnp.transpose(gg))
            beta.append(_sigmoid(tl))
        s_old = [s_src[bis[g], h] for g, h in pairs]
        s16 = [s.astype(BF16) for s in s_old]
        q16, k16, kf, v = [], [], [], []
        for g, h in pairs:
            q = _l2norm(conv_act(g, h * DN_DK, DN_DK)) * (DN_DK ** -0.5)
            k = _l2norm(conv_act(g, DN_H * DN_DK + h * DN_DK, DN_DK))
            q16.append(q.astype(BF16))
            k16.append(k.astype(BF16))
            kf.append(k)
            v.append(conv_act(g, 2 * DN_H * DN_DK + h * DN_DV, DN_DV))
        g_col = [gam[g][:, TAIL_A + h:TAIL_A + h + 1] for g, h in pairs]
        g_row = [gam_t[g][TAIL_A + h:TAIL_A + h + 1, :] for g, h in pairs]
        b_col = [beta[g][:, TAIL_B + h:TAIL_B + h + 1] for g, h in pairs]
        decay = [jnp.exp(jnp.where(ri >= ci, g_col[h] - g_row[h], NEG_INF)) for h in heads]
        eg = [jnp.exp(g_col[h]) for h in heads]
        kk = [_dot_nt(k16[h], k16[h]) for h in heads]
        k_s = [_dot(k16[h], s16[h]) for h in heads]
        q_s = [_dot(q16[h], s16[h]) for h in heads]
        qk = [_dot_nt(q16[h], k16[h]) for h in heads]
        m = [jnp.where(ri > ci, b_col[h] * kk[h] * decay[h], 0.0) for h in heads]
        rhs = [b_col[h] * (v[h] - eg[h] * k_s[h]) for h in heads]
        u = _unit_lower_solve(m, rhs, c, cv)
        o = [eg[h] * q_s[h] + _mm(qk[h] * decay[h], u[h]) for h in heads]
        kd = []
        for h in heads:
            dec = g_col[h][cv - 1:cv] - g_col[h]
            if cv < c:
                dec = jnp.where(row_c < cv, dec, NEG_INF)
            kd.append(kf[h] * jnp.exp(dec))
        s_new = [jnp.exp(g_col[h][cv - 1:cv]) * s_old[h] + _mm(kd[h], u[h], dot=_dot_tn, contract=0) for h in heads]
        for p, (g, h) in enumerate(pairs):
            vs = slice(h * DN_DV, (h + 1) * DN_DV)
            ms = jnp.mean(o[p] * o[p], axis=-1, keepdims=True)
            on = (o[p] * lax.rsqrt(ms + EPS)) * go_ref[...]
            mix_scr[bis[g], :, vs] = on * _silu(z_ref[bis[g], :, vs])
            s_dst[bis[g], h] = s_new[p]

    _for_each_group(nb, group, group_body)
    mix_ref[...] = mix_scr[...].astype(BF16)

    if not single_chunk:
        @pl.when(tc == ntc - 1)
        def _():
            so_ref[...] = s_scr[...]


def gdn_mixer(proj, tail, s0, conv0, cw, alog, dtb, go, layer, state_layer, *, batch, seq, c, cv, nb, group, n_layers,
              stack=None):
    ntc = seq // c
    assert batch % nb == 0 and seq % c == 0
    grid = (batch // nb, ntc)
    proj3 = proj.reshape(batch, seq, proj.shape[-1])
    tail3 = tail.reshape(batch, seq, TAIL_W)

    def tok(width, col):
        return pl.BlockSpec((nb, c, width), lambda b, t: (b, t, col // width))

    def lw(shape):
        nd = len(shape)
        return pl.BlockSpec((None,) + shape, lambda b, t: (layer,) + (0,) * nd)

    st_shape = (nb, DN_H, DN_DK, DN_DV)
    assert COL_QKVD % DN_QKV == 0
    stack_arg, stack_spec, aliases = _stack_alias(stack, 9)
    mix, state = pl.pallas_call(
        functools.partial(_gdn_kernel, nb=nb, group=group, c=c, cv=cv, single_chunk=ntc == 1),
        grid=grid,
        in_specs=[tok(DN_QKV, COL_QKVD),
                  tok(W_MIX, COL_ZD),
                  tok(TAIL_W, 0),
                  pl.BlockSpec((None,) + st_shape, lambda b, t: (state_layer, b, 0, 0, 0)),
                  pl.BlockSpec((None, nb, CONV_W - 1, DN_QKV), lambda b, t: (state_layer, b, 0, 0)),
                  lw((CONV_W, DN_QKV)), lw((1, TAIL_W)), lw((1, TAIL_W)), lw((1, DN_DV)), stack_spec],
        out_specs=[tok(W_MIX, 0),
                   pl.BlockSpec((None,) + st_shape, lambda b, t: (layer, b, 0, 0, 0))],
        out_shape=[jax.ShapeDtypeStruct((batch, seq, W_MIX), BF16),
                   jax.ShapeDtypeStruct((n_layers, batch, DN_H, DN_DK, DN_DV), F32)],
        input_output_aliases=aliases,
        scratch_shapes=[pltpu.VMEM(st_shape, F32),
                        pltpu.VMEM((nb, CONV_PAD, DN_QKV), F32),
                        pltpu.VMEM((nb, c, W_MIX), F32)],
        compiler_params=_cparams(2),
        name="gdn_mixer",
    )(proj3, proj3, tail3, s0, conv0, cw, alog, dtb, go, stack_arg)
    return mix.reshape(batch * seq, W_MIX), state


def _prepare_weights(w_in, s5_lam_re, s5_lam_im, s5_log_dt, s5_b_re, s5_b_im, s5_c_re, s5_c_im, s5_d,
                     s5_w_glu, lru_w_r, lru_w_i, gla_w_gate, dn_a_log, dn_dt_bias, w_out, ple_w, ple_gate_w):
    nl = w_in.shape[0]
    p = {}
    p["w_main"], p["w_tail"] = repack_w_in(jnp.swapaxes(w_in, 1, 2))
    p["w_out"] = w_out.astype(BF16)
    p["ple_w"] = ple_w.astype(BF16)
    p["ple_gate_w"] = ple_gate_w.astype(BF16)
    p["w_glu"] = s5_w_glu.astype(BF16)
    ldt = jnp.repeat(s5_log_dt, S5_STATE, axis=1).reshape(nl, 1, S5_CH)
    coef, cre, cim = s5_prep(s5_lam_re.reshape(nl, 1, S5_CH), s5_lam_im.reshape(nl, 1, S5_CH), ldt)
    bb_re, bb_im = s5_bb(cre.reshape(nl, S5_CH, 1), cim.reshape(nl, S5_CH, 1),
                         s5_b_re.reshape(nl, S5_CH, S5_GROUP), s5_b_im.reshape(nl, S5_CH, S5_GROUP))
    eye16 = jnp.eye(S5_U_GROUPS, dtype=F32)
    eye8 = jnp.eye(S5_Y_GROUPS, dtype=F32)

    def pack_u(bb):
        t = bb.reshape(nl, S5_GROUPS // S5_U_GROUPS, S5_U_GROUPS, S5_STATE, S5_GROUP)
        return jnp.einsum("lkgph,gG->lkghGp", t, eye16).reshape((nl,) + S5_WU_SHAPE).astype(BF16)

    def pack_y(cc):
        t = cc.reshape(nl, S5_GROUPS // S5_Y_GROUPS, S5_Y_GROUPS, S5_GROUP, S5_STATE)
        return jnp.einsum("ljghp,gG->ljgpGh", t, eye8).reshape((nl,) + S5_WY_SHAPE).astype(BF16)

    p["s5_coef"] = coef
    p["s5_wur"], p["s5_wui"] = pack_u(bb_re), pack_u(bb_im)
    p["s5_wyr"], p["s5_wyi"] = pack_y(s5_c_re), pack_y(s5_c_im)
    p["s5_d"] = s5_d.reshape(nl, 1, W_MIX)
    p["lru_wri"] = jnp.concatenate([lru_w_r, lru_w_i], axis=-1).astype(BF16)
    p["gla_wg"] = jnp.concatenate([gla_w_gate, jnp.zeros((nl, TAIL_W - GLA_RANK, GLA_H * GLA_DK), F32)],
                                  axis=1).astype(BF16)

    def tail_row(x):
        return jnp.pad(x, ((0, 0), (TAIL_A, TAIL_W - TAIL_A - DN_H))).reshape(nl, 1, TAIL_W)

    p["dn_alog"] = tail_row(dn_a_log)
    p["dn_dtb"] = tail_row(dn_dt_bias)
    return p


def kernel(x_prompt, x_sample, state_s5_re, state_s5_im, state_lru_h, state_lru_conv, state_gla, state_delta, state_delta_conv, p_prompt, p_sample, g_norm, w_in, s5_lam_re, s5_lam_im, s5_log_dt, s5_b_re, s5_b_im, s5_c_re, s5_c_im, s5_d, s5_w_glu, s5_b_glu, lru_conv_w, lru_conv_b, lru_w_r, lru_b_r, lru_w_i, lru_b_i, lru_lam, gla_w_gate, gla_b_gate, gla_g_out, dn_conv_w, dn_a_log, dn_dt_bias, dn_g_out, w_out, ple_w, ple_gate_w, ple_gate_b, g_final):
    nl = w_in.shape[0]
    bp, tp, _ = x_prompt.shape
    bs, ts, _ = x_sample.shape
    tsp = SAMPLE_T_PAD
    mp, ms = bp * tp, bs * ts

    p = _prepare_weights(w_in, s5_lam_re, s5_lam_im, s5_log_dt, s5_b_re, s5_b_im, s5_c_re, s5_c_im, s5_d,
                         s5_w_glu, lru_w_r, lru_w_i, gla_w_gate, dn_a_log, dn_dt_bias, w_out, ple_w, ple_gate_w)

    def vec(x):
        return x.reshape(nl, 1, x.shape[-1])

    g_norm3, b_glu3, cb3 = vec(g_norm), vec(s5_b_glu), vec(lru_conv_b)
    br3, bi3, lam3 = vec(lru_b_r), vec(lru_b_i), vec(lru_lam)
    bg3, go_c3, go_d3, pgb3 = vec(gla_b_gate), vec(gla_g_out), vec(dn_g_out), vec(ple_gate_b)
    pe_p = p_prompt.reshape(nl, mp, D_PLE).astype(BF16)
    pe_s = p_sample.reshape(nl, ms, D_PLE).astype(BF16)

    z_s5 = jnp.zeros((1, bp, 1, S5_CH), F32)
    z_lru = jnp.zeros((1, bp, 1, W_MIX), F32)
    z_lconv = jnp.zeros((1, bp, CONV_W - 1, W_MIX), F32)
    z_gla = jnp.zeros((1, bp, GLA_H, GLA_DK, GLA_DV), F32)
    z_dn = jnp.zeros((1, bp, DN_H, DN_DK, DN_DV), F32)
    z_dconv = jnp.zeros((1, bp, CONV_W - 1, DN_QKV), F32)
    c_s5r = state_s5_re.reshape(nl, bs, 1, S5_CH)
    c_s5i = state_s5_im.reshape(nl, bs, 1, S5_CH)
    c_lru = state_lru_h.reshape(nl, bs, 1, W_MIX)

    def mixers(proj, tail, layer, *, batch, seq, t_valid, st, sl, nb_scan, tt, c, gla_grp, nb_gdn, gdn_grp, stacks):
        s5r0, s5i0, lru0, lconv0, gla0, dn0, dconv0 = st
        gla_stack, dn_stack = stacks
        mix_a, s5r, s5i = s5_mixer(proj, s5r0, s5i0, p["s5_coef"], p["s5_wur"], p["s5_wui"], p["s5_wyr"],
                                   p["s5_wyi"], p["s5_d"], p["w_glu"], b_glu3, layer, sl,
                                   batch=batch, seq=seq, t_valid=t_valid, nb=nb_scan, tt=tt)
        mix_b, lruh = lru_mixer(proj, lru0, lconv0, lru_conv_w, cb3, p["lru_wri"], br3, bi3, lam3, layer, sl,
                                batch=batch, seq=seq, t_valid=t_valid, nb=nb_scan, tt=tt)
        mix_c, glas = gla_mixer(proj, tail, gla0, p["gla_wg"], bg3, go_c3, layer, sl,
                                batch=batch, seq=seq, c=c, cv=min(c, t_valid), nb=gla_grp, group=gla_grp, n_layers=nl,
                                stack=gla_stack)
        mix_d, dns = gdn_mixer(proj, tail, dn0, dconv0, dn_conv_w, p["dn_alog"], p["dn_dtb"], go_d3, layer, sl,
                               batch=batch, seq=seq, c=c, cv=min(c, t_valid), nb=nb_gdn, group=gdn_grp, n_layers=nl,
                               stack=dn_stack)
        return (mix_a, mix_b, mix_c, mix_d), (s5r, s5i, lruh, glas, dns)

    hp = x_prompt.reshape(mp, D_MODEL)
    hs = x_sample.reshape(ms, D_MODEL)
    new_p, new_s = [], []
    stacks_p = stacks_s = (None, None)
    hgp, ssqp = prenorm(hp, g_norm3, 0, tm=DENSE_TILES_SAMPLE["tm"])
    hgs, ssqs = prenorm(hs, g_norm3, 0, tm=DENSE_TILES_SAMPLE["tm"])
    for i in range(nl):
        nxt = min(i + 1, nl - 1)
        proj, tail = in_proj(hgp, ssqp, p["w_main"], p["w_tail"], i, **IN_PROJ_TILES_PROMPT)
        mixes, (s5r, s5i, lruh, glas, dns) = mixers(
            proj, tail, i, batch=bp, seq=tp, t_valid=tp,
            st=(z_s5, z_s5, z_lru, z_lconv, z_gla, z_dn, z_dconv), sl=0,
            nb_scan=1, tt=256, c=GLA_CHUNK, gla_grp=4, nb_gdn=4, gdn_grp=2, stacks=stacks_p)
        stacks_p = (glas, dns)
        proj3 = proj.reshape(bp, tp, N_MAIN)
        new_p.append((s5r.reshape(bp, S5_GROUPS, S5_STATE), s5i.reshape(bp, S5_GROUPS, S5_STATE),
                      lruh.reshape(bp, W_MIX), proj3[:, tp - (CONV_W - 1):, COL_XB:COL_XB + W_MIX],
                      None, None, proj3[:, tp - (CONV_W - 1):, COL_QKVD:COL_QKVD + DN_QKV]))
        hp, hpb = out_proj(mixes, p["w_out"], hp, i, **OUT_PROJ_TILES_PROMPT)
        hp, hgp, ssqp = ple(hp, hpb, pe_p, p["ple_gate_w"], pgb3, p["ple_w"], g_norm3, i, nxt, **DENSE_TILES_PROMPT)

        proj_pad, tail_pad = in_proj(hgs, ssqs, p["w_main"], p["w_tail"], i, seq_rows=ts, seq_pad=tsp,
                                     **DENSE_TILES_SAMPLE)
        proj3 = proj_pad.reshape(bs, tsp, N_MAIN)[:, :ts]
        mixes, (s5r, s5i, lruh, glas, dns) = mixers(
            proj_pad, tail_pad, i, batch=bs, seq=tsp, t_valid=ts,
            st=(c_s5r, c_s5i, c_lru, state_lru_conv, state_gla, state_delta, state_delta_conv), sl=i,
            nb_scan=32, tt=tsp, c=tsp, gla_grp=8, nb_gdn=8, gdn_grp=4, stacks=stacks_s)
        stacks_s = (glas, dns)
        mixes = tuple(m.reshape(bs, tsp, W_MIX)[:, :ts].reshape(ms, W_MIX) for m in mixes)
        new_s.append((s5r.reshape(bs, S5_GROUPS, S5_STATE), s5i.reshape(bs, S5_GROUPS, S5_STATE),
                      lruh.reshape(bs, W_MIX), proj3[:, ts - (CONV_W - 1):, COL_XB:COL_XB + W_MIX],
                      None, None, proj3[:, ts - (CONV_W - 1):, COL_QKVD:COL_QKVD + DN_QKV]))
        hs, hsb = out_proj(mixes, p["w_out"], hs, i, **DENSE_TILES_SAMPLE)
        hs, hgs, ssqs = ple(hs, hsb, pe_s, p["ple_gate_w"], pgb3, p["ple_w"], g_norm3, i, nxt, **DENSE_TILES_SAMPLE)

    g_fin = g_final.reshape(1, D_MODEL)
    y_prompt = final_norm(hp, g_fin, tm=256).reshape(bp, tp, D_MODEL)
    y_sample = final_norm(hs, g_fin, tm=256).reshape(bs, ts, D_MODEL)

    def stk(lst, j):
        return jnp.stack([s[j] for s in lst], axis=0)

    return (y_prompt, y_sample,
            stk(new_p, 0), stk(new_p, 1), stk(new_p, 2), stk(new_p, 3), stacks_p[0], stacks_p[1], stk(new_p, 6),
            stk(new_s, 0), stk(new_s, 1), stk(new_s, 2), stk(new_s, 3), stacks_s[0], stacks_s[1], stk(new_s, 6))
```
